```python
import jax, jax.numpy as jnp
from jax import lax
import numpy as np

D_MODEL = 2048
BATCH = 8
SEQ = 8192
DEPTH = 4

HEAD_DIM = 128
N_HEADS_A = 12
N_HEADS_B = 12
WIDTH_A = N_HEADS_A * HEAD_DIM
WIDTH_B = N_HEADS_B * HEAD_DIM
DILATED_PATTERNS = ((128, 1), (512, 4), (2048, 16))
Q_BLOCK = 128
ROPE_THETA = 10000.0
LN_EPS = 1e-5
DEEPNORM_ALPHA = float((2 * DEPTH) ** 0.25)
DEEPNORM_BETA = float((8 * DEPTH) ** -0.25)
FORGET_BIAS_INIT = 2.0

SPLIT_SIZES = (WIDTH_A, WIDTH_A, WIDTH_A, WIDTH_A,
               WIDTH_B, WIDTH_B, WIDTH_B, WIDTH_B,
               N_HEADS_B,
               2 * D_MODEL)
SPLIT_POINTS = tuple(int(v) for v in np.cumsum(SPLIT_SIZES)[:-1])
N_IN_COLS = int(sum(SPLIT_SIZES))

kernel_name = "hybrid_dilated_forgetting_attention_deepnorm"


def rope(t, pos):
    half = HEAD_DIM // 2
    inv_freq = ROPE_THETA ** (-jnp.arange(half, dtype=jnp.float32) / half)
    ang = pos.astype(jnp.float32)[:, None] * inv_freq[None, :]
    cos = jnp.cos(ang)[None, :, None, :]
    sin = jnp.sin(ang)[None, :, None, :]
    t32 = t.astype(jnp.float32)
    t1, t2 = t32[..., :half], t32[..., half:]
    return jnp.concatenate([t1 * cos - t2 * sin, t1 * sin + t2 * cos], axis=-1).astype(t.dtype)


def dilated_pattern(q, k, v, window, dilation):
    B, S, H, hd = q.shape
    span = window // dilation
    L = S // dilation
    nblk = -(-L // Q_BLOCK)
    Lp = nblk * Q_BLOCK

    def to_residue(t):
        t = t.reshape(B, L, dilation, H, hd).transpose(0, 2, 1, 3, 4)
        return jnp.pad(t, ((0, 0), (0, 0), (0, Lp - L), (0, 0), (0, 0)))

    def band_keys(t):
        tp = jnp.pad(t, ((0, 0), (0, 0), (Q_BLOCK, 0), (0, 0), (0, 0)))
        tp = tp.reshape(B, dilation, nblk + 1, Q_BLOCK, H, hd)
        return jnp.concatenate([tp[:, :, :-1], tp[:, :, 1:]], axis=3)

    qb = to_residue(q).reshape(B, dilation, nblk, Q_BLOCK, H, hd)
    kb = band_keys(to_residue(k))
    vb = band_keys(to_residue(v))

    n_idx = jnp.arange(nblk)[:, None, None]
    i_idx = jnp.arange(Q_BLOCK)[None, :, None]
    j_idx = jnp.arange(2 * Q_BLOCK)[None, None, :]
    dist = Q_BLOCK + i_idx - j_idx
    valid = (dist >= 0) & (dist <= span) & (n_idx * Q_BLOCK + j_idx - Q_BLOCK >= 0)
    valid = valid[None, None, :, None]

    scale = HEAD_DIM ** -0.5
    s = jnp.einsum('bdnihe,bdnjhe->bdnhij', qb, kb, preferred_element_type=jnp.float32) * scale
    s = jnp.where(valid, s, -jnp.inf)
    m = jnp.max(s, axis=-1, keepdims=True)
    p = jnp.exp(s - m)
    den = jnp.sum(p, axis=-1, keepdims=True)
    out = jnp.einsum('bdnhij,bdnjhe->bdnihe', p, vb.astype(jnp.float32))
    out = out / jnp.moveaxis(den, 3, 4)
    lse = jnp.moveaxis((m + jnp.log(den))[..., 0], 3, 4)

    out = out.reshape(B, dilation, Lp, H, hd)[:, :, :L].transpose(0, 2, 1, 3, 4).reshape(B, S, H, hd)
    lse = lse.reshape(B, dilation, Lp, H)[:, :, :L].transpose(0, 2, 1, 3).reshape(B, S, H)
    return out, lse


def dilated_mixture(q, k, v):
    outs, lses = [], []
    for window, dilation in DILATED_PATTERNS:
        o, l = dilated_pattern(q, k, v, window, dilation)
        outs.append(o)
        lses.append(l)
    w = jax.nn.softmax(jnp.stack(lses, axis=0), axis=0)
    return jnp.sum(w[..., None] * jnp.stack(outs, axis=0), axis=0).astype(q.dtype)


def forgetting_attention(q, k, v, log_f):
    B, S, H, hd = q.shape
    nblk = S // Q_BLOCK
    cum = jnp.cumsum(log_f, axis=1).transpose(0, 2, 1)
    q_blocks = q.reshape(B, nblk, Q_BLOCK, H, hd).transpose(1, 0, 2, 3, 4)
    c_blocks = cum.reshape(B, H, nblk, Q_BLOCK).transpose(2, 0, 1, 3)
    kpos = jnp.arange(S)
    scale = HEAD_DIM ** -0.5

    def block(args):
        n, qn, cn = args
        s = jnp.einsum('bihe,bjhe->bhij', qn, k, preferred_element_type=jnp.float32) * scale
        s = s + cn[..., :, None] - cum[..., None, :]
        qpos = n * Q_BLOCK + jnp.arange(Q_BLOCK)
        causal = kpos[None, :] <= qpos[:, None]
        s = jnp.where(causal[None, None], s, -jnp.inf)
        p = jax.nn.softmax(s, axis=-1)
        return jnp.einsum('bhij,bjhe->bihe', p, v.astype(jnp.float32)).astype(q.dtype)

    out = lax.map(block, (jnp.arange(nblk), q_blocks, c_blocks))
    return out.transpose(1, 0, 2, 3, 4).reshape(B, S, H, hd)


def layer_norm(x, g, b):
    x32 = x.astype(jnp.float32)
    mu = jnp.mean(x32, axis=-1, keepdims=True)
    var = jnp.mean(jnp.square(x32 - mu), axis=-1, keepdims=True)
    y = (x32 - mu) * lax.rsqrt(var + LN_EPS)
    return (y * g.astype(jnp.float32) + b.astype(jnp.float32)).astype(x.dtype)


def hybrid_layer(x, pos, w_in, b_forget, b_gate, w_up_a, w_up_b, w_out, ln_g, ln_b):
    B, S, _ = x.shape
    h = jnp.einsum('bsd,dc->bsc', x, w_in)
    qa, ka, va, za, qb, kb, vb, zb, f_logit, g_logit = jnp.split(h, SPLIT_POINTS, axis=-1)
    heads = lambda t, n: t.reshape(B, S, n, HEAD_DIM)

    qa = rope(heads(qa, N_HEADS_A), pos)
    ka = rope(heads(ka, N_HEADS_A), pos)
    out_a = dilated_mixture(qa, ka, heads(va, N_HEADS_A)).reshape(B, S, WIDTH_A)
    up_a = jnp.einsum('bsw,wd->bsd', out_a * jax.nn.silu(za), w_up_a)

    log_f = jax.nn.log_sigmoid((f_logit + b_forget).astype(jnp.float32))
    out_b = forgetting_attention(heads(qb, N_HEADS_B), heads(kb, N_HEADS_B),
                                 heads(vb, N_HEADS_B), log_f).reshape(B, S, WIDTH_B)
    up_b = jnp.einsum('bsw,wd->bsd', out_b * jax.nn.silu(zb), w_up_b)

    gates = jax.nn.sigmoid(g_logit + b_gate)
    g_a, g_b = gates[..., :D_MODEL], gates[..., D_MODEL:]
    y = jnp.einsum('bsd,de->bse', g_a * up_a + g_b * up_b, w_out)

    return layer_norm(DEEPNORM_ALPHA * x + y, ln_g, ln_b)


def _fwd_setup_inputs(seed: int = 0) -> dict:
    key = jax.random.key(seed)
    ks = jax.random.split(key, 10)
    x = jax.random.normal(ks[0], (BATCH, SEQ, D_MODEL), jnp.float32)

    offs = (0,) + SPLIT_POINTS
    col_scale = jnp.ones((N_IN_COLS,), jnp.float32)
    col_scale = col_scale.at[offs[2]:offs[3]].set(DEEPNORM_BETA)
    col_scale = col_scale.at[offs[6]:offs[7]].set(DEEPNORM_BETA)
    w_in = jax.random.normal(ks[1], (DEPTH, D_MODEL, N_IN_COLS), jnp.float32) * (D_MODEL ** -0.5) * col_scale
    b_forget = FORGET_BIAS_INIT + 0.1 * jax.random.normal(ks[2], (DEPTH, N_HEADS_B), jnp.float32)
    b_gate = 0.02 * jax.random.normal(ks[3], (DEPTH, 2 * D_MODEL), jnp.float32)
    w_up_a = jax.random.normal(ks[4], (DEPTH, WIDTH_A, D_MODEL), jnp.float32) * (WIDTH_A ** -0.5) * DEEPNORM_BETA
    w_up_b = jax.random.normal(ks[5], (DEPTH, WIDTH_B, D_MODEL), jnp.float32) * (WIDTH_B ** -0.5) * DEEPNORM_BETA
    w_out = jax.random.normal(ks[6], (DEPTH, D_MODEL, D_MODEL), jnp.float32) * (D_MODEL ** -0.5) * DEEPNORM_BETA
    ln_g = 1.0 + 0.02 * jax.random.normal(ks[7], (DEPTH, D_MODEL), jnp.float32)
    ln_b = 0.02 * jax.random.normal(ks[8], (DEPTH, D_MODEL), jnp.float32)
    return {"x": x, "w_in": w_in, "b_forget": b_forget, "b_gate": b_gate,
            "w_up_a": w_up_a, "w_up_b": w_up_b, "w_out": w_out,
            "ln_g": ln_g, "ln_b": ln_b}


def _fwd_reference(x, w_in, b_forget, b_gate, w_up_a, w_up_b, w_out, ln_g, ln_b):
    pos = jnp.arange(x.shape[1], dtype=jnp.int32)
    for l in range(DEPTH):
        x = hybrid_layer(x, pos, w_in[l], b_forget[l], b_gate[l],
                         w_up_a[l], w_up_b[l], w_out[l], ln_g[l], ln_b[l])
    return x


import jax as _jax
import jax.numpy as _jnp

TWIN_FORMAT = 'train_step'
FWD_PARAMS = ['x', 'w_in', 'b_forget', 'b_gate', 'w_up_a', 'w_up_b', 'w_out', 'ln_g', 'ln_b']
TWIN_WEIGHTS = ['w_in', 'b_forget', 'b_gate', 'w_up_a', 'w_up_b', 'w_out', 'ln_g', 'ln_b']
TWIN_DIFF_INPUT = 'x'
TWIN_INPUTS = ['x', 'w_in', 'b_forget', 'b_gate', 'w_up_a', 'w_up_b', 'w_out', 'ln_g', 'ln_b', 'loss_target', 'm_w_in', 'm_b_forget', 'm_b_gate', 'm_w_up_a', 'm_w_up_b', 'm_w_out', 'm_ln_g', 'm_ln_b', 'v_w_in', 'v_b_forget', 'v_b_gate', 'v_w_up_a', 'v_w_up_b', 'v_w_out', 'v_ln_g', 'v_ln_b']
TWIN_OUTPUTS = ['loss', 'grad_x', 'grad_w_in', 'grad_b_forget', 'grad_b_gate', 'grad_w_up_a', 'grad_w_up_b', 'grad_w_out', 'grad_ln_g', 'grad_ln_b', 'delta_w_in', 'delta_b_forget', 'delta_b_gate', 'delta_w_up_a', 'delta_w_up_b', 'delta_w_out', 'delta_ln_g', 'delta_ln_b', 'new_m_w_in', 'new_m_b_forget', 'new_m_b_gate', 'new_m_w_up_a', 'new_m_w_up_b', 'new_m_w_out', 'new_m_ln_g', 'new_m_ln_b', 'new_v_w_in', 'new_v_b_forget', 'new_v_b_gate', 'new_v_w_up_a', 'new_v_w_up_b', 'new_v_w_out', 'new_v_ln_g', 'new_v_ln_b']
TWIN_LEAF_KINDS = {'loss': 'loss', 'grad_x': 'grad_x', 'grad_w_in': 'grad_w', 'grad_b_forget': 'grad_w', 'grad_b_gate': 'grad_w', 'grad_w_up_a': 'grad_w', 'grad_w_up_b': 'grad_w', 'grad_w_out': 'grad_w', 'grad_ln_g': 'grad_w', 'grad_ln_b': 'grad_w', 'delta_w_in': 'delta_w', 'delta_b_forget': 'delta_w', 'delta_b_gate': 'delta_w', 'delta_w_up_a': 'delta_w', 'delta_w_up_b': 'delta_w', 'delta_w_out': 'delta_w', 'delta_ln_g': 'delta_w', 'delta_ln_b': 'delta_w', 'new_m_w_in': 'new_m', 'new_m_b_forget': 'new_m', 'new_m_b_gate': 'new_m', 'new_m_w_up_a': 'new_m', 'new_m_w_up_b': 'new_m', 'new_m_w_out': 'new_m', 'new_m_ln_g': 'new_m', 'new_m_ln_b': 'new_m', 'new_v_w_in': 'new_v', 'new_v_b_forget': 'new_v', 'new_v_b_gate': 'new_v', 'new_v_w_up_a': 'new_v', 'new_v_w_up_b': 'new_v', 'new_v_w_out': 'new_v', 'new_v_ln_g': 'new_v', 'new_v_ln_b': 'new_v'}


def _forward(args):
    return _fwd_reference(*[args[k] for k in FWD_PARAMS])


def _output_shape():
    def fwd():
        inp = _fwd_setup_inputs(0)
        return _fwd_reference(*[inp[k] for k in FWD_PARAMS])
    out = _jax.eval_shape(fwd)
    return out.shape, out.dtype

N_MICROBATCH = 1
ADAM_LR = 0.001
ADAM_B1 = 0.9
ADAM_B2 = 0.999
ADAM_EPS = 1e-08
ADAM_WD = 0.01
ADAM_STEP = 10
PER_EXAMPLE_BATCH_AXIS = {'x': 0, 'loss_target': 0}
SHARED_INPUTS = []
_WEIGHT_DTYPES = {'w_in': _jnp.float32, 'b_forget': _jnp.float32, 'b_gate': _jnp.float32, 'w_up_a': _jnp.float32, 'w_up_b': _jnp.float32, 'w_out': _jnp.float32, 'ln_g': _jnp.float32, 'ln_b': _jnp.float32}
MOMENT_SCALE = {'w_in': 8.618653e-04, 'b_forget': 5.166372e-03, 'b_gate': 2.360576e-04, 'w_up_a': 9.964723e-04, 'w_up_b': 1.778055e-03, 'w_out': 2.036235e-03, 'ln_g': 1.604849e+01, 'ln_b': 6.782336e-01}


def _to_microbatches(a, axis):
    t = _jnp.moveaxis(a, axis, 0)
    t = t.reshape((N_MICROBATCH, t.shape[0] // N_MICROBATCH) + t.shape[1:])
    return _jnp.moveaxis(t, 1, axis + 1)


def setup_inputs(seed: int = 0) -> dict:
    inp = _fwd_setup_inputs(seed)
    key = _jax.random.fold_in(_jax.random.key(seed), 7919)
    shape, _ = _output_shape()
    out = dict(inp)
    out["loss_target"] = _jax.random.normal(_jax.random.fold_in(key, 0), shape, _jnp.float32)
    for i, name in enumerate(TWIN_WEIGHTS):
        w = inp[name].astype(_jnp.float32)
        if MOMENT_SCALE is None:
            s = _jnp.sqrt(_jnp.mean(_jnp.square(w)) + 1e-30)
        else:
            s = MOMENT_SCALE[name]
        km, kv = _jax.random.split(_jax.random.fold_in(key, i + 1))
        out[name] = w
        out["m_" + name] = s * _jax.random.normal(km, w.shape, _jnp.float32)
        out["v_" + name] = (s * s) * _jax.random.uniform(kv, w.shape, _jnp.float32, 0.5, 1.5)
    if N_MICROBATCH > 1:
        for name, axis in PER_EXAMPLE_BATCH_AXIS.items():
            out[name] = _to_microbatches(out[name], axis)
    return {'x': out['x'], 'w_in': out['w_in'], 'b_forget': out['b_forget'], 'b_gate': out['b_gate'], 'w_up_a': out['w_up_a'], 'w_up_b': out['w_up_b'], 'w_out': out['w_out'], 'ln_g': out['ln_g'], 'ln_b': out['ln_b'], 'loss_target': out['loss_target'], 'm_w_in': out['m_w_in'], 'm_b_forget': out['m_b_forget'], 'm_b_gate': out['m_b_gate'], 'm_w_up_a': out['m_w_up_a'], 'm_w_up_b': out['m_w_up_b'], 'm_w_out': out['m_w_out'], 'm_ln_g': out['m_ln_g'], 'm_ln_b': out['m_ln_b'], 'v_w_in': out['v_w_in'], 'v_b_forget': out['v_b_forget'], 'v_b_gate': out['v_b_gate'], 'v_w_up_a': out['v_w_up_a'], 'v_w_up_b': out['v_w_up_b'], 'v_w_out': out['v_w_out'], 'v_ln_g': out['v_ln_g'], 'v_ln_b': out['v_ln_b']}


def _loss(weights, diff, rest, loss_target):
    with _jax.named_scope("forward"):
        args = {**rest, TWIN_DIFF_INPUT: diff, **{k: w.astype(_WEIGHT_DTYPES[k]) for k, w in weights.items()}}
        y = _forward(args)
    with _jax.named_scope("loss_head"):
        err = _jnp.square(y.astype(_jnp.float32) - loss_target)
        return 0.5 * _jnp.sum(_jnp.mean(err, axis=-1)) if err.ndim else 0.5 * err


def _adamw(w, g, m, v):
    m = ADAM_B1 * m + (1.0 - ADAM_B1) * g
    v = ADAM_B2 * v + (1.0 - ADAM_B2) * _jnp.square(g)
    m_hat = m / (1.0 - ADAM_B1 ** ADAM_STEP)
    v_hat = v / (1.0 - ADAM_B2 ** ADAM_STEP)
    delta = -ADAM_LR * (m_hat / (_jnp.sqrt(v_hat) + ADAM_EPS) + ADAM_WD * w)
    return delta, m, v


def reference(x, w_in, b_forget, b_gate, w_up_a, w_up_b, w_out, ln_g, ln_b, loss_target, m_w_in, m_b_forget, m_b_gate, m_w_up_a, m_w_up_b, m_w_out, m_ln_g, m_ln_b, v_w_in, v_b_forget, v_b_gate, v_w_up_a, v_w_up_b, v_w_out, v_ln_g, v_ln_b):
    given = dict(x=x, w_in=w_in, b_forget=b_forget, b_gate=b_gate, w_up_a=w_up_a, w_up_b=w_up_b, w_out=w_out, ln_g=ln_g, ln_b=ln_b, loss_target=loss_target, m_w_in=m_w_in, m_b_forget=m_b_forget, m_b_gate=m_b_gate, m_w_up_a=m_w_up_a, m_w_up_b=m_w_up_b, m_w_out=m_w_out, m_ln_g=m_ln_g, m_ln_b=m_ln_b, v_w_in=v_w_in, v_b_forget=v_b_forget, v_b_gate=v_b_gate, v_w_up_a=v_w_up_a, v_w_up_b=v_w_up_b, v_w_out=v_w_out, v_ln_g=v_ln_g, v_ln_b=v_ln_b)
    weights = {n: given[n] for n in TWIN_WEIGHTS}
    shared = {n: given[n] for n in SHARED_INPUTS}
    per_example = {n: given[n] for n in ['x']}
    grad_fn = _jax.value_and_grad(_loss, argnums=(0, 1))

    def one_microbatch(ex, loss_target):
        ex = dict(ex)
        diff = ex.pop(TWIN_DIFF_INPUT)
        return grad_fn(weights, diff, {**shared, **ex}, loss_target)

    if N_MICROBATCH == 1:
        loss, (grad_w, grad_x) = one_microbatch(per_example, given["loss_target"])
    else:
        def body(carry, xs):
            loss_sum, grad_sum = carry
            l_k, (gw_k, gx_k) = one_microbatch(xs[0], xs[1])
            with _jax.named_scope("update"):
                return (loss_sum + l_k, _jax.tree.map(_jnp.add, grad_sum, gw_k)), gx_k

        init = (_jnp.zeros((), _jnp.float32), _jax.tree.map(_jnp.zeros_like, weights))
        (loss, grad_w), grad_x = _jax.lax.scan(body, init, (per_example, given["loss_target"]))
    with _jax.named_scope("update"):
        delta_w, new_m, new_v = {}, {}, {}
        for n in TWIN_WEIGHTS:
            delta_w[n], new_m[n], new_v[n] = _adamw(weights[n], grad_w[n], given["m_" + n], given["v_" + n])
    return (loss, grad_x, *[grad_w[n] for n in TWIN_WEIGHTS], *[delta_w[n] for n in TWIN_WEIGHTS],
            *[new_m[n] for n in TWIN_WEIGHTS], *[new_v[n] for n in TWIN_WEIGHTS])
```

```python
import functools

import jax
import jax.numpy as jnp
from jax import lax
from jax.experimental import pallas as pl
from jax.experimental.pallas import tpu as pltpu

F32 = jnp.float32
BF16 = jnp.bfloat16

NDEV = 8
HEAD_DIM = 128
LANES = 128
Q_BLOCK = 128
DILATIONS = (1, 4, 16)
ROPE_THETA = 10000.0
LN_EPS = 1e-5
ATT_SCALE = HEAD_DIM ** -0.5
NEG = -1e30
FORGET_PAD = 256
BAND_CHUNK = 2048
HEAD_ROWS = 16
ADAM_LR, ADAM_B1, ADAM_B2, ADAM_EPS, ADAM_WD, ADAM_STEP = 0.001, 0.9, 0.999, 1e-08, 0.01, 10
VMEM_LIMIT = 56 * 1024 * 1024


def _tile(n, pref, mult=LANES):
    if n <= pref:
        return n
    t = (pref // mult) * mult
    while t >= mult:
        if n % t == 0:
            return t
        t -= mult
    return n


def _params(sem=None):
    return pltpu.CompilerParams(dimension_semantics=sem, vmem_limit_bytes=VMEM_LIMIT)


def _sigmoid(z):
    return 1.0 / (1.0 + jnp.exp(-z))


_NT = (((1,), (1,)), ((), ()))
_NN = (((1,), (0,)), ((), ()))
_TN = (((0,), (0,)), ((), ()))


def _dot(a, b, dims):
    return lax.dot_general(a, b, dims, preferred_element_type=F32)


def _mm(a, b, *, name, ta=False, tb=False, out_dtype=F32, tm=512, tn=512, tk=512, add=None, add_scale=1.0):
    if ta:
        K, M = a.shape
    else:
        M, K = a.shape
    if tb:
        N, K2 = b.shape
    else:
        K2, N = b.shape
    assert K == K2, (a.shape, b.shape, ta, tb)
    tm, tn, tk = _tile(M, tm), _tile(N, tn), _tile(K, tk)
    nk = K // tk
    dims = (((0 if ta else 1,), (1 if tb else 0,)), ((), ()))

    def body(*refs):
        if add is None:
            a_ref, b_ref, o_ref, acc_ref = refs
            add_ref = None
        else:
            a_ref, b_ref, add_ref, o_ref, acc_ref = refs
        k = pl.program_id(2)
        part = _dot(a_ref[...].astype(BF16), b_ref[...].astype(BF16), dims)

        @pl.when(k == 0)
        def _():
            acc_ref[...] = part

        @pl.when(k > 0)
        def _():
            acc_ref[...] += part

        @pl.when(k == nk - 1)
        def _():
            r = acc_ref[...]
            if add_ref is not None:
                r = r + add_scale * add_ref[...]
            o_ref[...] = r.astype(out_dtype)

    a_spec = pl.BlockSpec((tk, tm), lambda i, j, k: (k, i)) if ta else pl.BlockSpec((tm, tk), lambda i, j, k: (i, k))
    b_spec = pl.BlockSpec((tn, tk), lambda i, j, k: (j, k)) if tb else pl.BlockSpec((tk, tn), lambda i, j, k: (k, j))
    o_spec = pl.BlockSpec((tm, tn), lambda i, j, k: (i, j))
    in_specs, args = [a_spec, b_spec], [a, b]
    if add is not None:
        in_specs.append(o_spec)
        args.append(add)
    return pl.pallas_call(
        body, name=name, grid=(M // tm, N // tn, nk),
        in_specs=in_specs, out_specs=o_spec,
        out_shape=jax.ShapeDtypeStruct((M, N), out_dtype),
        scratch_shapes=[pltpu.VMEM((tm, tn), F32)],
        compiler_params=_params(("parallel", "parallel", "arbitrary")),
    )(*args)


def _ew(fn, ins, outs, *, rows, cols, tr, tc, name):
    assert rows % tr == 0 and cols % tc == 0

    def spec(kind, off):
        assert off % tc == 0, (off, tc)
        ob = off // tc
        if kind == "tile":
            return pl.BlockSpec((tr, tc), lambda j, i: (i, j + ob))
        return pl.BlockSpec((1, tc), lambda j, i: (0, j + ob))

    n_in = len(ins)

    def body(*refs):
        fn(pl.program_id(1), refs[:n_in], refs[n_in:])

    return pl.pallas_call(
        body, name=name, grid=(cols // tc, rows // tr),
        in_specs=[spec(k, o) for _, k, o in ins],
        out_specs=[spec(k, o) for _, _, k, o in outs],
        out_shape=[jax.ShapeDtypeStruct(s, d) for s, d, _, _ in outs],
        compiler_params=_params(("parallel", "arbitrary")),
    )(*[a for a, _, _ in ins])


def _rope_tables(S):
    half = HEAD_DIM // 2
    inv_freq = ROPE_THETA ** (-jnp.arange(half, dtype=F32) / half)
    ang = jnp.arange(S, dtype=jnp.int32).astype(F32)[:, None] * inv_freq[None, :]
    cos, sin = jnp.cos(ang), jnp.sin(ang)
    return jnp.concatenate([cos, cos], axis=1), jnp.concatenate([-sin, sin], axis=1)


def _prep_qkv(h, cos, sin, *, nha, nhb, S):
    nblk = 3 * nha + 3 * nhb
    ts = _tile(S, 1024, 8)

    def body(h_ref, cos_ref, sin_ref, o_ref):
        j = pl.program_id(1)

        @pl.when(j < 2 * nha)
        def _():
            t = h_ref[...]
            o_ref[...] = (t * cos_ref[...] + pltpu.roll(t, HEAD_DIM // 2, 1) * sin_ref[...]).astype(BF16)

        @pl.when(j >= 2 * nha)
        def _():
            o_ref[...] = h_ref[...].astype(BF16)

    return pl.pallas_call(
        body, name="prep_qkv", grid=(S // ts, nblk),
        in_specs=[pl.BlockSpec((ts, LANES), lambda i, j: (i, jnp.where(j < 3 * nha, j, j + nha))),
                  pl.BlockSpec((ts, LANES), lambda i, j: (i, 0)),
                  pl.BlockSpec((ts, LANES), lambda i, j: (i, 0))],
        out_specs=pl.BlockSpec((ts, LANES), lambda i, j: (i, j)),
        out_shape=jax.ShapeDtypeStruct((S, nblk * LANES), BF16),
        compiler_params=_params(("parallel", "arbitrary")),
    )(h, cos, sin)


def _band_mask(n):
    k0 = jnp.maximum(n - 1, 0) * Q_BLOCK
    dist = (n * Q_BLOCK + lax.broadcasted_iota(jnp.int32, (Q_BLOCK, 2 * Q_BLOCK), 0)) - (
        k0 + lax.broadcasted_iota(jnp.int32, (Q_BLOCK, 2 * Q_BLOCK), 1))
    return pl.multiple_of(k0, Q_BLOCK), (dist >= 0) & (dist <= Q_BLOCK)


def _band_fwd(qkv, d, *, nha, S):
    L = S // d
    QB = qkv.shape[1] // LANES
    Lc = min(L, BAND_CHUNK)
    nb = Lc // Q_BLOCK
    assert L % Lc == 0 and Lc % Q_BLOCK == 0 and L >= 2 * Q_BLOCK
    view = qkv.reshape(L, d * qkv.shape[1])

    def body(q_ref, k_ref, v_ref, o_ref, lse_ref):
        c0 = pl.program_id(1) * nb

        def step(n, carry):
            k0, valid = _band_mask(c0 + n)
            q0 = pl.multiple_of(n * Q_BLOCK, Q_BLOCK)
            q = q_ref[pl.ds(q0, Q_BLOCK), :]
            k = k_ref[pl.ds(k0, 2 * Q_BLOCK), :]
            v = v_ref[pl.ds(k0, 2 * Q_BLOCK), :]
            s = jnp.where(valid, _dot(q, k, _NT) * ATT_SCALE, NEG)
            m = jnp.max(s, axis=1, keepdims=True)
            p = jnp.exp(s - m)
            den = jnp.sum(p, axis=1, keepdims=True)
            o_ref[pl.ds(q0, Q_BLOCK), :] = _dot(p.astype(BF16), v, _NN) / den
            lse_ref[pl.ds(q0, Q_BLOCK), :] = jnp.broadcast_to(m + jnp.log(den), (Q_BLOCK, LANES))
            return carry

        lax.fori_loop(0, nb, step, 0)

    def col(off):
        return lambda he, c: (0, (he // nha) * QB + he % nha + off)

    chunk = pl.BlockSpec((Lc, LANES), lambda he, c: (c, he))
    o, lse = pl.pallas_call(
        body, name=f"band_fwd_d{d}", grid=(d * nha, L // Lc),
        in_specs=[pl.BlockSpec((Lc, LANES), lambda he, c: (c, (he // nha) * QB + he % nha)),
                  pl.BlockSpec((L, LANES), col(nha)), pl.BlockSpec((L, LANES), col(2 * nha))],
        out_specs=[chunk, chunk],
        out_shape=[jax.ShapeDtypeStruct((L, d * nha * LANES), F32)] * 2,
        compiler_params=_params(("parallel", "arbitrary")),
    )(view, view, view)
    return o.reshape(S, nha * LANES), lse.reshape(S, nha * LANES)


def _band_bwd(qkv, do, lse, delta, d, *, nha, S):
    L = S // d
    QB = qkv.shape[1] // LANES
    Lc = min(L, BAND_CHUNK)
    nb = Lc // Q_BLOCK
    W = nha * LANES
    view = qkv.reshape(L, d * qkv.shape[1])
    do_v, lse_v, delta_v = (t.reshape(L, d * W) for t in (do, lse, delta))

    def body(q_ref, k_ref, v_ref, do_ref, lse_ref, dl_ref, dq_ref, dk_ref, dv_ref):
        c0 = pl.program_id(1) * nb

        @pl.when(pl.program_id(1) == 0)
        def _():
            dk_ref[...] = jnp.zeros_like(dk_ref)
            dv_ref[...] = jnp.zeros_like(dv_ref)

        def step(n, carry):
            k0, valid = _band_mask(c0 + n)
            q0 = pl.multiple_of(n * Q_BLOCK, Q_BLOCK)
            q = q_ref[pl.ds(q0, Q_BLOCK), :]
            k = k_ref[pl.ds(k0, 2 * Q_BLOCK), :]
            v = v_ref[pl.ds(k0, 2 * Q_BLOCK), :]
            g = do_ref[pl.ds(q0, Q_BLOCK), :]
            s = _dot(q, k, _NT) * ATT_SCALE
            p = jnp.where(valid, jnp.exp(s - lse_ref[pl.ds(q0, Q_BLOCK), 0:1]), 0.0)
            dp = _dot(g, v, _NT)
            ds = (p * (dp - dl_ref[pl.ds(q0, Q_BLOCK), 0:1]) * ATT_SCALE).astype(BF16)
            dq_ref[pl.ds(q0, Q_BLOCK), :] = _dot(ds, k, _NN)
            dk_ref[pl.ds(k0, 2 * Q_BLOCK), :] += _dot(ds, q, _TN)
            dv_ref[pl.ds(k0, 2 * Q_BLOCK), :] += _dot(p.astype(BF16), g, _TN)
            return carry

        lax.fori_loop(0, nb, step, 0)

    def col(off):
        return lambda he, c: (0, (he // nha) * QB + he % nha + off)

    chunk = pl.BlockSpec((Lc, LANES), lambda he, c: (c, he))
    whole = pl.BlockSpec((L, LANES), lambda he, c: (0, he))
    outs = pl.pallas_call(
        body, name=f"band_bwd_d{d}", grid=(d * nha, L // Lc),
        in_specs=[pl.BlockSpec((Lc, LANES), lambda he, c: (c, (he // nha) * QB + he % nha)),
                  pl.BlockSpec((L, LANES), col(nha)), pl.BlockSpec((L, LANES), col(2 * nha)),
                  chunk, chunk, chunk],
        out_specs=[chunk, whole, whole],
        out_shape=[jax.ShapeDtypeStruct((L, d * W), F32)] * 3,
        compiler_params=_params(("parallel", "arbitrary")),
    )(view, view, view, do_v, lse_v, delta_v)
    return [t.reshape(S, W) for t in outs]


def _band_bwd_sum(parts, cos, sin, *, nha, S):
    ts = _tile(S, 512, 8)

    def body(*refs):
        ins, (cos_ref, sin_ref), outs = refs[:9], refs[9:11], refs[11:]
        cs, sn = cos_ref[...], sin_ref[...]
        for idx in range(3):
            g = ins[idx][...] + ins[3 + idx][...] + ins[6 + idx][...]
            if idx < 2:
                g = g * cs + pltpu.roll(g * sn, HEAD_DIM // 2, 1)
            outs[idx][...] = g.astype(BF16)

    spec = pl.BlockSpec((ts, LANES), lambda i, h: (i, h))
    tab_spec = pl.BlockSpec((ts, LANES), lambda i, h: (i, 0))
    return pl.pallas_call(
        body, name="band_bwd_sum", grid=(S // ts, nha),
        in_specs=[spec] * 9 + [tab_spec] * 2, out_specs=[spec] * 3,
        out_shape=[jax.ShapeDtypeStruct((S, nha * LANES), BF16)] * 3,
        compiler_params=_params(("parallel", "arbitrary")),
    )(*[t for part in parts for t in part], cos, sin)


def _mix_gate(os_, lses, h, *, z_off, W, S):
    tr, tc = _tile(S, 512, 8), _tile(W, 512)

    def fn(i, ins, outs):
        o1, o2, o3, l1, l2, l3, z_ref = ins
        out_ref, lse_ref, g_ref = outs
        a, b, c = l1[...], l2[...], l3[...]
        m = jnp.maximum(jnp.maximum(a, b), c)
        ea, eb, ec = jnp.exp(a - m), jnp.exp(b - m), jnp.exp(c - m)
        den = ea + eb + ec
        out = (ea * o1[...] + eb * o2[...] + ec * o3[...]) / den
        z = z_ref[...]
        out_ref[...] = out
        lse_ref[...] = m + jnp.log(den)
        g_ref[...] = (out * (z * _sigmoid(z))).astype(BF16)

    ins = [(t, "tile", 0) for t in (*os_, *lses)] + [(h, "tile", z_off)]
    outs = [((S, W), F32, "tile", 0), ((S, W), F32, "tile", 0), ((S, W), BF16, "tile", 0)]
    return _ew(fn, ins, outs, rows=S, cols=W, tr=tr, tc=tc, name="mix_gate")


def _gate(o, h, *, z_off, W, S):
    tr, tc = _tile(S, 512, 8), _tile(W, 512)

    def fn(i, ins, outs):
        z = ins[1][...]
        outs[0][...] = (ins[0][...] * (z * _sigmoid(z))).astype(BF16)

    return _ew(fn, [(o, "tile", 0), (h, "tile", z_off)], [((S, W), BF16, "tile", 0)],
               rows=S, cols=W, tr=tr, tc=tc, name="gate")[0]


def _gate_bwd(dg, o, h, *, z_off, W, S, name, lane0=None):
    tr = _tile(S, 512, 8)

    def fn(i, ins, outs):
        g, out, z = ins[0][...], ins[1][...], ins[2][...]
        sg = _sigmoid(z)
        do = g * (z * sg)
        outs[0][...] = do.astype(BF16)
        outs[1][...] = (g * out * (sg * (1.0 + z * (1.0 - sg)))).astype(BF16)
        delta = jnp.broadcast_to(jnp.sum(do * out, axis=1, keepdims=True), do.shape)
        if lane0 is not None:
            delta = jnp.where(lax.broadcasted_iota(jnp.int32, do.shape, 1) == 0, ins[3][...], delta)
        outs[2][...] = delta

    ins = [(dg, "tile", 0), (o, "tile", 0), (h, "tile", z_off)] + ([] if lane0 is None else [(lane0, "tile", 0)])
    outs = [((S, W), BF16, "tile", 0), ((S, W), BF16, "tile", 0), ((S, W), F32, "tile", 0)]
    return _ew(fn, ins, outs, rows=S, cols=W, tr=tr, tc=LANES, name=name)


def _split3(x):
    hi = x.astype(BF16)
    r = x - hi.astype(F32)
    mid = r.astype(BF16)
    lo = (r - mid.astype(F32)).astype(BF16)
    return hi, mid, lo


def _tri_dot(x, tri):
    hi, mid, lo = _split3(x)
    return _dot(hi, tri, _NN) + (_dot(mid, tri, _NN) + _dot(lo, tri, _NN))


def _log1p(u):
    w = 1.0 + u
    return jnp.where(w == 1.0, u, jnp.log(w) * (u / jnp.where(w == 1.0, 1.0, w - 1.0)))


def _lane_blocks(t):
    R, S = t.shape
    return t.reshape(R, S // LANES, LANES).transpose(1, 0, 2)


def _lane_unblocks(t):
    nb, R, _ = t.shape
    return t.transpose(1, 0, 2).reshape(R, nb * LANES)


def _forget_cumsum(f3, bf):
    nb, R, _ = f3.shape

    def body(f_ref, b_ref, c_ref):
        row = lax.broadcasted_iota(jnp.int32, (LANES, LANES), 0)
        colm = lax.broadcasted_iota(jnp.int32, (LANES, LANES), 1)
        tri = (row <= colm).astype(BF16)
        bias = b_ref[...]

        def step(n, carry):
            z = f_ref[n] + bias
            logf = jnp.minimum(z, 0.0) - _log1p(jnp.exp(-jnp.abs(z)))
            c = _tri_dot(logf, tri) + carry
            c_ref[n] = c
            return jnp.broadcast_to(c[:, LANES - 1:LANES], (R, LANES))

        lax.fori_loop(0, nb, step, jnp.zeros((R, LANES), F32))

    return pl.pallas_call(
        body, name="forget_cumsum", out_shape=jax.ShapeDtypeStruct(f3.shape, F32),
        compiler_params=_params(),
    )(f3, bf)


def _forget_bwd(dcq3, dck3, f3, bf):
    nb, R, _ = f3.shape

    def body(dcq_ref, dck_ref, f_ref, b_ref, df_ref, db_ref):
        row = lax.broadcasted_iota(jnp.int32, (LANES, LANES), 0)
        colm = lax.broadcasted_iota(jnp.int32, (LANES, LANES), 1)
        tri = (row >= colm).astype(BF16)
        bias = b_ref[...]

        def step(t, carry):
            tail, tot = carry
            n = nb - 1 - t
            r = _tri_dot(dcq_ref[n] + dck_ref[n], tri) + tail
            df = r * _sigmoid(-(f_ref[n] + bias))
            df_ref[n] = df
            tot = tot + jnp.broadcast_to(jnp.sum(df, axis=1, keepdims=True), (R, LANES))
            return jnp.broadcast_to(r[:, 0:1], (R, LANES)), tot

        _, tot = lax.fori_loop(0, nb, step, (jnp.zeros((R, LANES), F32), jnp.zeros((R, LANES), F32)))
        db_ref[...] = tot

    return pl.pallas_call(
        body, name="forget_bwd",
        out_shape=[jax.ShapeDtypeStruct(f3.shape, F32), jax.ShapeDtypeStruct((R, LANES), F32)],
        compiler_params=_params(),
    )(dcq3, dck3, f3, bf)


def _causal(t):
    return lax.broadcasted_iota(jnp.int32, (t, t), 0) >= lax.broadcasted_iota(jnp.int32, (t, t), 1)


def _fox_fwd(qkv, cq, ck, *, nha, nhb, S, tq):
    nq = S // tq
    base = 3 * nha

    def body(q_ref, k_ref, v_ref, cq_ref, ck_ref, o_ref, lq_ref):
        i = pl.program_id(1)
        q = q_ref[...]
        cqv = cq_ref[:, 0:1]

        def step(j, carry, masked):
            m, l, acc = carry
            off = pl.multiple_of(j * tq, tq)
            k = k_ref[pl.ds(off, tq), :]
            v = v_ref[pl.ds(off, tq), :]
            s = _dot(q, k, _NT) * ATT_SCALE + (cqv - ck_ref[j])
            if masked:
                s = jnp.where(_causal(tq), s, NEG)
            m_new = jnp.maximum(m, jnp.max(s, axis=1, keepdims=True))
            alpha = jnp.exp(m - m_new)
            p = jnp.exp(s - m_new)
            l = alpha * l + jnp.sum(p, axis=1, keepdims=True)
            acc = alpha * acc + _dot(p.astype(BF16), v, _NN)
            return m_new, l, acc

        init = (jnp.full((tq, 1), NEG, F32), jnp.zeros((tq, 1), F32), jnp.zeros((tq, LANES), F32))
        carry = lax.fori_loop(0, i, functools.partial(step, masked=False), init)
        m, l, acc = step(i, carry, True)
        o_ref[...] = acc / l
        lq_ref[...] = jnp.broadcast_to(m + jnp.log(l) - cqv, (tq, LANES))

    return pl.pallas_call(
        body, name="fox_fwd", grid=(nhb, nq),
        in_specs=[pl.BlockSpec((tq, LANES), lambda h, i: (i, base + h)),
                  pl.BlockSpec((S, LANES), lambda h, i: (0, base + nhb + h)),
                  pl.BlockSpec((S, LANES), lambda h, i: (0, base + 2 * nhb + h)),
                  pl.BlockSpec((None, tq, LANES), lambda h, i: (h, i, 0)),
                  pl.BlockSpec((None, nq, 1, tq), lambda h, i: (h, 0, 0, 0))],
        out_specs=[pl.BlockSpec((tq, LANES), lambda h, i: (i, h))] * 2,
        out_shape=[jax.ShapeDtypeStruct((S, nhb * LANES), F32)] * 2,
        compiler_params=_params(("parallel", "arbitrary")),
    )(qkv, qkv, qkv, cq, ck)


def _fox_bwd(qkv, do, stats, ck, *, nha, nhb, S, tq):
    nq = S // tq
    base = 3 * nha

    def body(q_ref, k_ref, v_ref, do_ref, st_ref, ck_ref, dq_ref, dk_ref, dv_ref, dr_ref, dc_ref):
        j = pl.program_id(1)

        @pl.when(j == 0)
        def _():
            dq_ref[...] = jnp.zeros_like(dq_ref)
            dr_ref[...] = jnp.zeros_like(dr_ref)

        k = k_ref[...]
        v = v_ref[...]
        ckv = ck_ref[...]

        def step(i, carry, masked):
            dk, dv, dc = carry
            off = pl.multiple_of(i * tq, tq)
            q = q_ref[pl.ds(off, tq), :]
            g = do_ref[pl.ds(off, tq), :]
            p = jnp.exp(_dot(q, k, _NT) * ATT_SCALE - ckv - st_ref[pl.ds(off, tq), 0:1])
            if masked:
                p = jnp.where(_causal(tq), p, 0.0)
            ds = p * (_dot(g, v, _NT) - st_ref[pl.ds(off, tq), 1:2])
            dc = dc - jnp.sum(ds, axis=0, keepdims=True)
            dr_ref[pl.ds(off, tq), :] += jnp.broadcast_to(jnp.sum(ds, axis=1, keepdims=True), (tq, LANES))
            dsb = (ds * ATT_SCALE).astype(BF16)
            dq_ref[pl.ds(off, tq), :] += _dot(dsb, k, _NN)
            dk = dk + _dot(dsb, q, _TN)
            dv = dv + _dot(p.astype(BF16), g, _TN)
            return dk, dv, dc

        init = (jnp.zeros((tq, LANES), F32), jnp.zeros((tq, LANES), F32), jnp.zeros((1, tq), F32))
        carry = step(j, init, True)
        dk, dv, dc = lax.fori_loop(j + 1, nq, functools.partial(step, masked=False), carry)
        dk_ref[...] = dk.astype(BF16)
        dv_ref[...] = dv.astype(BF16)
        dc_ref[...] = dc

    def whole(off):
        return pl.BlockSpec((S, LANES), lambda h, j: (0, off + h))

    kv_blk = pl.BlockSpec((tq, LANES), lambda h, j: (j, h))
    c_blk = pl.BlockSpec((None, None, 1, tq), lambda h, j: (h, j, 0, 0))
    return pl.pallas_call(
        body, name="fox_bwd", grid=(nhb, nq),
        in_specs=[whole(base),
                  pl.BlockSpec((tq, LANES), lambda h, j: (j, base + nhb + h)),
                  pl.BlockSpec((tq, LANES), lambda h, j: (j, base + 2 * nhb + h)),
                  whole(0), whole(0), c_blk],
        out_specs=[whole(0), kv_blk, kv_blk, whole(0), c_blk],
        out_shape=[jax.ShapeDtypeStruct((S, nhb * LANES), F32),
                   jax.ShapeDtypeStruct((S, nhb * LANES), BF16),
                   jax.ShapeDtypeStruct((S, nhb * LANES), BF16),
                   jax.ShapeDtypeStruct((S, nhb * LANES), F32),
                   jax.ShapeDtypeStruct((nhb, nq, 1, tq), F32)],
        compiler_params=_params(("parallel", "arbitrary")),
    )(qkv, qkv, qkv, do, stats, ck)


def _merge(up_a, up_b, h, b_gate, *, g_off, D, S):
    tr, tc = _tile(S, 512, 8), _tile(D, 512)

    def fn(i, ins, outs):
        ua, ub, la, lb, ba, bb = (r[...] for r in ins)
        outs[0][...] = (_sigmoid(la + ba) * ua + _sigmoid(lb + bb) * ub).astype(BF16)

    ins = [(up_a, "tile", 0), (up_b, "tile", 0), (h, "tile", g_off), (h, "tile", g_off + D),
           (b_gate, "row", 0), (b_gate, "row", D)]
    return _ew(fn, ins, [((S, D), BF16, "tile", 0)], rows=S, cols=D, tr=tr, tc=tc, name="merge")[0]


def _acc_rows(i, ref, val):
    @pl.when(i == 0)
    def _():
        ref[...] = val

    @pl.when(i > 0)
    def _():
        ref[...] += val


def _merge_bwd(dmix, up_a, up_b, h, b_gate, *, g_off, D, S):
    tr, tc = _tile(S, 512, 8), _tile(D, 512)

    def fn(i, ins, outs):
        dm, ua, ub, la, lb, ba, bb = (r[...] for r in ins)
        ga, gb = _sigmoid(la + ba), _sigmoid(lb + bb)
        outs[0][...] = (ga * dm).astype(BF16)
        outs[1][...] = (gb * dm).astype(BF16)
        dla = ua * dm * (ga * (1.0 - ga))
        dlb = ub * dm * (gb * (1.0 - gb))
        outs[2][...] = dla.astype(BF16)
        outs[3][...] = dlb.astype(BF16)
        _acc_rows(i, outs[4], jnp.sum(dla, axis=0, keepdims=True))
        _acc_rows(i, outs[5], jnp.sum(dlb, axis=0, keepdims=True))

    ins = [(dmix, "tile", 0), (up_a, "tile", 0), (up_b, "tile", 0), (h, "tile", g_off), (h, "tile", g_off + D),
           (b_gate, "row", 0), (b_gate, "row", D)]
    outs = [((S, D), BF16, "tile", 0)] * 4 + [((1, D), F32, "acc", 0)] * 2
    return _ew(fn, ins, outs, rows=S, cols=D, tr=tr, tc=tc, name="merge_bwd")


def _deepnorm(x, y, g, b, *, alpha, D, S):
    tr = _tile(S, 256, 8)

    def fn(i, ins, outs):
        z = alpha * ins[0][...] + ins[1][...]
        mu = jnp.mean(z, axis=1, keepdims=True)
        zc = z - mu
        var = jnp.mean(zc * zc, axis=1, keepdims=True)
        xn = zc * lax.rsqrt(var + LN_EPS) * ins[2][...] + ins[3][...]
        outs[0][...] = xn
        outs[1][...] = xn.astype(BF16)
        outs[2][...] = z

    ins = [(x, "tile", 0), (y, "tile", 0), (g, "row", 0), (b, "row", 0)]
    outs = [((S, D), F32, "tile", 0), ((S, D), BF16, "tile", 0), ((S, D), F32, "tile", 0)]
    return _ew(fn, ins, outs, rows=S, cols=D, tr=tr, tc=D, name="deepnorm")


def _deepnorm_bwd(dxn, z, g, *, D, S):
    tr = _tile(S, 256, 8)

    def fn(i, ins, outs):
        dx, zv = ins[0][...], ins[1][...]
        mu = jnp.mean(zv, axis=1, keepdims=True)
        zc = zv - mu
        rstd = lax.rsqrt(jnp.mean(zc * zc, axis=1, keepdims=True) + LN_EPS)
        u = zc * rstd
        du = dx * ins[2][...]
        dz = rstd * (du - jnp.mean(du, axis=1, keepdims=True) - u * jnp.mean(du * u, axis=1, keepdims=True))
        outs[0][...] = dz
        outs[1][...] = dz.astype(BF16)
        _acc_rows(i, outs[2], jnp.sum(dx * u, axis=0, keepdims=True))
        _acc_rows(i, outs[3], jnp.sum(dx, axis=0, keepdims=True))

    ins = [(dxn, "tile", 0), (z, "tile", 0), (g, "row", 0)]
    outs = [((S, D), F32, "tile", 0), ((S, D), BF16, "tile", 0), ((1, D), F32, "acc", 0), ((1, D), F32, "acc", 0)]
    return _ew(fn, ins, outs, rows=S, cols=D, tr=tr, tc=D, name="deepnorm_bwd")


def _loss_head(y, target, *, D, S):
    tr = _tile(S, 256, 8)

    def body(y_ref, t_ref, d_ref, l_ref):
        err = y_ref[...] - t_ref[...]
        d_ref[...] = err * (1.0 / D)
        part = 0.5 * jnp.sum(jnp.sum(err * err, axis=1, keepdims=True) * (1.0 / D), axis=0, keepdims=True)
        _acc_rows(pl.program_id(0), l_ref, jnp.broadcast_to(part, (1, LANES)))

    return pl.pallas_call(
        body, name="loss_head", grid=(S // tr,),
        in_specs=[pl.BlockSpec((tr, D), lambda i: (i, 0))] * 2,
        out_specs=[pl.BlockSpec((tr, D), lambda i: (i, 0)), pl.BlockSpec((1, LANES), lambda i: (0, 0))],
        out_shape=[jax.ShapeDtypeStruct((S, D), F32), jax.ShapeDtypeStruct((1, LANES), F32)],
        compiler_params=_params(("arbitrary",)),
    )(y, target)


_MESH = pl.DeviceIdType.MESH
_ANY = pl.BlockSpec(memory_space=pl.ANY)


def _all_gather(x, *, name):
    R, C = x.shape

    def body(x_ref, out_ref, send_sems, recv_sems, local_sem):
        x, y, c = lax.axis_index("x"), lax.axis_index("y"), lax.axis_index("c")
        me, sibling = (x, y, c), (x, y, 1 - c)
        chips = [(1 - x, y), (x, 1 - y), (1 - x, 1 - y)]

        def slot(px, py, pc):
            return out_ref.at[4 * px + 2 * py + pc]

        def copy(k, block, to, src=None):
            return pltpu.make_async_remote_copy(
                src_ref=slot(*block) if src is None else src, dst_ref=slot(*block),
                send_sem=send_sems.at[k], recv_sem=recv_sems.at[k], device_id=to, device_id_type=_MESH)

        mine = pltpu.make_async_copy(x_ref, slot(*me), local_sem)
        mine.start()
        first = [copy(0, me, sibling, src=x_ref)]
        first += [copy(1 + j, me, (*chip, c), src=x_ref) for j, chip in enumerate(chips)]
        for cp in first:
            cp.start()
        passed = [copy(4 + j, (*chip, c), sibling) for j, chip in enumerate(chips)]
        for j, chip in enumerate(chips):
            copy(1 + j, (*chip, c), me).wait_recv()
            passed[j].start()
        copy(0, sibling, me).wait_recv()
        for j, chip in enumerate(chips):
            copy(4 + j, (*chip, 1 - c), me).wait_recv()
        for cp in first + passed:
            cp.wait_send()
        mine.wait()

    return pl.pallas_call(
        body, name=name, out_shape=jax.ShapeDtypeStruct((NDEV, R, C), x.dtype),
        in_specs=[_ANY], out_specs=_ANY,
        scratch_shapes=[pltpu.SemaphoreType.DMA((7,)), pltpu.SemaphoreType.DMA((7,)), pltpu.SemaphoreType.DMA],
    )(x)


def _exchange(g, *, name):
    def body(g_ref, out_ref, send_sems, recv_sems, local_sem):
        x, y, c = lax.axis_index("x"), lax.axis_index("y"), lax.axis_index("c")
        me = 4 * x + 2 * y + c
        mine = pltpu.make_async_copy(g_ref.at[me], out_ref.at[me], local_sem)
        mine.start()

        def copy(k):
            px, py, pc = x ^ ((k >> 2) & 1), y ^ ((k >> 1) & 1), c ^ (k & 1)
            return pltpu.make_async_remote_copy(
                src_ref=g_ref.at[4 * px + 2 * py + pc], dst_ref=out_ref.at[me],
                send_sem=send_sems.at[k - 1], recv_sem=recv_sems.at[k - 1],
                device_id=(px, py, pc), device_id_type=_MESH)

        def landing(k):
            px, py, pc = x ^ ((k >> 2) & 1), y ^ ((k >> 1) & 1), c ^ (k & 1)
            return pltpu.make_async_remote_copy(
                src_ref=g_ref.at[me], dst_ref=out_ref.at[4 * px + 2 * py + pc],
                send_sem=send_sems.at[k - 1], recv_sem=recv_sems.at[k - 1],
                device_id=(px, py, pc), device_id_type=_MESH)

        copies = [copy(k) for k in range(1, NDEV)]
        for cp in copies:
            cp.start()
        for k in range(1, NDEV):
            landing(k).wait_recv()
        for cp in copies:
            cp.wait_send()
        mine.wait()

    return pl.pallas_call(
        body, name=name, out_shape=jax.ShapeDtypeStruct(g.shape, g.dtype),
        in_specs=[_ANY], out_specs=_ANY,
        scratch_shapes=[pltpu.SemaphoreType.DMA((7,)), pltpu.SemaphoreType.DMA((7,)), pltpu.SemaphoreType.DMA],
    )(g)


def _sum_slots(parts, *, name):
    n, R, C = parts.shape
    tr, tc = _tile(R, 64, 8), _tile(C, 2048)

    def body(p_ref, o_ref):
        acc = p_ref[0]
        for k in range(1, n):
            acc = acc + p_ref[k]
        o_ref[...] = acc

    return pl.pallas_call(
        body, name=name, grid=(R // tr, C // tc),
        in_specs=[pl.BlockSpec((n, tr, tc), lambda i, j: (0, i, j))],
        out_specs=pl.BlockSpec((tr, tc), lambda i, j: (i, j)),
        out_shape=jax.ShapeDtypeStruct((R, C), F32),
        compiler_params=_params(("parallel", "parallel")),
    )(parts)


def _adamw(w, g, m, v, *, name):
    R, C = w.shape
    budget = 2 << 20
    tr = R if R * C * 4 <= budget else _tile(R, max(8, (budget // (C * 4)) // 8 * 8), 8)

    def body(w_ref, g_ref, m_ref, v_ref, d_ref, nm_ref, nv_ref):
        gv = g_ref[...]
        nm = ADAM_B1 * m_ref[...] + (1.0 - ADAM_B1) * gv
        nv = ADAM_B2 * v_ref[...] + (1.0 - ADAM_B2) * (gv * gv)
        m_hat = nm / (1.0 - ADAM_B1 ** ADAM_STEP)
        v_hat = nv / (1.0 - ADAM_B2 ** ADAM_STEP)
        d_ref[...] = -ADAM_LR * (m_hat / (jnp.sqrt(v_hat) + ADAM_EPS) + ADAM_WD * w_ref[...])
        nm_ref[...] = nm
        nv_ref[...] = nv

    spec = pl.BlockSpec((tr, C), lambda i: (i, 0))
    return pl.pallas_call(
        body, name=name, grid=(R // tr,), in_specs=[spec] * 4, out_specs=[spec] * 3,
        out_shape=[jax.ShapeDtypeStruct((R, C), F32)] * 3,
        compiler_params=_params(("parallel",)),
    )(w, g, m, v)


def _pad_cols(w, *, nq, ng):
    nf = w.shape[-1] - nq - ng
    pad = jnp.zeros(w.shape[:-1] + (FORGET_PAD - nf,), w.dtype)
    return jnp.concatenate([w[..., :nq], w[..., nq + nf:], w[..., nq:nq + nf], pad], axis=-1)


def _unpad_cols(w, *, nq, ng, nf):
    return jnp.concatenate([w[..., :nq], w[..., nq + ng:nq + ng + nf], w[..., nq:nq + ng]], axis=-1)


def kernel(x, w_in, b_forget, b_gate, w_up_a, w_up_b, w_out, ln_g, ln_b, loss_target, m_w_in, m_b_forget, m_b_gate, m_w_up_a, m_w_up_b, m_w_out, m_ln_g, m_ln_b, v_w_in, v_b_forget, v_b_gate, v_w_up_a, v_w_up_b, v_w_out, v_ln_g, v_ln_b):
    depth = w_in.shape[0]
    _, S, D = x.shape
    WA, WB = w_up_a.shape[1], w_up_b.shape[1]
    nha, nhb = WA // HEAD_DIM, WB // HEAD_DIM
    assert nhb == b_forget.shape[1] and WA == WB and nhb <= HEAD_ROWS
    nq, ng, nf = 4 * WA + 4 * WB, 2 * D, nhb
    assert w_in.shape[2] == nq + nf + ng
    ncp = nq + ng + FORGET_PAD
    off_za, off_zb, off_g, off_f = 3 * WA, 4 * WA + 3 * WB, nq, nq + ng
    alpha = float((2 * depth) ** 0.25)
    dsh = D // NDEV
    tq = _tile(S, 512)
    nqb = S // tq

    x0 = x[0]
    target = loss_target[0]
    cos, sin = _rope_tables(S)

    w_in_p = _pad_cols(w_in, nq=nq, ng=ng).astype(BF16)
    w_up_a_t = jnp.swapaxes(w_up_a, 1, 2).astype(BF16)
    w_up_b_t = jnp.swapaxes(w_up_b, 1, 2).astype(BF16)
    w_out_c = w_out.astype(BF16)
    W_in, W_ua, W_ub, W_out = [], [], [], []
    for l in range(depth):
        W_in.append(_all_gather(w_in_p[l], name=f"gather_w_in_{l}").reshape(D, ncp))
        W_ua.append(_all_gather(w_up_a_t[l], name=f"gather_w_up_a_{l}").reshape(D, WA))
        W_ub.append(_all_gather(w_up_b_t[l], name=f"gather_w_up_b_{l}").reshape(D, WB))
        W_out.append(_all_gather(w_out_c[l], name=f"gather_w_out_{l}").reshape(D, D))

    bf_rows = [jnp.broadcast_to(jnp.pad(b_forget[l], (0, HEAD_ROWS - nhb))[:, None], (HEAD_ROWS, LANES))
               for l in range(depth)]

    saved = []
    xf, xb = x0, x0.astype(BF16)
    for l in range(depth):
        h = _mm(xb, W_in[l], name="in_proj", tm=1024, tn=1280, tk=2048)
        qkv = _prep_qkv(h, cos, sin, nha=nha, nhb=nhb, S=S)
        bands = [_band_fwd(qkv, d, nha=nha, S=S) for d in DILATIONS]
        out_a, lse_a, ga = _mix_gate([o for o, _ in bands], [s for _, s in bands], h, z_off=off_za, W=WA, S=S)
        f3 = _lane_blocks(jnp.pad(h[:, off_f:off_f + nhb].T, ((0, HEAD_ROWS - nhb), (0, 0))))
        c = _lane_unblocks(_forget_cumsum(f3, bf_rows[l]))[:nhb]
        cq = jnp.broadcast_to(c[:, :, None], (nhb, S, LANES))
        ck = c.reshape(nhb, nqb, 1, tq)
        out_b, lq_b = _fox_fwd(qkv, cq, ck, nha=nha, nhb=nhb, S=S, tq=tq)
        gb = _gate(out_b, h, z_off=off_zb, W=WB, S=S)
        up_a = _mm(ga, W_ua[l], tb=True, name="up_a", tm=1024, tn=1024, tk=1536)
        up_b = _mm(gb, W_ub[l], tb=True, name="up_b", tm=1024, tn=1024, tk=1536)
        bg = b_gate[l][None, :]
        mix = _merge(up_a, up_b, h, bg, g_off=off_g, D=D, S=S)
        y = _mm(mix, W_out[l], name="out_proj", tm=1024, tn=1024, tk=2048)
        xn_f, xn_b, z = _deepnorm(xf, y, ln_g[l][None, :], ln_b[l][None, :], alpha=alpha, D=D, S=S)
        saved.append(dict(xb=xb, h=h, qkv=qkv, out_a=out_a, lse_a=lse_a, ga=ga, f3=f3, ck=ck, out_b=out_b,
                          lq_b=lq_b, gb=gb, up_a=up_a, up_b=up_b, mix=mix, z=z, bg=bg))
        xf, xb = xn_f, xn_b

    dx, loss_row = _loss_head(xf, target, D=D, S=S)

    recv = [None] * depth
    small = [None] * depth
    for l in reversed(range(depth)):
        sv = saved[l]
        h, qkv = sv["h"], sv["qkv"]
        dz, dzb, d_lng, d_lnb = _deepnorm_bwd(dx, sv["z"], ln_g[l][None, :], D=D, S=S)
        g_w_out = _mm(sv["mix"], dzb, ta=True, name="g_w_out", tm=512, tn=1024, tk=1024)
        dmix = _mm(dzb, W_out[l], tb=True, name="d_mix", tm=1024, tn=1024, tk=2048)
        dup_a, dup_b, dgl_a, dgl_b, d_bga, d_bgb = _merge_bwd(dmix, sv["up_a"], sv["up_b"], h, sv["bg"],
                                                              g_off=off_g, D=D, S=S)
        g_w_ua = _mm(dup_a, sv["ga"], ta=True, name="g_w_up_a", tm=512, tn=1536, tk=1024)
        g_w_ub = _mm(dup_b, sv["gb"], ta=True, name="g_w_up_b", tm=512, tn=1536, tk=1024)
        dga = _mm(dup_a, W_ua[l], name="d_ga", tm=1024, tn=1536, tk=2048)
        dgb = _mm(dup_b, W_ub[l], name="d_gb", tm=1024, tn=1536, tk=2048)
        do_a, dza, delta_a = _gate_bwd(dga, sv["out_a"], h, z_off=off_za, W=WA, S=S, name="gate_bwd_a")
        do_b, dzb_, stats_b = _gate_bwd(dgb, sv["out_b"], h, z_off=off_zb, W=WB, S=S, name="gate_bwd_b",
                                        lane0=sv["lq_b"])
        parts = [_band_bwd(qkv, do_a, sv["lse_a"], delta_a, d, nha=nha, S=S) for d in DILATIONS]
        dqa, dka, dva = _band_bwd_sum(parts, cos, sin, nha=nha, S=S)
        dqb, dkb, dvb, dcq, dck = _fox_bwd(qkv, do_b, stats_b, sv["ck"], nha=nha, nhb=nhb, S=S, tq=tq)
        head_rows = lambda t: _lane_blocks(jnp.pad(t, ((0, HEAD_ROWS - nhb), (0, 0))))
        df3, d_bf = _forget_bwd(head_rows(dcq[:, ::LANES].T), head_rows(dck.reshape(nhb, S)), sv["f3"], bf_rows[l])
        dfl = jnp.pad(_lane_unblocks(df3)[:nhb].T, ((0, 0), (0, FORGET_PAD - nhb))).astype(BF16)
        dh = jnp.concatenate([dqa, dka, dva, dza, dqb.astype(BF16), dkb, dvb, dzb_, dgl_a, dgl_b, dfl], axis=1)
        g_w_in = _mm(sv["xb"], dh, ta=True, name="g_w_in", tm=512, tn=1280, tk=1024)
        dx = _mm(dh, W_in[l], tb=True, name="d_x", tm=1024, tn=1024, tk=1280, add=dz, add_scale=alpha)
        recv[l] = (
            _exchange(g_w_in.reshape(NDEV, dsh, ncp), name=f"xchg_w_in_{l}"),
            _exchange(g_w_ua.reshape(NDEV, dsh, WA), name=f"xchg_w_up_a_{l}"),
            _exchange(g_w_ub.reshape(NDEV, dsh, WB), name=f"xchg_w_up_b_{l}"),
            _exchange(g_w_out.reshape(NDEV, dsh, D), name=f"xchg_w_out_{l}"),
        )
        small[l] = jnp.concatenate([d_bf[:nhb, 0], jnp.zeros((LANES - nhb,), F32), d_bga[0], d_bgb[0],
                                    d_lng[0], d_lnb[0]])

    n_small = LANES + 2 * D + 2 * D
    flat = jnp.concatenate(small + [loss_row[0]])
    rows = -(-flat.shape[0] // (8 * LANES)) * 8
    flat = jnp.pad(flat, (0, rows * LANES - flat.shape[0])).reshape(rows, LANES)
    tot = _sum_slots(_all_gather(flat, name="gather_small"), name="sum_small").reshape(-1)
    loss = tot[depth * n_small]
    sm = tot[:depth * n_small].reshape(depth, n_small)
    g_bf, g_bg = sm[:, :nhb], sm[:, LANES:LANES + 2 * D]
    g_lg, g_lb = sm[:, LANES + 2 * D:LANES + 3 * D], sm[:, LANES + 3 * D:]

    g_in = jnp.stack([_unpad_cols(_sum_slots(recv[l][0], name="sum_w_in"), nq=nq, ng=ng, nf=nf) for l in range(depth)])
    g_ua = jnp.stack([_sum_slots(recv[l][1], name="sum_w_up_a").T for l in range(depth)])
    g_ub = jnp.stack([_sum_slots(recv[l][2], name="sum_w_up_b").T for l in range(depth)])
    g_out = jnp.stack([_sum_slots(recv[l][3], name="sum_w_out") for l in range(depth)])

    def adam(w, g, m, v, name):
        shp = w.shape
        d_, m_, v_ = _adamw(*[t.reshape(-1, shp[-1]) for t in (w, g, m, v)], name=name)
        return d_.reshape(shp), m_.reshape(shp), v_.reshape(shp)

    grads = [g_in, g_bf, g_bg, g_ua, g_ub, g_out, g_lg, g_lb]
    ws = [w_in, b_forget, b_gate, w_up_a, w_up_b, w_out, ln_g, ln_b]
    ms = [m_w_in, m_b_forget, m_b_gate, m_w_up_a, m_w_up_b, m_w_out, m_ln_g, m_ln_b]
    vs = [v_w_in, v_b_forget, v_b_gate, v_w_up_a, v_w_up_b, v_w_out, v_ln_g, v_ln_b]
    names = ["w_in", "b_forget", "b_gate", "w_up_a", "w_up_b", "w_out", "ln_g", "ln_b"]
    upd = [adam(w, g, m, v, f"adamw_{n}") for w, g, m, v, n in zip(ws, grads, ms, vs, names)]
    return (loss, dx[None], *grads, *[u[0] for u in upd], *[u[1] for u in upd], *[u[2] for u in upd])
```

```python
import functools

import jax
import jax.numpy as jnp
from jax import lax
from jax.experimental import pallas as pl
from jax.experimental.pallas import tpu as pltpu

F32 = jnp.float32
BF16 = jnp.bfloat16

NDEV = 8
HEAD_DIM = 128
LANES = 128
Q_BLOCK = 128
DILATIONS = (1, 4, 16)
ROPE_THETA = 10000.0
LN_EPS = 1e-5
ATT_SCALE = HEAD_DIM ** -0.5
NEG = -1e30
FORGET_PAD = 256
BAND_CHUNK = 2048
BAND_UNROLL = 8
FOX_TILE_FWD = 1024
FOX_TILE_BWD = 1024
HEAD_ROWS = 16
ADAM_LR, ADAM_B1, ADAM_B2, ADAM_EPS, ADAM_WD, ADAM_STEP = 0.001, 0.9, 0.999, 1e-08, 0.01, 10
VMEM_LIMIT = 56 * 1024 * 1024


def _tile(n, pref, mult=LANES):
    if n <= pref:
        return n
    t = (pref // mult) * mult
    while t >= mult:
        if n % t == 0:
            return t
        t -= mult
    return n


def _params(sem=None):
    return pltpu.CompilerParams(dimension_semantics=sem, vmem_limit_bytes=VMEM_LIMIT)


def _sigmoid(z):
    return 1.0 / (1.0 + jnp.exp(-z))


_NT = (((1,), (1,)), ((), ()))
_NN = (((1,), (0,)), ((), ()))
_TN = (((0,), (0,)), ((), ()))


def _dot(a, b, dims):
    return lax.dot_general(a, b, dims, preferred_element_type=F32)


def _mm(a, b, *, name, ta=False, tb=False, out_dtype=F32, tm=512, tn=512, tk=512, add=None, add_scale=1.0):
    if ta:
        K, M = a.shape
    else:
        M, K = a.shape
    if tb:
        N, K2 = b.shape
    else:
        K2, N = b.shape
    assert K == K2, (a.shape, b.shape, ta, tb)
    tm, tn, tk = _tile(M, tm), _tile(N, tn), _tile(K, tk)
    nk = K // tk
    dims = (((0 if ta else 1,), (1 if tb else 0,)), ((), ()))

    def body(*refs):
        if add is None:
            a_ref, b_ref, o_ref, acc_ref = refs
            add_ref = None
        else:
            a_ref, b_ref, add_ref, o_ref, acc_ref = refs
        k = pl.program_id(2)
        part = _dot(a_ref[...].astype(BF16), b_ref[...].astype(BF16), dims)

        @pl.when(k == 0)
        def _():
            acc_ref[...] = part

        @pl.when(k > 0)
        def _():
            acc_ref[...] += part

        @pl.when(k == nk - 1)
        def _():
            r = acc_ref[...]
            if add_ref is not None:
                r = r + add_scale * add_ref[...]
            o_ref[...] = r.astype(out_dtype)

    a_spec = pl.BlockSpec((tk, tm), lambda i, j, k: (k, i)) if ta else pl.BlockSpec((tm, tk), lambda i, j, k: (i, k))
    b_spec = pl.BlockSpec((tn, tk), lambda i, j, k: (j, k)) if tb else pl.BlockSpec((tk, tn), lambda i, j, k: (k, j))
    o_spec = pl.BlockSpec((tm, tn), lambda i, j, k: (i, j))
    in_specs, args = [a_spec, b_spec], [a, b]
    if add is not None:
        in_specs.append(o_spec)
        args.append(add)
    return pl.pallas_call(
        body, name=name, grid=(M // tm, N // tn, nk),
        in_specs=in_specs, out_specs=o_spec,
        out_shape=jax.ShapeDtypeStruct((M, N), out_dtype),
        scratch_shapes=[pltpu.VMEM((tm, tn), F32)],
        compiler_params=_params(("parallel", "parallel", "arbitrary")),
    )(*args)


def _ew(fn, ins, outs, *, rows, cols, tr, tc, name):
    assert rows % tr == 0 and cols % tc == 0

    def spec(kind, off):
        assert off % tc == 0, (off, tc)
        ob = off // tc
        if kind == "tile":
            return pl.BlockSpec((tr, tc), lambda j, i: (i, j + ob))
        if kind == "tab":
            return pl.BlockSpec((tr, LANES), lambda j, i: (i, 0))
        return pl.BlockSpec((1, tc), lambda j, i: (0, j + ob))

    n_in = len(ins)

    def body(*refs):
        fn(pl.program_id(1), refs[:n_in], refs[n_in:])

    return pl.pallas_call(
        body, name=name, grid=(cols // tc, rows // tr),
        in_specs=[spec(k, o) for _, k, o in ins],
        out_specs=[spec(k, o) for _, _, k, o in outs],
        out_shape=[jax.ShapeDtypeStruct(s, d) for s, d, _, _ in outs],
        compiler_params=_params(("parallel", "arbitrary")),
    )(*[a for a, _, _ in ins])


def _heads(t):
    return [t[:, e:e + LANES] for e in range(0, t.shape[1], LANES)]


def _rope_tables(S):
    half = HEAD_DIM // 2
    inv_freq = ROPE_THETA ** (-jnp.arange(half, dtype=F32) / half)
    ang = jnp.arange(S, dtype=jnp.int32).astype(F32)[:, None] * inv_freq[None, :]
    cos, sin = jnp.cos(ang), jnp.sin(ang)
    return jnp.concatenate([cos, cos], axis=1), jnp.concatenate([-sin, sin], axis=1)


def _rope(t, cs, sn):
    return jnp.concatenate([g * cs + pltpu.roll(g, HEAD_DIM // 2, 1) * sn for g in _heads(t)], axis=1)


def _rope_t(t, cs, sn):
    return jnp.concatenate([g * cs + pltpu.roll(g * sn, HEAD_DIM // 2, 1) for g in _heads(t)], axis=1)


def _prep_a(h, cos, sin, *, W, S):
    tr, tc = _tile(S, 512, 8), _tile(W, 512)

    def fn(i, ins, outs):
        outs[0][...] = _rope(ins[0][...], ins[1][...], ins[2][...])

    return _ew(fn, [(h, "tile", 0), (cos, "tab", 0), (sin, "tab", 0)], [((S, 2 * W), F32, "tile", 0)],
               rows=S, cols=2 * W, tr=tr, tc=tc, name="prep_a")[0]


def _prep_b(h, *, off, W, S):
    tr, tc = _tile(S, 512, 8), _tile(W, 512)

    def fn(i, ins, outs):
        scale = jnp.where(pl.program_id(0) * tc < W, ATT_SCALE, 1.0).astype(F32)
        outs[0][...] = (ins[0][...] * scale).astype(BF16)

    return _ew(fn, [(h, "tile", off)], [((S, 3 * W), BF16, "tile", 0)], rows=S, cols=3 * W, tr=tr, tc=tc,
               name="prep_b")[0]


def _rope_bwd(dq, dk, dv, cos, sin, *, W, S):
    tr, tc = _tile(S, 512, 8), _tile(W, 512)

    def fn(i, ins, outs):
        cs, sn = ins[3][...], ins[4][...]
        outs[0][...] = _rope_t(ins[0][...], cs, sn).astype(BF16)
        outs[1][...] = _rope_t(ins[1][...], cs, sn).astype(BF16)
        outs[2][...] = ins[2][...].astype(BF16)

    ins = [(dq, "tile", 0), (dk, "tile", 0), (dv, "tile", 0), (cos, "tab", 0), (sin, "tab", 0)]
    return _ew(fn, ins, [((S, W), BF16, "tile", 0)] * 3, rows=S, cols=W, tr=tr, tc=tc, name="rope_bwd")


def _band_mask(n):
    k0 = jnp.maximum(n - 1, 0) * Q_BLOCK
    dist = (n * Q_BLOCK + lax.broadcasted_iota(jnp.int32, (Q_BLOCK, 2 * Q_BLOCK), 0)) - (
        k0 + lax.broadcasted_iota(jnp.int32, (Q_BLOCK, 2 * Q_BLOCK), 1))
    return pl.multiple_of(k0, Q_BLOCK), (dist >= 0) & (dist <= Q_BLOCK)


def _band_rows(it, c, d, nbc):
    if d == 1:
        k0, valid = _band_mask(c * nbc + it)
        return pl.ds(pl.multiple_of(it * Q_BLOCK, Q_BLOCK), Q_BLOCK), pl.ds(k0, 2 * Q_BLOCK), valid
    r, n = it // nbc, it % nbc
    k0, valid = _band_mask(c * nbc + n)
    return (pl.ds(r + d * Q_BLOCK * n, Q_BLOCK, stride=d), pl.ds(r + d * k0, 2 * Q_BLOCK, stride=d), valid)


def _band_chunk(S):
    ch = min(S, BAND_CHUNK)
    assert S % ch == 0 and ch % (Q_BLOCK * max(DILATIONS)) == 0 and S >= 2 * Q_BLOCK * max(DILATIONS)
    return ch


def _band_fwd(qk, h, *, nha, S, v_off, z_off):
    CH = _band_chunk(S)
    W = nha * LANES
    slab = 256

    def body(q_ref, k_ref, v_ref, z_ref, o_ref, lse_ref, g_ref, *scratch):
        os_refs, ls_refs = scratch[:len(DILATIONS)], scratch[len(DILATIONS):]
        c = pl.program_id(1)
        for gi, d in enumerate(DILATIONS):
            nbc = CH // (Q_BLOCK * d)

            def step(it, carry, gi=gi, d=d, nbc=nbc):
                rq, rk, valid = _band_rows(it, c, d, nbc)
                q = q_ref[rq, :].astype(BF16)
                k = k_ref[rk, :].astype(BF16)
                v = v_ref[rk, :].astype(BF16)
                s = jnp.where(valid, _dot(q, k, _NT) * ATT_SCALE, NEG)
                m = jnp.max(s, axis=1, keepdims=True)
                p = jnp.exp(s - m)
                den = jnp.sum(p, axis=1, keepdims=True)
                os_refs[gi][rq, :] = _dot(p.astype(BF16), v, _NN) / den
                ls_refs[gi][rq, :] = jnp.broadcast_to(m + jnp.log(den), (Q_BLOCK, LANES))
                return carry

            lax.fori_loop(0, CH // Q_BLOCK, step, 0, unroll=BAND_UNROLL)

        def mix(t, carry):
            rows = pl.ds(pl.multiple_of(t * slab, slab), slab)
            a, b, cc = (r[rows, :] for r in ls_refs)
            m = jnp.maximum(jnp.maximum(a, b), cc)
            ea, eb, ec = jnp.exp(a - m), jnp.exp(b - m), jnp.exp(cc - m)
            den = ea + eb + ec
            out = (ea * os_refs[0][rows, :] + eb * os_refs[1][rows, :] + ec * os_refs[2][rows, :]) / den
            z = z_ref[rows, :]
            o_ref[rows, :] = out
            lse_ref[rows, :] = m + jnp.log(den)
            g_ref[rows, :] = (out * (z * _sigmoid(z))).astype(BF16)
            return carry

        lax.fori_loop(0, CH // slab, mix, 0)

    chunk = pl.BlockSpec((CH, LANES), lambda hd, c: (c, hd))
    return pl.pallas_call(
        body, name="band_fwd", grid=(nha, S // CH),
        in_specs=[chunk,
                  pl.BlockSpec((S, LANES), lambda hd, c: (0, nha + hd)),
                  pl.BlockSpec((S, LANES), lambda hd, c: (0, v_off // LANES + hd)),
                  pl.BlockSpec((CH, LANES), lambda hd, c: (c, z_off // LANES + hd))],
        out_specs=[chunk, chunk, chunk],
        out_shape=[jax.ShapeDtypeStruct((S, W), F32), jax.ShapeDtypeStruct((S, W), F32),
                   jax.ShapeDtypeStruct((S, W), BF16)],
        scratch_shapes=[pltpu.VMEM((CH, LANES), F32)] * (2 * len(DILATIONS)),
        compiler_params=_params(("parallel", "arbitrary")),
    )(qk, qk, h, h)


def _band_bwd(qk, h, do, stats, *, nha, S, v_off):
    CH = _band_chunk(S)
    W = nha * LANES

    def body(q_ref, k_ref, v_ref, do_ref, st_ref, dq_ref, dk_ref, dv_ref):
        c = pl.program_id(1)

        @pl.when(c == 0)
        def _():
            dk_ref[...] = jnp.zeros_like(dk_ref)
            dv_ref[...] = jnp.zeros_like(dv_ref)

        for gi, d in enumerate(DILATIONS):
            nbc = CH // (Q_BLOCK * d)

            def step(it, carry, gi=gi, d=d, nbc=nbc):
                rq, rk, valid = _band_rows(it, c, d, nbc)
                q = q_ref[rq, :].astype(BF16)
                k = k_ref[rk, :].astype(BF16)
                v = v_ref[rk, :].astype(BF16)
                g = do_ref[rq, :].astype(BF16)
                st = st_ref[rq, :]
                s = _dot(q, k, _NT) * ATT_SCALE
                p = jnp.where(valid, jnp.exp(s - st[:, 0:1]), 0.0)
                ds = (p * (_dot(g, v, _NT) - st[:, 1:2]) * ATT_SCALE).astype(BF16)
                dq = _dot(ds, k, _NN)
                if gi == 0:
                    dq_ref[rq, :] = dq
                else:
                    dq_ref[rq, :] += dq
                dk_ref[rk, :] += _dot(ds, q, _TN)
                dv_ref[rk, :] += _dot(p.astype(BF16), g, _TN)
                return carry

            lax.fori_loop(0, CH // Q_BLOCK, step, 0, unroll=BAND_UNROLL // 2)

    chunk = pl.BlockSpec((CH, LANES), lambda hd, c: (c, hd))
    whole = pl.BlockSpec((S, LANES), lambda hd, c: (0, hd))
    return pl.pallas_call(
        body, name="band_bwd", grid=(nha, S // CH),
        in_specs=[chunk,
                  pl.BlockSpec((S, LANES), lambda hd, c: (0, nha + hd)),
                  pl.BlockSpec((S, LANES), lambda hd, c: (0, v_off // LANES + hd)),
                  chunk, chunk],
        out_specs=[chunk, whole, whole],
        out_shape=[jax.ShapeDtypeStruct((S, W), F32)] * 3,
        compiler_params=_params(("parallel", "arbitrary")),
    )(qk, qk, h, do, stats)


def _gate(o, h, *, z_off, W, S):
    tr, tc = _tile(S, 512, 8), _tile(W, 512)

    def fn(i, ins, outs):
        z = ins[1][...]
        outs[0][...] = (ins[0][...] * (z * _sigmoid(z))).astype(BF16)

    return _ew(fn, [(o, "tile", 0), (h, "tile", z_off)], [((S, W), BF16, "tile", 0)],
               rows=S, cols=W, tr=tr, tc=tc, name="gate")[0]


def _gate_bwd(dg, o, h, lane0, *, z_off, W, S, name, do_dtype):
    tr, tc = _tile(S, 512, 8), _tile(W, 512)

    def fn(i, ins, outs):
        g, out, z = ins[0][...], ins[1][...], ins[2][...]
        sg = _sigmoid(z)
        do = g * (z * sg)
        outs[0][...] = do.astype(do_dtype)
        outs[1][...] = (g * out * (sg * (1.0 + z * (1.0 - sg)))).astype(BF16)
        delta = jnp.concatenate([jnp.broadcast_to(jnp.sum(t, axis=1, keepdims=True), t.shape)
                                 for t in _heads(do * out)], axis=1)
        first = lax.broadcasted_iota(jnp.int32, delta.shape, 1) % LANES == 0
        outs[2][...] = jnp.where(first, ins[3][...], delta)

    ins = [(dg, "tile", 0), (o, "tile", 0), (h, "tile", z_off), (lane0, "tile", 0)]
    outs = [((S, W), do_dtype, "tile", 0), ((S, W), BF16, "tile", 0), ((S, W), F32, "tile", 0)]
    return _ew(fn, ins, outs, rows=S, cols=W, tr=tr, tc=tc, name=name)


def _split3(x):
    hi = x.astype(BF16)
    r = x - hi.astype(F32)
    mid = r.astype(BF16)
    lo = (r - mid.astype(F32)).astype(BF16)
    return hi, mid, lo


def _tri_dot(x, tri):
    hi, mid, lo = _split3(x)
    return _dot(hi, tri, _NN) + (_dot(mid, tri, _NN) + _dot(lo, tri, _NN))


def _log1p(u):
    w = 1.0 + u
    return jnp.where(w == 1.0, u, jnp.log(w) * (u / jnp.where(w == 1.0, 1.0, w - 1.0)))


def _lane_blocks(t):
    R, S = t.shape
    return t.reshape(R, S // LANES, LANES).transpose(1, 0, 2)


def _lane_unblocks(t):
    nb, R, _ = t.shape
    return t.transpose(1, 0, 2).reshape(R, nb * LANES)


def _forget_cumsum(f3, bf):
    nb, R, _ = f3.shape

    def body(f_ref, b_ref, c_ref):
        row = lax.broadcasted_iota(jnp.int32, (LANES, LANES), 0)
        colm = lax.broadcasted_iota(jnp.int32, (LANES, LANES), 1)
        tri = (row <= colm).astype(BF16)
        bias = b_ref[...]

        def step(n, carry):
            z = f_ref[n] + bias
            logf = jnp.minimum(z, 0.0) - _log1p(jnp.exp(-jnp.abs(z)))
            c = _tri_dot(logf, tri) + carry
            c_ref[n] = c
            return jnp.broadcast_to(c[:, LANES - 1:LANES], (R, LANES))

        lax.fori_loop(0, nb, step, jnp.zeros((R, LANES), F32))

    return pl.pallas_call(
        body, name="forget_cumsum", out_shape=jax.ShapeDtypeStruct(f3.shape, F32),
        compiler_params=_params(),
    )(f3, bf)


def _forget_bwd(dcq3, dck3, f3, bf):
    nb, R, _ = f3.shape

    def body(dcq_ref, dck_ref, f_ref, b_ref, df_ref, db_ref):
        row = lax.broadcasted_iota(jnp.int32, (LANES, LANES), 0)
        colm = lax.broadcasted_iota(jnp.int32, (LANES, LANES), 1)
        tri = (row >= colm).astype(BF16)
        bias = b_ref[...]

        def step(t, carry):
            tail, tot = carry
            n = nb - 1 - t
            r = _tri_dot(dcq_ref[n] + dck_ref[n], tri) + tail
            df = r * _sigmoid(-(f_ref[n] + bias))
            df_ref[n] = df
            tot = tot + jnp.broadcast_to(jnp.sum(df, axis=1, keepdims=True), (R, LANES))
            return jnp.broadcast_to(r[:, 0:1], (R, LANES)), tot

        _, tot = lax.fori_loop(0, nb, step, (jnp.zeros((R, LANES), F32), jnp.zeros((R, LANES), F32)))
        db_ref[...] = tot

    return pl.pallas_call(
        body, name="forget_bwd",
        out_shape=[jax.ShapeDtypeStruct(f3.shape, F32), jax.ShapeDtypeStruct((R, LANES), F32)],
        compiler_params=_params(),
    )(dcq3, dck3, f3, bf)


def _causal(t):
    return lax.broadcasted_iota(jnp.int32, (t, t), 0) >= lax.broadcasted_iota(jnp.int32, (t, t), 1)


def _fox_fwd(qkv, ck, *, nhb, S, tq):
    nq = S // tq

    def body(q_ref, k_ref, v_ref, ck_ref, o_ref, lq_ref):
        i = pl.program_id(1)
        q = q_ref[...]

        def step(j, carry, masked):
            m, l, acc = carry
            off = pl.multiple_of(j * tq, tq)
            k = k_ref[pl.ds(off, tq), :]
            v = v_ref[pl.ds(off, tq), :]
            s = _dot(q, k, _NT) - ck_ref[j]
            if masked:
                s = jnp.where(_causal(tq), s, NEG)
            m_new = jnp.maximum(m, jnp.max(s, axis=1, keepdims=True))
            alpha = jnp.exp(m - m_new)
            p = jnp.exp(s - m_new)
            l = alpha * l + jnp.sum(p, axis=1, keepdims=True)
            acc = alpha * acc + _dot(p.astype(BF16), v, _NN)
            return m_new, l, acc

        init = (jnp.full((tq, 1), NEG, F32), jnp.zeros((tq, 1), F32), jnp.zeros((tq, LANES), F32))
        carry = lax.fori_loop(0, i, functools.partial(step, masked=False), init)
        m, l, acc = step(i, carry, True)
        o_ref[...] = acc / l
        lq_ref[...] = jnp.broadcast_to(m + jnp.log(l), (tq, LANES))

    return pl.pallas_call(
        body, name="fox_fwd", grid=(nhb, nq),
        in_specs=[pl.BlockSpec((tq, LANES), lambda h, i: (i, h)),
                  pl.BlockSpec((S, LANES), lambda h, i: (0, nhb + h)),
                  pl.BlockSpec((S, LANES), lambda h, i: (0, 2 * nhb + h)),
                  pl.BlockSpec((None, nq, 1, tq), lambda h, i: (h, 0, 0, 0))],
        out_specs=[pl.BlockSpec((tq, LANES), lambda h, i: (i, h))] * 2,
        out_shape=[jax.ShapeDtypeStruct((S, nhb * LANES), F32)] * 2,
        compiler_params=_params(("parallel", "arbitrary")),
    )(qkv, qkv, qkv, ck)


def _fox_bwd(qkv, do, stats, ck, *, nhb, S, tq):
    nq = S // tq

    def body(q_ref, k_ref, v_ref, do_ref, st_ref, ck_ref, dq_ref, dk_ref, dv_ref, dr_ref, dc_ref):
        j = pl.program_id(1)

        @pl.when(j == 0)
        def _():
            dq_ref[...] = jnp.zeros_like(dq_ref)
            dr_ref[...] = jnp.zeros_like(dr_ref)

        k = k_ref[...]
        v = v_ref[...]
        ckv = ck_ref[...]

        def step(i, carry, masked):
            dk, dv, dc = carry
            off = pl.multiple_of(i * tq, tq)
            q = q_ref[pl.ds(off, tq), :]
            g = do_ref[pl.ds(off, tq), :]
            p = jnp.exp(_dot(q, k, _NT) - ckv - st_ref[pl.ds(off, tq), 0:1])
            if masked:
                p = jnp.where(_causal(tq), p, 0.0)
            ds = p * (_dot(g, v, _NT) - st_ref[pl.ds(off, tq), 1:2])
            dc = dc - jnp.sum(ds, axis=0, keepdims=True)
            dr_ref[pl.ds(off, tq), :] += jnp.broadcast_to(jnp.sum(ds, axis=1, keepdims=True), (tq, LANES))
            dsb = ds.astype(BF16)
            dq_ref[pl.ds(off, tq), :] += _dot(dsb, k, _NN)
            dk = dk + _dot(dsb, q, _TN)
            dv = dv + _dot(p.astype(BF16), g, _TN)
            return dk, dv, dc

        init = (jnp.zeros((tq, LANES), F32), jnp.zeros((tq, LANES), F32), jnp.zeros((1, tq), F32))
        carry = step(j, init, True)
        dk, dv, dc = lax.fori_loop(j + 1, nq, functools.partial(step, masked=False), carry)
        dk_ref[...] = dk.astype(BF16)
        dv_ref[...] = dv.astype(BF16)
        dc_ref[...] = dc

        @pl.when(j == nq - 1)
        def _():
            dq_ref[...] = dq_ref[...] * ATT_SCALE

    def whole(off):
        return pl.BlockSpec((S, LANES), lambda h, j: (0, off + h))

    kv_blk = pl.BlockSpec((tq, LANES), lambda h, j: (j, h))
    c_blk = pl.BlockSpec((None, None, 1, tq), lambda h, j: (h, j, 0, 0))
    return pl.pallas_call(
        body, name="fox_bwd", grid=(nhb, nq),
        in_specs=[whole(0),
                  pl.BlockSpec((tq, LANES), lambda h, j: (j, nhb + h)),
                  pl.BlockSpec((tq, LANES), lambda h, j: (j, 2 * nhb + h)),
                  whole(0), whole(0), c_blk],
        out_specs=[whole(0), kv_blk, kv_blk, whole(0), c_blk],
        out_shape=[jax.ShapeDtypeStruct((S, nhb * LANES), F32),
                   jax.ShapeDtypeStruct((S, nhb * LANES), BF16),
                   jax.ShapeDtypeStruct((S, nhb * LANES), BF16),
                   jax.ShapeDtypeStruct((S, nhb * LANES), F32),
                   jax.ShapeDtypeStruct((nhb, nq, 1, tq), F32)],
        compiler_params=_params(("parallel", "arbitrary")),
    )(qkv, qkv, qkv, do, stats, ck)


def _merge(up_a, up_b, h, b_gate, *, g_off, D, S):
    tr, tc = _tile(S, 512, 8), _tile(D, 512)

    def fn(i, ins, outs):
        ua, ub, la, lb, ba, bb = (r[...] for r in ins)
        outs[0][...] = (_sigmoid(la + ba) * ua + _sigmoid(lb + bb) * ub).astype(BF16)

    ins = [(up_a, "tile", 0), (up_b, "tile", 0), (h, "tile", g_off), (h, "tile", g_off + D),
           (b_gate, "row", 0), (b_gate, "row", D)]
    return _ew(fn, ins, [((S, D), BF16, "tile", 0)], rows=S, cols=D, tr=tr, tc=tc, name="merge")[0]


def _acc_rows(i, ref, val):
    @pl.when(i == 0)
    def _():
        ref[...] = val

    @pl.when(i > 0)
    def _():
        ref[...] += val


def _merge_bwd(dmix, up_a, up_b, h, b_gate, *, g_off, D, S):
    tr, tc = _tile(S, 512, 8), _tile(D, 512)

    def fn(i, ins, outs):
        dm, ua, ub, la, lb, ba, bb = (r[...] for r in ins)
        ga, gb = _sigmoid(la + ba), _sigmoid(lb + bb)
        outs[0][...] = (ga * dm).astype(BF16)
        outs[1][...] = (gb * dm).astype(BF16)
        dla = ua * dm * (ga * (1.0 - ga))
        dlb = ub * dm * (gb * (1.0 - gb))
        outs[2][...] = dla.astype(BF16)
        outs[3][...] = dlb.astype(BF16)
        _acc_rows(i, outs[4], jnp.sum(dla, axis=0, keepdims=True))
        _acc_rows(i, outs[5], jnp.sum(dlb, axis=0, keepdims=True))

    ins = [(dmix, "tile", 0), (up_a, "tile", 0), (up_b, "tile", 0), (h, "tile", g_off), (h, "tile", g_off + D),
           (b_gate, "row", 0), (b_gate, "row", D)]
    outs = [((S, D), BF16, "tile", 0)] * 4 + [((1, D), F32, "acc", 0)] * 2
    return _ew(fn, ins, outs, rows=S, cols=D, tr=tr, tc=tc, name="merge_bwd")


def _deepnorm(x, y, g, b, *, alpha, D, S):
    tr = _tile(S, 256, 8)

    def fn(i, ins, outs):
        z = alpha * ins[0][...] + ins[1][...]
        mu = jnp.mean(z, axis=1, keepdims=True)
        zc = z - mu
        var = jnp.mean(zc * zc, axis=1, keepdims=True)
        xn = zc * lax.rsqrt(var + LN_EPS) * ins[2][...] + ins[3][...]
        outs[0][...] = xn
        outs[1][...] = xn.astype(BF16)
        outs[2][...] = z

    ins = [(x, "tile", 0), (y, "tile", 0), (g, "row", 0), (b, "row", 0)]
    outs = [((S, D), F32, "tile", 0), ((S, D), BF16, "tile", 0), ((S, D), F32, "tile", 0)]
    return _ew(fn, ins, outs, rows=S, cols=D, tr=tr, tc=D, name="deepnorm")


def _deepnorm_bwd(dxn, z, g, *, D, S):
    tr = _tile(S, 256, 8)

    def fn(i, ins, outs):
        dx, zv = ins[0][...], ins[1][...]
        mu = jnp.mean(zv, axis=1, keepdims=True)
        zc = zv - mu
        rstd = lax.rsqrt(jnp.mean(zc * zc, axis=1, keepdims=True) + LN_EPS)
        u = zc * rstd
        du = dx * ins[2][...]
        dz = rstd * (du - jnp.mean(du, axis=1, keepdims=True) - u * jnp.mean(du * u, axis=1, keepdims=True))
        outs[0][...] = dz
        outs[1][...] = dz.astype(BF16)
        _acc_rows(i, outs[2], jnp.sum(dx * u, axis=0, keepdims=True))
        _acc_rows(i, outs[3], jnp.sum(dx, axis=0, keepdims=True))

    ins = [(dxn, "tile", 0), (z, "tile", 0), (g, "row", 0)]
    outs = [((S, D), F32, "tile", 0), ((S, D), BF16, "tile", 0), ((1, D), F32, "acc", 0), ((1, D), F32, "acc", 0)]
    return _ew(fn, ins, outs, rows=S, cols=D, tr=tr, tc=D, name="deepnorm_bwd")


def _loss_head(y, target, *, D, S):
    tr = _tile(S, 256, 8)

    def body(y_ref, t_ref, d_ref, l_ref):
        err = y_ref[...] - t_ref[...]
        d_ref[...] = err * (1.0 / D)
        part = 0.5 * jnp.sum(jnp.sum(err * err, axis=1, keepdims=True) * (1.0 / D), axis=0, keepdims=True)
        _acc_rows(pl.program_id(0), l_ref, jnp.broadcast_to(part, (1, LANES)))

    return pl.pallas_call(
        body, name="loss_head", grid=(S // tr,),
        in_specs=[pl.BlockSpec((tr, D), lambda i: (i, 0))] * 2,
        out_specs=[pl.BlockSpec((tr, D), lambda i: (i, 0)), pl.BlockSpec((1, LANES), lambda i: (0, 0))],
        out_shape=[jax.ShapeDtypeStruct((S, D), F32), jax.ShapeDtypeStruct((1, LANES), F32)],
        compiler_params=_params(("arbitrary",)),
    )(y, target)


_MESH = pl.DeviceIdType.MESH
_ANY = pl.BlockSpec(memory_space=pl.ANY)


def _all_gather(x, *, name):
    R, C = x.shape

    def body(x_ref, out_ref, send_sems, recv_sems, local_sem):
        x, y, c = lax.axis_index("x"), lax.axis_index("y"), lax.axis_index("c")
        me, sibling = (x, y, c), (x, y, 1 - c)
        chips = [(1 - x, y), (x, 1 - y), (1 - x, 1 - y)]

        def slot(px, py, pc):
            return out_ref.at[4 * px + 2 * py + pc]

        def copy(k, block, to, src=None):
            return pltpu.make_async_remote_copy(
                src_ref=slot(*block) if src is None else src, dst_ref=slot(*block),
                send_sem=send_sems.at[k], recv_sem=recv_sems.at[k], device_id=to, device_id_type=_MESH)

        mine = pltpu.make_async_copy(x_ref, slot(*me), local_sem)
        mine.start()
        first = [copy(0, me, sibling, src=x_ref)]
        first += [copy(1 + j, me, (*chip, c), src=x_ref) for j, chip in enumerate(chips)]
        for cp in first:
            cp.start()
        passed = [copy(4 + j, (*chip, c), sibling) for j, chip in enumerate(chips)]
        for j, chip in enumerate(chips):
            copy(1 + j, (*chip, c), me).wait_recv()
            passed[j].start()
        copy(0, sibling, me).wait_recv()
        for j, chip in enumerate(chips):
            copy(4 + j, (*chip, 1 - c), me).wait_recv()
        for cp in first + passed:
            cp.wait_send()
        mine.wait()

    return pl.pallas_call(
        body, name=name, out_shape=jax.ShapeDtypeStruct((NDEV, R, C), x.dtype),
        in_specs=[_ANY], out_specs=_ANY,
        scratch_shapes=[pltpu.SemaphoreType.DMA((7,)), pltpu.SemaphoreType.DMA((7,)), pltpu.SemaphoreType.DMA],
    )(x)


def _exchange(g, *, name):
    def body(g_ref, out_ref, send_sems, recv_sems, local_sem):
        x, y, c = lax.axis_index("x"), lax.axis_index("y"), lax.axis_index("c")
        me = 4 * x + 2 * y + c
        mine = pltpu.make_async_copy(g_ref.at[me], out_ref.at[me], local_sem)
        mine.start()

        def copy(k):
            px, py, pc = x ^ ((k >> 2) & 1), y ^ ((k >> 1) & 1), c ^ (k & 1)
            return pltpu.make_async_remote_copy(
                src_ref=g_ref.at[4 * px + 2 * py + pc], dst_ref=out_ref.at[me],
                send_sem=send_sems.at[k - 1], recv_sem=recv_sems.at[k - 1],
                device_id=(px, py, pc), device_id_type=_MESH)

        def landing(k):
            px, py, pc = x ^ ((k >> 2) & 1), y ^ ((k >> 1) & 1), c ^ (k & 1)
            return pltpu.make_async_remote_copy(
                src_ref=g_ref.at[me], dst_ref=out_ref.at[4 * px + 2 * py + pc],
                send_sem=send_sems.at[k - 1], recv_sem=recv_sems.at[k - 1],
                device_id=(px, py, pc), device_id_type=_MESH)

        copies = [copy(k) for k in range(1, NDEV)]
        for cp in copies:
            cp.start()
        for k in range(1, NDEV):
            landing(k).wait_recv()
        for cp in copies:
            cp.wait_send()
        mine.wait()

    return pl.pallas_call(
        body, name=name, out_shape=jax.ShapeDtypeStruct(g.shape, g.dtype),
        in_specs=[_ANY], out_specs=_ANY,
        scratch_shapes=[pltpu.SemaphoreType.DMA((7,)), pltpu.SemaphoreType.DMA((7,)), pltpu.SemaphoreType.DMA],
    )(g)


def _sum_slots(parts, *, name):
    n, R, C = parts.shape
    tr, tc = _tile(R, 64, 16), _tile(C, 2048)

    def body(p_ref, o_ref):
        acc = p_ref[0].astype(F32)
        for k in range(1, n):
            acc = acc + p_ref[k].astype(F32)
        o_ref[...] = acc

    return pl.pallas_call(
        body, name=name, grid=(R // tr, C // tc),
        in_specs=[pl.BlockSpec((n, tr, tc), lambda i, j: (0, i, j))],
        out_specs=pl.BlockSpec((tr, tc), lambda i, j: (i, j)),
        out_shape=jax.ShapeDtypeStruct((R, C), F32),
        compiler_params=_params(("parallel", "parallel")),
    )(parts)


def _adamw(w, g, m, v, *, name):
    R, C = w.shape
    budget = 2 << 20
    tr = R if R * C * 4 <= budget else _tile(R, max(8, (budget // (C * 4)) // 8 * 8), 8)

    def body(w_ref, g_ref, m_ref, v_ref, d_ref, nm_ref, nv_ref):
        gv = g_ref[...]
        nm = ADAM_B1 * m_ref[...] + (1.0 - ADAM_B1) * gv
        nv = ADAM_B2 * v_ref[...] + (1.0 - ADAM_B2) * (gv * gv)
        m_hat = nm / (1.0 - ADAM_B1 ** ADAM_STEP)
        v_hat = nv / (1.0 - ADAM_B2 ** ADAM_STEP)
        d_ref[...] = -ADAM_LR * (m_hat / (jnp.sqrt(v_hat) + ADAM_EPS) + ADAM_WD * w_ref[...])
        nm_ref[...] = nm
        nv_ref[...] = nv

    spec = pl.BlockSpec((tr, C), lambda i: (i, 0))
    return pl.pallas_call(
        body, name=name, grid=(R // tr,), in_specs=[spec] * 4, out_specs=[spec] * 3,
        out_shape=[jax.ShapeDtypeStruct((R, C), F32)] * 3,
        compiler_params=_params(("parallel",)),
    )(w, g, m, v)


def _pad_cols(w, *, nq, ng):
    nf = w.shape[-1] - nq - ng
    pad = jnp.zeros(w.shape[:-1] + (FORGET_PAD - nf,), w.dtype)
    return jnp.concatenate([w[..., :nq], w[..., nq + nf:], w[..., nq:nq + nf], pad], axis=-1)


def _unpad_cols(w, *, nq, ng, nf):
    return jnp.concatenate([w[..., :nq], w[..., nq + ng:nq + ng + nf], w[..., nq:nq + ng]], axis=-1)


def kernel(x, w_in, b_forget, b_gate, w_up_a, w_up_b, w_out, ln_g, ln_b, loss_target, m_w_in, m_b_forget, m_b_gate, m_w_up_a, m_w_up_b, m_w_out, m_ln_g, m_ln_b, v_w_in, v_b_forget, v_b_gate, v_w_up_a, v_w_up_b, v_w_out, v_ln_g, v_ln_b):
    depth = w_in.shape[0]
    _, S, D = x.shape
    WA, WB = w_up_a.shape[1], w_up_b.shape[1]
    nha, nhb = WA // HEAD_DIM, WB // HEAD_DIM
    assert nhb == b_forget.shape[1] and WA == WB and nhb <= HEAD_ROWS
    nq, ng, nf = 4 * WA + 4 * WB, 2 * D, nhb
    assert w_in.shape[2] == nq + nf + ng
    ncp = nq + ng + FORGET_PAD
    off_va, off_za, off_b, off_zb, off_g, off_f = 2 * WA, 3 * WA, 4 * WA, 4 * WA + 3 * WB, nq, nq + ng
    alpha = float((2 * depth) ** 0.25)
    dsh = D // NDEV
    tq_f, tq_b = _tile(S, FOX_TILE_FWD), _tile(S, FOX_TILE_BWD)

    x0 = x[0]
    target = loss_target[0]
    cos, sin = _rope_tables(S)

    w_in_p = _pad_cols(w_in, nq=nq, ng=ng).astype(BF16)
    w_up_a_t = jnp.swapaxes(w_up_a, 1, 2).astype(BF16)
    w_up_b_t = jnp.swapaxes(w_up_b, 1, 2).astype(BF16)
    w_out_c = w_out.astype(BF16)
    W_in, W_ua, W_ub, W_out = [], [], [], []
    for l in range(depth):
        W_in.append(_all_gather(w_in_p[l], name=f"gather_w_in_{l}").reshape(D, ncp))
        W_ua.append(_all_gather(w_up_a_t[l], name=f"gather_w_up_a_{l}").reshape(D, WA))
        W_ub.append(_all_gather(w_up_b_t[l], name=f"gather_w_up_b_{l}").reshape(D, WB))
        W_out.append(_all_gather(w_out_c[l], name=f"gather_w_out_{l}").reshape(D, D))

    bf_rows = [jnp.broadcast_to(jnp.pad(b_forget[l], (0, HEAD_ROWS - nhb))[:, None], (HEAD_ROWS, LANES))
               for l in range(depth)]

    saved = []
    xf, xb = x0, x0.astype(BF16)
    for l in range(depth):
        h = _mm(xb, W_in[l], name="in_proj", tm=1024, tn=1280, tk=2048)
        qk_a = _prep_a(h, cos, sin, W=WA, S=S)
        qkv_b = _prep_b(h, off=off_b, W=WB, S=S)
        out_a, lse_a, ga = _band_fwd(qk_a, h, nha=nha, S=S, v_off=off_va, z_off=off_za)
        f3 = _lane_blocks(jnp.pad(h[:, off_f:off_f + nhb].T, ((0, HEAD_ROWS - nhb), (0, 0))))
        c = _lane_unblocks(_forget_cumsum(f3, bf_rows[l]))[:nhb]
        out_b, lq_b = _fox_fwd(qkv_b, c.reshape(nhb, S // tq_f, 1, tq_f), nhb=nhb, S=S, tq=tq_f)
        ck = c.reshape(nhb, S // tq_b, 1, tq_b)
        gb = _gate(out_b, h, z_off=off_zb, W=WB, S=S)
        up_a = _mm(ga, W_ua[l], tb=True, name="up_a", tm=1024, tn=1024, tk=1536)
        up_b = _mm(gb, W_ub[l], tb=True, name="up_b", tm=1024, tn=1024, tk=1536)
        bg = b_gate[l][None, :]
        mix = _merge(up_a, up_b, h, bg, g_off=off_g, D=D, S=S)
        y = _mm(mix, W_out[l], name="out_proj", tm=1024, tn=1024, tk=2048)
        xn_f, xn_b, z = _deepnorm(xf, y, ln_g[l][None, :], ln_b[l][None, :], alpha=alpha, D=D, S=S)
        saved.append(dict(xb=xb, h=h, qk_a=qk_a, qkv_b=qkv_b, out_a=out_a, lse_a=lse_a, ga=ga, f3=f3, ck=ck,
                          out_b=out_b, lq_b=lq_b, gb=gb, up_a=up_a, up_b=up_b, mix=mix, z=z, bg=bg))
        xf, xb = xn_f, xn_b

    dx, loss_row = _loss_head(xf, target, D=D, S=S)

    recv = [None] * depth
    small = [None] * depth
    for l in reversed(range(depth)):
        sv = saved[l]
        h = sv["h"]
        dz, dzb, d_lng, d_lnb = _deepnorm_bwd(dx, sv["z"], ln_g[l][None, :], D=D, S=S)
        g_w_out = _mm(sv["mix"], dzb, ta=True, name="g_w_out", tm=512, tn=1024, tk=1024, out_dtype=BF16)
        dmix = _mm(dzb, W_out[l], tb=True, name="d_mix", tm=1024, tn=1024, tk=2048)
        dup_a, dup_b, dgl_a, dgl_b, d_bga, d_bgb = _merge_bwd(dmix, sv["up_a"], sv["up_b"], h, sv["bg"],
                                                              g_off=off_g, D=D, S=S)
        g_w_ua = _mm(dup_a, sv["ga"], ta=True, name="g_w_up_a", tm=512, tn=1536, tk=1024, out_dtype=BF16)
        g_w_ub = _mm(dup_b, sv["gb"], ta=True, name="g_w_up_b", tm=512, tn=1536, tk=1024, out_dtype=BF16)
        dga = _mm(dup_a, W_ua[l], name="d_ga", tm=1024, tn=1536, tk=2048)
        dgb = _mm(dup_b, W_ub[l], name="d_gb", tm=1024, tn=1536, tk=2048)
        do_a, dza, stats_a = _gate_bwd(dga, sv["out_a"], h, sv["lse_a"], z_off=off_za, W=WA, S=S,
                                       name="gate_bwd_a", do_dtype=F32)
        do_b, dzb_, stats_b = _gate_bwd(dgb, sv["out_b"], h, sv["lq_b"], z_off=off_zb, W=WB, S=S,
                                        name="gate_bwd_b", do_dtype=BF16)
        dqa, dka, dva = _rope_bwd(*_band_bwd(sv["qk_a"], h, do_a, stats_a, nha=nha, S=S, v_off=off_va),
                                  cos, sin, W=WA, S=S)
        dqb, dkb, dvb, dcq, dck = _fox_bwd(sv["qkv_b"], do_b, stats_b, sv["ck"], nhb=nhb, S=S, tq=tq_b)
        head_rows = lambda t: _lane_blocks(jnp.pad(t, ((0, HEAD_ROWS - nhb), (0, 0))))
        df3, d_bf = _forget_bwd(head_rows(dcq[:, ::LANES].T), head_rows(dck.reshape(nhb, S)), sv["f3"], bf_rows[l])
        dfl = jnp.pad(_lane_unblocks(df3)[:nhb].T, ((0, 0), (0, FORGET_PAD - nhb))).astype(BF16)
        dh = jnp.concatenate([dqa, dka, dva, dza, dqb.astype(BF16), dkb, dvb, dzb_, dgl_a, dgl_b, dfl], axis=1)
        g_w_in = _mm(sv["xb"], dh, ta=True, name="g_w_in", tm=512, tn=1280, tk=1024, out_dtype=BF16)
        dx = _mm(dh, W_in[l], tb=True, name="d_x", tm=1024, tn=1024, tk=1280, add=dz, add_scale=alpha)
        recv[l] = (
            _exchange(g_w_in.reshape(NDEV, dsh, ncp), name=f"xchg_w_in_{l}"),
            _exchange(g_w_ua.reshape(NDEV, dsh, WA), name=f"xchg_w_up_a_{l}"),
            _exchange(g_w_ub.reshape(NDEV, dsh, WB), name=f"xchg_w_up_b_{l}"),
            _exchange(g_w_out.reshape(NDEV, dsh, D), name=f"xchg_w_out_{l}"),
        )
        small[l] = jnp.concatenate([d_bf[:nhb, 0], jnp.zeros((LANES - nhb,), F32), d_bga[0], d_bgb[0],
                                    d_lng[0], d_lnb[0]])

    n_small = LANES + 2 * D + 2 * D
    flat = jnp.concatenate(small + [loss_row[0]])
    rows = -(-flat.shape[0] // (8 * LANES)) * 8
    flat = jnp.pad(flat, (0, rows * LANES - flat.shape[0])).reshape(rows, LANES)
    tot = _sum_slots(_all_gather(flat, name="gather_small"), name="sum_small").reshape(-1)
    loss = tot[depth * n_small]
    sm = tot[:depth * n_small].reshape(depth, n_small)
    g_bf, g_bg = sm[:, :nhb], sm[:, LANES:LANES + 2 * D]
    g_lg, g_lb = sm[:, LANES + 2 * D:LANES + 3 * D], sm[:, LANES + 3 * D:]

    g_in = jnp.stack([_unpad_cols(_sum_slots(recv[l][0], name="sum_w_in"), nq=nq, ng=ng, nf=nf) for l in range(depth)])
    g_ua = jnp.stack([_sum_slots(recv[l][1], name="sum_w_up_a").T for l in range(depth)])
    g_ub = jnp.stack([_sum_slots(recv[l][2], name="sum_w_up_b").T for l in range(depth)])
    g_out = jnp.stack([_sum_slots(recv[l][3], name="sum_w_out") for l in range(depth)])

    def adam(w, g, m, v, name):
        shp = w.shape
        d_, m_, v_ = _adamw(*[t.reshape(-1, shp[-1]) for t in (w, g, m, v)], name=name)
        return d_.reshape(shp), m_.reshape(shp), v_.reshape(shp)

    grads = [g_in, g_bf, g_bg, g_ua, g_ub, g_out, g_lg, g_lb]
    ws = [w_in, b_forget, b_gate, w_up_a, w_up_b, w_out, ln_g, ln_b]
    ms = [m_w_in, m_b_forget, m_b_gate, m_w_up_a, m_w_up_b, m_w_out, m_ln_g, m_ln_b]
    vs = [v_w_in, v_b_forget, v_b_gate, v_w_up_a, v_w_up_b, v_w_out, v_ln_g, v_ln_b]
    names = ["w_in", "b_forget", "b_gate", "w_up_a", "w_up_b", "w_out", "ln_g", "ln_b"]
    upd = [adam(w, g, m, v, f"adamw_{n}") for w, g, m, v, n in zip(ws, grads, ms, vs, names)]
    return (loss, dx[None], *grads, *[u[0] for u in upd], *[u[1] for u in upd], *[u[2] for u in upd])
```

```python
import functools

import jax
import jax.numpy as jnp
from jax import lax
from jax.experimental import pallas as pl
from jax.experimental.pallas import tpu as pltpu

F32 = jnp.float32
BF16 = jnp.bfloat16

NDEV = 8
HEAD_DIM = 128
LANES = 128
Q_BLOCK = 128
DILATIONS = (1, 4, 16)
ROPE_THETA = 10000.0
LN_EPS = 1e-5
ATT_SCALE = HEAD_DIM ** -0.5
NEG = -1e30
FORGET_PAD = 256
BAND_CHUNK = 2048
BAND_UNROLL = 8
FOX_TILE_FWD = 1024
FOX_TILE_BWD = 1024
FOX_SUB = 2
HEAD_ROWS = 16
ADAM_LR, ADAM_B1, ADAM_B2, ADAM_EPS, ADAM_WD, ADAM_STEP = 0.001, 0.9, 0.999, 1e-08, 0.01, 10
VMEM_LIMIT = 56 * 1024 * 1024


def _tile(n, pref, mult=LANES):
    if n <= pref:
        return n
    t = (pref // mult) * mult
    while t >= mult:
        if n % t == 0:
            return t
        t -= mult
    return n


def _params(sem=None):
    return pltpu.CompilerParams(dimension_semantics=sem, vmem_limit_bytes=VMEM_LIMIT)


def _sigmoid(z):
    return 1.0 / (1.0 + jnp.exp(-z))


_NT = (((1,), (1,)), ((), ()))
_NN = (((1,), (0,)), ((), ()))
_TN = (((0,), (0,)), ((), ()))


def _dot(a, b, dims):
    return lax.dot_general(a, b, dims, preferred_element_type=F32)


def _mm(a, b, *, name, ta=False, tb=False, out_dtype=F32, tm=512, tn=512, tk=512, add=None, add_scale=1.0,
        side=None):
    if ta:
        K, M = a.shape
    else:
        M, K = a.shape
    if tb:
        N, K2 = b.shape
    else:
        K2, N = b.shape
    assert K == K2, (a.shape, b.shape, ta, tb)
    tm, tn, tk = _tile(M, tm), _tile(N, tn), _tile(K, tk)
    ni, nj, nk = M // tm, N // tn, K // tk
    dims = (((0 if ta else 1,), (1 if tb else 0,)), ((), ()))
    n_side = 0 if side is None else len(side.ins)
    n_in = 2 + (add is not None)

    def body(*refs):
        a_ref, b_ref = refs[:2]
        add_ref = refs[2] if add is not None else None
        side_in = refs[n_in:n_in + n_side]
        o_ref = refs[n_in + n_side]
        side_out = refs[n_in + n_side + 1:n_in + 2 * n_side + 1]
        acc_ref = refs[n_in + 2 * n_side + 1]
        sems = refs[n_in + 2 * n_side + 2:]
        k = pl.program_id(2)
        if side is not None:
            @pl.when((pl.program_id(0) == 0) & (pl.program_id(1) == 0) & (k == 0))
            def _():
                side.start(side_in, side_out, *sems)

        part = _dot(a_ref[...].astype(BF16), b_ref[...].astype(BF16), dims)

        @pl.when(k == 0)
        def _():
            acc_ref[...] = part

        @pl.when(k > 0)
        def _():
            acc_ref[...] += part

        @pl.when(k == nk - 1)
        def _():
            r = acc_ref[...]
            if add_ref is not None:
                r = r + add_scale * add_ref[...]
            o_ref[...] = r.astype(out_dtype)

        if side is not None:
            @pl.when((pl.program_id(0) == ni - 1) & (pl.program_id(1) == nj - 1) & (k == nk - 1))
            def _():
                side.finish(side_in, side_out, *sems)

    a_spec = pl.BlockSpec((tk, tm), lambda i, j, k: (k, i)) if ta else pl.BlockSpec((tm, tk), lambda i, j, k: (i, k))
    b_spec = pl.BlockSpec((tn, tk), lambda i, j, k: (j, k)) if tb else pl.BlockSpec((tk, tn), lambda i, j, k: (k, j))
    o_spec = pl.BlockSpec((tm, tn), lambda i, j, k: (i, j))
    in_specs, args = [a_spec, b_spec], [a, b]
    if add is not None:
        in_specs.append(o_spec)
        args.append(add)
    out_shape = jax.ShapeDtypeStruct((M, N), out_dtype)
    scratch = [pltpu.VMEM((tm, tn), F32)]
    if side is None:
        return pl.pallas_call(
            body, name=name, grid=(ni, nj, nk), in_specs=in_specs, out_specs=o_spec, out_shape=out_shape,
            scratch_shapes=scratch, compiler_params=_params(("parallel", "parallel", "arbitrary")),
        )(*args)
    res = pl.pallas_call(
        body, name=name, grid=(ni, nj, nk), in_specs=in_specs + [_ANY] * n_side,
        out_specs=[o_spec] + [_ANY] * n_side, out_shape=[out_shape] + side.outs,
        scratch_shapes=scratch + _side_scratch(side),
        compiler_params=_params(("arbitrary", "arbitrary", "arbitrary")),
    )(*args, *side.ins)
    return res[0], res[1:]


def _ew(fn, ins, outs, *, rows, cols, tr, tc, name):
    assert rows % tr == 0 and cols % tc == 0

    def spec(kind, off):
        assert off % tc == 0, (off, tc)
        ob = off // tc
        if kind == "tile":
            return pl.BlockSpec((tr, tc), lambda j, i: (i, j + ob))
        if kind == "tab":
            return pl.BlockSpec((tr, LANES), lambda j, i: (i, 0))
        return pl.BlockSpec((1, tc), lambda j, i: (0, j + ob))

    n_in = len(ins)

    def body(*refs):
        fn(pl.program_id(1), refs[:n_in], refs[n_in:])

    return pl.pallas_call(
        body, name=name, grid=(cols // tc, rows // tr),
        in_specs=[spec(k, o) for _, k, o in ins],
        out_specs=[spec(k, o) for _, _, k, o in outs],
        out_shape=[jax.ShapeDtypeStruct(s, d) for s, d, _, _ in outs],
        compiler_params=_params(("parallel", "arbitrary")),
    )(*[a for a, _, _ in ins])


def _heads(t):
    return [t[:, e:e + LANES] for e in range(0, t.shape[1], LANES)]


def _rope_tables(S):
    half = HEAD_DIM // 2
    inv_freq = ROPE_THETA ** (-jnp.arange(half, dtype=F32) / half)
    ang = jnp.arange(S, dtype=jnp.int32).astype(F32)[:, None] * inv_freq[None, :]
    cos, sin = jnp.cos(ang), jnp.sin(ang)
    return jnp.concatenate([cos, cos], axis=1), jnp.concatenate([-sin, sin], axis=1)


def _rope(t, cs, sn):
    return jnp.concatenate([g * cs + pltpu.roll(g, HEAD_DIM // 2, 1) * sn for g in _heads(t)], axis=1)


def _rope_t(t, cs, sn):
    return jnp.concatenate([g * cs + pltpu.roll(g * sn, HEAD_DIM // 2, 1) for g in _heads(t)], axis=1)


def _prep_a(h, cos, sin, *, W, S):
    tr, tc = _tile(S, 512, 8), _tile(W, 512)

    def fn(i, ins, outs):
        outs[0][...] = _rope(ins[0][...], ins[1][...], ins[2][...])

    return _ew(fn, [(h, "tile", 0), (cos, "tab", 0), (sin, "tab", 0)], [((S, 2 * W), F32, "tile", 0)],
               rows=S, cols=2 * W, tr=tr, tc=tc, name="prep_a")[0]


def _prep_b(h, *, off, W, S):
    tr, tc = _tile(S, 512, 8), _tile(W, 512)

    def fn(i, ins, outs):
        scale = jnp.where(pl.program_id(0) * tc < W, ATT_SCALE, 1.0).astype(F32)
        outs[0][...] = (ins[0][...] * scale).astype(BF16)

    return _ew(fn, [(h, "tile", off)], [((S, 3 * W), BF16, "tile", 0)], rows=S, cols=3 * W, tr=tr, tc=tc,
               name="prep_b")[0]


def _rope_bwd(dq, dk, dv, cos, sin, *, W, S):
    tr, tc = _tile(S, 512, 8), _tile(W, 512)

    def fn(i, ins, outs):
        cs, sn = ins[3][...], ins[4][...]
        outs[0][...] = _rope_t(ins[0][...], cs, sn).astype(BF16)
        outs[1][...] = _rope_t(ins[1][...], cs, sn).astype(BF16)
        outs[2][...] = ins[2][...].astype(BF16)

    ins = [(dq, "tile", 0), (dk, "tile", 0), (dv, "tile", 0), (cos, "tab", 0), (sin, "tab", 0)]
    return _ew(fn, ins, [((S, W), BF16, "tile", 0)] * 3, rows=S, cols=W, tr=tr, tc=tc, name="rope_bwd")


def _band_mask(n):
    k0 = jnp.maximum(n - 1, 0) * Q_BLOCK
    dist = (n * Q_BLOCK + lax.broadcasted_iota(jnp.int32, (Q_BLOCK, 2 * Q_BLOCK), 0)) - (
        k0 + lax.broadcasted_iota(jnp.int32, (Q_BLOCK, 2 * Q_BLOCK), 1))
    return pl.multiple_of(k0, Q_BLOCK), (dist >= 0) & (dist <= Q_BLOCK)


def _band_rows(it, c, d, nbc):
    if d == 1:
        k0, valid = _band_mask(c * nbc + it)
        return pl.ds(pl.multiple_of(it * Q_BLOCK, Q_BLOCK), Q_BLOCK), pl.ds(k0, 2 * Q_BLOCK), valid
    r, n = it // nbc, it % nbc
    k0, valid = _band_mask(c * nbc + n)
    return (pl.ds(r + d * Q_BLOCK * n, Q_BLOCK, stride=d), pl.ds(r + d * k0, 2 * Q_BLOCK, stride=d), valid)


def _band_chunk(S):
    ch = min(S, BAND_CHUNK)
    assert S % ch == 0 and ch % (Q_BLOCK * max(DILATIONS)) == 0 and S >= 2 * Q_BLOCK * max(DILATIONS)
    return ch


def _band_fwd(qk, h, *, nha, S, v_off, z_off):
    CH = _band_chunk(S)
    W = nha * LANES
    slab = 256

    def body(q_ref, k_ref, v_ref, z_ref, o_ref, lse_ref, g_ref, *scratch):
        os_refs, ls_refs = scratch[:len(DILATIONS)], scratch[len(DILATIONS):]
        c = pl.program_id(1)
        for gi, d in enumerate(DILATIONS):
            nbc = CH // (Q_BLOCK * d)

            def step(it, carry, gi=gi, d=d, nbc=nbc):
                rq, rk, valid = _band_rows(it, c, d, nbc)
                q = q_ref[rq, :].astype(BF16)
                k = k_ref[rk, :].astype(BF16)
                v = v_ref[rk, :].astype(BF16)
                s = jnp.where(valid, _dot(q, k, _NT) * ATT_SCALE, NEG)
                m = jnp.max(s, axis=1, keepdims=True)
                p = jnp.exp(s - m)
                den = jnp.sum(p, axis=1, keepdims=True)
                os_refs[gi][rq, :] = _dot(p.astype(BF16), v, _NN) / den
                ls_refs[gi][rq, :] = jnp.broadcast_to(m + jnp.log(den), (Q_BLOCK, LANES))
                return carry

            lax.fori_loop(0, CH // Q_BLOCK, step, 0, unroll=BAND_UNROLL)

        def mix(t, carry):
            rows = pl.ds(pl.multiple_of(t * slab, slab), slab)
            a, b, cc = (r[rows, :] for r in ls_refs)
            m = jnp.maximum(jnp.maximum(a, b), cc)
            ea, eb, ec = jnp.exp(a - m), jnp.exp(b - m), jnp.exp(cc - m)
            den = ea + eb + ec
            out = (ea * os_refs[0][rows, :] + eb * os_refs[1][rows, :] + ec * os_refs[2][rows, :]) / den
            z = z_ref[rows, :]
            o_ref[rows, :] = out
            lse_ref[rows, :] = m + jnp.log(den)
            g_ref[rows, :] = (out * (z * _sigmoid(z))).astype(BF16)
            return carry

        lax.fori_loop(0, CH // slab, mix, 0)

    chunk = pl.BlockSpec((CH, LANES), lambda hd, c: (c, hd))
    return pl.pallas_call(
        body, name="band_fwd", grid=(nha, S // CH),
        in_specs=[chunk,
                  pl.BlockSpec((S, LANES), lambda hd, c: (0, nha + hd)),
                  pl.BlockSpec((S, LANES), lambda hd, c: (0, v_off // LANES + hd)),
                  pl.BlockSpec((CH, LANES), lambda hd, c: (c, z_off // LANES + hd))],
        out_specs=[chunk, chunk, chunk],
        out_shape=[jax.ShapeDtypeStruct((S, W), F32), jax.ShapeDtypeStruct((S, W), F32),
                   jax.ShapeDtypeStruct((S, W), BF16)],
        scratch_shapes=[pltpu.VMEM((CH, LANES), F32)] * (2 * len(DILATIONS)),
        compiler_params=_params(("parallel", "arbitrary")),
    )(qk, qk, h, h)


def _band_bwd(qk, h, do, stats, *, nha, S, v_off):
    CH = _band_chunk(S)
    W = nha * LANES

    def body(q_ref, k_ref, v_ref, do_ref, st_ref, dq_ref, dk_ref, dv_ref):
        c = pl.program_id(1)

        @pl.when(c == 0)
        def _():
            dk_ref[...] = jnp.zeros_like(dk_ref)
            dv_ref[...] = jnp.zeros_like(dv_ref)

        for gi, d in enumerate(DILATIONS):
            nbc = CH // (Q_BLOCK * d)

            def step(it, carry, gi=gi, d=d, nbc=nbc):
                rq, rk, valid = _band_rows(it, c, d, nbc)
                q = q_ref[rq, :].astype(BF16)
                k = k_ref[rk, :].astype(BF16)
                v = v_ref[rk, :].astype(BF16)
                g = do_ref[rq, :].astype(BF16)
                st = st_ref[rq, :]
                s = _dot(q, k, _NT) * ATT_SCALE
                p = jnp.where(valid, jnp.exp(s - st[:, 0:1]), 0.0)
                ds = (p * (_dot(g, v, _NT) - st[:, 1:2]) * ATT_SCALE).astype(BF16)
                dq = _dot(ds, k, _NN)
                if gi == 0:
                    dq_ref[rq, :] = dq
                else:
                    dq_ref[rq, :] += dq
                dk_ref[rk, :] += _dot(ds, q, _TN)
                dv_ref[rk, :] += _dot(p.astype(BF16), g, _TN)
                return carry

            lax.fori_loop(0, CH // Q_BLOCK, step, 0, unroll=BAND_UNROLL // 2)

    chunk = pl.BlockSpec((CH, LANES), lambda hd, c: (c, hd))
    whole = pl.BlockSpec((S, LANES), lambda hd, c: (0, hd))
    return pl.pallas_call(
        body, name="band_bwd", grid=(nha, S // CH),
        in_specs=[chunk,
                  pl.BlockSpec((S, LANES), lambda hd, c: (0, nha + hd)),
                  pl.BlockSpec((S, LANES), lambda hd, c: (0, v_off // LANES + hd)),
                  chunk, chunk],
        out_specs=[chunk, whole, whole],
        out_shape=[jax.ShapeDtypeStruct((S, W), F32)] * 3,
        compiler_params=_params(("parallel", "arbitrary")),
    )(qk, qk, h, do, stats)


def _gate(o, h, *, z_off, W, S):
    tr, tc = _tile(S, 512, 8), _tile(W, 512)

    def fn(i, ins, outs):
        z = ins[1][...]
        outs[0][...] = (ins[0][...] * (z * _sigmoid(z))).astype(BF16)

    return _ew(fn, [(o, "tile", 0), (h, "tile", z_off)], [((S, W), BF16, "tile", 0)],
               rows=S, cols=W, tr=tr, tc=tc, name="gate")[0]


def _gate_bwd(dg, o, h, lane0, *, z_off, W, S, name, do_dtype):
    tr, tc = _tile(S, 512, 8), _tile(W, 512)

    def fn(i, ins, outs):
        g, out, z = ins[0][...], ins[1][...], ins[2][...]
        sg = _sigmoid(z)
        do = g * (z * sg)
        outs[0][...] = do.astype(do_dtype)
        outs[1][...] = (g * out * (sg * (1.0 + z * (1.0 - sg)))).astype(BF16)
        delta = jnp.concatenate([jnp.broadcast_to(jnp.sum(t, axis=1, keepdims=True), t.shape)
                                 for t in _heads(do * out)], axis=1)
        first = lax.broadcasted_iota(jnp.int32, delta.shape, 1) % LANES == 0
        outs[2][...] = jnp.where(first, ins[3][...], delta)

    ins = [(dg, "tile", 0), (o, "tile", 0), (h, "tile", z_off), (lane0, "tile", 0)]
    outs = [((S, W), do_dtype, "tile", 0), ((S, W), BF16, "tile", 0), ((S, W), F32, "tile", 0)]
    return _ew(fn, ins, outs, rows=S, cols=W, tr=tr, tc=tc, name=name)


def _split3(x):
    hi = x.astype(BF16)
    r = x - hi.astype(F32)
    mid = r.astype(BF16)
    lo = (r - mid.astype(F32)).astype(BF16)
    return hi, mid, lo


def _tri_dot(x, tri):
    hi, mid, lo = _split3(x)
    return _dot(hi, tri, _NN) + (_dot(mid, tri, _NN) + _dot(lo, tri, _NN))


def _log1p(u):
    w = 1.0 + u
    return jnp.where(w == 1.0, u, jnp.log(w) * (u / jnp.where(w == 1.0, 1.0, w - 1.0)))


def _lane_blocks(t):
    R, S = t.shape
    return t.reshape(R, S // LANES, LANES).transpose(1, 0, 2)


def _lane_unblocks(t):
    nb, R, _ = t.shape
    return t.transpose(1, 0, 2).reshape(R, nb * LANES)


def _forget_cumsum(f3, bf):
    nb, R, _ = f3.shape

    def body(f_ref, b_ref, c_ref):
        row = lax.broadcasted_iota(jnp.int32, (LANES, LANES), 0)
        colm = lax.broadcasted_iota(jnp.int32, (LANES, LANES), 1)
        tri = (row <= colm).astype(BF16)
        bias = b_ref[...]

        def step(n, carry):
            z = f_ref[n] + bias
            logf = jnp.minimum(z, 0.0) - _log1p(jnp.exp(-jnp.abs(z)))
            c = _tri_dot(logf, tri) + carry
            c_ref[n] = c
            return jnp.broadcast_to(c[:, LANES - 1:LANES], (R, LANES))

        lax.fori_loop(0, nb, step, jnp.zeros((R, LANES), F32))

    return pl.pallas_call(
        body, name="forget_cumsum", out_shape=jax.ShapeDtypeStruct(f3.shape, F32),
        compiler_params=_params(),
    )(f3, bf)


def _forget_bwd(dcq3, dck3, f3, bf):
    nb, R, _ = f3.shape

    def body(dcq_ref, dck_ref, f_ref, b_ref, df_ref, db_ref):
        row = lax.broadcasted_iota(jnp.int32, (LANES, LANES), 0)
        colm = lax.broadcasted_iota(jnp.int32, (LANES, LANES), 1)
        tri = (row >= colm).astype(BF16)
        bias = b_ref[...]

        def step(t, carry):
            tail, tot = carry
            n = nb - 1 - t
            r = _tri_dot(dcq_ref[n] + dck_ref[n], tri) + tail
            df = r * _sigmoid(-(f_ref[n] + bias))
            df_ref[n] = df
            tot = tot + jnp.broadcast_to(jnp.sum(df, axis=1, keepdims=True), (R, LANES))
            return jnp.broadcast_to(r[:, 0:1], (R, LANES)), tot

        _, tot = lax.fori_loop(0, nb, step, (jnp.zeros((R, LANES), F32), jnp.zeros((R, LANES), F32)))
        db_ref[...] = tot

    return pl.pallas_call(
        body, name="forget_bwd",
        out_shape=[jax.ShapeDtypeStruct(f3.shape, F32), jax.ShapeDtypeStruct((R, LANES), F32)],
        compiler_params=_params(),
    )(dcq3, dck3, f3, bf)


def _causal(t):
    return lax.broadcasted_iota(jnp.int32, (t, t), 0) >= lax.broadcasted_iota(jnp.int32, (t, t), 1)


def _fox_fwd(qkv, ck, *, nhb, S, tq):
    nq = S // tq

    ts = tq // FOX_SUB

    def body(q_ref, k_ref, v_ref, ck_ref, o_ref, lq_ref):
        i = pl.program_id(1)
        qs = [q_ref[r * ts:(r + 1) * ts, :] for r in range(FOX_SUB)]

        def step(j, carry, masked):
            off = pl.multiple_of(j * tq, tq)
            k = k_ref[pl.ds(off, tq), :]
            v = v_ref[pl.ds(off, tq), :]
            ckj = ck_ref[j]
            new = []
            for r, (m, l, acc) in enumerate(carry):
                nk = (r + 1) * ts if masked else tq
                s = _dot(qs[r], k[:nk], _NT) - ckj[:, :nk]
                if masked:
                    rows = r * ts + lax.broadcasted_iota(jnp.int32, (ts, nk), 0)
                    s = jnp.where(rows >= lax.broadcasted_iota(jnp.int32, (ts, nk), 1), s, NEG)
                m_new = jnp.maximum(m, jnp.max(s, axis=1, keepdims=True))
                alpha = jnp.exp(m - m_new)
                p = jnp.exp(s - m_new)
                l = alpha * l + jnp.sum(p, axis=1, keepdims=True)
                acc = alpha * acc + _dot(p.astype(BF16), v[:nk], _NN)
                new.append((m_new, l, acc))
            return tuple(new)

        init = tuple((jnp.full((ts, 1), NEG, F32), jnp.zeros((ts, 1), F32), jnp.zeros((ts, LANES), F32))
                     for _ in range(FOX_SUB))
        carry = lax.fori_loop(0, i, functools.partial(step, masked=False), init)
        for r, (m, l, acc) in enumerate(step(i, carry, True)):
            o_ref[r * ts:(r + 1) * ts, :] = acc / l
            lq_ref[r * ts:(r + 1) * ts, :] = jnp.broadcast_to(m + jnp.log(l), (ts, LANES))

    return pl.pallas_call(
        body, name="fox_fwd", grid=(nhb, nq),
        in_specs=[pl.BlockSpec((tq, LANES), lambda h, i: (i, h)),
                  pl.BlockSpec((S, LANES), lambda h, i: (0, nhb + h)),
                  pl.BlockSpec((S, LANES), lambda h, i: (0, 2 * nhb + h)),
                  pl.BlockSpec((None, nq, 1, tq), lambda h, i: (h, 0, 0, 0))],
        out_specs=[pl.BlockSpec((tq, LANES), lambda h, i: (i, h))] * 2,
        out_shape=[jax.ShapeDtypeStruct((S, nhb * LANES), F32)] * 2,
        compiler_params=_params(("parallel", "arbitrary")),
    )(qkv, qkv, qkv, ck)


def _fox_bwd(qkv, do, stats, ck, *, nhb, S, tq):
    nq = S // tq
    ts = tq // FOX_SUB

    def body(q_ref, k_ref, v_ref, do_ref, st_ref, ck_ref, dq_ref, dk_ref, dv_ref, dr_ref, dc_ref):
        j = pl.program_id(1)

        @pl.when(j == 0)
        def _():
            dq_ref[...] = jnp.zeros_like(dq_ref)
            dr_ref[...] = jnp.zeros_like(dr_ref)

        k = k_ref[...]
        v = v_ref[...]
        ckv = ck_ref[...]

        def step(i, carry, masked):
            dk, dv, dc = carry
            for r in range(FOX_SUB):
                rows = pl.ds(pl.multiple_of(i * tq + r * ts, ts), ts)
                q = q_ref[rows, :]
                g = do_ref[rows, :]
                st = st_ref[rows, :]
                p = jnp.exp(_dot(q, k, _NT) - ckv - st[:, 0:1])
                if masked:
                    rr = r * ts + lax.broadcasted_iota(jnp.int32, (ts, tq), 0)
                    p = jnp.where(rr >= lax.broadcasted_iota(jnp.int32, (ts, tq), 1), p, 0.0)
                ds = p * (_dot(g, v, _NT) - st[:, 1:2])
                dc = dc - jnp.sum(ds, axis=0, keepdims=True)
                dr_ref[rows, :] += jnp.broadcast_to(jnp.sum(ds, axis=1, keepdims=True), (ts, LANES))
                dsb = ds.astype(BF16)
                dq_ref[rows, :] += _dot(dsb, k, _NN)
                dk = dk + _dot(dsb, q, _TN)
                dv = dv + _dot(p.astype(BF16), g, _TN)
            return dk, dv, dc

        init = (jnp.zeros((tq, LANES), F32), jnp.zeros((tq, LANES), F32), jnp.zeros((1, tq), F32))
        carry = step(j, init, True)
        dk, dv, dc = lax.fori_loop(j + 1, nq, functools.partial(step, masked=False), carry)
        dk_ref[...] = dk.astype(BF16)
        dv_ref[...] = dv.astype(BF16)
        dc_ref[...] = dc

        @pl.when(j == nq - 1)
        def _():
            dq_ref[...] = dq_ref[...] * ATT_SCALE

    def whole(off):
        return pl.BlockSpec((S, LANES), lambda h, j: (0, off + h))

    kv_blk = pl.BlockSpec((tq, LANES), lambda h, j: (j, h))
    c_blk = pl.BlockSpec((None, None, 1, tq), lambda h, j: (h, j, 0, 0))
    return pl.pallas_call(
        body, name="fox_bwd", grid=(nhb, nq),
        in_specs=[whole(0),
                  pl.BlockSpec((tq, LANES), lambda h, j: (j, nhb + h)),
                  pl.BlockSpec((tq, LANES), lambda h, j: (j, 2 * nhb + h)),
                  whole(0), whole(0), c_blk],
        out_specs=[whole(0), kv_blk, kv_blk, whole(0), c_blk],
        out_shape=[jax.ShapeDtypeStruct((S, nhb * LANES), F32),
                   jax.ShapeDtypeStruct((S, nhb * LANES), BF16),
                   jax.ShapeDtypeStruct((S, nhb * LANES), BF16),
                   jax.ShapeDtypeStruct((S, nhb * LANES), F32),
                   jax.ShapeDtypeStruct((nhb, nq, 1, tq), F32)],
        compiler_params=_params(("parallel", "arbitrary")),
    )(qkv, qkv, qkv, do, stats, ck)


def _merge(up_a, up_b, h, b_gate, *, g_off, D, S):
    tr, tc = _tile(S, 512, 8), _tile(D, 512)

    def fn(i, ins, outs):
        ua, ub, la, lb, ba, bb = (r[...] for r in ins)
        outs[0][...] = (_sigmoid(la + ba) * ua + _sigmoid(lb + bb) * ub).astype(BF16)

    ins = [(up_a, "tile", 0), (up_b, "tile", 0), (h, "tile", g_off), (h, "tile", g_off + D),
           (b_gate, "row", 0), (b_gate, "row", D)]
    return _ew(fn, ins, [((S, D), BF16, "tile", 0)], rows=S, cols=D, tr=tr, tc=tc, name="merge")[0]


def _acc_rows(i, ref, val):
    @pl.when(i == 0)
    def _():
        ref[...] = val

    @pl.when(i > 0)
    def _():
        ref[...] += val


def _merge_bwd(dmix, up_a, up_b, h, b_gate, *, g_off, D, S):
    tr, tc = _tile(S, 512, 8), _tile(D, 512)

    def fn(i, ins, outs):
        dm, ua, ub, la, lb, ba, bb = (r[...] for r in ins)
        ga, gb = _sigmoid(la + ba), _sigmoid(lb + bb)
        outs[0][...] = (ga * dm).astype(BF16)
        outs[1][...] = (gb * dm).astype(BF16)
        dla = ua * dm * (ga * (1.0 - ga))
        dlb = ub * dm * (gb * (1.0 - gb))
        outs[2][...] = dla.astype(BF16)
        outs[3][...] = dlb.astype(BF16)
        _acc_rows(i, outs[4], jnp.sum(dla, axis=0, keepdims=True))
        _acc_rows(i, outs[5], jnp.sum(dlb, axis=0, keepdims=True))

    ins = [(dmix, "tile", 0), (up_a, "tile", 0), (up_b, "tile", 0), (h, "tile", g_off), (h, "tile", g_off + D),
           (b_gate, "row", 0), (b_gate, "row", D)]
    outs = [((S, D), BF16, "tile", 0)] * 4 + [((1, D), F32, "acc", 0)] * 2
    return _ew(fn, ins, outs, rows=S, cols=D, tr=tr, tc=tc, name="merge_bwd")


def _deepnorm(x, y, g, b, *, alpha, D, S):
    tr = _tile(S, 256, 8)

    def fn(i, ins, outs):
        z = alpha * ins[0][...] + ins[1][...]
        mu = jnp.mean(z, axis=1, keepdims=True)
        zc = z - mu
        var = jnp.mean(zc * zc, axis=1, keepdims=True)
        xn = zc * lax.rsqrt(var + LN_EPS) * ins[2][...] + ins[3][...]
        outs[0][...] = xn
        outs[1][...] = xn.astype(BF16)
        outs[2][...] = z

    ins = [(x, "tile", 0), (y, "tile", 0), (g, "row", 0), (b, "row", 0)]
    outs = [((S, D), F32, "tile", 0), ((S, D), BF16, "tile", 0), ((S, D), F32, "tile", 0)]
    return _ew(fn, ins, outs, rows=S, cols=D, tr=tr, tc=D, name="deepnorm")


def _deepnorm_bwd(dxn, z, g, *, D, S):
    tr = _tile(S, 256, 8)

    def fn(i, ins, outs):
        dx, zv = ins[0][...], ins[1][...]
        mu = jnp.mean(zv, axis=1, keepdims=True)
        zc = zv - mu
        rstd = lax.rsqrt(jnp.mean(zc * zc, axis=1, keepdims=True) + LN_EPS)
        u = zc * rstd
        du = dx * ins[2][...]
        dz = rstd * (du - jnp.mean(du, axis=1, keepdims=True) - u * jnp.mean(du * u, axis=1, keepdims=True))
        outs[0][...] = dz
        outs[1][...] = dz.astype(BF16)
        _acc_rows(i, outs[2], jnp.sum(dx * u, axis=0, keepdims=True))
        _acc_rows(i, outs[3], jnp.sum(dx, axis=0, keepdims=True))

    ins = [(dxn, "tile", 0), (z, "tile", 0), (g, "row", 0)]
    outs = [((S, D), F32, "tile", 0), ((S, D), BF16, "tile", 0), ((1, D), F32, "acc", 0), ((1, D), F32, "acc", 0)]
    return _ew(fn, ins, outs, rows=S, cols=D, tr=tr, tc=D, name="deepnorm_bwd")


def _loss_head(y, target, *, D, S):
    tr = _tile(S, 256, 8)

    def body(y_ref, t_ref, d_ref, l_ref):
        err = y_ref[...] - t_ref[...]
        d_ref[...] = err * (1.0 / D)
        part = 0.5 * jnp.sum(jnp.sum(err * err, axis=1, keepdims=True) * (1.0 / D), axis=0, keepdims=True)
        _acc_rows(pl.program_id(0), l_ref, jnp.broadcast_to(part, (1, LANES)))

    return pl.pallas_call(
        body, name="loss_head", grid=(S // tr,),
        in_specs=[pl.BlockSpec((tr, D), lambda i: (i, 0))] * 2,
        out_specs=[pl.BlockSpec((tr, D), lambda i: (i, 0)), pl.BlockSpec((1, LANES), lambda i: (0, 0))],
        out_shape=[jax.ShapeDtypeStruct((S, D), F32), jax.ShapeDtypeStruct((1, LANES), F32)],
        compiler_params=_params(("arbitrary",)),
    )(y, target)


_MESH = pl.DeviceIdType.MESH
_ANY = pl.BlockSpec(memory_space=pl.ANY)


class _Gather:
    def __init__(self, xs):
        self.ins = list(xs)
        self.outs = [jax.ShapeDtypeStruct((NDEV,) + x.shape, x.dtype) for x in xs]

    def _ctx(self, a, x_ref, out_ref, send_sems, recv_sems):
        x, y, c = lax.axis_index("x"), lax.axis_index("y"), lax.axis_index("c")
        me, sibling = (x, y, c), (x, y, 1 - c)
        chips = [(1 - x, y), (x, 1 - y), (1 - x, 1 - y)]

        def slot(px, py, pc):
            return out_ref.at[4 * px + 2 * py + pc]

        def copy(k, block, to, src=None):
            return pltpu.make_async_remote_copy(
                src_ref=slot(*block) if src is None else src, dst_ref=slot(*block),
                send_sem=send_sems.at[7 * a + k], recv_sem=recv_sems.at[7 * a + k], device_id=to, device_id_type=_MESH)

        first = [copy(0, me, sibling, src=x_ref)] + [copy(1 + j, me, (*chip, c), src=x_ref)
                                                      for j, chip in enumerate(chips)]
        return c, me, sibling, chips, slot, copy, first

    def start(self, in_refs, out_refs, send_sems, recv_sems, local_sems):
        for a, (x_ref, out_ref) in enumerate(zip(in_refs, out_refs)):
            _, me, _, _, slot, _, first = self._ctx(a, x_ref, out_ref, send_sems, recv_sems)
            pltpu.make_async_copy(x_ref, slot(*me), local_sems.at[a]).start()
            for cp in first:
                cp.start()

    def finish(self, in_refs, out_refs, send_sems, recv_sems, local_sems):
        for a, (x_ref, out_ref) in enumerate(zip(in_refs, out_refs)):
            c, me, sibling, chips, slot, copy, first = self._ctx(a, x_ref, out_ref, send_sems, recv_sems)
            passed = [copy(4 + j, (*chip, c), sibling) for j, chip in enumerate(chips)]
            for j, chip in enumerate(chips):
                copy(1 + j, (*chip, c), me).wait_recv()
                passed[j].start()
            copy(0, sibling, me).wait_recv()
            for j, chip in enumerate(chips):
                copy(4 + j, (*chip, 1 - c), me).wait_recv()
            for cp in first + passed:
                cp.wait_send()
            pltpu.make_async_copy(x_ref, slot(*me), local_sems.at[a]).wait()


class _Exchange:
    def __init__(self, gs):
        self.ins = list(gs)
        self.outs = [jax.ShapeDtypeStruct(g.shape, g.dtype) for g in gs]

    def _copies(self, a, g_ref, out_ref, send_sems, recv_sems, arrivals=True):
        x, y, c = lax.axis_index("x"), lax.axis_index("y"), lax.axis_index("c")
        me = 4 * x + 2 * y + c
        out, back = [], []
        for k in range(1, NDEV):
            px, py, pc = x ^ ((k >> 2) & 1), y ^ ((k >> 1) & 1), c ^ (k & 1)
            peer = 4 * px + 2 * py + pc
            sems = dict(send_sem=send_sems.at[7 * a + k - 1], recv_sem=recv_sems.at[7 * a + k - 1],
                        device_id=(px, py, pc), device_id_type=_MESH)
            out.append(pltpu.make_async_remote_copy(src_ref=g_ref.at[peer], dst_ref=out_ref.at[me], **sems))
            if arrivals:
                back.append(pltpu.make_async_remote_copy(src_ref=g_ref.at[me], dst_ref=out_ref.at[peer], **sems))
        return me, out, back

    def start(self, in_refs, out_refs, send_sems, recv_sems, local_sems):
        for a, (g_ref, out_ref) in enumerate(zip(in_refs, out_refs)):
            me, out, _ = self._copies(a, g_ref, out_ref, send_sems, recv_sems, arrivals=False)
            pltpu.make_async_copy(g_ref.at[me], out_ref.at[me], local_sems.at[a]).start()
            for cp in out:
                cp.start()

    def finish(self, in_refs, out_refs, send_sems, recv_sems, local_sems):
        for a, (g_ref, out_ref) in enumerate(zip(in_refs, out_refs)):
            me, out, back = self._copies(a, g_ref, out_ref, send_sems, recv_sems)
            for cp in back:
                cp.wait_recv()
            for cp in out:
                cp.wait_send()
            pltpu.make_async_copy(g_ref.at[me], out_ref.at[me], local_sems.at[a]).wait()


def _side_scratch(side):
    n = len(side.ins)
    return [pltpu.SemaphoreType.DMA((7 * n,)), pltpu.SemaphoreType.DMA((7 * n,)), pltpu.SemaphoreType.DMA((n,))]


def _run_side(side, *, name):
    n = len(side.ins)

    def body(*refs):
        ins, outs, sems = refs[:n], refs[n:2 * n], refs[2 * n:]
        side.start(ins, outs, *sems)
        side.finish(ins, outs, *sems)

    return pl.pallas_call(
        body, name=name, out_shape=side.outs, in_specs=[_ANY] * n, out_specs=[_ANY] * n,
        scratch_shapes=_side_scratch(side),
    )(*side.ins)


def _all_gather(x, *, name):
    return _run_side(_Gather([x]), name=name)[0]


def _sum_slots(parts, *, name):
    n, R, C = parts.shape
    tr, tc = _tile(R, 64, 16), _tile(C, 2048)

    def body(p_ref, o_ref):
        acc = p_ref[0].astype(F32)
        for k in range(1, n):
            acc = acc + p_ref[k].astype(F32)
        o_ref[...] = acc

    return pl.pallas_call(
        body, name=name, grid=(R // tr, C // tc),
        in_specs=[pl.BlockSpec((n, tr, tc), lambda i, j: (0, i, j))],
        out_specs=pl.BlockSpec((tr, tc), lambda i, j: (i, j)),
        out_shape=jax.ShapeDtypeStruct((R, C), F32),
        compiler_params=_params(("parallel", "parallel")),
    )(parts)


def _adamw(w, g, m, v, *, name):
    R, C = w.shape
    budget = 2 << 20
    tr = R if R * C * 4 <= budget else _tile(R, max(8, (budget // (C * 4)) // 8 * 8), 8)

    def body(w_ref, g_ref, m_ref, v_ref, d_ref, nm_ref, nv_ref):
        gv = g_ref[...]
        nm = ADAM_B1 * m_ref[...] + (1.0 - ADAM_B1) * gv
        nv = ADAM_B2 * v_ref[...] + (1.0 - ADAM_B2) * (gv * gv)
        m_hat = nm / (1.0 - ADAM_B1 ** ADAM_STEP)
        v_hat = nv / (1.0 - ADAM_B2 ** ADAM_STEP)
        d_ref[...] = -ADAM_LR * (m_hat / (jnp.sqrt(v_hat) + ADAM_EPS) + ADAM_WD * w_ref[...])
        nm_ref[...] = nm
        nv_ref[...] = nv

    spec = pl.BlockSpec((tr, C), lambda i: (i, 0))
    return pl.pallas_call(
        body, name=name, grid=(R // tr,), in_specs=[spec] * 4, out_specs=[spec] * 3,
        out_shape=[jax.ShapeDtypeStruct((R, C), F32)] * 3,
        compiler_params=_params(("parallel",)),
    )(w, g, m, v)


def _pad_cols(w, *, nq, ng):
    nf = w.shape[-1] - nq - ng
    pad = jnp.zeros(w.shape[:-1] + (FORGET_PAD - nf,), w.dtype)
    return jnp.concatenate([w[..., :nq], w[..., nq + nf:], w[..., nq:nq + nf], pad], axis=-1)


def _unpad_cols(w, *, nq, ng, nf):
    return jnp.concatenate([w[..., :nq], w[..., nq + ng:nq + ng + nf], w[..., nq:nq + ng]], axis=-1)


def kernel(x, w_in, b_forget, b_gate, w_up_a, w_up_b, w_out, ln_g, ln_b, loss_target, m_w_in, m_b_forget, m_b_gate, m_w_up_a, m_w_up_b, m_w_out, m_ln_g, m_ln_b, v_w_in, v_b_forget, v_b_gate, v_w_up_a, v_w_up_b, v_w_out, v_ln_g, v_ln_b):
    depth = w_in.shape[0]
    _, S, D = x.shape
    WA, WB = w_up_a.shape[1], w_up_b.shape[1]
    nha, nhb = WA // HEAD_DIM, WB // HEAD_DIM
    assert nhb == b_forget.shape[1] and WA == WB and nhb <= HEAD_ROWS
    nq, ng, nf = 4 * WA + 4 * WB, 2 * D, nhb
    assert w_in.shape[2] == nq + nf + ng
    ncp = nq + ng + FORGET_PAD
    off_va, off_za, off_b, off_zb, off_g, off_f = 2 * WA, 3 * WA, 4 * WA, 4 * WA + 3 * WB, nq, nq + ng
    alpha = float((2 * depth) ** 0.25)
    dsh = D // NDEV
    tq_f, tq_b = _tile(S, FOX_TILE_FWD), _tile(S, FOX_TILE_BWD)

    x0 = x[0]
    target = loss_target[0]
    cos, sin = _rope_tables(S)

    w_in_p = _pad_cols(w_in, nq=nq, ng=ng).astype(BF16)
    w_up_a_t = jnp.swapaxes(w_up_a, 1, 2).astype(BF16)
    w_up_b_t = jnp.swapaxes(w_up_b, 1, 2).astype(BF16)
    w_out_c = w_out.astype(BF16)
    W_in, W_ua, W_ub, W_out = ([None] * depth for _ in range(4))

    def gather_of(l):
        return _Gather([w_in_p[l], w_up_a_t[l], w_up_b_t[l], w_out_c[l]])

    def keep_weights(l, got):
        W_in[l], W_ua[l] = got[0].reshape(D, ncp), got[1].reshape(D, WA)
        W_ub[l], W_out[l] = got[2].reshape(D, WB), got[3].reshape(D, D)

    keep_weights(0, _run_side(gather_of(0), name="gather_w_0"))

    bf_rows = [jnp.broadcast_to(jnp.pad(b_forget[l], (0, HEAD_ROWS - nhb))[:, None], (HEAD_ROWS, LANES))
               for l in range(depth)]

    saved = []
    xf, xb = x0, x0.astype(BF16)
    for l in range(depth):
        if l + 1 < depth:
            h, got = _mm(xb, W_in[l], name="in_proj_gather", tm=1024, tn=1280, tk=2048, side=gather_of(l + 1))
            keep_weights(l + 1, got)
        else:
            h = _mm(xb, W_in[l], name="in_proj", tm=1024, tn=1280, tk=2048)
        qk_a = _prep_a(h, cos, sin, W=WA, S=S)
        qkv_b = _prep_b(h, off=off_b, W=WB, S=S)
        out_a, lse_a, ga = _band_fwd(qk_a, h, nha=nha, S=S, v_off=off_va, z_off=off_za)
        f3 = _lane_blocks(jnp.pad(h[:, off_f:off_f + nhb].T, ((0, HEAD_ROWS - nhb), (0, 0))))
        c = _lane_unblocks(_forget_cumsum(f3, bf_rows[l]))[:nhb]
        out_b, lq_b = _fox_fwd(qkv_b, c.reshape(nhb, S // tq_f, 1, tq_f), nhb=nhb, S=S, tq=tq_f)
        ck = c.reshape(nhb, S // tq_b, 1, tq_b)
        gb = _gate(out_b, h, z_off=off_zb, W=WB, S=S)
        up_a = _mm(ga, W_ua[l], tb=True, name="up_a", tm=1024, tn=1024, tk=1536)
        up_b = _mm(gb, W_ub[l], tb=True, name="up_b", tm=1024, tn=1024, tk=1536)
        bg = b_gate[l][None, :]
        mix = _merge(up_a, up_b, h, bg, g_off=off_g, D=D, S=S)
        y = _mm(mix, W_out[l], name="out_proj", tm=1024, tn=1024, tk=2048)
        xn_f, xn_b, z = _deepnorm(xf, y, ln_g[l][None, :], ln_b[l][None, :], alpha=alpha, D=D, S=S)
        saved.append(dict(xb=xb, h=h, qk_a=qk_a, qkv_b=qkv_b, out_a=out_a, lse_a=lse_a, ga=ga, f3=f3, ck=ck,
                          out_b=out_b, lq_b=lq_b, gb=gb, up_a=up_a, up_b=up_b, mix=mix, z=z, bg=bg))
        xf, xb = xn_f, xn_b

    dx, loss_row = _loss_head(xf, target, D=D, S=S)

    recv = [None] * depth
    small = [None] * depth
    for l in reversed(range(depth)):
        sv = saved[l]
        h = sv["h"]
        dz, dzb, d_lng, d_lnb = _deepnorm_bwd(dx, sv["z"], ln_g[l][None, :], D=D, S=S)
        g_w_out = _mm(sv["mix"], dzb, ta=True, name="g_w_out", tm=512, tn=1024, tk=1024, out_dtype=BF16)
        dmix = _mm(dzb, W_out[l], tb=True, name="d_mix", tm=1024, tn=1024, tk=2048)
        dup_a, dup_b, dgl_a, dgl_b, d_bga, d_bgb = _merge_bwd(dmix, sv["up_a"], sv["up_b"], h, sv["bg"],
                                                              g_off=off_g, D=D, S=S)
        g_w_ua = _mm(dup_a, sv["ga"], ta=True, name="g_w_up_a", tm=512, tn=1536, tk=1024, out_dtype=BF16)
        g_w_ub = _mm(dup_b, sv["gb"], ta=True, name="g_w_up_b", tm=512, tn=1536, tk=1024, out_dtype=BF16)
        dga = _mm(dup_a, W_ua[l], name="d_ga", tm=1024, tn=1536, tk=2048)
        dgb = _mm(dup_b, W_ub[l], name="d_gb", tm=1024, tn=1536, tk=2048)
        do_a, dza, stats_a = _gate_bwd(dga, sv["out_a"], h, sv["lse_a"], z_off=off_za, W=WA, S=S,
                                       name="gate_bwd_a", do_dtype=F32)
        do_b, dzb_, stats_b = _gate_bwd(dgb, sv["out_b"], h, sv["lq_b"], z_off=off_zb, W=WB, S=S,
                                        name="gate_bwd_b", do_dtype=BF16)
        dqa, dka, dva = _rope_bwd(*_band_bwd(sv["qk_a"], h, do_a, stats_a, nha=nha, S=S, v_off=off_va),
                                  cos, sin, W=WA, S=S)
        dqb, dkb, dvb, dcq, dck = _fox_bwd(sv["qkv_b"], do_b, stats_b, sv["ck"], nhb=nhb, S=S, tq=tq_b)
        head_rows = lambda t: _lane_blocks(jnp.pad(t, ((0, HEAD_ROWS - nhb), (0, 0))))
        df3, d_bf = _forget_bwd(head_rows(dcq[:, ::LANES].T), head_rows(dck.reshape(nhb, S)), sv["f3"], bf_rows[l])
        dfl = jnp.pad(_lane_unblocks(df3)[:nhb].T, ((0, 0), (0, FORGET_PAD - nhb))).astype(BF16)
        dh = jnp.concatenate([dqa, dka, dva, dza, dqb.astype(BF16), dkb, dvb, dzb_, dgl_a, dgl_b, dfl], axis=1)
        chunks = lambda t: t.reshape(NDEV, dsh, t.shape[1])
        g_w_in, got3 = _mm(sv["xb"], dh, ta=True, name="g_w_in_xchg", tm=512, tn=1280, tk=4096, out_dtype=BF16,
                           side=_Exchange([chunks(g_w_ua), chunks(g_w_ub), chunks(g_w_out)]))
        dx, got1 = _mm(dh, W_in[l], tb=True, name="d_x_xchg", tm=1024, tn=512, tk=3328, add=dz, add_scale=alpha,
                       side=_Exchange([chunks(g_w_in)]))
        recv[l] = (got1[0], *got3)
        small[l] = jnp.concatenate([d_bf[:nhb, 0], jnp.zeros((LANES - nhb,), F32), d_bga[0], d_bgb[0],
                                    d_lng[0], d_lnb[0]])

    n_small = LANES + 2 * D + 2 * D
    flat = jnp.concatenate(small + [loss_row[0]])
    rows = -(-flat.shape[0] // (8 * LANES)) * 8
    flat = jnp.pad(flat, (0, rows * LANES - flat.shape[0])).reshape(rows, LANES)
    tot = _sum_slots(_all_gather(flat, name="gather_small"), name="sum_small").reshape(-1)
    loss = tot[depth * n_small]
    sm = tot[:depth * n_small].reshape(depth, n_small)
    g_bf, g_bg = sm[:, :nhb], sm[:, LANES:LANES + 2 * D]
    g_lg, g_lb = sm[:, LANES + 2 * D:LANES + 3 * D], sm[:, LANES + 3 * D:]

    g_in = jnp.stack([_unpad_cols(_sum_slots(recv[l][0], name="sum_w_in"), nq=nq, ng=ng, nf=nf) for l in range(depth)])
    g_ua = jnp.stack([_sum_slots(recv[l][1], name="sum_w_up_a").T for l in range(depth)])
    g_ub = jnp.stack([_sum_slots(recv[l][2], name="sum_w_up_b").T for l in range(depth)])
    g_out = jnp.stack([_sum_slots(recv[l][3], name="sum_w_out") for l in range(depth)])

    def adam(w, g, m, v, name):
        shp = w.shape
        d_, m_, v_ = _adamw(*[t.reshape(-1, shp[-1]) for t in (w, g, m, v)], name=name)
        return d_.reshape(shp), m_.reshape(shp), v_.reshape(shp)

    grads = [g_in, g_bf, g_bg, g_ua, g_ub, g_out, g_lg, g_lb]
    ws = [w_in, b_forget, b_gate, w_up_a, w_up_b, w_out, ln_g, ln_b]
    ms = [m_w_in, m_b_forget, m_b_gate, m_w_up_a, m_w_up_b, m_w_out, m_ln_g, m_ln_b]
    vs = [v_w_in, v_b_forget, v_b_gate, v_w_up_a, v_w_up_b, v_w_out, v_ln_g, v_ln_b]
    names = ["w_in", "b_forget", "b_gate", "w_up_a", "w_up_b", "w_out", "ln_g", "ln_b"]
    upd = [adam(w, g, m, v, f"adamw_{n}") for w, g, m, v, n in zip(ws, grads, ms, vs, names)]
    return (loss, dx[None], *grads, *[u[0] for u in upd], *[u[1] for u in upd], *[u[2] for u in upd])
```

```python
import functools

import jax
import jax.numpy as jnp
from jax import lax
from jax.experimental import pallas as pl
from jax.experimental.pallas import tpu as pltpu

F32 = jnp.float32
BF16 = jnp.bfloat16

NDEV = 8
HEAD_DIM = 128
LANES = 128
Q_BLOCK = 128
DILATIONS = (1, 4, 16)
ROPE_THETA = 10000.0
LN_EPS = 1e-5
ATT_SCALE = HEAD_DIM ** -0.5
NEG = -1e30
FORGET_PAD = 256
BAND_CHUNK = 2048
BAND_UNROLL = 8
FOX_TILE_FWD = 1024
FOX_TILE_BWD = 1024
FOX_SUB = 2
HEAD_ROWS = 16
ADAM_LR, ADAM_B1, ADAM_B2, ADAM_EPS, ADAM_WD, ADAM_STEP = 0.001, 0.9, 0.999, 1e-08, 0.01, 10
VMEM_LIMIT = 56 * 1024 * 1024


def _tile(n, pref, mult=LANES):
    if n <= pref:
        return n
    t = (pref // mult) * mult
    while t >= mult:
        if n % t == 0:
            return t
        t -= mult
    return n


def _params(sem=None):
    return pltpu.CompilerParams(dimension_semantics=sem, vmem_limit_bytes=VMEM_LIMIT)


def _sigmoid(z):
    return 1.0 / (1.0 + jnp.exp(-z))


_NT = (((1,), (1,)), ((), ()))
_NN = (((1,), (0,)), ((), ()))
_TN = (((0,), (0,)), ((), ()))


def _dot(a, b, dims):
    return lax.dot_general(a, b, dims, preferred_element_type=F32)


def _mm(a, b, *, name, ta=False, tb=False, out_dtype=F32, tm=512, tn=512, tk=512, add=None, add_scale=1.0,
        side=None):
    if ta:
        K, M = a.shape
    else:
        M, K = a.shape
    if tb:
        N, K2 = b.shape
    else:
        K2, N = b.shape
    assert K == K2, (a.shape, b.shape, ta, tb)
    tm, tn, tk = _tile(M, tm), _tile(N, tn), _tile(K, tk)
    ni, nj, nk = M // tm, N // tn, K // tk
    dims = (((0 if ta else 1,), (1 if tb else 0,)), ((), ()))
    n_side = 0 if side is None else len(side.ins)
    n_in = 2 + (add is not None)

    def body(*refs):
        a_ref, b_ref = refs[:2]
        add_ref = refs[2] if add is not None else None
        side_in = refs[n_in:n_in + n_side]
        o_ref = refs[n_in + n_side]
        side_out = refs[n_in + n_side + 1:n_in + 2 * n_side + 1]
        acc_ref = refs[n_in + 2 * n_side + 1]
        sems = refs[n_in + 2 * n_side + 2:]
        k = pl.program_id(2)
        if side is not None:
            @pl.when((pl.program_id(0) == 0) & (pl.program_id(1) == 0) & (k == 0))
            def _():
                side.start(side_in, side_out, *sems)

        part = _dot(a_ref[...].astype(BF16), b_ref[...].astype(BF16), dims)

        @pl.when(k == 0)
        def _():
            acc_ref[...] = part

        @pl.when(k > 0)
        def _():
            acc_ref[...] += part

        @pl.when(k == nk - 1)
        def _():
            r = acc_ref[...]
            if add_ref is not None:
                r = r + add_scale * add_ref[...]
            o_ref[...] = r.astype(out_dtype)

        if side is not None:
            step = (pl.program_id(0) * nj + pl.program_id(1)) * nk + k
            steps = ni * nj * nk

            @pl.when(step == max((3 * steps) // 4 - 1, 0))
            def _():
                side.middle(side_in, side_out, *sems)

            @pl.when(step == steps - 1)
            def _():
                side.finish(side_in, side_out, *sems)

    a_spec = pl.BlockSpec((tk, tm), lambda i, j, k: (k, i)) if ta else pl.BlockSpec((tm, tk), lambda i, j, k: (i, k))
    b_spec = pl.BlockSpec((tn, tk), lambda i, j, k: (j, k)) if tb else pl.BlockSpec((tk, tn), lambda i, j, k: (k, j))
    o_spec = pl.BlockSpec((tm, tn), lambda i, j, k: (i, j))
    in_specs, args = [a_spec, b_spec], [a, b]
    if add is not None:
        in_specs.append(o_spec)
        args.append(add)
    out_shape = jax.ShapeDtypeStruct((M, N), out_dtype)
    scratch = [pltpu.VMEM((tm, tn), F32)]
    if side is None:
        return pl.pallas_call(
            body, name=name, grid=(ni, nj, nk), in_specs=in_specs, out_specs=o_spec, out_shape=out_shape,
            scratch_shapes=scratch, compiler_params=_params(("parallel", "parallel", "arbitrary")),
        )(*args)
    res = pl.pallas_call(
        body, name=name, grid=(ni, nj, nk), in_specs=in_specs + [_ANY] * n_side,
        out_specs=[o_spec] + [_ANY] * n_side, out_shape=[out_shape] + side.outs,
        scratch_shapes=scratch + _side_scratch(side),
        compiler_params=_params(("arbitrary", "arbitrary", "arbitrary")),
    )(*args, *side.ins)
    return res[0], res[1:]


def _ew(fn, ins, outs, *, rows, cols, tr, tc, name):
    assert rows % tr == 0 and cols % tc == 0

    def spec(kind, off):
        assert off % tc == 0, (off, tc)
        ob = off // tc
        if kind == "tile":
            return pl.BlockSpec((tr, tc), lambda j, i: (i, j + ob))
        if kind == "tab":
            return pl.BlockSpec((tr, LANES), lambda j, i: (i, 0))
        return pl.BlockSpec((1, tc), lambda j, i: (0, j + ob))

    n_in = len(ins)

    def body(*refs):
        fn(pl.program_id(1), refs[:n_in], refs[n_in:])

    return pl.pallas_call(
        body, name=name, grid=(cols // tc, rows // tr),
        in_specs=[spec(k, o) for _, k, o in ins],
        out_specs=[spec(k, o) for _, _, k, o in outs],
        out_shape=[jax.ShapeDtypeStruct(s, d) for s, d, _, _ in outs],
        compiler_params=_params(("parallel", "arbitrary")),
    )(*[a for a, _, _ in ins])


def _heads(t):
    return [t[:, e:e + LANES] for e in range(0, t.shape[1], LANES)]


def _rope_tables(S):
    half = HEAD_DIM // 2
    inv_freq = ROPE_THETA ** (-jnp.arange(half, dtype=F32) / half)
    ang = jnp.arange(S, dtype=jnp.int32).astype(F32)[:, None] * inv_freq[None, :]
    cos, sin = jnp.cos(ang), jnp.sin(ang)
    return jnp.concatenate([cos, cos], axis=1), jnp.concatenate([-sin, sin], axis=1)


def _rope(t, cs, sn):
    return jnp.concatenate([g * cs + pltpu.roll(g, HEAD_DIM // 2, 1) * sn for g in _heads(t)], axis=1)


def _rope_t(t, cs, sn):
    return jnp.concatenate([g * cs + pltpu.roll(g * sn, HEAD_DIM // 2, 1) for g in _heads(t)], axis=1)


def _prep_a(h, cos, sin, *, W, S):
    tr, tc = _tile(S, 512, 8), _tile(W, 512)

    def fn(i, ins, outs):
        outs[0][...] = _rope(ins[0][...], ins[1][...], ins[2][...])

    return _ew(fn, [(h, "tile", 0), (cos, "tab", 0), (sin, "tab", 0)], [((S, 2 * W), F32, "tile", 0)],
               rows=S, cols=2 * W, tr=tr, tc=tc, name="prep_a")[0]


def _prep_b(h, *, off, W, S):
    tr, tc = _tile(S, 512, 8), _tile(W, 512)

    def fn(i, ins, outs):
        scale = jnp.where(pl.program_id(0) * tc < W, ATT_SCALE, 1.0).astype(F32)
        outs[0][...] = (ins[0][...] * scale).astype(BF16)

    return _ew(fn, [(h, "tile", off)], [((S, 3 * W), BF16, "tile", 0)], rows=S, cols=3 * W, tr=tr, tc=tc,
               name="prep_b")[0]


def _rope_bwd(dq, dk, dv, cos, sin, *, W, S):
    tr, tc = _tile(S, 512, 8), _tile(W, 512)

    def fn(i, ins, outs):
        cs, sn = ins[3][...], ins[4][...]
        outs[0][...] = _rope_t(ins[0][...], cs, sn).astype(BF16)
        outs[1][...] = _rope_t(ins[1][...], cs, sn).astype(BF16)
        outs[2][...] = ins[2][...].astype(BF16)

    ins = [(dq, "tile", 0), (dk, "tile", 0), (dv, "tile", 0), (cos, "tab", 0), (sin, "tab", 0)]
    return _ew(fn, ins, [((S, W), BF16, "tile", 0)] * 3, rows=S, cols=W, tr=tr, tc=tc, name="rope_bwd")


def _band_biases():
    dist = (lax.broadcasted_iota(jnp.int32, (Q_BLOCK, 2 * Q_BLOCK), 0)
            - lax.broadcasted_iota(jnp.int32, (Q_BLOCK, 2 * Q_BLOCK), 1))
    first = jnp.where(dist >= 0, 0.0, NEG).astype(F32)
    other = jnp.where((dist + Q_BLOCK >= 0) & (dist <= 0), 0.0, NEG).astype(F32)
    return first, other


def _band_rows(it, c, d, nbc, biases):
    r, n = (0, it) if d == 1 else (it // nbc, it % nbc)
    ng = c * nbc + n
    k0 = pl.multiple_of(jnp.maximum(ng - 1, 0) * Q_BLOCK, Q_BLOCK)
    bias = jnp.where(ng == 0, biases[0], biases[1])
    if d == 1:
        return pl.ds(pl.multiple_of(it * Q_BLOCK, Q_BLOCK), Q_BLOCK), pl.ds(k0, 2 * Q_BLOCK), bias
    return (pl.ds(r + d * Q_BLOCK * n, Q_BLOCK, stride=d), pl.ds(r + d * k0, 2 * Q_BLOCK, stride=d), bias)


def _band_chunk(S):
    ch = min(S, BAND_CHUNK)
    assert S % ch == 0 and ch % (Q_BLOCK * max(DILATIONS)) == 0 and S >= 2 * Q_BLOCK * max(DILATIONS)
    return ch


def _band_fwd(qk, h, *, nha, S, v_off, z_off):
    CH = _band_chunk(S)
    W = nha * LANES
    slab = 256

    def body(q_ref, k_ref, v_ref, z_ref, o_ref, lse_ref, g_ref, *scratch):
        os_refs, ls_refs = scratch[:len(DILATIONS)], scratch[len(DILATIONS):]
        c = pl.program_id(1)
        biases = _band_biases()
        for gi, d in enumerate(DILATIONS):
            nbc = CH // (Q_BLOCK * d)

            def step(it, carry, gi=gi, d=d, nbc=nbc):
                rq, rk, bias = _band_rows(it, c, d, nbc, biases)
                q = q_ref[rq, :].astype(BF16)
                k = k_ref[rk, :].astype(BF16)
                v = v_ref[rk, :].astype(BF16)
                s = _dot(q, k, _NT) * ATT_SCALE + bias
                m = jnp.max(s, axis=1, keepdims=True)
                p = jnp.exp(s - m)
                den = jnp.sum(p, axis=1, keepdims=True)
                os_refs[gi][rq, :] = _dot(p.astype(BF16), v, _NN) / den
                ls_refs[gi][rq, :] = jnp.broadcast_to(m + jnp.log(den), (Q_BLOCK, LANES))
                return carry

            lax.fori_loop(0, CH // Q_BLOCK, step, 0, unroll=BAND_UNROLL)

        def mix(t, carry):
            rows = pl.ds(pl.multiple_of(t * slab, slab), slab)
            a, b, cc = (r[rows, :] for r in ls_refs)
            m = jnp.maximum(jnp.maximum(a, b), cc)
            ea, eb, ec = jnp.exp(a - m), jnp.exp(b - m), jnp.exp(cc - m)
            den = ea + eb + ec
            out = (ea * os_refs[0][rows, :] + eb * os_refs[1][rows, :] + ec * os_refs[2][rows, :]) / den
            z = z_ref[rows, :]
            o_ref[rows, :] = out
            lse_ref[rows, :] = m + jnp.log(den)
            g_ref[rows, :] = (out * (z * _sigmoid(z))).astype(BF16)
            return carry

        lax.fori_loop(0, CH // slab, mix, 0)

    chunk = pl.BlockSpec((CH, LANES), lambda hd, c: (c, hd))
    return pl.pallas_call(
        body, name="band_fwd", grid=(nha, S // CH),
        in_specs=[chunk,
                  pl.BlockSpec((S, LANES), lambda hd, c: (0, nha + hd)),
                  pl.BlockSpec((S, LANES), lambda hd, c: (0, v_off // LANES + hd)),
                  pl.BlockSpec((CH, LANES), lambda hd, c: (c, z_off // LANES + hd))],
        out_specs=[chunk, chunk, chunk],
        out_shape=[jax.ShapeDtypeStruct((S, W), F32), jax.ShapeDtypeStruct((S, W), F32),
                   jax.ShapeDtypeStruct((S, W), BF16)],
        scratch_shapes=[pltpu.VMEM((CH, LANES), F32)] * (2 * len(DILATIONS)),
        compiler_params=_params(("parallel", "arbitrary")),
    )(qk, qk, h, h)


def _band_bwd(qk, h, do, stats, *, nha, S, v_off):
    CH = _band_chunk(S)
    W = nha * LANES

    def body(q_ref, k_ref, v_ref, do_ref, st_ref, dq_ref, dk_ref, dv_ref):
        c = pl.program_id(1)

        @pl.when(c == 0)
        def _():
            dk_ref[...] = jnp.zeros_like(dk_ref)
            dv_ref[...] = jnp.zeros_like(dv_ref)

        biases = _band_biases()
        for gi, d in enumerate(DILATIONS):
            nbc = CH // (Q_BLOCK * d)

            def step(it, carry, gi=gi, d=d, nbc=nbc):
                rq, rk, bias = _band_rows(it, c, d, nbc, biases)
                q = q_ref[rq, :].astype(BF16)
                k = k_ref[rk, :].astype(BF16)
                v = v_ref[rk, :].astype(BF16)
                g = do_ref[rq, :].astype(BF16)
                st = st_ref[rq, :]
                p = jnp.exp(_dot(q, k, _NT) * ATT_SCALE + (bias - st[:, 0:1]))
                ds = (p * (_dot(g, v, _NT) - st[:, 1:2]) * ATT_SCALE).astype(BF16)
                dq = _dot(ds, k, _NN)
                if gi == 0:
                    dq_ref[rq, :] = dq
                else:
                    dq_ref[rq, :] += dq
                dk_ref[rk, :] += _dot(ds, q, _TN)
                dv_ref[rk, :] += _dot(p.astype(BF16), g, _TN)
                return carry

            lax.fori_loop(0, CH // Q_BLOCK, step, 0, unroll=BAND_UNROLL // 2)

    chunk = pl.BlockSpec((CH, LANES), lambda hd, c: (c, hd))
    whole = pl.BlockSpec((S, LANES), lambda hd, c: (0, hd))
    return pl.pallas_call(
        body, name="band_bwd", grid=(nha, S // CH),
        in_specs=[chunk,
                  pl.BlockSpec((S, LANES), lambda hd, c: (0, nha + hd)),
                  pl.BlockSpec((S, LANES), lambda hd, c: (0, v_off // LANES + hd)),
                  chunk, chunk],
        out_specs=[chunk, whole, whole],
        out_shape=[jax.ShapeDtypeStruct((S, W), F32)] * 3,
        compiler_params=_params(("parallel", "arbitrary")),
    )(qk, qk, h, do, stats)


def _gate_bwd(dg, o, h, lane0, *, z_off, W, S, name, do_dtype):
    tr, tc = _tile(S, 512, 8), _tile(W, 512)

    def fn(i, ins, outs):
        g, out, z = ins[0][...], ins[1][...], ins[2][...]
        sg = _sigmoid(z)
        do = g * (z * sg)
        outs[0][...] = do.astype(do_dtype)
        outs[1][...] = (g * out * (sg * (1.0 + z * (1.0 - sg)))).astype(BF16)
        delta = jnp.concatenate([jnp.broadcast_to(jnp.sum(t, axis=1, keepdims=True), t.shape)
                                 for t in _heads(do * out)], axis=1)
        first = lax.broadcasted_iota(jnp.int32, delta.shape, 1) % LANES == 0
        outs[2][...] = jnp.where(first, ins[3][...], delta)

    ins = [(dg, "tile", 0), (o, "tile", 0), (h, "tile", z_off), (lane0, "tile", 0)]
    outs = [((S, W), do_dtype, "tile", 0), ((S, W), BF16, "tile", 0), ((S, W), F32, "tile", 0)]
    return _ew(fn, ins, outs, rows=S, cols=W, tr=tr, tc=tc, name=name)


def _split3(x):
    hi = x.astype(BF16)
    r = x - hi.astype(F32)
    mid = r.astype(BF16)
    lo = (r - mid.astype(F32)).astype(BF16)
    return hi, mid, lo


def _tri_dot(x, tri):
    hi, mid, lo = _split3(x)
    return _dot(hi, tri, _NN) + (_dot(mid, tri, _NN) + _dot(lo, tri, _NN))


def _log1p(u):
    w = 1.0 + u
    return jnp.where(w == 1.0, u, jnp.log(w) * (u / jnp.where(w == 1.0, 1.0, w - 1.0)))


def _lane_blocks(t):
    R, S = t.shape
    return t.reshape(R, S // LANES, LANES).transpose(1, 0, 2)


def _lane_unblocks(t):
    nb, R, _ = t.shape
    return t.transpose(1, 0, 2).reshape(R, nb * LANES)


def _forget_cumsum(f3, bf):
    nb, R, _ = f3.shape

    def body(f_ref, b_ref, c_ref):
        row = lax.broadcasted_iota(jnp.int32, (LANES, LANES), 0)
        colm = lax.broadcasted_iota(jnp.int32, (LANES, LANES), 1)
        tri = (row <= colm).astype(BF16)
        bias = b_ref[...]

        def step(n, carry):
            z = f_ref[n] + bias
            logf = jnp.minimum(z, 0.0) - _log1p(jnp.exp(-jnp.abs(z)))
            c = _tri_dot(logf, tri) + carry
            c_ref[n] = c
            return jnp.broadcast_to(c[:, LANES - 1:LANES], (R, LANES))

        lax.fori_loop(0, nb, step, jnp.zeros((R, LANES), F32))

    return pl.pallas_call(
        body, name="forget_cumsum", out_shape=jax.ShapeDtypeStruct(f3.shape, F32),
        compiler_params=_params(),
    )(f3, bf)


def _forget_bwd(dcq3, dck3, f3, bf):
    nb, R, _ = f3.shape

    def body(dcq_ref, dck_ref, f_ref, b_ref, df_ref, db_ref):
        row = lax.broadcasted_iota(jnp.int32, (LANES, LANES), 0)
        colm = lax.broadcasted_iota(jnp.int32, (LANES, LANES), 1)
        tri = (row >= colm).astype(BF16)
        bias = b_ref[...]

        def step(t, carry):
            tail, tot = carry
            n = nb - 1 - t
            r = _tri_dot(dcq_ref[n] + dck_ref[n], tri) + tail
            df = r * _sigmoid(-(f_ref[n] + bias))
            df_ref[n] = df
            tot = tot + jnp.broadcast_to(jnp.sum(df, axis=1, keepdims=True), (R, LANES))
            return jnp.broadcast_to(r[:, 0:1], (R, LANES)), tot

        _, tot = lax.fori_loop(0, nb, step, (jnp.zeros((R, LANES), F32), jnp.zeros((R, LANES), F32)))
        db_ref[...] = tot

    return pl.pallas_call(
        body, name="forget_bwd",
        out_shape=[jax.ShapeDtypeStruct(f3.shape, F32), jax.ShapeDtypeStruct((R, LANES), F32)],
        compiler_params=_params(),
    )(dcq3, dck3, f3, bf)


def _causal(t):
    return lax.broadcasted_iota(jnp.int32, (t, t), 0) >= lax.broadcasted_iota(jnp.int32, (t, t), 1)


def _fox_fwd(qkv, ck, h, *, z_off, nhb, S, tq):
    nq = S // tq
    ts = tq // FOX_SUB

    def body(q_ref, k_ref, v_ref, ck_ref, z_ref, o_ref, lq_ref, g_ref):
        i = pl.program_id(1)
        qs = [q_ref[r * ts:(r + 1) * ts, :] for r in range(FOX_SUB)]

        def step(j, carry, masked):
            off = pl.multiple_of(j * tq, tq)
            k = k_ref[pl.ds(off, tq), :]
            v = v_ref[pl.ds(off, tq), :]
            ckj = ck_ref[j]
            new = []
            for r, (m, l, acc) in enumerate(carry):
                nk = (r + 1) * ts if masked else tq
                s = _dot(qs[r], k[:nk], _NT) - ckj[:, :nk]
                if masked:
                    rows = r * ts + lax.broadcasted_iota(jnp.int32, (ts, nk), 0)
                    s = jnp.where(rows >= lax.broadcasted_iota(jnp.int32, (ts, nk), 1), s, NEG)
                m_new = jnp.maximum(m, jnp.max(s, axis=1, keepdims=True))
                alpha = jnp.exp(m - m_new)
                p = jnp.exp(s - m_new)
                l = alpha * l + jnp.sum(p, axis=1, keepdims=True)
                acc = alpha * acc + _dot(p.astype(BF16), v[:nk], _NN)
                new.append((m_new, l, acc))
            return tuple(new)

        init = tuple((jnp.full((ts, 1), NEG, F32), jnp.zeros((ts, 1), F32), jnp.zeros((ts, LANES), F32))
                     for _ in range(FOX_SUB))
        carry = lax.fori_loop(0, i, functools.partial(step, masked=False), init)
        for r, (m, l, acc) in enumerate(step(i, carry, True)):
            rows = slice(r * ts, (r + 1) * ts)
            out = acc / l
            z = z_ref[rows, :]
            o_ref[rows, :] = out
            lq_ref[rows, :] = jnp.broadcast_to(m + jnp.log(l), (ts, LANES))
            g_ref[rows, :] = (out * (z * _sigmoid(z))).astype(BF16)

    blk = pl.BlockSpec((tq, LANES), lambda hd, i: (i, hd))
    return pl.pallas_call(
        body, name="fox_fwd", grid=(nhb, nq),
        in_specs=[blk,
                  pl.BlockSpec((S, LANES), lambda hd, i: (0, nhb + hd)),
                  pl.BlockSpec((S, LANES), lambda hd, i: (0, 2 * nhb + hd)),
                  pl.BlockSpec((None, nq, 1, tq), lambda hd, i: (hd, 0, 0, 0)),
                  pl.BlockSpec((tq, LANES), lambda hd, i: (i, z_off // LANES + hd))],
        out_specs=[blk, blk, blk],
        out_shape=[jax.ShapeDtypeStruct((S, nhb * LANES), F32), jax.ShapeDtypeStruct((S, nhb * LANES), F32),
                   jax.ShapeDtypeStruct((S, nhb * LANES), BF16)],
        compiler_params=_params(("parallel", "arbitrary")),
    )(qkv, qkv, qkv, ck, h)


def _fox_bwd(qkv, do, stats, ck, *, nhb, S, tq, side=None):
    nq = S // tq
    ts = tq // FOX_SUB
    n_side = 0 if side is None else len(side.ins)

    def body(*refs):
        q_ref, k_ref, v_ref, do_ref, st_ref, ck_ref = refs[:6]
        side_in = refs[6:6 + n_side]
        dq_ref, dk_ref, dv_ref, dr_ref, dc_ref = refs[6 + n_side:11 + n_side]
        side_out, sems = refs[11 + n_side:11 + 2 * n_side], refs[11 + 2 * n_side:]
        j = pl.program_id(1)
        if side is not None:
            @pl.when((pl.program_id(0) == 0) & (j == 0))
            def _():
                side.start(side_in, side_out, *sems)

        @pl.when(j == 0)
        def _():
            dq_ref[...] = jnp.zeros_like(dq_ref)
            dr_ref[...] = jnp.zeros_like(dr_ref)

        k = k_ref[...]
        v = v_ref[...]
        ckv = ck_ref[...]

        def step(i, carry, masked):
            dk, dv, dc = carry
            for r in range(FOX_SUB):
                rows = pl.ds(pl.multiple_of(i * tq + r * ts, ts), ts)
                q = q_ref[rows, :]
                g = do_ref[rows, :]
                st = st_ref[rows, :]
                p = jnp.exp(_dot(q, k, _NT) - ckv - st[:, 0:1])
                if masked:
                    rr = r * ts + lax.broadcasted_iota(jnp.int32, (ts, tq), 0)
                    p = jnp.where(rr >= lax.broadcasted_iota(jnp.int32, (ts, tq), 1), p, 0.0)
                ds = p * (_dot(g, v, _NT) - st[:, 1:2])
                dc = dc - jnp.sum(ds, axis=0, keepdims=True)
                dr_ref[rows, :] += jnp.broadcast_to(jnp.sum(ds, axis=1, keepdims=True), (ts, LANES))
                dsb = ds.astype(BF16)
                dq_ref[rows, :] += _dot(dsb, k, _NN)
                dk = dk + _dot(dsb, q, _TN)
                dv = dv + _dot(p.astype(BF16), g, _TN)
            return dk, dv, dc

        init = (jnp.zeros((tq, LANES), F32), jnp.zeros((tq, LANES), F32), jnp.zeros((1, tq), F32))
        carry = step(j, init, True)
        dk, dv, dc = lax.fori_loop(j + 1, nq, functools.partial(step, masked=False), carry)
        dk_ref[...] = dk.astype(BF16)
        dv_ref[...] = dv.astype(BF16)
        dc_ref[...] = dc

        @pl.when(j == nq - 1)
        def _():
            dq_ref[...] = dq_ref[...] * ATT_SCALE

        if side is not None:
            @pl.when((pl.program_id(0) == nhb - 1) & (j == nq - 1))
            def _():
                side.middle(side_in, side_out, *sems)
                side.finish(side_in, side_out, *sems)

    def whole(off):
        return pl.BlockSpec((S, LANES), lambda h, j: (0, off + h))

    kv_blk = pl.BlockSpec((tq, LANES), lambda h, j: (j, h))
    c_blk = pl.BlockSpec((None, None, 1, tq), lambda h, j: (h, j, 0, 0))
    in_specs = [whole(0),
                pl.BlockSpec((tq, LANES), lambda h, j: (j, nhb + h)),
                pl.BlockSpec((tq, LANES), lambda h, j: (j, 2 * nhb + h)),
                whole(0), whole(0), c_blk]
    out_specs = [whole(0), kv_blk, kv_blk, whole(0), c_blk]
    out_shape = [jax.ShapeDtypeStruct((S, nhb * LANES), F32),
                 jax.ShapeDtypeStruct((S, nhb * LANES), BF16),
                 jax.ShapeDtypeStruct((S, nhb * LANES), BF16),
                 jax.ShapeDtypeStruct((S, nhb * LANES), F32),
                 jax.ShapeDtypeStruct((nhb, nq, 1, tq), F32)]
    args = (qkv, qkv, qkv, do, stats, ck)
    if side is None:
        return pl.pallas_call(
            body, name="fox_bwd", grid=(nhb, nq), in_specs=in_specs, out_specs=out_specs, out_shape=out_shape,
            compiler_params=_params(("parallel", "arbitrary")),
        )(*args)
    res = pl.pallas_call(
        body, name="fox_bwd_xchg", grid=(nhb, nq), in_specs=in_specs + [_ANY] * n_side,
        out_specs=out_specs + [_ANY] * n_side, out_shape=out_shape + side.outs,
        scratch_shapes=_side_scratch(side), compiler_params=_params(("arbitrary", "arbitrary")),
    )(*args, *side.ins)
    return (*res[:5], res[5:])


def _merge(up_a, up_b, h, b_gate, *, g_off, D, S):
    tr, tc = _tile(S, 512, 8), _tile(D, 512)

    def fn(i, ins, outs):
        ua, ub, la, lb, ba, bb = (r[...] for r in ins)
        outs[0][...] = (_sigmoid(la + ba) * ua + _sigmoid(lb + bb) * ub).astype(BF16)

    ins = [(up_a, "tile", 0), (up_b, "tile", 0), (h, "tile", g_off), (h, "tile", g_off + D),
           (b_gate, "row", 0), (b_gate, "row", D)]
    return _ew(fn, ins, [((S, D), BF16, "tile", 0)], rows=S, cols=D, tr=tr, tc=tc, name="merge")[0]


def _acc_rows(i, ref, val):
    @pl.when(i == 0)
    def _():
        ref[...] = val

    @pl.when(i > 0)
    def _():
        ref[...] += val


def _merge_bwd(dmix, up_a, up_b, h, b_gate, *, g_off, D, S):
    tr, tc = _tile(S, 512, 8), _tile(D, 512)

    def fn(i, ins, outs):
        dm, ua, ub, la, lb, ba, bb = (r[...] for r in ins)
        ga, gb = _sigmoid(la + ba), _sigmoid(lb + bb)
        outs[0][...] = (ga * dm).astype(BF16)
        outs[1][...] = (gb * dm).astype(BF16)
        dla = ua * dm * (ga * (1.0 - ga))
        dlb = ub * dm * (gb * (1.0 - gb))
        outs[2][...] = dla.astype(BF16)
        outs[3][...] = dlb.astype(BF16)
        _acc_rows(i, outs[4], jnp.sum(dla, axis=0, keepdims=True))
        _acc_rows(i, outs[5], jnp.sum(dlb, axis=0, keepdims=True))

    ins = [(dmix, "tile", 0), (up_a, "tile", 0), (up_b, "tile", 0), (h, "tile", g_off), (h, "tile", g_off + D),
           (b_gate, "row", 0), (b_gate, "row", D)]
    outs = [((S, D), BF16, "tile", 0)] * 4 + [((1, D), F32, "acc", 0)] * 2
    return _ew(fn, ins, outs, rows=S, cols=D, tr=tr, tc=tc, name="merge_bwd")


def _deepnorm(x, y, g, b, *, alpha, D, S):
    tr = _tile(S, 256, 8)

    def fn(i, ins, outs):
        z = alpha * ins[0][...] + ins[1][...]
        mu = jnp.mean(z, axis=1, keepdims=True)
        zc = z - mu
        var = jnp.mean(zc * zc, axis=1, keepdims=True)
        xn = zc * lax.rsqrt(var + LN_EPS) * ins[2][...] + ins[3][...]
        outs[0][...] = xn
        outs[1][...] = xn.astype(BF16)
        outs[2][...] = z

    ins = [(x, "tile", 0), (y, "tile", 0), (g, "row", 0), (b, "row", 0)]
    outs = [((S, D), F32, "tile", 0), ((S, D), BF16, "tile", 0), ((S, D), F32, "tile", 0)]
    return _ew(fn, ins, outs, rows=S, cols=D, tr=tr, tc=D, name="deepnorm")


def _deepnorm_bwd(dxn, z, g, *, D, S):
    tr = _tile(S, 256, 8)

    def fn(i, ins, outs):
        dx, zv = ins[0][...], ins[1][...]
        mu = jnp.mean(zv, axis=1, keepdims=True)
        zc = zv - mu
        rstd = lax.rsqrt(jnp.mean(zc * zc, axis=1, keepdims=True) + LN_EPS)
        u = zc * rstd
        du = dx * ins[2][...]
        dz = rstd * (du - jnp.mean(du, axis=1, keepdims=True) - u * jnp.mean(du * u, axis=1, keepdims=True))
        outs[0][...] = dz
        outs[1][...] = dz.astype(BF16)
        _acc_rows(i, outs[2], jnp.sum(dx * u, axis=0, keepdims=True))
        _acc_rows(i, outs[3], jnp.sum(dx, axis=0, keepdims=True))

    ins = [(dxn, "tile", 0), (z, "tile", 0), (g, "row", 0)]
    outs = [((S, D), F32, "tile", 0), ((S, D), BF16, "tile", 0), ((1, D), F32, "acc", 0), ((1, D), F32, "acc", 0)]
    return _ew(fn, ins, outs, rows=S, cols=D, tr=tr, tc=D, name="deepnorm_bwd")


def _loss_head(y, target, *, D, S):
    tr = _tile(S, 256, 8)

    def body(y_ref, t_ref, d_ref, l_ref):
        err = y_ref[...] - t_ref[...]
        d_ref[...] = err * (1.0 / D)
        part = 0.5 * jnp.sum(jnp.sum(err * err, axis=1, keepdims=True) * (1.0 / D), axis=0, keepdims=True)
        _acc_rows(pl.program_id(0), l_ref, jnp.broadcast_to(part, (1, LANES)))

    return pl.pallas_call(
        body, name="loss_head", grid=(S // tr,),
        in_specs=[pl.BlockSpec((tr, D), lambda i: (i, 0))] * 2,
        out_specs=[pl.BlockSpec((tr, D), lambda i: (i, 0)), pl.BlockSpec((1, LANES), lambda i: (0, 0))],
        out_shape=[jax.ShapeDtypeStruct((S, D), F32), jax.ShapeDtypeStruct((1, LANES), F32)],
        compiler_params=_params(("arbitrary",)),
    )(y, target)


_MESH = pl.DeviceIdType.MESH
_ANY = pl.BlockSpec(memory_space=pl.ANY)


class _Gather:
    def __init__(self, xs):
        self.ins = list(xs)
        self.outs = [jax.ShapeDtypeStruct((NDEV,) + x.shape, x.dtype) for x in xs]

    def _ctx(self, a, x_ref, out_ref, send_sems, recv_sems):
        x, y, c = lax.axis_index("x"), lax.axis_index("y"), lax.axis_index("c")
        me, sibling = (x, y, c), (x, y, 1 - c)
        chips = [(1 - x, y), (x, 1 - y), (1 - x, 1 - y)]

        def slot(px, py, pc):
            return out_ref.at[4 * px + 2 * py + pc]

        def copy(k, block, to, src=None):
            return pltpu.make_async_remote_copy(
                src_ref=slot(*block) if src is None else src, dst_ref=slot(*block),
                send_sem=send_sems.at[7 * a + k], recv_sem=recv_sems.at[7 * a + k], device_id=to, device_id_type=_MESH)

        def first():
            return [copy(0, me, sibling, src=x_ref)] + [copy(1 + j, me, (*chip, c), src=x_ref)
                                                        for j, chip in enumerate(chips)]

        def passed():
            return [copy(4 + j, (*chip, c), sibling) for j, chip in enumerate(chips)]

        return c, me, sibling, chips, slot, copy, first, passed

    def start(self, in_refs, out_refs, send_sems, recv_sems, local_sems):
        for a, (x_ref, out_ref) in enumerate(zip(in_refs, out_refs)):
            _, me, _, _, slot, _, first, _ = self._ctx(a, x_ref, out_ref, send_sems, recv_sems)
            pltpu.make_async_copy(x_ref, slot(*me), local_sems.at[a]).start()
            for cp in first():
                cp.start()

    def middle(self, in_refs, out_refs, send_sems, recv_sems, local_sems):
        for a, (x_ref, out_ref) in enumerate(zip(in_refs, out_refs)):
            c, me, _, chips, _, copy, _, passed = self._ctx(a, x_ref, out_ref, send_sems, recv_sems)
            for j, (chip, cp) in enumerate(zip(chips, passed())):
                copy(1 + j, (*chip, c), me).wait_recv()
                cp.start()

    def finish(self, in_refs, out_refs, send_sems, recv_sems, local_sems):
        for a, (x_ref, out_ref) in enumerate(zip(in_refs, out_refs)):
            c, me, sibling, chips, slot, copy, first, passed = self._ctx(a, x_ref, out_ref, send_sems, recv_sems)
            copy(0, sibling, me).wait_recv()
            for j, chip in enumerate(chips):
                copy(4 + j, (*chip, 1 - c), me).wait_recv()
            for cp in first() + passed():
                cp.wait_send()
            pltpu.make_async_copy(x_ref, slot(*me), local_sems.at[a]).wait()


class _Exchange:
    def __init__(self, gs):
        self.ins = list(gs)
        self.outs = [jax.ShapeDtypeStruct(g.shape, g.dtype) for g in gs]

    def _copies(self, a, g_ref, out_ref, send_sems, recv_sems, arrivals=True):
        x, y, c = lax.axis_index("x"), lax.axis_index("y"), lax.axis_index("c")
        me = 4 * x + 2 * y + c
        out, back = [], []
        for k in range(1, NDEV):
            px, py, pc = x ^ ((k >> 2) & 1), y ^ ((k >> 1) & 1), c ^ (k & 1)
            peer = 4 * px + 2 * py + pc
            sems = dict(send_sem=send_sems.at[7 * a + k - 1], recv_sem=recv_sems.at[7 * a + k - 1],
                        device_id=(px, py, pc), device_id_type=_MESH)
            out.append(pltpu.make_async_remote_copy(src_ref=g_ref.at[peer], dst_ref=out_ref.at[me], **sems))
            if arrivals:
                back.append(pltpu.make_async_remote_copy(src_ref=g_ref.at[me], dst_ref=out_ref.at[peer], **sems))
        return me, out, back

    def start(self, in_refs, out_refs, send_sems, recv_sems, local_sems):
        for a, (g_ref, out_ref) in enumerate(zip(in_refs, out_refs)):
            me, out, _ = self._copies(a, g_ref, out_ref, send_sems, recv_sems, arrivals=False)
            pltpu.make_async_copy(g_ref.at[me], out_ref.at[me], local_sems.at[a]).start()
            for cp in out:
                cp.start()

    def middle(self, in_refs, out_refs, send_sems, recv_sems, local_sems):
        pass

    def finish(self, in_refs, out_refs, send_sems, recv_sems, local_sems):
        for a, (g_ref, out_ref) in enumerate(zip(in_refs, out_refs)):
            me, out, back = self._copies(a, g_ref, out_ref, send_sems, recv_sems)
            for cp in back:
                cp.wait_recv()
            for cp in out:
                cp.wait_send()
            pltpu.make_async_copy(g_ref.at[me], out_ref.at[me], local_sems.at[a]).wait()


def _side_scratch(side):
    n = len(side.ins)
    return [pltpu.SemaphoreType.DMA((7 * n,)), pltpu.SemaphoreType.DMA((7 * n,)), pltpu.SemaphoreType.DMA((n,))]


def _run_side(side, *, name):
    n = len(side.ins)

    def body(*refs):
        ins, outs, sems = refs[:n], refs[n:2 * n], refs[2 * n:]
        side.start(ins, outs, *sems)
        side.middle(ins, outs, *sems)
        side.finish(ins, outs, *sems)

    return pl.pallas_call(
        body, name=name, out_shape=side.outs, in_specs=[_ANY] * n, out_specs=[_ANY] * n,
        scratch_shapes=_side_scratch(side),
    )(*side.ins)


def _all_gather(x, *, name):
    return _run_side(_Gather([x]), name=name)[0]


def _sum_slots(parts, *, name):
    n, R, C = parts.shape
    tr, tc = _tile(R, 64, 16), _tile(C, 2048)

    def body(p_ref, o_ref):
        acc = p_ref[0].astype(F32)
        for k in range(1, n):
            acc = acc + p_ref[k].astype(F32)
        o_ref[...] = acc

    return pl.pallas_call(
        body, name=name, grid=(R // tr, C // tc),
        in_specs=[pl.BlockSpec((n, tr, tc), lambda i, j: (0, i, j))],
        out_specs=pl.BlockSpec((tr, tc), lambda i, j: (i, j)),
        out_shape=jax.ShapeDtypeStruct((R, C), F32),
        compiler_params=_params(("parallel", "parallel")),
    )(parts)


def _adamw(w, g, m, v, *, name):
    R, C = w.shape
    budget = 2 << 20
    tr = R if R * C * 4 <= budget else _tile(R, max(8, (budget // (C * 4)) // 8 * 8), 8)

    def body(w_ref, g_ref, m_ref, v_ref, d_ref, nm_ref, nv_ref):
        gv = g_ref[...]
        nm = ADAM_B1 * m_ref[...] + (1.0 - ADAM_B1) * gv
        nv = ADAM_B2 * v_ref[...] + (1.0 - ADAM_B2) * (gv * gv)
        m_hat = nm / (1.0 - ADAM_B1 ** ADAM_STEP)
        v_hat = nv / (1.0 - ADAM_B2 ** ADAM_STEP)
        d_ref[...] = -ADAM_LR * (m_hat / (jnp.sqrt(v_hat) + ADAM_EPS) + ADAM_WD * w_ref[...])
        nm_ref[...] = nm
        nv_ref[...] = nv

    spec = pl.BlockSpec((tr, C), lambda i: (i, 0))
    return pl.pallas_call(
        body, name=name, grid=(R // tr,), in_specs=[spec] * 4, out_specs=[spec] * 3,
        out_shape=[jax.ShapeDtypeStruct((R, C), F32)] * 3,
        compiler_params=_params(("parallel",)),
    )(w, g, m, v)


def _pad_cols(w, *, nq, ng):
    nf = w.shape[-1] - nq - ng
    pad = jnp.zeros(w.shape[:-1] + (FORGET_PAD - nf,), w.dtype)
    return jnp.concatenate([w[..., :nq], w[..., nq + nf:], w[..., nq:nq + nf], pad], axis=-1)


def _unpad_cols(w, *, nq, ng, nf):
    return jnp.concatenate([w[..., :nq], w[..., nq + ng:nq + ng + nf], w[..., nq:nq + ng]], axis=-1)


def kernel(x, w_in, b_forget, b_gate, w_up_a, w_up_b, w_out, ln_g, ln_b, loss_target, m_w_in, m_b_forget, m_b_gate, m_w_up_a, m_w_up_b, m_w_out, m_ln_g, m_ln_b, v_w_in, v_b_forget, v_b_gate, v_w_up_a, v_w_up_b, v_w_out, v_ln_g, v_ln_b):
    depth = w_in.shape[0]
    _, S, D = x.shape
    WA, WB = w_up_a.shape[1], w_up_b.shape[1]
    nha, nhb = WA // HEAD_DIM, WB // HEAD_DIM
    assert nhb == b_forget.shape[1] and WA == WB and nhb <= HEAD_ROWS
    nq, ng, nf = 4 * WA + 4 * WB, 2 * D, nhb
    assert w_in.shape[2] == nq + nf + ng
    ncp = nq + ng + FORGET_PAD
    off_va, off_za, off_b, off_zb, off_g, off_f = 2 * WA, 3 * WA, 4 * WA, 4 * WA + 3 * WB, nq, nq + ng
    alpha = float((2 * depth) ** 0.25)
    dsh = D // NDEV
    tq_f, tq_b = _tile(S, FOX_TILE_FWD), _tile(S, FOX_TILE_BWD)

    x0 = x[0]
    target = loss_target[0]
    cos, sin = _rope_tables(S)

    w_in_p = _pad_cols(w_in, nq=nq, ng=ng).astype(BF16)
    w_up_a_t = jnp.swapaxes(w_up_a, 1, 2).astype(BF16)
    w_up_b_t = jnp.swapaxes(w_up_b, 1, 2).astype(BF16)
    w_out_c = w_out.astype(BF16)
    W_in, W_ua, W_ub, W_out = ([None] * depth for _ in range(4))

    def gather_of(l):
        return _Gather([w_in_p[l], w_up_a_t[l], w_up_b_t[l], w_out_c[l]])

    def keep_weights(l, got):
        W_in[l], W_ua[l] = got[0].reshape(D, ncp), got[1].reshape(D, WA)
        W_ub[l], W_out[l] = got[2].reshape(D, WB), got[3].reshape(D, D)

    keep_weights(0, _run_side(gather_of(0), name="gather_w_0"))

    bf_rows = [jnp.broadcast_to(jnp.pad(b_forget[l], (0, HEAD_ROWS - nhb))[:, None], (HEAD_ROWS, LANES))
               for l in range(depth)]

    saved = []
    xf, xb = x0, x0.astype(BF16)
    for l in range(depth):
        if l + 1 < depth:
            h, got = _mm(xb, W_in[l], name="in_proj_gather", tm=1024, tn=1280, tk=2048, side=gather_of(l + 1))
            keep_weights(l + 1, got)
        else:
            h = _mm(xb, W_in[l], name="in_proj", tm=1024, tn=1280, tk=2048)
        qk_a = _prep_a(h, cos, sin, W=WA, S=S)
        qkv_b = _prep_b(h, off=off_b, W=WB, S=S)
        out_a, lse_a, ga = _band_fwd(qk_a, h, nha=nha, S=S, v_off=off_va, z_off=off_za)
        f3 = _lane_blocks(jnp.pad(h[:, off_f:off_f + nhb].T, ((0, HEAD_ROWS - nhb), (0, 0))))
        c = _lane_unblocks(_forget_cumsum(f3, bf_rows[l]))[:nhb]
        out_b, lq_b, gb = _fox_fwd(qkv_b, c.reshape(nhb, S // tq_f, 1, tq_f), h, z_off=off_zb, nhb=nhb, S=S, tq=tq_f)
        ck = c.reshape(nhb, S // tq_b, 1, tq_b)
        up_a = _mm(ga, W_ua[l], tb=True, name="up_a", tm=1024, tn=1024, tk=1536)
        up_b = _mm(gb, W_ub[l], tb=True, name="up_b", tm=1024, tn=1024, tk=1536)
        bg = b_gate[l][None, :]
        mix = _merge(up_a, up_b, h, bg, g_off=off_g, D=D, S=S)
        y = _mm(mix, W_out[l], name="out_proj", tm=1024, tn=1024, tk=2048)
        xn_f, xn_b, z = _deepnorm(xf, y, ln_g[l][None, :], ln_b[l][None, :], alpha=alpha, D=D, S=S)
        saved.append(dict(xb=xb, h=h, qk_a=qk_a, qkv_b=qkv_b, out_a=out_a, lse_a=lse_a, ga=ga, f3=f3, ck=ck,
                          out_b=out_b, lq_b=lq_b, gb=gb, up_a=up_a, up_b=up_b, mix=mix, z=z, bg=bg))
        xf, xb = xn_f, xn_b

    dx, loss_row = _loss_head(xf, target, D=D, S=S)

    recv = [None] * depth
    small = [None] * depth
    pending = None
    for l in reversed(range(depth)):
        sv = saved[l]
        h = sv["h"]
        dz, dzb, d_lng, d_lnb = _deepnorm_bwd(dx, sv["z"], ln_g[l][None, :], D=D, S=S)
        g_w_out = _mm(sv["mix"], dzb, ta=True, name="g_w_out", tm=512, tn=1024, tk=1024, out_dtype=BF16)
        dmix = _mm(dzb, W_out[l], tb=True, name="d_mix", tm=1024, tn=1024, tk=2048)
        dup_a, dup_b, dgl_a, dgl_b, d_bga, d_bgb = _merge_bwd(dmix, sv["up_a"], sv["up_b"], h, sv["bg"],
                                                              g_off=off_g, D=D, S=S)
        g_w_ua = _mm(dup_a, sv["ga"], ta=True, name="g_w_up_a", tm=512, tn=1536, tk=1024, out_dtype=BF16)
        g_w_ub = _mm(dup_b, sv["gb"], ta=True, name="g_w_up_b", tm=512, tn=1536, tk=1024, out_dtype=BF16)
        dga = _mm(dup_a, W_ua[l], name="d_ga", tm=1024, tn=1536, tk=2048)
        dgb = _mm(dup_b, W_ub[l], name="d_gb", tm=1024, tn=1536, tk=2048)
        do_a, dza, stats_a = _gate_bwd(dga, sv["out_a"], h, sv["lse_a"], z_off=off_za, W=WA, S=S,
                                       name="gate_bwd_a", do_dtype=F32)
        do_b, dzb_, stats_b = _gate_bwd(dgb, sv["out_b"], h, sv["lq_b"], z_off=off_zb, W=WB, S=S,
                                        name="gate_bwd_b", do_dtype=BF16)
        dqa, dka, dva = _rope_bwd(*_band_bwd(sv["qk_a"], h, do_a, stats_a, nha=nha, S=S, v_off=off_va),
                                  cos, sin, W=WA, S=S)
        if pending is None:
            dqb, dkb, dvb, dcq, dck = _fox_bwd(sv["qkv_b"], do_b, stats_b, sv["ck"], nhb=nhb, S=S, tq=tq_b)
        else:
            dqb, dkb, dvb, dcq, dck, got = _fox_bwd(sv["qkv_b"], do_b, stats_b, sv["ck"], nhb=nhb, S=S, tq=tq_b,
                                                    side=_Exchange([pending]))
            recv[l + 1] = (got[0], *recv[l + 1])
        head_rows = lambda t: _lane_blocks(jnp.pad(t, ((0, HEAD_ROWS - nhb), (0, 0))))
        df3, d_bf = _forget_bwd(head_rows(dcq[:, ::LANES].T), head_rows(dck.reshape(nhb, S)), sv["f3"], bf_rows[l])
        dfl = jnp.pad(_lane_unblocks(df3)[:nhb].T, ((0, 0), (0, FORGET_PAD - nhb))).astype(BF16)
        dh = jnp.concatenate([dqa, dka, dva, dza, dqb.astype(BF16), dkb, dvb, dzb_, dgl_a, dgl_b, dfl], axis=1)
        chunks = lambda t: t.reshape(NDEV, dsh, t.shape[1])
        g_w_in, got3 = _mm(sv["xb"], dh, ta=True, name="g_w_in_xchg", tm=512, tn=1280, tk=4096, out_dtype=BF16,
                           side=_Exchange([chunks(g_w_ua), chunks(g_w_ub), chunks(g_w_out)]))
        if l > 0:
            dx = _mm(dh, W_in[l], tb=True, name="d_x", tm=1024, tn=512, tk=3328, add=dz, add_scale=alpha)
            pending, recv[l] = chunks(g_w_in), tuple(got3)
        else:
            dx, got1 = _mm(dh, W_in[l], tb=True, name="d_x_xchg", tm=1024, tn=512, tk=3328, add=dz, add_scale=alpha,
                           side=_Exchange([chunks(g_w_in)]))
            recv[l] = (got1[0], *got3)
        small[l] = jnp.concatenate([d_bf[:nhb, 0], jnp.zeros((LANES - nhb,), F32), d_bga[0], d_bgb[0],
                                    d_lng[0], d_lnb[0]])

    n_small = LANES + 2 * D + 2 * D
    flat = jnp.concatenate(small + [loss_row[0]])
    rows = -(-flat.shape[0] // (8 * LANES)) * 8
    flat = jnp.pad(flat, (0, rows * LANES - flat.shape[0])).reshape(rows, LANES)
    tot = _sum_slots(_all_gather(flat, name="gather_small"), name="sum_small").reshape(-1)
    loss = tot[depth * n_small]
    sm = tot[:depth * n_small].reshape(depth, n_small)
    g_bf, g_bg = sm[:, :nhb], sm[:, LANES:LANES + 2 * D]
    g_lg, g_lb = sm[:, LANES + 2 * D:LANES + 3 * D], sm[:, LANES + 3 * D:]

    g_in = jnp.stack([_unpad_cols(_sum_slots(recv[l][0], name="sum_w_in"), nq=nq, ng=ng, nf=nf) for l in range(depth)])
    g_ua = jnp.stack([_sum_slots(recv[l][1], name="sum_w_up_a").T for l in range(depth)])
    g_ub = jnp.stack([_sum_slots(recv[l][2], name="sum_w_up_b").T for l in range(depth)])
    g_out = jnp.stack([_sum_slots(recv[l][3], name="sum_w_out") for l in range(depth)])

    def adam(w, g, m, v, name):
        shp = w.shape
        d_, m_, v_ = _adamw(*[t.reshape(-1, shp[-1]) for t in (w, g, m, v)], name=name)
        return d_.reshape(shp), m_.reshape(shp), v_.reshape(shp)

    grads = [g_in, g_bf, g_bg, g_ua, g_ub, g_out, g_lg, g_lb]
    ws = [w_in, b_forget, b_gate, w_up_a, w_up_b, w_out, ln_g, ln_b]
    ms = [m_w_in, m_b_forget, m_b_gate, m_w_up_a, m_w_up_b, m_w_out, m_ln_g, m_ln_b]
    vs = [v_w_in, v_b_forget, v_b_gate, v_w_up_a, v_w_up_b, v_w_out, v_ln_g, v_ln_b]
    names = ["w_in", "b_forget", "b_gate", "w_up_a", "w_up_b", "w_out", "ln_g", "ln_b"]
    upd = [adam(w, g, m, v, f"adamw_{n}") for w, g, m, v, n in zip(ws, grads, ms, vs, names)]
    return (loss, dx[None], *grads, *[u[0] for u in upd], *[u[1] for u in upd], *[u[2] for u in upd])
```

```python
import functools

import jax
import jax.numpy as jnp
from jax import lax
from jax.experimental import pallas as pl
from jax.experimental.pallas import tpu as pltpu

F32 = jnp.float32
BF16 = jnp.bfloat16

NDEV = 8
HEAD_DIM = 128
LANES = 128
Q_BLOCK = 128
DILATIONS = (1, 4, 16)
ROPE_THETA = 10000.0
LN_EPS = 1e-5
ATT_SCALE = HEAD_DIM ** -0.5
NEG = -1e30
FORGET_PAD = 256
BAND_CHUNK = 2048
BAND_UNROLL = 8
FOX_TILE_FWD = 1024
FOX_TILE_BWD = 1024
FOX_SUB = 2
HEAD_ROWS = 16
ADAM_LR, ADAM_B1, ADAM_B2, ADAM_EPS, ADAM_WD, ADAM_STEP = 0.001, 0.9, 0.999, 1e-08, 0.01, 10
VMEM_LIMIT = 56 * 1024 * 1024


def _tile(n, pref, mult=LANES):
    if n <= pref:
        return n
    t = (pref // mult) * mult
    while t >= mult:
        if n % t == 0:
            return t
        t -= mult
    return n


def _params(sem=None):
    return pltpu.CompilerParams(dimension_semantics=sem, vmem_limit_bytes=VMEM_LIMIT)


def _sigmoid(z):
    return 1.0 / (1.0 + jnp.exp(-z))


_NT = (((1,), (1,)), ((), ()))
_NN = (((1,), (0,)), ((), ()))
_TN = (((0,), (0,)), ((), ()))


def _dot(a, b, dims):
    return lax.dot_general(a, b, dims, preferred_element_type=F32)


def _mm(a, b, *, name, ta=False, tb=False, out_dtype=F32, tm=512, tn=512, tk=512, add=None, add_scale=1.0,
        side=None):
    if ta:
        K, M = a.shape
    else:
        M, K = a.shape
    if tb:
        N, K2 = b.shape
    else:
        K2, N = b.shape
    assert K == K2, (a.shape, b.shape, ta, tb)
    tm, tn, tk = _tile(M, tm), _tile(N, tn), _tile(K, tk)
    ni, nj, nk = M // tm, N // tn, K // tk
    dims = (((0 if ta else 1,), (1 if tb else 0,)), ((), ()))
    n_side = 0 if side is None else len(side.ins)
    n_in = 2 + (add is not None)

    def body(*refs):
        a_ref, b_ref = refs[:2]
        add_ref = refs[2] if add is not None else None
        side_in = refs[n_in:n_in + n_side]
        o_ref = refs[n_in + n_side]
        side_out = refs[n_in + n_side + 1:n_in + 2 * n_side + 1]
        acc_ref = refs[n_in + 2 * n_side + 1]
        sems = refs[n_in + 2 * n_side + 2:]
        k = pl.program_id(2)
        if side is not None:
            @pl.when((pl.program_id(0) == 0) & (pl.program_id(1) == 0) & (k == 0))
            def _():
                side.start(side_in, side_out, *sems)

        part = _dot(a_ref[...].astype(BF16), b_ref[...].astype(BF16), dims)

        @pl.when(k == 0)
        def _():
            acc_ref[...] = part

        @pl.when(k > 0)
        def _():
            acc_ref[...] += part

        @pl.when(k == nk - 1)
        def _():
            r = acc_ref[...]
            if add_ref is not None:
                r = r + add_scale * add_ref[...]
            o_ref[...] = r.astype(out_dtype)

        if side is not None:
            step = (pl.program_id(0) * nj + pl.program_id(1)) * nk + k
            steps = ni * nj * nk

            @pl.when(step == max((3 * steps) // 4 - 1, 0))
            def _():
                side.middle(side_in, side_out, *sems)

            @pl.when(step == steps - 1)
            def _():
                side.finish(side_in, side_out, *sems)

    a_spec = pl.BlockSpec((tk, tm), lambda i, j, k: (k, i)) if ta else pl.BlockSpec((tm, tk), lambda i, j, k: (i, k))
    b_spec = pl.BlockSpec((tn, tk), lambda i, j, k: (j, k)) if tb else pl.BlockSpec((tk, tn), lambda i, j, k: (k, j))
    o_spec = pl.BlockSpec((tm, tn), lambda i, j, k: (i, j))
    in_specs, args = [a_spec, b_spec], [a, b]
    if add is not None:
        in_specs.append(o_spec)
        args.append(add)
    out_shape = jax.ShapeDtypeStruct((M, N), out_dtype)
    scratch = [pltpu.VMEM((tm, tn), F32)]
    if side is None:
        return pl.pallas_call(
            body, name=name, grid=(ni, nj, nk), in_specs=in_specs, out_specs=o_spec, out_shape=out_shape,
            scratch_shapes=scratch, compiler_params=_params(("parallel", "parallel", "arbitrary")),
        )(*args)
    res = pl.pallas_call(
        body, name=name, grid=(ni, nj, nk), in_specs=in_specs + [_ANY] * n_side,
        out_specs=[o_spec] + [_ANY] * n_side, out_shape=[out_shape] + side.outs,
        scratch_shapes=scratch + _side_scratch(side),
        compiler_params=_params(("arbitrary", "arbitrary", "arbitrary")),
    )(*args, *side.ins)
    return res[0], res[1:]


def _ew(fn, ins, outs, *, rows, cols, tr, tc, name):
    assert rows % tr == 0 and cols % tc == 0

    def spec(kind, off):
        assert off % tc == 0, (off, tc)
        ob = off // tc
        if kind == "tile":
            return pl.BlockSpec((tr, tc), lambda j, i: (i, j + ob))
        if kind == "tab":
            return pl.BlockSpec((tr, LANES), lambda j, i: (i, 0))
        return pl.BlockSpec((1, tc), lambda j, i: (0, j + ob))

    n_in = len(ins)

    def body(*refs):
        fn(pl.program_id(1), refs[:n_in], refs[n_in:])

    return pl.pallas_call(
        body, name=name, grid=(cols // tc, rows // tr),
        in_specs=[spec(k, o) for _, k, o in ins],
        out_specs=[spec(k, o) for _, _, k, o in outs],
        out_shape=[jax.ShapeDtypeStruct(s, d) for s, d, _, _ in outs],
        compiler_params=_params(("parallel", "arbitrary")),
    )(*[a for a, _, _ in ins])


def _heads(t):
    return [t[:, e:e + LANES] for e in range(0, t.shape[1], LANES)]


def _rope_tables(S):
    half = HEAD_DIM // 2
    inv_freq = ROPE_THETA ** (-jnp.arange(half, dtype=F32) / half)
    ang = jnp.arange(S, dtype=jnp.int32).astype(F32)[:, None] * inv_freq[None, :]
    cos, sin = jnp.cos(ang), jnp.sin(ang)
    return jnp.concatenate([cos, cos], axis=1), jnp.concatenate([-sin, sin], axis=1)


def _rope(t, cs, sn):
    return jnp.concatenate([g * cs + pltpu.roll(g, HEAD_DIM // 2, 1) * sn for g in _heads(t)], axis=1)


def _rope_t(t, cs, sn):
    return jnp.concatenate([g * cs + pltpu.roll(g * sn, HEAD_DIM // 2, 1) for g in _heads(t)], axis=1)


def _prep_a(h, cos, sin, *, W, S):
    tr, tc = _tile(S, 1024, 8), _tile(W, 512)

    def fn(i, ins, outs):
        outs[0][...] = _rope(ins[0][...], ins[1][...], ins[2][...])

    return _ew(fn, [(h, "tile", 0), (cos, "tab", 0), (sin, "tab", 0)], [((S, 2 * W), F32, "tile", 0)],
               rows=S, cols=2 * W, tr=tr, tc=tc, name="prep_a")[0]


def _prep_b(h, *, off, W, S):
    tr, tc = _tile(S, 1024, 8), _tile(W, 512)

    def fn(i, ins, outs):
        scale = jnp.where(pl.program_id(0) * tc < W, ATT_SCALE, 1.0).astype(F32)
        outs[0][...] = (ins[0][...] * scale).astype(BF16)

    return _ew(fn, [(h, "tile", off)], [((S, 3 * W), BF16, "tile", 0)], rows=S, cols=3 * W, tr=tr, tc=tc,
               name="prep_b")[0]


def _rope_bwd(dq, dk, dv, cos, sin, *, W, S):
    tr, tc = _tile(S, 512, 8), _tile(W, 512)

    def fn(i, ins, outs):
        cs, sn = ins[3][...], ins[4][...]
        outs[0][...] = _rope_t(ins[0][...], cs, sn).astype(BF16)
        outs[1][...] = _rope_t(ins[1][...], cs, sn).astype(BF16)
        outs[2][...] = ins[2][...].astype(BF16)

    ins = [(dq, "tile", 0), (dk, "tile", 0), (dv, "tile", 0), (cos, "tab", 0), (sin, "tab", 0)]
    return _ew(fn, ins, [((S, W), BF16, "tile", 0)] * 3, rows=S, cols=W, tr=tr, tc=tc, name="rope_bwd")


def _band_biases():
    dist = (lax.broadcasted_iota(jnp.int32, (Q_BLOCK, 2 * Q_BLOCK), 0)
            - lax.broadcasted_iota(jnp.int32, (Q_BLOCK, 2 * Q_BLOCK), 1))
    first = jnp.where(dist >= 0, 0.0, NEG).astype(F32)
    other = jnp.where((dist + Q_BLOCK >= 0) & (dist <= 0), 0.0, NEG).astype(F32)
    return first, other


def _band_rows(it, c, d, nbc, biases):
    r, n = (0, it) if d == 1 else (it // nbc, it % nbc)
    ng = c * nbc + n
    k0 = pl.multiple_of(jnp.maximum(ng - 1, 0) * Q_BLOCK, Q_BLOCK)
    bias = jnp.where(ng == 0, biases[0], biases[1])
    if d == 1:
        return pl.ds(pl.multiple_of(it * Q_BLOCK, Q_BLOCK), Q_BLOCK), pl.ds(k0, 2 * Q_BLOCK), bias
    return (pl.ds(r + d * Q_BLOCK * n, Q_BLOCK, stride=d), pl.ds(r + d * k0, 2 * Q_BLOCK, stride=d), bias)


def _band_chunk(S):
    ch = min(S, BAND_CHUNK)
    assert S % ch == 0 and ch % (Q_BLOCK * max(DILATIONS)) == 0 and S >= 2 * Q_BLOCK * max(DILATIONS)
    return ch


def _band_fwd(qk, h, *, nha, S, v_off, z_off):
    CH = _band_chunk(S)
    W = nha * LANES
    slab = 256

    def body(q_ref, k_ref, v_ref, z_ref, o_ref, lse_ref, g_ref, *scratch):
        os_refs, ls_refs = scratch[:len(DILATIONS)], scratch[len(DILATIONS):]
        c = pl.program_id(1)
        biases = _band_biases()
        for gi, d in enumerate(DILATIONS):
            nbc = CH // (Q_BLOCK * d)

            def step(it, carry, gi=gi, d=d, nbc=nbc):
                rq, rk, bias = _band_rows(it, c, d, nbc, biases)
                q = q_ref[rq, :].astype(BF16)
                k = k_ref[rk, :].astype(BF16)
                v = v_ref[rk, :].astype(BF16)
                s = _dot(q, k, _NT) * ATT_SCALE + bias
                m = jnp.max(s, axis=1, keepdims=True)
                p = jnp.exp(s - m)
                den = jnp.sum(p, axis=1, keepdims=True)
                os_refs[gi][rq, :] = _dot(p.astype(BF16), v, _NN) / den
                ls_refs[gi][rq, :] = jnp.broadcast_to(m + jnp.log(den), (Q_BLOCK, LANES))
                return carry

            lax.fori_loop(0, CH // Q_BLOCK, step, 0, unroll=BAND_UNROLL)

        def mix(t, carry):
            rows = pl.ds(pl.multiple_of(t * slab, slab), slab)
            a, b, cc = (r[rows, :] for r in ls_refs)
            m = jnp.maximum(jnp.maximum(a, b), cc)
            ea, eb, ec = jnp.exp(a - m), jnp.exp(b - m), jnp.exp(cc - m)
            den = ea + eb + ec
            out = (ea * os_refs[0][rows, :] + eb * os_refs[1][rows, :] + ec * os_refs[2][rows, :]) / den
            z = z_ref[rows, :]
            o_ref[rows, :] = out
            lse_ref[rows, :] = m + jnp.log(den)
            g_ref[rows, :] = (out * (z * _sigmoid(z))).astype(BF16)
            return carry

        lax.fori_loop(0, CH // slab, mix, 0)

    chunk = pl.BlockSpec((CH, LANES), lambda hd, c: (c, hd))
    return pl.pallas_call(
        body, name="band_fwd", grid=(nha, S // CH),
        in_specs=[chunk,
                  pl.BlockSpec((S, LANES), lambda hd, c: (0, nha + hd)),
                  pl.BlockSpec((S, LANES), lambda hd, c: (0, v_off // LANES + hd)),
                  pl.BlockSpec((CH, LANES), lambda hd, c: (c, z_off // LANES + hd))],
        out_specs=[chunk, chunk, chunk],
        out_shape=[jax.ShapeDtypeStruct((S, W), F32), jax.ShapeDtypeStruct((S, W), F32),
                   jax.ShapeDtypeStruct((S, W), BF16)],
        scratch_shapes=[pltpu.VMEM((CH, LANES), F32)] * (2 * len(DILATIONS)),
        compiler_params=_params(("parallel", "arbitrary")),
    )(qk, qk, h, h)


def _band_bwd(qk, h, do, stats, *, nha, S, v_off):
    CH = _band_chunk(S)
    W = nha * LANES

    def body(q_ref, k_ref, v_ref, do_ref, st_ref, dq_ref, dk_ref, dv_ref):
        c = pl.program_id(1)

        @pl.when(c == 0)
        def _():
            dk_ref[...] = jnp.zeros_like(dk_ref)
            dv_ref[...] = jnp.zeros_like(dv_ref)

        biases = _band_biases()
        for gi, d in enumerate(DILATIONS):
            nbc = CH // (Q_BLOCK * d)

            def step(it, carry, gi=gi, d=d, nbc=nbc):
                rq, rk, bias = _band_rows(it, c, d, nbc, biases)
                q = q_ref[rq, :].astype(BF16)
                k = k_ref[rk, :].astype(BF16)
                v = v_ref[rk, :].astype(BF16)
                g = do_ref[rq, :].astype(BF16)
                st = st_ref[rq, :]
                p = jnp.exp(_dot(q, k, _NT) * ATT_SCALE + (bias - st[:, 0:1]))
                ds = (p * (_dot(g, v, _NT) - st[:, 1:2]) * ATT_SCALE).astype(BF16)
                dq = _dot(ds, k, _NN)
                if gi == 0:
                    dq_ref[rq, :] = dq
                else:
                    dq_ref[rq, :] += dq
                dk_ref[rk, :] += _dot(ds, q, _TN)
                dv_ref[rk, :] += _dot(p.astype(BF16), g, _TN)
                return carry

            lax.fori_loop(0, CH // Q_BLOCK, step, 0, unroll=BAND_UNROLL // 2)

    chunk = pl.BlockSpec((CH, LANES), lambda hd, c: (c, hd))
    whole = pl.BlockSpec((S, LANES), lambda hd, c: (0, hd))
    return pl.pallas_call(
        body, name="band_bwd", grid=(nha, S // CH),
        in_specs=[chunk,
                  pl.BlockSpec((S, LANES), lambda hd, c: (0, nha + hd)),
                  pl.BlockSpec((S, LANES), lambda hd, c: (0, v_off // LANES + hd)),
                  chunk, chunk],
        out_specs=[chunk, whole, whole],
        out_shape=[jax.ShapeDtypeStruct((S, W), F32)] * 3,
        compiler_params=_params(("parallel", "arbitrary")),
    )(qk, qk, h, do, stats)


def _gate_bwd(dg, o, h, lane0, *, z_off, W, S, name, do_dtype):
    tr, tc = _tile(S, 512, 8), _tile(W, 512)

    def fn(i, ins, outs):
        g, out, z = ins[0][...], ins[1][...], ins[2][...]
        sg = _sigmoid(z)
        do = g * (z * sg)
        outs[0][...] = do.astype(do_dtype)
        outs[1][...] = (g * out * (sg * (1.0 + z * (1.0 - sg)))).astype(BF16)
        delta = jnp.concatenate([jnp.broadcast_to(jnp.sum(t, axis=1, keepdims=True), t.shape)
                                 for t in _heads(do * out)], axis=1)
        first = lax.broadcasted_iota(jnp.int32, delta.shape, 1) % LANES == 0
        outs[2][...] = jnp.where(first, ins[3][...], delta)

    ins = [(dg, "tile", 0), (o, "tile", 0), (h, "tile", z_off), (lane0, "tile", 0)]
    outs = [((S, W), do_dtype, "tile", 0), ((S, W), BF16, "tile", 0), ((S, W), F32, "tile", 0)]
    return _ew(fn, ins, outs, rows=S, cols=W, tr=tr, tc=tc, name=name)


def _split3(x):
    hi = x.astype(BF16)
    r = x - hi.astype(F32)
    mid = r.astype(BF16)
    lo = (r - mid.astype(F32)).astype(BF16)
    return hi, mid, lo


def _tri_dot(x, tri):
    hi, mid, lo = _split3(x)
    return _dot(hi, tri, _NN) + (_dot(mid, tri, _NN) + _dot(lo, tri, _NN))


def _log1p(u):
    w = 1.0 + u
    return jnp.where(w == 1.0, u, jnp.log(w) * (u / jnp.where(w == 1.0, 1.0, w - 1.0)))


def _lane_blocks(t):
    R, S = t.shape
    return t.reshape(R, S // LANES, LANES).transpose(1, 0, 2)


def _lane_unblocks(t):
    nb, R, _ = t.shape
    return t.transpose(1, 0, 2).reshape(R, nb * LANES)


def _forget_cumsum(f3, bf):
    nb, R, _ = f3.shape

    def body(f_ref, b_ref, c_ref):
        row = lax.broadcasted_iota(jnp.int32, (LANES, LANES), 0)
        colm = lax.broadcasted_iota(jnp.int32, (LANES, LANES), 1)
        tri = (row <= colm).astype(BF16)
        bias = b_ref[...]

        def step(n, carry):
            z = f_ref[n] + bias
            logf = jnp.minimum(z, 0.0) - _log1p(jnp.exp(-jnp.abs(z)))
            c = _tri_dot(logf, tri) + carry
            c_ref[n] = c
            return jnp.broadcast_to(c[:, LANES - 1:LANES], (R, LANES))

        lax.fori_loop(0, nb, step, jnp.zeros((R, LANES), F32))

    return pl.pallas_call(
        body, name="forget_cumsum", out_shape=jax.ShapeDtypeStruct(f3.shape, F32),
        compiler_params=_params(),
    )(f3, bf)


def _forget_bwd(dcq3, dck3, f3, bf):
    nb, R, _ = f3.shape

    def body(dcq_ref, dck_ref, f_ref, b_ref, df_ref, db_ref):
        row = lax.broadcasted_iota(jnp.int32, (LANES, LANES), 0)
        colm = lax.broadcasted_iota(jnp.int32, (LANES, LANES), 1)
        tri = (row >= colm).astype(BF16)
        bias = b_ref[...]

        def step(t, carry):
            tail, tot = carry
            n = nb - 1 - t
            r = _tri_dot(dcq_ref[n] + dck_ref[n], tri) + tail
            df = r * _sigmoid(-(f_ref[n] + bias))
            df_ref[n] = df
            tot = tot + jnp.broadcast_to(jnp.sum(df, axis=1, keepdims=True), (R, LANES))
            return jnp.broadcast_to(r[:, 0:1], (R, LANES)), tot

        _, tot = lax.fori_loop(0, nb, step, (jnp.zeros((R, LANES), F32), jnp.zeros((R, LANES), F32)))
        db_ref[...] = tot

    return pl.pallas_call(
        body, name="forget_bwd",
        out_shape=[jax.ShapeDtypeStruct(f3.shape, F32), jax.ShapeDtypeStruct((R, LANES), F32)],
        compiler_params=_params(),
    )(dcq3, dck3, f3, bf)


def _causal(t):
    return lax.broadcasted_iota(jnp.int32, (t, t), 0) >= lax.broadcasted_iota(jnp.int32, (t, t), 1)


def _fox_fwd(qkv, ck, h, *, z_off, nhb, S, tq):
    nq = S // tq
    ts = tq // FOX_SUB

    def body(q_ref, k_ref, v_ref, ck_ref, z_ref, o_ref, lq_ref, g_ref):
        i = pl.program_id(1)
        qs = [q_ref[r * ts:(r + 1) * ts, :] for r in range(FOX_SUB)]

        def step(j, carry, masked):
            off = pl.multiple_of(j * tq, tq)
            k = k_ref[pl.ds(off, tq), :]
            v = v_ref[pl.ds(off, tq), :]
            ckj = ck_ref[j]
            new = []
            for r, (m, l, acc) in enumerate(carry):
                nk = (r + 1) * ts if masked else tq
                s = _dot(qs[r], k[:nk], _NT) - ckj[:, :nk]
                if masked:
                    rows = r * ts + lax.broadcasted_iota(jnp.int32, (ts, nk), 0)
                    s = jnp.where(rows >= lax.broadcasted_iota(jnp.int32, (ts, nk), 1), s, NEG)
                m_new = jnp.maximum(m, jnp.max(s, axis=1, keepdims=True))
                alpha = jnp.exp(m - m_new)
                p = jnp.exp(s - m_new)
                l = alpha * l + jnp.sum(p, axis=1, keepdims=True)
                acc = alpha * acc + _dot(p.astype(BF16), v[:nk], _NN)
                new.append((m_new, l, acc))
            return tuple(new)

        init = tuple((jnp.full((ts, 1), NEG, F32), jnp.zeros((ts, 1), F32), jnp.zeros((ts, LANES), F32))
                     for _ in range(FOX_SUB))
        carry = lax.fori_loop(0, i, functools.partial(step, masked=False), init)
        for r, (m, l, acc) in enumerate(step(i, carry, True)):
            rows = slice(r * ts, (r + 1) * ts)
            out = acc / l
            z = z_ref[rows, :]
            o_ref[rows, :] = out
            lq_ref[rows, :] = jnp.broadcast_to(m + jnp.log(l), (ts, LANES))
            g_ref[rows, :] = (out * (z * _sigmoid(z))).astype(BF16)

    blk = pl.BlockSpec((tq, LANES), lambda hd, i: (i, hd))
    return pl.pallas_call(
        body, name="fox_fwd", grid=(nhb, nq),
        in_specs=[blk,
                  pl.BlockSpec((S, LANES), lambda hd, i: (0, nhb + hd)),
                  pl.BlockSpec((S, LANES), lambda hd, i: (0, 2 * nhb + hd)),
                  pl.BlockSpec((None, nq, 1, tq), lambda hd, i: (hd, 0, 0, 0)),
                  pl.BlockSpec((tq, LANES), lambda hd, i: (i, z_off // LANES + hd))],
        out_specs=[blk, blk, blk],
        out_shape=[jax.ShapeDtypeStruct((S, nhb * LANES), F32), jax.ShapeDtypeStruct((S, nhb * LANES), F32),
                   jax.ShapeDtypeStruct((S, nhb * LANES), BF16)],
        compiler_params=_params(("parallel", "arbitrary")),
    )(qkv, qkv, qkv, ck, h)


def _fox_bwd(qkv, do, stats, ck, *, nhb, S, tq, side=None):
    nq = S // tq
    ts = tq // FOX_SUB
    n_side = 0 if side is None else len(side.ins)

    def body(*refs):
        q_ref, k_ref, v_ref, do_ref, st_ref, ck_ref = refs[:6]
        side_in = refs[6:6 + n_side]
        dq_ref, dk_ref, dv_ref, dr_ref, dc_ref = refs[6 + n_side:11 + n_side]
        side_out, sems = refs[11 + n_side:11 + 2 * n_side], refs[11 + 2 * n_side:]
        j = pl.program_id(1)
        if side is not None:
            @pl.when((pl.program_id(0) == 0) & (j == 0))
            def _():
                side.start(side_in, side_out, *sems)

        @pl.when(j == 0)
        def _():
            dq_ref[...] = jnp.zeros_like(dq_ref)
            dr_ref[...] = jnp.zeros_like(dr_ref)

        k = k_ref[...]
        v = v_ref[...]
        ckv = ck_ref[...]

        def step(i, carry, masked):
            dk, dv, dc = carry
            for r in range(FOX_SUB):
                rows = pl.ds(pl.multiple_of(i * tq + r * ts, ts), ts)
                q = q_ref[rows, :]
                g = do_ref[rows, :]
                st = st_ref[rows, :]
                p = jnp.exp(_dot(q, k, _NT) - ckv - st[:, 0:1])
                if masked:
                    rr = r * ts + lax.broadcasted_iota(jnp.int32, (ts, tq), 0)
                    p = jnp.where(rr >= lax.broadcasted_iota(jnp.int32, (ts, tq), 1), p, 0.0)
                ds = p * (_dot(g, v, _NT) - st[:, 1:2])
                dc = dc - jnp.sum(ds, axis=0, keepdims=True)
                dr_ref[rows, :] += jnp.broadcast_to(jnp.sum(ds, axis=1, keepdims=True), (ts, LANES))
                dsb = ds.astype(BF16)
                dq_ref[rows, :] += _dot(dsb, k, _NN)
                dk = dk + _dot(dsb, q, _TN)
                dv = dv + _dot(p.astype(BF16), g, _TN)
            return dk, dv, dc

        init = (jnp.zeros((tq, LANES), F32), jnp.zeros((tq, LANES), F32), jnp.zeros((1, tq), F32))
        carry = step(j, init, True)
        dk, dv, dc = lax.fori_loop(j + 1, nq, functools.partial(step, masked=False), carry)
        dk_ref[...] = dk.astype(BF16)
        dv_ref[...] = dv.astype(BF16)
        dc_ref[...] = dc

        @pl.when(j == nq - 1)
        def _():
            dq_ref[...] = dq_ref[...] * ATT_SCALE

        if side is not None:
            @pl.when((pl.program_id(0) == nhb - 1) & (j == nq - 1))
            def _():
                side.middle(side_in, side_out, *sems)
                side.finish(side_in, side_out, *sems)

    def whole(off):
        return pl.BlockSpec((S, LANES), lambda h, j: (0, off + h))

    kv_blk = pl.BlockSpec((tq, LANES), lambda h, j: (j, h))
    c_blk = pl.BlockSpec((None, None, 1, tq), lambda h, j: (h, j, 0, 0))
    in_specs = [whole(0),
                pl.BlockSpec((tq, LANES), lambda h, j: (j, nhb + h)),
                pl.BlockSpec((tq, LANES), lambda h, j: (j, 2 * nhb + h)),
                whole(0), whole(0), c_blk]
    out_specs = [whole(0), kv_blk, kv_blk, whole(0), c_blk]
    out_shape = [jax.ShapeDtypeStruct((S, nhb * LANES), F32),
                 jax.ShapeDtypeStruct((S, nhb * LANES), BF16),
                 jax.ShapeDtypeStruct((S, nhb * LANES), BF16),
                 jax.ShapeDtypeStruct((S, nhb * LANES), F32),
                 jax.ShapeDtypeStruct((nhb, nq, 1, tq), F32)]
    args = (qkv, qkv, qkv, do, stats, ck)
    if side is None:
        return pl.pallas_call(
            body, name="fox_bwd", grid=(nhb, nq), in_specs=in_specs, out_specs=out_specs, out_shape=out_shape,
            compiler_params=_params(("parallel", "arbitrary")),
        )(*args)
    res = pl.pallas_call(
        body, name="fox_bwd_xchg", grid=(nhb, nq), in_specs=in_specs + [_ANY] * n_side,
        out_specs=out_specs + [_ANY] * n_side, out_shape=out_shape + side.outs,
        scratch_shapes=_side_scratch(side), compiler_params=_params(("arbitrary", "arbitrary")),
    )(*args, *side.ins)
    return (*res[:5], res[5:])


def _merge(up_a, up_b, h, b_gate, *, g_off, D, S):
    tr, tc = _tile(S, 512, 8), _tile(D, 512)

    def fn(i, ins, outs):
        ua, ub, la, lb, ba, bb = (r[...] for r in ins)
        outs[0][...] = (_sigmoid(la + ba) * ua + _sigmoid(lb + bb) * ub).astype(BF16)

    ins = [(up_a, "tile", 0), (up_b, "tile", 0), (h, "tile", g_off), (h, "tile", g_off + D),
           (b_gate, "row", 0), (b_gate, "row", D)]
    return _ew(fn, ins, [((S, D), BF16, "tile", 0)], rows=S, cols=D, tr=tr, tc=tc, name="merge")[0]


def _acc_rows(i, ref, val):
    @pl.when(i == 0)
    def _():
        ref[...] = val

    @pl.when(i > 0)
    def _():
        ref[...] += val


def _merge_bwd(dmix, up_a, up_b, h, b_gate, *, g_off, D, S):
    tr, tc = _tile(S, 512, 8), _tile(D, 512)

    def fn(i, ins, outs):
        dm, ua, ub, la, lb, ba, bb = (r[...] for r in ins)
        ga, gb = _sigmoid(la + ba), _sigmoid(lb + bb)
        outs[0][...] = (ga * dm).astype(BF16)
        outs[1][...] = (gb * dm).astype(BF16)
        dla = ua * dm * (ga * (1.0 - ga))
        dlb = ub * dm * (gb * (1.0 - gb))
        outs[2][...] = dla.astype(BF16)
        outs[3][...] = dlb.astype(BF16)
        _acc_rows(i, outs[4], jnp.sum(dla, axis=0, keepdims=True))
        _acc_rows(i, outs[5], jnp.sum(dlb, axis=0, keepdims=True))

    ins = [(dmix, "tile", 0), (up_a, "tile", 0), (up_b, "tile", 0), (h, "tile", g_off), (h, "tile", g_off + D),
           (b_gate, "row", 0), (b_gate, "row", D)]
    outs = [((S, D), BF16, "tile", 0)] * 4 + [((1, D), F32, "acc", 0)] * 2
    return _ew(fn, ins, outs, rows=S, cols=D, tr=tr, tc=tc, name="merge_bwd")


def _deepnorm(x, y, g, b, *, alpha, D, S):
    tr = _tile(S, 256, 8)

    def fn(i, ins, outs):
        z = alpha * ins[0][...] + ins[1][...]
        mu = jnp.mean(z, axis=1, keepdims=True)
        zc = z - mu
        var = jnp.mean(zc * zc, axis=1, keepdims=True)
        xn = zc * lax.rsqrt(var + LN_EPS) * ins[2][...] + ins[3][...]
        outs[0][...] = xn
        outs[1][...] = xn.astype(BF16)
        outs[2][...] = z

    ins = [(x, "tile", 0), (y, "tile", 0), (g, "row", 0), (b, "row", 0)]
    outs = [((S, D), F32, "tile", 0), ((S, D), BF16, "tile", 0), ((S, D), F32, "tile", 0)]
    return _ew(fn, ins, outs, rows=S, cols=D, tr=tr, tc=D, name="deepnorm")


def _deepnorm_bwd(dxn, z, g, *, D, S):
    tr = _tile(S, 256, 8)

    def fn(i, ins, outs):
        dx, zv = ins[0][...], ins[1][...]
        mu = jnp.mean(zv, axis=1, keepdims=True)
        zc = zv - mu
        rstd = lax.rsqrt(jnp.mean(zc * zc, axis=1, keepdims=True) + LN_EPS)
        u = zc * rstd
        du = dx * ins[2][...]
        dz = rstd * (du - jnp.mean(du, axis=1, keepdims=True) - u * jnp.mean(du * u, axis=1, keepdims=True))
        outs[0][...] = dz
        outs[1][...] = dz.astype(BF16)
        _acc_rows(i, outs[2], jnp.sum(dx * u, axis=0, keepdims=True))
        _acc_rows(i, outs[3], jnp.sum(dx, axis=0, keepdims=True))

    ins = [(dxn, "tile", 0), (z, "tile", 0), (g, "row", 0)]
    outs = [((S, D), F32, "tile", 0), ((S, D), BF16, "tile", 0), ((1, D), F32, "acc", 0), ((1, D), F32, "acc", 0)]
    return _ew(fn, ins, outs, rows=S, cols=D, tr=tr, tc=D, name="deepnorm_bwd")


def _loss_head(y, target, *, D, S):
    tr = _tile(S, 256, 8)

    def body(y_ref, t_ref, d_ref, l_ref):
        err = y_ref[...] - t_ref[...]
        d_ref[...] = err * (1.0 / D)
        part = 0.5 * jnp.sum(jnp.sum(err * err, axis=1, keepdims=True) * (1.0 / D), axis=0, keepdims=True)
        _acc_rows(pl.program_id(0), l_ref, jnp.broadcast_to(part, (1, LANES)))

    return pl.pallas_call(
        body, name="loss_head", grid=(S // tr,),
        in_specs=[pl.BlockSpec((tr, D), lambda i: (i, 0))] * 2,
        out_specs=[pl.BlockSpec((tr, D), lambda i: (i, 0)), pl.BlockSpec((1, LANES), lambda i: (0, 0))],
        out_shape=[jax.ShapeDtypeStruct((S, D), F32), jax.ShapeDtypeStruct((1, LANES), F32)],
        compiler_params=_params(("arbitrary",)),
    )(y, target)


_MESH = pl.DeviceIdType.MESH
_ANY = pl.BlockSpec(memory_space=pl.ANY)


class _Gather:
    def __init__(self, xs):
        self.ins = list(xs)
        self.outs = [jax.ShapeDtypeStruct((NDEV,) + x.shape, x.dtype) for x in xs]

    def _ctx(self, a, x_ref, out_ref, send_sems, recv_sems):
        x, y, c = lax.axis_index("x"), lax.axis_index("y"), lax.axis_index("c")
        me, sibling = (x, y, c), (x, y, 1 - c)
        chips = [(1 - x, y), (x, 1 - y), (1 - x, 1 - y)]

        def slot(px, py, pc):
            return out_ref.at[4 * px + 2 * py + pc]

        def copy(k, block, to, src=None):
            return pltpu.make_async_remote_copy(
                src_ref=slot(*block) if src is None else src, dst_ref=slot(*block),
                send_sem=send_sems.at[7 * a + k], recv_sem=recv_sems.at[7 * a + k], device_id=to, device_id_type=_MESH)

        def first():
            return [copy(0, me, sibling, src=x_ref)] + [copy(1 + j, me, (*chip, c), src=x_ref)
                                                        for j, chip in enumerate(chips)]

        def passed():
            return [copy(4 + j, (*chip, c), sibling) for j, chip in enumerate(chips)]

        return c, me, sibling, chips, slot, copy, first, passed

    def start(self, in_refs, out_refs, send_sems, recv_sems, local_sems):
        for a, (x_ref, out_ref) in enumerate(zip(in_refs, out_refs)):
            _, me, _, _, slot, _, first, _ = self._ctx(a, x_ref, out_ref, send_sems, recv_sems)
            pltpu.make_async_copy(x_ref, slot(*me), local_sems.at[a]).start()
            for cp in first():
                cp.start()

    def middle(self, in_refs, out_refs, send_sems, recv_sems, local_sems):
        for a, (x_ref, out_ref) in enumerate(zip(in_refs, out_refs)):
            c, me, _, chips, _, copy, _, passed = self._ctx(a, x_ref, out_ref, send_sems, recv_sems)
            for j, (chip, cp) in enumerate(zip(chips, passed())):
                copy(1 + j, (*chip, c), me).wait_recv()
                cp.start()

    def finish(self, in_refs, out_refs, send_sems, recv_sems, local_sems):
        for a, (x_ref, out_ref) in enumerate(zip(in_refs, out_refs)):
            c, me, sibling, chips, slot, copy, first, passed = self._ctx(a, x_ref, out_ref, send_sems, recv_sems)
            copy(0, sibling, me).wait_recv()
            for j, chip in enumerate(chips):
                copy(4 + j, (*chip, 1 - c), me).wait_recv()
            for cp in first() + passed():
                cp.wait_send()
            pltpu.make_async_copy(x_ref, slot(*me), local_sems.at[a]).wait()


class _Exchange:
    def __init__(self, gs):
        self.ins = list(gs)
        self.outs = [jax.ShapeDtypeStruct(g.shape, g.dtype) for g in gs]

    def _copies(self, a, g_ref, out_ref, send_sems, recv_sems, arrivals=True):
        x, y, c = lax.axis_index("x"), lax.axis_index("y"), lax.axis_index("c")
        me = 4 * x + 2 * y + c
        out, back = [], []
        for k in range(1, NDEV):
            px, py, pc = x ^ ((k >> 2) & 1), y ^ ((k >> 1) & 1), c ^ (k & 1)
            peer = 4 * px + 2 * py + pc
            sems = dict(send_sem=send_sems.at[7 * a + k - 1], recv_sem=recv_sems.at[7 * a + k - 1],
                        device_id=(px, py, pc), device_id_type=_MESH)
            out.append(pltpu.make_async_remote_copy(src_ref=g_ref.at[peer], dst_ref=out_ref.at[me], **sems))
            if arrivals:
                back.append(pltpu.make_async_remote_copy(src_ref=g_ref.at[me], dst_ref=out_ref.at[peer], **sems))
        return me, out, back

    def start(self, in_refs, out_refs, send_sems, recv_sems, local_sems):
        for a, (g_ref, out_ref) in enumerate(zip(in_refs, out_refs)):
            me, out, _ = self._copies(a, g_ref, out_ref, send_sems, recv_sems, arrivals=False)
            pltpu.make_async_copy(g_ref.at[me], out_ref.at[me], local_sems.at[a]).start()
            for cp in out:
                cp.start()

    def middle(self, in_refs, out_refs, send_sems, recv_sems, local_sems):
        pass

    def finish(self, in_refs, out_refs, send_sems, recv_sems, local_sems):
        for a, (g_ref, out_ref) in enumerate(zip(in_refs, out_refs)):
            me, out, back = self._copies(a, g_ref, out_ref, send_sems, recv_sems)
            for cp in back:
                cp.wait_recv()
            for cp in out:
                cp.wait_send()
            pltpu.make_async_copy(g_ref.at[me], out_ref.at[me], local_sems.at[a]).wait()


def _side_scratch(side):
    n = len(side.ins)
    return [pltpu.SemaphoreType.DMA((7 * n,)), pltpu.SemaphoreType.DMA((7 * n,)), pltpu.SemaphoreType.DMA((n,))]


def _run_side(side, *, name):
    n = len(side.ins)

    def body(*refs):
        ins, outs, sems = refs[:n], refs[n:2 * n], refs[2 * n:]
        side.start(ins, outs, *sems)
        side.middle(ins, outs, *sems)
        side.finish(ins, outs, *sems)

    return pl.pallas_call(
        body, name=name, out_shape=side.outs, in_specs=[_ANY] * n, out_specs=[_ANY] * n,
        scratch_shapes=_side_scratch(side),
    )(*side.ins)


def _all_gather(x, *, name):
    return _run_side(_Gather([x]), name=name)[0]


def _sum_slots(parts, *, name):
    n, R, C = parts.shape
    tr, tc = _tile(R, 64, 16), _tile(C, 2048)

    def body(p_ref, o_ref):
        acc = p_ref[0].astype(F32)
        for k in range(1, n):
            acc = acc + p_ref[k].astype(F32)
        o_ref[...] = acc

    return pl.pallas_call(
        body, name=name, grid=(R // tr, C // tc),
        in_specs=[pl.BlockSpec((n, tr, tc), lambda i, j: (0, i, j))],
        out_specs=pl.BlockSpec((tr, tc), lambda i, j: (i, j)),
        out_shape=jax.ShapeDtypeStruct((R, C), F32),
        compiler_params=_params(("parallel", "parallel")),
    )(parts)


def _adamw(w, g, m, v, *, name):
    R, C = w.shape
    budget = 2 << 20
    tr = R if R * C * 4 <= budget else _tile(R, max(8, (budget // (C * 4)) // 8 * 8), 8)

    def body(w_ref, g_ref, m_ref, v_ref, d_ref, nm_ref, nv_ref):
        gv = g_ref[...]
        nm = ADAM_B1 * m_ref[...] + (1.0 - ADAM_B1) * gv
        nv = ADAM_B2 * v_ref[...] + (1.0 - ADAM_B2) * (gv * gv)
        m_hat = nm / (1.0 - ADAM_B1 ** ADAM_STEP)
        v_hat = nv / (1.0 - ADAM_B2 ** ADAM_STEP)
        d_ref[...] = -ADAM_LR * (m_hat / (jnp.sqrt(v_hat) + ADAM_EPS) + ADAM_WD * w_ref[...])
        nm_ref[...] = nm
        nv_ref[...] = nv

    spec = pl.BlockSpec((tr, C), lambda i: (i, 0))
    return pl.pallas_call(
        body, name=name, grid=(R // tr,), in_specs=[spec] * 4, out_specs=[spec] * 3,
        out_shape=[jax.ShapeDtypeStruct((R, C), F32)] * 3,
        compiler_params=_params(("parallel",)),
    )(w, g, m, v)


def _pad_cols(w, *, nq, ng):
    nf = w.shape[-1] - nq - ng
    pad = jnp.zeros(w.shape[:-1] + (FORGET_PAD - nf,), w.dtype)
    return jnp.concatenate([w[..., :nq], w[..., nq + nf:], w[..., nq:nq + nf], pad], axis=-1)


def _unpad_cols(w, *, nq, ng, nf):
    return jnp.concatenate([w[..., :nq], w[..., nq + ng:nq + ng + nf], w[..., nq:nq + ng]], axis=-1)


def kernel(x, w_in, b_forget, b_gate, w_up_a, w_up_b, w_out, ln_g, ln_b, loss_target, m_w_in, m_b_forget, m_b_gate, m_w_up_a, m_w_up_b, m_w_out, m_ln_g, m_ln_b, v_w_in, v_b_forget, v_b_gate, v_w_up_a, v_w_up_b, v_w_out, v_ln_g, v_ln_b):
    depth = w_in.shape[0]
    _, S, D = x.shape
    WA, WB = w_up_a.shape[1], w_up_b.shape[1]
    nha, nhb = WA // HEAD_DIM, WB // HEAD_DIM
    assert nhb == b_forget.shape[1] and WA == WB and nhb <= HEAD_ROWS
    nq, ng, nf = 4 * WA + 4 * WB, 2 * D, nhb
    assert w_in.shape[2] == nq + nf + ng
    ncp = nq + ng + FORGET_PAD
    off_va, off_za, off_b, off_zb, off_g, off_f = 2 * WA, 3 * WA, 4 * WA, 4 * WA + 3 * WB, nq, nq + ng
    alpha = float((2 * depth) ** 0.25)
    dsh = D // NDEV
    tq_f, tq_b = _tile(S, FOX_TILE_FWD), _tile(S, FOX_TILE_BWD)

    x0 = x[0]
    target = loss_target[0]
    cos, sin = _rope_tables(S)

    w_in_p = _pad_cols(w_in, nq=nq, ng=ng).astype(BF16)
    w_up_a_t = jnp.swapaxes(w_up_a, 1, 2).astype(BF16)
    w_up_b_t = jnp.swapaxes(w_up_b, 1, 2).astype(BF16)
    w_out_c = w_out.astype(BF16)
    W_in, W_ua, W_ub, W_out = ([None] * depth for _ in range(4))

    def others(l):
        return [w_up_a_t[l], w_up_b_t[l], w_out_c[l]]

    def keep_others(l, got):
        W_ua[l], W_ub[l], W_out[l] = got[0].reshape(D, WA), got[1].reshape(D, WB), got[2].reshape(D, D)

    W_in[0] = _run_side(_Gather([w_in_p[0]]), name="gather_w_in_0")[0].reshape(D, ncp)

    bf_rows = [jnp.broadcast_to(jnp.pad(b_forget[l], (0, HEAD_ROWS - nhb))[:, None], (HEAD_ROWS, LANES))
               for l in range(depth)]

    saved = []
    xf, xb = x0, x0.astype(BF16)
    for l in range(depth):
        wanted = (others(0) if l == 0 else []) + ([w_in_p[l + 1]] + others(l + 1) if l + 1 < depth else [])
        if wanted:
            h, got = _mm(xb, W_in[l], name="in_proj_gather", tm=1024, tn=1280, tk=2048, side=_Gather(wanted))
            if l == 0:
                keep_others(0, got[:3])
            if l + 1 < depth:
                W_in[l + 1] = got[-4].reshape(D, ncp)
                keep_others(l + 1, got[-3:])
        else:
            h = _mm(xb, W_in[l], name="in_proj", tm=1024, tn=1280, tk=2048)
        qk_a = _prep_a(h, cos, sin, W=WA, S=S)
        qkv_b = _prep_b(h, off=off_b, W=WB, S=S)
        out_a, lse_a, ga = _band_fwd(qk_a, h, nha=nha, S=S, v_off=off_va, z_off=off_za)
        f3 = _lane_blocks(jnp.pad(h[:, off_f:off_f + nhb].T, ((0, HEAD_ROWS - nhb), (0, 0))))
        c = _lane_unblocks(_forget_cumsum(f3, bf_rows[l]))[:nhb]
        out_b, lq_b, gb = _fox_fwd(qkv_b, c.reshape(nhb, S // tq_f, 1, tq_f), h, z_off=off_zb, nhb=nhb, S=S, tq=tq_f)
        ck = c.reshape(nhb, S // tq_b, 1, tq_b)
        up_a = _mm(ga, W_ua[l], tb=True, name="up_a", tm=1024, tn=1024, tk=1536)
        up_b = _mm(gb, W_ub[l], tb=True, name="up_b", tm=1024, tn=1024, tk=1536)
        bg = b_gate[l][None, :]
        mix = _merge(up_a, up_b, h, bg, g_off=off_g, D=D, S=S)
        y = _mm(mix, W_out[l], name="out_proj", tm=1024, tn=1024, tk=2048)
        xn_f, xn_b, z = _deepnorm(xf, y, ln_g[l][None, :], ln_b[l][None, :], alpha=alpha, D=D, S=S)
        saved.append(dict(xb=xb, h=h, qk_a=qk_a, qkv_b=qkv_b, out_a=out_a, lse_a=lse_a, ga=ga, f3=f3, ck=ck,
                          out_b=out_b, lq_b=lq_b, gb=gb, up_a=up_a, up_b=up_b, mix=mix, z=z, bg=bg))
        xf, xb = xn_f, xn_b

    dx, loss_row = _loss_head(xf, target, D=D, S=S)

    recv = [None] * depth
    small = [None] * depth
    pending = None
    for l in reversed(range(depth)):
        sv = saved[l]
        h = sv["h"]
        dz, dzb, d_lng, d_lnb = _deepnorm_bwd(dx, sv["z"], ln_g[l][None, :], D=D, S=S)
        g_w_out = _mm(sv["mix"], dzb, ta=True, name="g_w_out", tm=512, tn=1024, tk=4096, out_dtype=BF16)
        dmix = _mm(dzb, W_out[l], tb=True, name="d_mix", tm=1024, tn=1024, tk=2048)
        dup_a, dup_b, dgl_a, dgl_b, d_bga, d_bgb = _merge_bwd(dmix, sv["up_a"], sv["up_b"], h, sv["bg"],
                                                              g_off=off_g, D=D, S=S)
        g_w_ua = _mm(dup_a, sv["ga"], ta=True, name="g_w_up_a", tm=512, tn=1536, tk=4096, out_dtype=BF16)
        g_w_ub = _mm(dup_b, sv["gb"], ta=True, name="g_w_up_b", tm=512, tn=1536, tk=4096, out_dtype=BF16)
        dga = _mm(dup_a, W_ua[l], name="d_ga", tm=1024, tn=1536, tk=2048)
        dgb = _mm(dup_b, W_ub[l], name="d_gb", tm=1024, tn=1536, tk=2048)
        do_a, dza, stats_a = _gate_bwd(dga, sv["out_a"], h, sv["lse_a"], z_off=off_za, W=WA, S=S,
                                       name="gate_bwd_a", do_dtype=F32)
        do_b, dzb_, stats_b = _gate_bwd(dgb, sv["out_b"], h, sv["lq_b"], z_off=off_zb, W=WB, S=S,
                                        name="gate_bwd_b", do_dtype=BF16)
        dqa, dka, dva = _rope_bwd(*_band_bwd(sv["qk_a"], h, do_a, stats_a, nha=nha, S=S, v_off=off_va),
                                  cos, sin, W=WA, S=S)
        if pending is None:
            dqb, dkb, dvb, dcq, dck = _fox_bwd(sv["qkv_b"], do_b, stats_b, sv["ck"], nhb=nhb, S=S, tq=tq_b)
        else:
            dqb, dkb, dvb, dcq, dck, got = _fox_bwd(sv["qkv_b"], do_b, stats_b, sv["ck"], nhb=nhb, S=S, tq=tq_b,
                                                    side=_Exchange([pending]))
            recv[l + 1] = (got[0], *recv[l + 1])
        head_rows = lambda t: _lane_blocks(jnp.pad(t, ((0, HEAD_ROWS - nhb), (0, 0))))
        df3, d_bf = _forget_bwd(head_rows(dcq[:, ::LANES].T), head_rows(dck.reshape(nhb, S)), sv["f3"], bf_rows[l])
        dfl = jnp.pad(_lane_unblocks(df3)[:nhb].T, ((0, 0), (0, FORGET_PAD - nhb))).astype(BF16)
        dh = jnp.concatenate([dqa, dka, dva, dza, dqb.astype(BF16), dkb, dvb, dzb_, dgl_a, dgl_b, dfl], axis=1)
        chunks = lambda t: t.reshape(NDEV, dsh, t.shape[1])
        g_w_in, got3 = _mm(sv["xb"], dh, ta=True, name="g_w_in_xchg", tm=512, tn=1280, tk=4096, out_dtype=BF16,
                           side=_Exchange([chunks(g_w_ua), chunks(g_w_ub), chunks(g_w_out)]))
        if l > 0:
            dx = _mm(dh, W_in[l], tb=True, name="d_x", tm=1024, tn=512, tk=3328, add=dz, add_scale=alpha)
            pending, recv[l] = chunks(g_w_in), tuple(got3)
        else:
            dx, got1 = _mm(dh, W_in[l], tb=True, name="d_x_xchg", tm=1024, tn=512, tk=3328, add=dz, add_scale=alpha,
                           side=_Exchange([chunks(g_w_in)]))
            recv[l] = (got1[0], *got3)
        small[l] = jnp.concatenate([d_bf[:nhb, 0], jnp.zeros((LANES - nhb,), F32), d_bga[0], d_bgb[0],
                                    d_lng[0], d_lnb[0]])

    n_small = LANES + 2 * D + 2 * D
    flat = jnp.concatenate(small + [loss_row[0]])
    rows = -(-flat.shape[0] // (8 * LANES)) * 8
    flat = jnp.pad(flat, (0, rows * LANES - flat.shape[0])).reshape(rows, LANES)
    tot = _sum_slots(_all_gather(flat, name="gather_small"), name="sum_small").reshape(-1)
    loss = tot[depth * n_small]
    sm = tot[:depth * n_small].reshape(depth, n_small)
    g_bf, g_bg = sm[:, :nhb], sm[:, LANES:LANES + 2 * D]
    g_lg, g_lb = sm[:, LANES + 2 * D:LANES + 3 * D], sm[:, LANES + 3 * D:]

    g_in = jnp.stack([_unpad_cols(_sum_slots(recv[l][0], name="sum_w_in"), nq=nq, ng=ng, nf=nf) for l in range(depth)])
    g_ua = jnp.stack([_sum_slots(recv[l][1], name="sum_w_up_a").T for l in range(depth)])
    g_ub = jnp.stack([_sum_slots(recv[l][2], name="sum_w_up_b").T for l in range(depth)])
    g_out = jnp.stack([_sum_slots(recv[l][3], name="sum_w_out") for l in range(depth)])

    def adam(w, g, m, v, name):
        shp = w.shape
        d_, m_, v_ = _adamw(*[t.reshape(-1, shp[-1]) for t in (w, g, m, v)], name=name)
        return d_.reshape(shp), m_.reshape(shp), v_.reshape(shp)

    grads = [g_in, g_bf, g_bg, g_ua, g_ub, g_out, g_lg, g_lb]
    ws = [w_in, b_forget, b_gate, w_up_a, w_up_b, w_out, ln_g, ln_b]
    ms = [m_w_in, m_b_forget, m_b_gate, m_w_up_a, m_w_up_b, m_w_out, m_ln_g, m_ln_b]
    vs = [v_w_in, v_b_forget, v_b_gate, v_w_up_a, v_w_up_b, v_w_out, v_ln_g, v_ln_b]
    names = ["w_in", "b_forget", "b_gate", "w_up_a", "w_up_b", "w_out", "ln_g", "ln_b"]
    upd = [adam(w, g, m, v, f"adamw_{n}") for w, g, m, v, n in zip(ws, grads, ms, vs, names)]
    return (loss, dx[None], *grads, *[u[0] for u in upd], *[u[1] for u in upd], *[u[2] for u in upd])
```

```python
import jax
import jax.numpy as jnp
from jax import lax
from jax.experimental import pallas as pl
from jax.experimental.pallas import tpu as pltpu

F32 = jnp.float32
BF16 = jnp.bfloat16

NDEV = 8
HEAD_DIM = 128
LANES = 128
Q_BLOCK = 128
DILATIONS = (1, 4, 16)
ROPE_THETA = 10000.0
LN_EPS = 1e-5
ATT_SCALE = HEAD_DIM ** -0.5
NEG = -1e30
FORGET_PAD = 256
BAND_CHUNK = 2048
BAND_UNROLL = 16
FOX_TILE_FWD = 1024
FOX_TILE_BWD = 1024
FOX_SUB = 2
FOX_DIAG_SUB = 4
HEAD_ROWS = 16
ADAM_LR, ADAM_B1, ADAM_B2, ADAM_EPS, ADAM_WD, ADAM_STEP = 0.001, 0.9, 0.999, 1e-08, 0.01, 10
VMEM_LIMIT = 56 * 1024 * 1024


def _tile(n, pref, mult=LANES):
    if n <= pref:
        return n
    t = (pref // mult) * mult
    while t >= mult:
        if n % t == 0:
            return t
        t -= mult
    return n


def _params(sem=None):
    return pltpu.CompilerParams(dimension_semantics=sem, vmem_limit_bytes=VMEM_LIMIT)


def _sigmoid(z):
    return 1.0 / (1.0 + jnp.exp(-z))


_NT = (((1,), (1,)), ((), ()))
_NN = (((1,), (0,)), ((), ()))
_TN = (((0,), (0,)), ((), ()))


def _dot(a, b, dims):
    return lax.dot_general(a, b, dims, preferred_element_type=F32)


def _mm(a, b, *, name, ta=False, tb=False, out_dtype=F32, tm=512, tn=512, tk=512, add=None, add_scale=1.0,
        side=None):
    if ta:
        K, M = a.shape
    else:
        M, K = a.shape
    if tb:
        N, K2 = b.shape
    else:
        K2, N = b.shape
    assert K == K2, (a.shape, b.shape, ta, tb)
    tm, tn, tk = _tile(M, tm), _tile(N, tn), _tile(K, tk)
    ni, nj, nk = M // tm, N // tn, K // tk
    dims = (((0 if ta else 1,), (1 if tb else 0,)), ((), ()))
    n_side = 0 if side is None else len(side.ins)
    n_in = 2 + (add is not None)

    def body(*refs):
        a_ref, b_ref = refs[:2]
        add_ref = refs[2] if add is not None else None
        side_in = refs[n_in:n_in + n_side]
        o_ref = refs[n_in + n_side]
        side_out = refs[n_in + n_side + 1:n_in + 2 * n_side + 1]
        acc_ref = refs[n_in + 2 * n_side + 1]
        sems = refs[n_in + 2 * n_side + 2:]
        k = pl.program_id(2)
        if side is not None:
            @pl.when((pl.program_id(0) == 0) & (pl.program_id(1) == 0) & (k == 0))
            def _():
                side.start(side_in, side_out, *sems)

        part = _dot(a_ref[...].astype(BF16), b_ref[...].astype(BF16), dims)

        @pl.when(k == 0)
        def _():
            acc_ref[...] = part

        @pl.when(k > 0)
        def _():
            acc_ref[...] += part

        @pl.when(k == nk - 1)
        def _():
            r = acc_ref[...]
            if add_ref is not None:
                r = r + add_scale * add_ref[...]
            o_ref[...] = r.astype(out_dtype)

        if side is not None:
            step = (pl.program_id(0) * nj + pl.program_id(1)) * nk + k
            steps = ni * nj * nk

            @pl.when(step == max((3 * steps) // 4 - 1, 0))
            def _():
                side.middle(side_in, side_out, *sems)

            @pl.when(step == steps - 1)
            def _():
                side.finish(side_in, side_out, *sems)

    a_spec = pl.BlockSpec((tk, tm), lambda i, j, k: (k, i)) if ta else pl.BlockSpec((tm, tk), lambda i, j, k: (i, k))
    b_spec = pl.BlockSpec((tn, tk), lambda i, j, k: (j, k)) if tb else pl.BlockSpec((tk, tn), lambda i, j, k: (k, j))
    o_spec = pl.BlockSpec((tm, tn), lambda i, j, k: (i, j))
    in_specs, args = [a_spec, b_spec], [a, b]
    if add is not None:
        in_specs.append(o_spec)
        args.append(add)
    out_shape = jax.ShapeDtypeStruct((M, N), out_dtype)
    scratch = [pltpu.VMEM((tm, tn), F32)]
    if side is None:
        return pl.pallas_call(
            body, name=name, grid=(ni, nj, nk), in_specs=in_specs, out_specs=o_spec, out_shape=out_shape,
            scratch_shapes=scratch, compiler_params=_params(("parallel", "parallel", "arbitrary")),
        )(*args)
    res = pl.pallas_call(
        body, name=name, grid=(ni, nj, nk), in_specs=in_specs + [_ANY] * n_side,
        out_specs=[o_spec] + [_ANY] * n_side, out_shape=[out_shape] + side.outs,
        scratch_shapes=scratch + _side_scratch(side),
        compiler_params=_params(("arbitrary", "arbitrary", "arbitrary")),
    )(*args, *side.ins)
    return res[0], res[1:]


def _ew(fn, ins, outs, *, rows, cols, tr, tc, name):
    assert rows % tr == 0 and cols % tc == 0

    def spec(kind, off):
        assert off % tc == 0, (off, tc)
        ob = off // tc
        if kind == "tile":
            return pl.BlockSpec((tr, tc), lambda j, i: (i, j + ob))
        if kind == "tab":
            return pl.BlockSpec((tr, LANES), lambda j, i: (i, 0))
        return pl.BlockSpec((1, tc), lambda j, i: (0, j + ob))

    n_in = len(ins)

    def body(*refs):
        fn(pl.program_id(1), refs[:n_in], refs[n_in:])

    return pl.pallas_call(
        body, name=name, grid=(cols // tc, rows // tr),
        in_specs=[spec(k, o) for _, k, o in ins],
        out_specs=[spec(k, o) for _, _, k, o in outs],
        out_shape=[jax.ShapeDtypeStruct(s, d) for s, d, _, _ in outs],
        compiler_params=_params(("parallel", "arbitrary")),
    )(*[a for a, _, _ in ins])


def _heads(t):
    return [t[:, e:e + LANES] for e in range(0, t.shape[1], LANES)]


def _rope_tables(S):
    half = HEAD_DIM // 2
    inv_freq = ROPE_THETA ** (-jnp.arange(half, dtype=F32) / half)
    ang = jnp.arange(S, dtype=jnp.int32).astype(F32)[:, None] * inv_freq[None, :]
    cos, sin = jnp.cos(ang), jnp.sin(ang)
    return jnp.concatenate([cos, cos], axis=1), jnp.concatenate([-sin, sin], axis=1)


def _rope(t, cs, sn):
    return jnp.concatenate([g * cs + pltpu.roll(g, HEAD_DIM // 2, 1) * sn for g in _heads(t)], axis=1)


def _rope_t(t, cs, sn):
    return jnp.concatenate([g * cs + pltpu.roll(g * sn, HEAD_DIM // 2, 1) for g in _heads(t)], axis=1)


def _prep_a(h, cos, sin, *, W, S):
    tr, tc = _tile(S, 1024, 8), _tile(W, 512)

    def fn(i, ins, outs):
        outs[0][...] = _rope(ins[0][...], ins[1][...], ins[2][...])

    return _ew(fn, [(h, "tile", 0), (cos, "tab", 0), (sin, "tab", 0)], [((S, 2 * W), F32, "tile", 0)],
               rows=S, cols=2 * W, tr=tr, tc=tc, name="prep_a")[0]


def _prep_b(h, *, off, W, S):
    tr, tc = _tile(S, 1024, 8), _tile(W, 512)

    def fn(i, ins, outs):
        scale = jnp.where(pl.program_id(0) * tc < W, ATT_SCALE, 1.0).astype(F32)
        outs[0][...] = (ins[0][...] * scale).astype(BF16)

    return _ew(fn, [(h, "tile", off)], [((S, 3 * W), BF16, "tile", 0)], rows=S, cols=3 * W, tr=tr, tc=tc,
               name="prep_b")[0]


def _rope_bwd(dq, dk, dv, cos, sin, *, W, S):
    tr, tc = _tile(S, 1024, 8), _tile(W, 512)

    def fn(i, ins, outs):
        cs, sn = ins[3][...], ins[4][...]
        outs[0][...] = _rope_t(ins[0][...], cs, sn).astype(BF16)
        outs[1][...] = _rope_t(ins[1][...], cs, sn).astype(BF16)
        outs[2][...] = ins[2][...].astype(BF16)

    ins = [(dq, "tile", 0), (dk, "tile", 0), (dv, "tile", 0), (cos, "tab", 0), (sin, "tab", 0)]
    return _ew(fn, ins, [((S, W), BF16, "tile", 0)] * 3, rows=S, cols=W, tr=tr, tc=tc, name="rope_bwd")


def _band_biases():
    dist = (lax.broadcasted_iota(jnp.int32, (Q_BLOCK, 2 * Q_BLOCK), 0)
            - lax.broadcasted_iota(jnp.int32, (Q_BLOCK, 2 * Q_BLOCK), 1))
    first = jnp.where(dist >= 0, 0.0, NEG).astype(F32)
    other = jnp.where((dist + Q_BLOCK >= 0) & (dist <= 0), 0.0, NEG).astype(F32)
    return first, other


def _band_rows(it, c, d, nbc, biases):
    r, n = (0, it) if d == 1 else (it // nbc, it % nbc)
    ng = c * nbc + n
    k0 = pl.multiple_of(jnp.maximum(ng - 1, 0) * Q_BLOCK, Q_BLOCK)
    bias = jnp.where(ng == 0, biases[0], biases[1])
    if d == 1:
        return pl.ds(pl.multiple_of(it * Q_BLOCK, Q_BLOCK), Q_BLOCK), pl.ds(k0, 2 * Q_BLOCK), bias
    return (pl.ds(r + d * Q_BLOCK * n, Q_BLOCK, stride=d), pl.ds(r + d * k0, 2 * Q_BLOCK, stride=d), bias)


def _band_chunk(S):
    ch = min(S, BAND_CHUNK)
    assert S % ch == 0 and ch % (Q_BLOCK * max(DILATIONS)) == 0 and S >= 2 * Q_BLOCK * max(DILATIONS)
    return ch


def _band_fwd(qk, h, *, nha, S, v_off, z_off):
    CH = _band_chunk(S)
    W = nha * LANES
    slab = 256

    def body(q_ref, k_ref, v_ref, z_ref, o_ref, lse_ref, g_ref, *scratch):
        os_refs, ls_refs = scratch[:len(DILATIONS)], scratch[len(DILATIONS):]
        c = pl.program_id(1)
        biases = _band_biases()
        for gi, d in enumerate(DILATIONS):
            nbc = CH // (Q_BLOCK * d)

            def step(it, carry, gi=gi, d=d, nbc=nbc):
                rq, rk, bias = _band_rows(it, c, d, nbc, biases)
                q = q_ref[rq, :].astype(BF16)
                k = k_ref[rk, :].astype(BF16)
                v = v_ref[rk, :].astype(BF16)
                s = _dot(q, k, _NT) * ATT_SCALE + bias
                m = jnp.max(s, axis=1, keepdims=True)
                p = jnp.exp(s - m)
                den = jnp.sum(p, axis=1, keepdims=True)
                os_refs[gi][rq, :] = _dot(p.astype(BF16), v, _NN) / den
                ls_refs[gi][rq, :] = jnp.broadcast_to(m + jnp.log(den), (Q_BLOCK, LANES))
                return carry

            lax.fori_loop(0, CH // Q_BLOCK, step, 0, unroll=BAND_UNROLL)

        def mix(t, carry):
            rows = pl.ds(pl.multiple_of(t * slab, slab), slab)
            a, b, cc = (r[rows, :] for r in ls_refs)
            m = jnp.maximum(jnp.maximum(a, b), cc)
            ea, eb, ec = jnp.exp(a - m), jnp.exp(b - m), jnp.exp(cc - m)
            den = ea + eb + ec
            out = (ea * os_refs[0][rows, :] + eb * os_refs[1][rows, :] + ec * os_refs[2][rows, :]) / den
            z = z_ref[rows, :]
            o_ref[rows, :] = out
            lse_ref[rows, :] = m + jnp.log(den)
            g_ref[rows, :] = (out * (z * _sigmoid(z))).astype(BF16)
            return carry

        lax.fori_loop(0, CH // slab, mix, 0)

    chunk = pl.BlockSpec((CH, LANES), lambda hd, c: (c, hd))
    return pl.pallas_call(
        body, name="band_fwd", grid=(nha, S // CH),
        in_specs=[chunk,
                  pl.BlockSpec((S, LANES), lambda hd, c: (0, nha + hd)),
                  pl.BlockSpec((S, LANES), lambda hd, c: (0, v_off // LANES + hd)),
                  pl.BlockSpec((CH, LANES), lambda hd, c: (c, z_off // LANES + hd))],
        out_specs=[chunk, chunk, chunk],
        out_shape=[jax.ShapeDtypeStruct((S, W), F32), jax.ShapeDtypeStruct((S, W), F32),
                   jax.ShapeDtypeStruct((S, W), BF16)],
        scratch_shapes=[pltpu.VMEM((CH, LANES), F32)] * (2 * len(DILATIONS)),
        compiler_params=_params(("parallel", "arbitrary")),
    )(qk, qk, h, h)


def _band_bwd(qk, h, do, stats, *, nha, S, v_off):
    CH = _band_chunk(S)
    W = nha * LANES

    def body(q_ref, k_ref, v_ref, do_ref, st_ref, dq_ref, dk_ref, dv_ref):
        c = pl.program_id(1)

        @pl.when(c == 0)
        def _():
            dk_ref[...] = jnp.zeros_like(dk_ref)
            dv_ref[...] = jnp.zeros_like(dv_ref)

        biases = _band_biases()
        for gi, d in enumerate(DILATIONS):
            nbc = CH // (Q_BLOCK * d)

            def step(it, carry, gi=gi, d=d, nbc=nbc):
                rq, rk, bias = _band_rows(it, c, d, nbc, biases)
                q = q_ref[rq, :].astype(BF16)
                k = k_ref[rk, :].astype(BF16)
                v = v_ref[rk, :].astype(BF16)
                g = do_ref[rq, :].astype(BF16)
                st = st_ref[rq, :]
                p = jnp.exp(_dot(q, k, _NT) * ATT_SCALE + (bias - st[:, 0:1]))
                ds = (p * (_dot(g, v, _NT) - st[:, 1:2]) * ATT_SCALE).astype(BF16)
                dq = _dot(ds, k, _NN)
                if gi == 0:
                    dq_ref[rq, :] = dq
                else:
                    dq_ref[rq, :] += dq
                dk_ref[rk, :] += _dot(ds, q, _TN)
                dv_ref[rk, :] += _dot(p.astype(BF16), g, _TN)
                return carry

            lax.fori_loop(0, CH // Q_BLOCK, step, 0, unroll=BAND_UNROLL)

    chunk = pl.BlockSpec((CH, LANES), lambda hd, c: (c, hd))
    whole = pl.BlockSpec((S, LANES), lambda hd, c: (0, hd))
    return pl.pallas_call(
        body, name="band_bwd", grid=(nha, S // CH),
        in_specs=[chunk,
                  pl.BlockSpec((S, LANES), lambda hd, c: (0, nha + hd)),
                  pl.BlockSpec((S, LANES), lambda hd, c: (0, v_off // LANES + hd)),
                  chunk, chunk],
        out_specs=[chunk, whole, whole],
        out_shape=[jax.ShapeDtypeStruct((S, W), F32)] * 3,
        compiler_params=_params(("parallel", "arbitrary")),
    )(qk, qk, h, do, stats)


def _gate_bwd(dg, o, h, lane0, *, z_off, W, S, name, do_dtype):
    tr, tc = _tile(S, 512, 8), _tile(W, 512)

    def fn(i, ins, outs):
        g, out, z = ins[0][...], ins[1][...], ins[2][...]
        sg = _sigmoid(z)
        do = g * (z * sg)
        outs[0][...] = do.astype(do_dtype)
        outs[1][...] = (g * out * (sg * (1.0 + z * (1.0 - sg)))).astype(BF16)
        delta = jnp.concatenate([jnp.broadcast_to(jnp.sum(t, axis=1, keepdims=True), t.shape)
                                 for t in _heads(do * out)], axis=1)
        first = lax.broadcasted_iota(jnp.int32, delta.shape, 1) % LANES == 0
        outs[2][...] = jnp.where(first, ins[3][...], delta)

    ins = [(dg, "tile", 0), (o, "tile", 0), (h, "tile", z_off), (lane0, "tile", 0)]
    outs = [((S, W), do_dtype, "tile", 0), ((S, W), BF16, "tile", 0), ((S, W), F32, "tile", 0)]
    return _ew(fn, ins, outs, rows=S, cols=W, tr=tr, tc=tc, name=name)


def _split3(x):
    hi = x.astype(BF16)
    r = x - hi.astype(F32)
    mid = r.astype(BF16)
    lo = (r - mid.astype(F32)).astype(BF16)
    return hi, mid, lo


def _tri_dot(x, tri):
    hi, mid, lo = _split3(x)
    return _dot(hi, tri, _NN) + (_dot(mid, tri, _NN) + _dot(lo, tri, _NN))


def _log1p(u):
    w = 1.0 + u
    return jnp.where(w == 1.0, u, jnp.log(w) * (u / jnp.where(w == 1.0, 1.0, w - 1.0)))


def _lane_blocks(t):
    R, S = t.shape
    return t.reshape(R, S // LANES, LANES).transpose(1, 0, 2)


def _lane_unblocks(t):
    nb, R, _ = t.shape
    return t.transpose(1, 0, 2).reshape(R, nb * LANES)


def _forget_cumsum(f3, bf):
    nb, R, _ = f3.shape

    def body(f_ref, b_ref, c_ref):
        row = lax.broadcasted_iota(jnp.int32, (LANES, LANES), 0)
        colm = lax.broadcasted_iota(jnp.int32, (LANES, LANES), 1)
        tri = (row <= colm).astype(BF16)
        bias = b_ref[...]

        def step(n, carry):
            z = f_ref[n] + bias
            logf = jnp.minimum(z, 0.0) - _log1p(jnp.exp(-jnp.abs(z)))
            c = _tri_dot(logf, tri) + carry
            c_ref[n] = c
            return jnp.broadcast_to(c[:, LANES - 1:LANES], (R, LANES))

        lax.fori_loop(0, nb, step, jnp.zeros((R, LANES), F32))

    return pl.pallas_call(
        body, name="forget_cumsum", out_shape=jax.ShapeDtypeStruct(f3.shape, F32),
        compiler_params=_params(),
    )(f3, bf)


def _forget_bwd(dcq3, dck3, f3, bf):
    nb, R, _ = f3.shape

    def body(dcq_ref, dck_ref, f_ref, b_ref, df_ref, db_ref):
        row = lax.broadcasted_iota(jnp.int32, (LANES, LANES), 0)
        colm = lax.broadcasted_iota(jnp.int32, (LANES, LANES), 1)
        tri = (row >= colm).astype(BF16)
        bias = b_ref[...]

        def step(t, carry):
            tail, tot = carry
            n = nb - 1 - t
            r = _tri_dot(dcq_ref[n] + dck_ref[n], tri) + tail
            df = r * _sigmoid(-(f_ref[n] + bias))
            df_ref[n] = df
            tot = tot + jnp.broadcast_to(jnp.sum(df, axis=1, keepdims=True), (R, LANES))
            return jnp.broadcast_to(r[:, 0:1], (R, LANES)), tot

        _, tot = lax.fori_loop(0, nb, step, (jnp.zeros((R, LANES), F32), jnp.zeros((R, LANES), F32)))
        db_ref[...] = tot

    return pl.pallas_call(
        body, name="forget_bwd",
        out_shape=[jax.ShapeDtypeStruct(f3.shape, F32), jax.ShapeDtypeStruct((R, LANES), F32)],
        compiler_params=_params(),
    )(dcq3, dck3, f3, bf)


def _causal(t):
    return lax.broadcasted_iota(jnp.int32, (t, t), 0) >= lax.broadcasted_iota(jnp.int32, (t, t), 1)


def _fox_fwd(qkv, ck, h, *, z_off, nhb, S, tq):
    nq = S // tq
    ts = td = tq // FOX_SUB
    n_diag = tq // td

    def body(q_ref, k_ref, v_ref, ck_ref, z_ref, o_ref, lq_ref, g_ref):
        i = pl.program_id(1)
        qs = [q_ref[r * ts:(r + 1) * ts, :] for r in range(FOX_SUB)]

        def absorb(q, k, v, ckj, m, l, acc, row0=None):
            s = _dot(q, k, _NT) - ckj
            if row0 is not None:
                rows = row0 + lax.broadcasted_iota(jnp.int32, s.shape, 0)
                s = jnp.where(rows >= lax.broadcasted_iota(jnp.int32, s.shape, 1), s, NEG)
            m_new = jnp.maximum(m, jnp.max(s, axis=1, keepdims=True))
            alpha = jnp.exp(m - m_new)
            p = jnp.exp(s - m_new)
            return m_new, alpha * l + jnp.sum(p, axis=1, keepdims=True), alpha * acc + _dot(p.astype(BF16), v, _NN)

        def step(j, carry):
            off = pl.multiple_of(j * tq, tq)
            k, v, ckj = k_ref[pl.ds(off, tq), :], v_ref[pl.ds(off, tq), :], ck_ref[j]
            return tuple(absorb(qs[r], k, v, ckj, *carry[r]) for r in range(FOX_SUB))

        init = tuple((jnp.full((ts, 1), NEG, F32), jnp.zeros((ts, 1), F32), jnp.zeros((ts, LANES), F32))
                     for _ in range(FOX_SUB))
        carry = lax.fori_loop(0, i, step, init)

        off = pl.multiple_of(i * tq, tq)
        k, v, cki = k_ref[pl.ds(off, tq), :], v_ref[pl.ds(off, tq), :], ck_ref[i]
        for f in range(n_diag):
            rows = slice(f * td, (f + 1) * td)
            part = slice(f * td % ts, f * td % ts + td)
            m, l, acc = (t[part] for t in carry[f * td // ts])
            nk = (f + 1) * td
            m, l, acc = absorb(q_ref[rows, :], k[:nk], v[:nk], cki[:, :nk], m, l, acc, row0=f * td)
            out = acc / l
            z = z_ref[rows, :]
            o_ref[rows, :] = out
            lq_ref[rows, :] = jnp.broadcast_to(m + jnp.log(l), (td, LANES))
            g_ref[rows, :] = (out * (z * _sigmoid(z))).astype(BF16)

    blk = pl.BlockSpec((tq, LANES), lambda hd, i: (i, hd))
    return pl.pallas_call(
        body, name="fox_fwd", grid=(nhb, nq),
        in_specs=[blk,
                  pl.BlockSpec((S, LANES), lambda hd, i: (0, nhb + hd)),
                  pl.BlockSpec((S, LANES), lambda hd, i: (0, 2 * nhb + hd)),
                  pl.BlockSpec((None, nq, 1, tq), lambda hd, i: (hd, 0, 0, 0)),
                  pl.BlockSpec((tq, LANES), lambda hd, i: (i, z_off // LANES + hd))],
        out_specs=[blk, blk, blk],
        out_shape=[jax.ShapeDtypeStruct((S, nhb * LANES), F32), jax.ShapeDtypeStruct((S, nhb * LANES), F32),
                   jax.ShapeDtypeStruct((S, nhb * LANES), BF16)],
        compiler_params=_params(("parallel", "arbitrary")),
    )(qkv, qkv, qkv, ck, h)


def _fox_bwd(qkv, do, stats, ck, *, nhb, S, tq, side=None):
    nq = S // tq
    ts, td = tq // FOX_SUB, tq // FOX_DIAG_SUB
    n_side = 0 if side is None else len(side.ins)

    def body(*refs):
        q_ref, k_ref, v_ref, do_ref, st_ref, ck_ref = refs[:6]
        side_in = refs[6:6 + n_side]
        dq_ref, dk_ref, dv_ref, dr_ref, dc_ref = refs[6 + n_side:11 + n_side]
        side_out, sems = refs[11 + n_side:11 + 2 * n_side], refs[11 + 2 * n_side:]
        j = pl.program_id(1)
        if side is not None:
            @pl.when((pl.program_id(0) == 0) & (j == 0))
            def _():
                side.start(side_in, side_out, *sems)

        @pl.when(j == 0)
        def _():
            dq_ref[...] = jnp.zeros_like(dq_ref)
            dr_ref[...] = jnp.zeros_like(dr_ref)

        k = k_ref[...]
        v = v_ref[...]
        ckv = ck_ref[...]

        def group(start, n, nk, carry, row0=None):
            dk, dv, dc = carry
            rows = pl.ds(pl.multiple_of(start, n), n)
            q, g, st = q_ref[rows, :], do_ref[rows, :], st_ref[rows, :]
            p = jnp.exp(_dot(q, k[:nk], _NT) - ckv[:, :nk] - st[:, 0:1])
            if row0 is not None:
                rr = row0 + lax.broadcasted_iota(jnp.int32, (n, nk), 0)
                p = jnp.where(rr >= lax.broadcasted_iota(jnp.int32, (n, nk), 1), p, 0.0)
            ds = p * (_dot(g, v[:nk], _NT) - st[:, 1:2])
            dr_ref[rows, :] += jnp.broadcast_to(jnp.sum(ds, axis=1, keepdims=True), (n, LANES))
            dsb = ds.astype(BF16)
            dq_ref[rows, :] += _dot(dsb, k[:nk], _NN)
            upd = (_dot(dsb, q, _TN), _dot(p.astype(BF16), g, _TN), -jnp.sum(ds, axis=0, keepdims=True))
            if nk < tq:
                upd = (jnp.concatenate([upd[0], jnp.zeros((tq - nk, LANES), F32)], axis=0),
                       jnp.concatenate([upd[1], jnp.zeros((tq - nk, LANES), F32)], axis=0),
                       jnp.concatenate([upd[2], jnp.zeros((1, tq - nk), F32)], axis=1))
            return dk + upd[0], dv + upd[1], dc + upd[2]

        def step(i, carry):
            for r in range(FOX_SUB):
                carry = group(i * tq + r * ts, ts, tq, carry)
            return carry

        carry = (jnp.zeros((tq, LANES), F32), jnp.zeros((tq, LANES), F32), jnp.zeros((1, tq), F32))
        for f in range(FOX_DIAG_SUB):
            carry = group(j * tq + f * td, td, (f + 1) * td, carry, row0=f * td)
        dk, dv, dc = lax.fori_loop(j + 1, nq, step, carry)
        dk_ref[...] = dk.astype(BF16)
        dv_ref[...] = dv.astype(BF16)
        dc_ref[...] = dc

        @pl.when(j == nq - 1)
        def _():
            dq_ref[...] = dq_ref[...] * ATT_SCALE

        if side is not None:
            @pl.when((pl.program_id(0) == nhb - 1) & (j == nq - 1))
            def _():
                side.middle(side_in, side_out, *sems)
                side.finish(side_in, side_out, *sems)

    def whole(off):
        return pl.BlockSpec((S, LANES), lambda h, j: (0, off + h))

    kv_blk = pl.BlockSpec((tq, LANES), lambda h, j: (j, h))
    c_blk = pl.BlockSpec((None, None, 1, tq), lambda h, j: (h, j, 0, 0))
    in_specs = [whole(0),
                pl.BlockSpec((tq, LANES), lambda h, j: (j, nhb + h)),
                pl.BlockSpec((tq, LANES), lambda h, j: (j, 2 * nhb + h)),
                whole(0), whole(0), c_blk]
    out_specs = [whole(0), kv_blk, kv_blk, whole(0), c_blk]
    out_shape = [jax.ShapeDtypeStruct((S, nhb * LANES), F32),
                 jax.ShapeDtypeStruct((S, nhb * LANES), BF16),
                 jax.ShapeDtypeStruct((S, nhb * LANES), BF16),
                 jax.ShapeDtypeStruct((S, nhb * LANES), F32),
                 jax.ShapeDtypeStruct((nhb, nq, 1, tq), F32)]
    args = (qkv, qkv, qkv, do, stats, ck)
    if side is None:
        return pl.pallas_call(
            body, name="fox_bwd", grid=(nhb, nq), in_specs=in_specs, out_specs=out_specs, out_shape=out_shape,
            compiler_params=_params(("parallel", "arbitrary")),
        )(*args)
    res = pl.pallas_call(
        body, name="fox_bwd_xchg", grid=(nhb, nq), in_specs=in_specs + [_ANY] * n_side,
        out_specs=out_specs + [_ANY] * n_side, out_shape=out_shape + side.outs,
        scratch_shapes=_side_scratch(side), compiler_params=_params(("arbitrary", "arbitrary")),
    )(*args, *side.ins)
    return (*res[:5], res[5:])


def _merge(up_a, up_b, h, b_gate, *, g_off, D, S):
    tr, tc = _tile(S, 512, 8), _tile(D, 512)

    def fn(i, ins, outs):
        ua, ub, la, lb, ba, bb = (r[...] for r in ins)
        outs[0][...] = (_sigmoid(la + ba) * ua + _sigmoid(lb + bb) * ub).astype(BF16)

    ins = [(up_a, "tile", 0), (up_b, "tile", 0), (h, "tile", g_off), (h, "tile", g_off + D),
           (b_gate, "row", 0), (b_gate, "row", D)]
    return _ew(fn, ins, [((S, D), BF16, "tile", 0)], rows=S, cols=D, tr=tr, tc=tc, name="merge")[0]


def _acc_rows(i, ref, val):
    @pl.when(i == 0)
    def _():
        ref[...] = val

    @pl.when(i > 0)
    def _():
        ref[...] += val


def _merge_bwd(dmix, up_a, up_b, h, b_gate, *, g_off, D, S):
    tr, tc = _tile(S, 512, 8), _tile(D, 512)

    def fn(i, ins, outs):
        dm, ua, ub, la, lb, ba, bb = (r[...] for r in ins)
        ga, gb = _sigmoid(la + ba), _sigmoid(lb + bb)
        outs[0][...] = (ga * dm).astype(BF16)
        outs[1][...] = (gb * dm).astype(BF16)
        dla = ua * dm * (ga * (1.0 - ga))
        dlb = ub * dm * (gb * (1.0 - gb))
        outs[2][...] = dla.astype(BF16)
        outs[3][...] = dlb.astype(BF16)
        _acc_rows(i, outs[4], jnp.sum(dla, axis=0, keepdims=True))
        _acc_rows(i, outs[5], jnp.sum(dlb, axis=0, keepdims=True))

    ins = [(dmix, "tile", 0), (up_a, "tile", 0), (up_b, "tile", 0), (h, "tile", g_off), (h, "tile", g_off + D),
           (b_gate, "row", 0), (b_gate, "row", D)]
    outs = [((S, D), BF16, "tile", 0)] * 4 + [((1, D), F32, "acc", 0)] * 2
    return _ew(fn, ins, outs, rows=S, cols=D, tr=tr, tc=tc, name="merge_bwd")


def _deepnorm(x, y, g, b, *, alpha, D, S):
    tr = _tile(S, 256, 8)

    def fn(i, ins, outs):
        z = alpha * ins[0][...] + ins[1][...]
        mu = jnp.mean(z, axis=1, keepdims=True)
        zc = z - mu
        var = jnp.mean(zc * zc, axis=1, keepdims=True)
        xn = zc * lax.rsqrt(var + LN_EPS) * ins[2][...] + ins[3][...]
        outs[0][...] = xn
        outs[1][...] = xn.astype(BF16)
        outs[2][...] = z

    ins = [(x, "tile", 0), (y, "tile", 0), (g, "row", 0), (b, "row", 0)]
    outs = [((S, D), F32, "tile", 0), ((S, D), BF16, "tile", 0), ((S, D), F32, "tile", 0)]
    return _ew(fn, ins, outs, rows=S, cols=D, tr=tr, tc=D, name="deepnorm")


def _deepnorm_bwd(dxn, z, g, *, D, S):
    tr = _tile(S, 256, 8)

    def fn(i, ins, outs):
        dx, zv = ins[0][...], ins[1][...]
        mu = jnp.mean(zv, axis=1, keepdims=True)
        zc = zv - mu
        rstd = lax.rsqrt(jnp.mean(zc * zc, axis=1, keepdims=True) + LN_EPS)
        u = zc * rstd
        du = dx * ins[2][...]
        dz = rstd * (du - jnp.mean(du, axis=1, keepdims=True) - u * jnp.mean(du * u, axis=1, keepdims=True))
        outs[0][...] = dz
        outs[1][...] = dz.astype(BF16)
        _acc_rows(i, outs[2], jnp.sum(dx * u, axis=0, keepdims=True))
        _acc_rows(i, outs[3], jnp.sum(dx, axis=0, keepdims=True))

    ins = [(dxn, "tile", 0), (z, "tile", 0), (g, "row", 0)]
    outs = [((S, D), F32, "tile", 0), ((S, D), BF16, "tile", 0), ((1, D), F32, "acc", 0), ((1, D), F32, "acc", 0)]
    return _ew(fn, ins, outs, rows=S, cols=D, tr=tr, tc=D, name="deepnorm_bwd")


def _loss_head(y, target, *, D, S):
    tr = _tile(S, 256, 8)

    def body(y_ref, t_ref, d_ref, l_ref):
        err = y_ref[...] - t_ref[...]
        d_ref[...] = err * (1.0 / D)
        part = 0.5 * jnp.sum(jnp.sum(err * err, axis=1, keepdims=True) * (1.0 / D), axis=0, keepdims=True)
        _acc_rows(pl.program_id(0), l_ref, jnp.broadcast_to(part, (1, LANES)))

    return pl.pallas_call(
        body, name="loss_head", grid=(S // tr,),
        in_specs=[pl.BlockSpec((tr, D), lambda i: (i, 0))] * 2,
        out_specs=[pl.BlockSpec((tr, D), lambda i: (i, 0)), pl.BlockSpec((1, LANES), lambda i: (0, 0))],
        out_shape=[jax.ShapeDtypeStruct((S, D), F32), jax.ShapeDtypeStruct((1, LANES), F32)],
        compiler_params=_params(("arbitrary",)),
    )(y, target)


_MESH = pl.DeviceIdType.MESH
_ANY = pl.BlockSpec(memory_space=pl.ANY)


class _Gather:
    def __init__(self, xs):
        self.ins = list(xs)
        self.outs = [jax.ShapeDtypeStruct((NDEV,) + x.shape, x.dtype) for x in xs]

    def _ctx(self, a, x_ref, out_ref, send_sems, recv_sems):
        x, y, c = lax.axis_index("x"), lax.axis_index("y"), lax.axis_index("c")
        me, sibling = (x, y, c), (x, y, 1 - c)
        chips = [(1 - x, y), (x, 1 - y), (1 - x, 1 - y)]

        def slot(px, py, pc):
            return out_ref.at[4 * px + 2 * py + pc]

        def copy(k, block, to, src=None):
            return pltpu.make_async_remote_copy(
                src_ref=slot(*block) if src is None else src, dst_ref=slot(*block),
                send_sem=send_sems.at[7 * a + k], recv_sem=recv_sems.at[7 * a + k], device_id=to, device_id_type=_MESH)

        def first():
            return [copy(0, me, sibling, src=x_ref)] + [copy(1 + j, me, (*chip, c), src=x_ref)
                                                        for j, chip in enumerate(chips)]

        def passed():
            return [copy(4 + j, (*chip, c), sibling) for j, chip in enumerate(chips)]

        return c, me, sibling, chips, slot, copy, first, passed

    def start(self, in_refs, out_refs, send_sems, recv_sems, local_sems):
        for a, (x_ref, out_ref) in enumerate(zip(in_refs, out_refs)):
            _, me, _, _, slot, _, first, _ = self._ctx(a, x_ref, out_ref, send_sems, recv_sems)
            pltpu.make_async_copy(x_ref, slot(*me), local_sems.at[a]).start()
            for cp in first():
                cp.start()

    def middle(self, in_refs, out_refs, send_sems, recv_sems, local_sems):
        for a, (x_ref, out_ref) in enumerate(zip(in_refs, out_refs)):
            c, me, _, chips, _, copy, _, passed = self._ctx(a, x_ref, out_ref, send_sems, recv_sems)
            for j, (chip, cp) in enumerate(zip(chips, passed())):
                copy(1 + j, (*chip, c), me).wait_recv()
                cp.start()

    def finish(self, in_refs, out_refs, send_sems, recv_sems, local_sems):
        for a, (x_ref, out_ref) in enumerate(zip(in_refs, out_refs)):
            c, me, sibling, chips, slot, copy, first, passed = self._ctx(a, x_ref, out_ref, send_sems, recv_sems)
            copy(0, sibling, me).wait_recv()
            for j, chip in enumerate(chips):
                copy(4 + j, (*chip, 1 - c), me).wait_recv()
            for cp in first() + passed():
                cp.wait_send()
            pltpu.make_async_copy(x_ref, slot(*me), local_sems.at[a]).wait()


class _Exchange:
    def __init__(self, gs):
        self.ins = list(gs)
        self.outs = [jax.ShapeDtypeStruct(g.shape, g.dtype) for g in gs]

    def _copies(self, a, g_ref, out_ref, send_sems, recv_sems, arrivals=True):
        x, y, c = lax.axis_index("x"), lax.axis_index("y"), lax.axis_index("c")
        me = 4 * x + 2 * y + c
        out, back = [], []
        for k in range(1, NDEV):
            px, py, pc = x ^ ((k >> 2) & 1), y ^ ((k >> 1) & 1), c ^ (k & 1)
            peer = 4 * px + 2 * py + pc
            sems = dict(send_sem=send_sems.at[7 * a + k - 1], recv_sem=recv_sems.at[7 * a + k - 1],
                        device_id=(px, py, pc), device_id_type=_MESH)
            out.append(pltpu.make_async_remote_copy(src_ref=g_ref.at[peer], dst_ref=out_ref.at[me], **sems))
            if arrivals:
                back.append(pltpu.make_async_remote_copy(src_ref=g_ref.at[me], dst_ref=out_ref.at[peer], **sems))
        return me, out, back

    def start(self, in_refs, out_refs, send_sems, recv_sems, local_sems):
        for a, (g_ref, out_ref) in enumerate(zip(in_refs, out_refs)):
            me, out, _ = self._copies(a, g_ref, out_ref, send_sems, recv_sems, arrivals=False)
            pltpu.make_async_copy(g_ref.at[me], out_ref.at[me], local_sems.at[a]).start()
            for cp in out:
                cp.start()

    def middle(self, in_refs, out_refs, send_sems, recv_sems, local_sems):
        pass

    def finish(self, in_refs, out_refs, send_sems, recv_sems, local_sems):
        for a, (g_ref, out_ref) in enumerate(zip(in_refs, out_refs)):
            me, out, back = self._copies(a, g_ref, out_ref, send_sems, recv_sems)
            for cp in back:
                cp.wait_recv()
            for cp in out:
                cp.wait_send()
            pltpu.make_async_copy(g_ref.at[me], out_ref.at[me], local_sems.at[a]).wait()


def _side_scratch(side):
    n = len(side.ins)
    return [pltpu.SemaphoreType.DMA((7 * n,)), pltpu.SemaphoreType.DMA((7 * n,)), pltpu.SemaphoreType.DMA((n,))]


def _run_side(side, *, name):
    n = len(side.ins)

    def body(*refs):
        ins, outs, sems = refs[:n], refs[n:2 * n], refs[2 * n:]
        side.start(ins, outs, *sems)
        side.middle(ins, outs, *sems)
        side.finish(ins, outs, *sems)

    return pl.pallas_call(
        body, name=name, out_shape=side.outs, in_specs=[_ANY] * n, out_specs=[_ANY] * n,
        scratch_shapes=_side_scratch(side),
    )(*side.ins)


def _all_gather(x, *, name):
    return _run_side(_Gather([x]), name=name)[0]


def _sum_slots(parts, *, name):
    n, R, C = parts.shape
    tr, tc = _tile(R, 64, 16), _tile(C, 2048)

    def body(p_ref, o_ref):
        acc = p_ref[0].astype(F32)
        for k in range(1, n):
            acc = acc + p_ref[k].astype(F32)
        o_ref[...] = acc

    return pl.pallas_call(
        body, name=name, grid=(R // tr, C // tc),
        in_specs=[pl.BlockSpec((n, tr, tc), lambda i, j: (0, i, j))],
        out_specs=pl.BlockSpec((tr, tc), lambda i, j: (i, j)),
        out_shape=jax.ShapeDtypeStruct((R, C), F32),
        compiler_params=_params(("parallel", "parallel")),
    )(parts)


def _adamw(w, g, m, v, *, name):
    R, C = w.shape
    budget = 2 << 20
    tr = R if R * C * 4 <= budget else _tile(R, max(8, (budget // (C * 4)) // 8 * 8), 8)

    def body(w_ref, g_ref, m_ref, v_ref, d_ref, nm_ref, nv_ref):
        gv = g_ref[...]
        nm = ADAM_B1 * m_ref[...] + (1.0 - ADAM_B1) * gv
        nv = ADAM_B2 * v_ref[...] + (1.0 - ADAM_B2) * (gv * gv)
        m_hat = nm / (1.0 - ADAM_B1 ** ADAM_STEP)
        v_hat = nv / (1.0 - ADAM_B2 ** ADAM_STEP)
        d_ref[...] = -ADAM_LR * (m_hat / (jnp.sqrt(v_hat) + ADAM_EPS) + ADAM_WD * w_ref[...])
        nm_ref[...] = nm
        nv_ref[...] = nv

    spec = pl.BlockSpec((tr, C), lambda i: (i, 0))
    return pl.pallas_call(
        body, name=name, grid=(R // tr,), in_specs=[spec] * 4, out_specs=[spec] * 3,
        out_shape=[jax.ShapeDtypeStruct((R, C), F32)] * 3,
        compiler_params=_params(("parallel",)),
    )(w, g, m, v)


def _pad_cols(w, *, nq, ng):
    nf = w.shape[-1] - nq - ng
    pad = jnp.zeros(w.shape[:-1] + (FORGET_PAD - nf,), w.dtype)
    return jnp.concatenate([w[..., :nq], w[..., nq + nf:], w[..., nq:nq + nf], pad], axis=-1)


def _unpad_cols(w, *, nq, ng, nf):
    return jnp.concatenate([w[..., :nq], w[..., nq + ng:nq + ng + nf], w[..., nq:nq + ng]], axis=-1)


def kernel(x, w_in, b_forget, b_gate, w_up_a, w_up_b, w_out, ln_g, ln_b, loss_target, m_w_in, m_b_forget, m_b_gate, m_w_up_a, m_w_up_b, m_w_out, m_ln_g, m_ln_b, v_w_in, v_b_forget, v_b_gate, v_w_up_a, v_w_up_b, v_w_out, v_ln_g, v_ln_b):
    depth = w_in.shape[0]
    _, S, D = x.shape
    WA, WB = w_up_a.shape[1], w_up_b.shape[1]
    nha, nhb = WA // HEAD_DIM, WB // HEAD_DIM
    assert nhb == b_forget.shape[1] and WA == WB and nhb <= HEAD_ROWS
    nq, ng, nf = 4 * WA + 4 * WB, 2 * D, nhb
    assert w_in.shape[2] == nq + nf + ng
    ncp = nq + ng + FORGET_PAD
    off_va, off_za, off_b, off_zb, off_g, off_f = 2 * WA, 3 * WA, 4 * WA, 4 * WA + 3 * WB, nq, nq + ng
    alpha = float((2 * depth) ** 0.25)
    dsh = D // NDEV
    tq_f, tq_b = _tile(S, FOX_TILE_FWD), _tile(S, FOX_TILE_BWD)

    x0 = x[0]
    target = loss_target[0]
    cos, sin = _rope_tables(S)

    w_in_p = _pad_cols(w_in, nq=nq, ng=ng).astype(BF16)
    w_up_a_t = jnp.swapaxes(w_up_a, 1, 2).astype(BF16)
    w_up_b_t = jnp.swapaxes(w_up_b, 1, 2).astype(BF16)
    w_out_c = w_out.astype(BF16)
    W_in, W_ua, W_ub, W_out = ([None] * depth for _ in range(4))

    def others(l):
        return [w_up_a_t[l], w_up_b_t[l], w_out_c[l]]

    def keep_others(l, got):
        W_ua[l], W_ub[l], W_out[l] = got[0].reshape(D, WA), got[1].reshape(D, WB), got[2].reshape(D, D)

    W_in[0] = _run_side(_Gather([w_in_p[0]]), name="gather_w_in_0")[0].reshape(D, ncp)

    bf_rows = [jnp.broadcast_to(jnp.pad(b_forget[l], (0, HEAD_ROWS - nhb))[:, None], (HEAD_ROWS, LANES))
               for l in range(depth)]

    saved = []
    xf, xb = x0, x0.astype(BF16)
    for l in range(depth):
        wanted = (others(0) if l == 0 else []) + ([w_in_p[l + 1]] + others(l + 1) if l + 1 < depth else [])
        if wanted:
            h, got = _mm(xb, W_in[l], name="in_proj_gather", tm=1024, tn=1280, tk=2048, side=_Gather(wanted))
            if l == 0:
                keep_others(0, got[:3])
            if l + 1 < depth:
                W_in[l + 1] = got[-4].reshape(D, ncp)
                keep_others(l + 1, got[-3:])
        else:
            h = _mm(xb, W_in[l], name="in_proj", tm=1024, tn=1280, tk=2048)
        qk_a = _prep_a(h, cos, sin, W=WA, S=S)
        qkv_b = _prep_b(h, off=off_b, W=WB, S=S)
        out_a, lse_a, ga = _band_fwd(qk_a, h, nha=nha, S=S, v_off=off_va, z_off=off_za)
        f3 = _lane_blocks(jnp.pad(h[:, off_f:off_f + nhb].T, ((0, HEAD_ROWS - nhb), (0, 0))))
        c = _lane_unblocks(_forget_cumsum(f3, bf_rows[l]))[:nhb]
        out_b, lq_b, gb = _fox_fwd(qkv_b, c.reshape(nhb, S // tq_f, 1, tq_f), h, z_off=off_zb, nhb=nhb, S=S, tq=tq_f)
        ck = c.reshape(nhb, S // tq_b, 1, tq_b)
        up_a = _mm(ga, W_ua[l], tb=True, name="up_a", tm=1024, tn=1024, tk=1536)
        up_b = _mm(gb, W_ub[l], tb=True, name="up_b", tm=1024, tn=1024, tk=1536)
        bg = b_gate[l][None, :]
        mix = _merge(up_a, up_b, h, bg, g_off=off_g, D=D, S=S)
        y = _mm(mix, W_out[l], name="out_proj", tm=1024, tn=1024, tk=2048)
        xn_f, xn_b, z = _deepnorm(xf, y, ln_g[l][None, :], ln_b[l][None, :], alpha=alpha, D=D, S=S)
        saved.append(dict(xb=xb, h=h, qk_a=qk_a, qkv_b=qkv_b, out_a=out_a, lse_a=lse_a, ga=ga, f3=f3, ck=ck,
                          out_b=out_b, lq_b=lq_b, gb=gb, up_a=up_a, up_b=up_b, mix=mix, z=z, bg=bg))
        xf, xb = xn_f, xn_b

    dx, loss_row = _loss_head(xf, target, D=D, S=S)

    recv = [None] * depth
    small = [None] * depth
    pending = None
    for l in reversed(range(depth)):
        sv = saved[l]
        h = sv["h"]
        dz, dzb, d_lng, d_lnb = _deepnorm_bwd(dx, sv["z"], ln_g[l][None, :], D=D, S=S)
        g_w_out = _mm(sv["mix"], dzb, ta=True, name="g_w_out", tm=512, tn=1024, tk=4096, out_dtype=BF16)
        dmix = _mm(dzb, W_out[l], tb=True, name="d_mix", tm=1024, tn=1024, tk=2048)
        dup_a, dup_b, dgl_a, dgl_b, d_bga, d_bgb = _merge_bwd(dmix, sv["up_a"], sv["up_b"], h, sv["bg"],
                                                              g_off=off_g, D=D, S=S)
        g_w_ua = _mm(dup_a, sv["ga"], ta=True, name="g_w_up_a", tm=512, tn=1536, tk=4096, out_dtype=BF16)
        g_w_ub = _mm(dup_b, sv["gb"], ta=True, name="g_w_up_b", tm=512, tn=1536, tk=4096, out_dtype=BF16)
        dga = _mm(dup_a, W_ua[l], name="d_ga", tm=1024, tn=1536, tk=2048)
        dgb = _mm(dup_b, W_ub[l], name="d_gb", tm=1024, tn=1536, tk=2048)
        do_a, dza, stats_a = _gate_bwd(dga, sv["out_a"], h, sv["lse_a"], z_off=off_za, W=WA, S=S,
                                       name="gate_bwd_a", do_dtype=F32)
        do_b, dzb_, stats_b = _gate_bwd(dgb, sv["out_b"], h, sv["lq_b"], z_off=off_zb, W=WB, S=S,
                                        name="gate_bwd_b", do_dtype=BF16)
        dqa, dka, dva = _rope_bwd(*_band_bwd(sv["qk_a"], h, do_a, stats_a, nha=nha, S=S, v_off=off_va),
                                  cos, sin, W=WA, S=S)
        if pending is None:
            dqb, dkb, dvb, dcq, dck = _fox_bwd(sv["qkv_b"], do_b, stats_b, sv["ck"], nhb=nhb, S=S, tq=tq_b)
        else:
            dqb, dkb, dvb, dcq, dck, got = _fox_bwd(sv["qkv_b"], do_b, stats_b, sv["ck"], nhb=nhb, S=S, tq=tq_b,
                                                    side=_Exchange([pending]))
            recv[l + 1] = (got[0], *recv[l + 1])
        head_rows = lambda t: _lane_blocks(jnp.pad(t, ((0, HEAD_ROWS - nhb), (0, 0))))
        df3, d_bf = _forget_bwd(head_rows(dcq[:, ::LANES].T), head_rows(dck.reshape(nhb, S)), sv["f3"], bf_rows[l])
        dfl = jnp.pad(_lane_unblocks(df3)[:nhb].T, ((0, 0), (0, FORGET_PAD - nhb))).astype(BF16)
        dh = jnp.concatenate([dqa, dka, dva, dza, dqb.astype(BF16), dkb, dvb, dzb_, dgl_a, dgl_b, dfl], axis=1)
        chunks = lambda t: t.reshape(NDEV, dsh, t.shape[1])
        g_w_in, got3 = _mm(sv["xb"], dh, ta=True, name="g_w_in_xchg", tm=512, tn=1280, tk=4096, out_dtype=BF16,
                           side=_Exchange([chunks(g_w_ua), chunks(g_w_ub), chunks(g_w_out)]))
        if l > 0:
            dx = _mm(dh, W_in[l], tb=True, name="d_x", tm=1024, tn=512, tk=3328, add=dz, add_scale=alpha)
            pending, recv[l] = chunks(g_w_in), tuple(got3)
        else:
            dx, got1 = _mm(dh, W_in[l], tb=True, name="d_x_xchg", tm=1024, tn=512, tk=3328, add=dz, add_scale=alpha,
                           side=_Exchange([chunks(g_w_in)]))
            recv[l] = (got1[0], *got3)
        small[l] = jnp.concatenate([d_bf[:nhb, 0], jnp.zeros((LANES - nhb,), F32), d_bga[0], d_bgb[0],
                                    d_lng[0], d_lnb[0]])

    n_small = LANES + 2 * D + 2 * D
    flat = jnp.concatenate(small + [loss_row[0]])
    rows = -(-flat.shape[0] // (8 * LANES)) * 8
    flat = jnp.pad(flat, (0, rows * LANES - flat.shape[0])).reshape(rows, LANES)
    tot = _sum_slots(_all_gather(flat, name="gather_small"), name="sum_small").reshape(-1)
    loss = tot[depth * n_small]
    sm = tot[:depth * n_small].reshape(depth, n_small)
    g_bf, g_bg = sm[:, :nhb], sm[:, LANES:LANES + 2 * D]
    g_lg, g_lb = sm[:, LANES + 2 * D:LANES + 3 * D], sm[:, LANES + 3 * D:]

    g_in = jnp.stack([_unpad_cols(_sum_slots(recv[l][0], name="sum_w_in"), nq=nq, ng=ng, nf=nf) for l in range(depth)])
    g_ua = jnp.stack([_sum_slots(recv[l][1], name="sum_w_up_a").T for l in range(depth)])
    g_ub = jnp.stack([_sum_slots(recv[l][2], name="sum_w_up_b").T for l in range(depth)])
    g_out = jnp.stack([_sum_slots(recv[l][3], name="sum_w_out") for l in range(depth)])

    def adam(w, g, m, v, name):
        shp = w.shape
        d_, m_, v_ = _adamw(*[t.reshape(-1, shp[-1]) for t in (w, g, m, v)], name=name)
        return d_.reshape(shp), m_.reshape(shp), v_.reshape(shp)

    grads = [g_in, g_bf, g_bg, g_ua, g_ub, g_out, g_lg, g_lb]
    ws = [w_in, b_forget, b_gate, w_up_a, w_up_b, w_out, ln_g, ln_b]
    ms = [m_w_in, m_b_forget, m_b_gate, m_w_up_a, m_w_up_b, m_w_out, m_ln_g, m_ln_b]
    vs = [v_w_in, v_b_forget, v_b_gate, v_w_up_a, v_w_up_b, v_w_out, v_ln_g, v_ln_b]
    names = ["w_in", "b_forget", "b_gate", "w_up_a", "w_up_b", "w_out", "ln_g", "ln_b"]
    upd = [adam(w, g, m, v, f"adamw_{n}") for w, g, m, v, n in zip(ws, grads, ms, vs, names)]
    return (loss, dx[None], *grads, *[u[0] for u in upd], *[u[1] for u in upd], *[u[2] for u in upd])
```

```python
import jax
import jax.numpy as jnp
from jax import lax
from jax.experimental import pallas as pl
from jax.experimental.pallas import tpu as pltpu

F32 = jnp.float32
BF16 = jnp.bfloat16

NDEV = 8
HEAD_DIM = 128
LANES = 128
Q_BLOCK = 128
DILATIONS = (1, 4, 16)
ROPE_THETA = 10000.0
LN_EPS = 1e-5
ATT_SCALE = HEAD_DIM ** -0.5
NEG = -1e30
FORGET_PAD = 256
BAND_CHUNK = 2048
BAND_UNROLL = 16
FOX_TILE_FWD = 2048
FOX_TILE_BWD = 1024
FOX_ROWS = 512
FOX_DIAG_SUB = 4
HEAD_ROWS = 16
ADAM_LR, ADAM_B1, ADAM_B2, ADAM_EPS, ADAM_WD, ADAM_STEP = 0.001, 0.9, 0.999, 1e-08, 0.01, 10
VMEM_LIMIT = 56 * 1024 * 1024


def _tile(n, pref, mult=LANES):
    if n <= pref:
        return n
    t = (pref // mult) * mult
    while t >= mult:
        if n % t == 0:
            return t
        t -= mult
    return n


def _params(sem=None):
    return pltpu.CompilerParams(dimension_semantics=sem, vmem_limit_bytes=VMEM_LIMIT)


def _sigmoid(z):
    return 1.0 / (1.0 + jnp.exp(-z))


_NT = (((1,), (1,)), ((), ()))
_NN = (((1,), (0,)), ((), ()))
_TN = (((0,), (0,)), ((), ()))


def _dot(a, b, dims):
    return lax.dot_general(a, b, dims, preferred_element_type=F32)


def _mm(a, b, *, name, ta=False, tb=False, out_dtype=F32, tm=512, tn=512, tk=512, add=None, add_scale=1.0,
        side=None):
    if ta:
        K, M = a.shape
    else:
        M, K = a.shape
    if tb:
        N, K2 = b.shape
    else:
        K2, N = b.shape
    assert K == K2, (a.shape, b.shape, ta, tb)
    tm, tn, tk = _tile(M, tm), _tile(N, tn), _tile(K, tk)
    ni, nj, nk = M // tm, N // tn, K // tk
    dims = (((0 if ta else 1,), (1 if tb else 0,)), ((), ()))
    n_side = 0 if side is None else len(side.ins)
    n_in = 2 + (add is not None)

    def body(*refs):
        a_ref, b_ref = refs[:2]
        add_ref = refs[2] if add is not None else None
        side_in = refs[n_in:n_in + n_side]
        o_ref = refs[n_in + n_side]
        side_out = refs[n_in + n_side + 1:n_in + 2 * n_side + 1]
        acc_ref = refs[n_in + 2 * n_side + 1]
        sems = refs[n_in + 2 * n_side + 2:]
        k = pl.program_id(2)
        if side is not None:
            @pl.when((pl.program_id(0) == 0) & (pl.program_id(1) == 0) & (k == 0))
            def _():
                side.start(side_in, side_out, *sems)

        part = _dot(a_ref[...].astype(BF16), b_ref[...].astype(BF16), dims)

        @pl.when(k == 0)
        def _():
            acc_ref[...] = part

        @pl.when(k > 0)
        def _():
            acc_ref[...] += part

        @pl.when(k == nk - 1)
        def _():
            r = acc_ref[...]
            if add_ref is not None:
                r = r + add_scale * add_ref[...]
            o_ref[...] = r.astype(out_dtype)

        if side is not None:
            step = (pl.program_id(0) * nj + pl.program_id(1)) * nk + k
            steps = ni * nj * nk

            @pl.when(step == max((3 * steps) // 4 - 1, 0))
            def _():
                side.middle(side_in, side_out, *sems)

            @pl.when(step == steps - 1)
            def _():
                side.finish(side_in, side_out, *sems)

    a_spec = pl.BlockSpec((tk, tm), lambda i, j, k: (k, i)) if ta else pl.BlockSpec((tm, tk), lambda i, j, k: (i, k))
    b_spec = pl.BlockSpec((tn, tk), lambda i, j, k: (j, k)) if tb else pl.BlockSpec((tk, tn), lambda i, j, k: (k, j))
    o_spec = pl.BlockSpec((tm, tn), lambda i, j, k: (i, j))
    in_specs, args = [a_spec, b_spec], [a, b]
    if add is not None:
        in_specs.append(o_spec)
        args.append(add)
    out_shape = jax.ShapeDtypeStruct((M, N), out_dtype)
    scratch = [pltpu.VMEM((tm, tn), F32)]
    if side is None:
        return pl.pallas_call(
            body, name=name, grid=(ni, nj, nk), in_specs=in_specs, out_specs=o_spec, out_shape=out_shape,
            scratch_shapes=scratch, compiler_params=_params(("parallel", "parallel", "arbitrary")),
        )(*args)
    res = pl.pallas_call(
        body, name=name, grid=(ni, nj, nk), in_specs=in_specs + [_ANY] * n_side,
        out_specs=[o_spec] + [_ANY] * n_side, out_shape=[out_shape] + side.outs,
        scratch_shapes=scratch + _side_scratch(side),
        compiler_params=_params(("arbitrary", "arbitrary", "arbitrary")),
    )(*args, *side.ins)
    return res[0], res[1:]


def _ew(fn, ins, outs, *, rows, cols, tr, tc, name):
    assert rows % tr == 0 and cols % tc == 0

    def spec(kind, off):
        assert off % tc == 0, (off, tc)
        ob = off // tc
        if kind == "tile":
            return pl.BlockSpec((tr, tc), lambda j, i: (i, j + ob))
        if kind == "tab":
            return pl.BlockSpec((tr, LANES), lambda j, i: (i, 0))
        return pl.BlockSpec((1, tc), lambda j, i: (0, j + ob))

    n_in = len(ins)

    def body(*refs):
        fn(pl.program_id(1), refs[:n_in], refs[n_in:])

    return pl.pallas_call(
        body, name=name, grid=(cols // tc, rows // tr),
        in_specs=[spec(k, o) for _, k, o in ins],
        out_specs=[spec(k, o) for _, _, k, o in outs],
        out_shape=[jax.ShapeDtypeStruct(s, d) for s, d, _, _ in outs],
        compiler_params=_params(("parallel", "arbitrary")),
    )(*[a for a, _, _ in ins])


def _heads(t):
    return [t[:, e:e + LANES] for e in range(0, t.shape[1], LANES)]


def _rope_tables(S):
    half = HEAD_DIM // 2
    inv_freq = ROPE_THETA ** (-jnp.arange(half, dtype=F32) / half)
    ang = jnp.arange(S, dtype=jnp.int32).astype(F32)[:, None] * inv_freq[None, :]
    cos, sin = jnp.cos(ang), jnp.sin(ang)
    return jnp.concatenate([cos, cos], axis=1), jnp.concatenate([-sin, sin], axis=1)


def _rope(t, cs, sn):
    return jnp.concatenate([g * cs + pltpu.roll(g, HEAD_DIM // 2, 1) * sn for g in _heads(t)], axis=1)


def _rope_t(t, cs, sn):
    return jnp.concatenate([g * cs + pltpu.roll(g * sn, HEAD_DIM // 2, 1) for g in _heads(t)], axis=1)


def _prep_a(h, cos, sin, *, W, S):
    tr, tc = _tile(S, 1024, 8), _tile(W, 512)

    def fn(i, ins, outs):
        outs[0][...] = _rope(ins[0][...], ins[1][...], ins[2][...])

    return _ew(fn, [(h, "tile", 0), (cos, "tab", 0), (sin, "tab", 0)], [((S, 2 * W), F32, "tile", 0)],
               rows=S, cols=2 * W, tr=tr, tc=tc, name="prep_a")[0]


def _prep_b(h, *, off, W, S):
    tr, tc = _tile(S, 1024, 8), _tile(W, 512)

    def fn(i, ins, outs):
        scale = jnp.where(pl.program_id(0) * tc < W, ATT_SCALE, 1.0).astype(F32)
        outs[0][...] = (ins[0][...] * scale).astype(BF16)

    return _ew(fn, [(h, "tile", off)], [((S, 3 * W), BF16, "tile", 0)], rows=S, cols=3 * W, tr=tr, tc=tc,
               name="prep_b")[0]


def _rope_bwd(dq, dk, dv, cos, sin, *, W, S):
    tr, tc = _tile(S, 1024, 8), _tile(W, 512)

    def fn(i, ins, outs):
        cs, sn = ins[3][...], ins[4][...]
        outs[0][...] = _rope_t(ins[0][...], cs, sn).astype(BF16)
        outs[1][...] = _rope_t(ins[1][...], cs, sn).astype(BF16)
        outs[2][...] = ins[2][...].astype(BF16)

    ins = [(dq, "tile", 0), (dk, "tile", 0), (dv, "tile", 0), (cos, "tab", 0), (sin, "tab", 0)]
    return _ew(fn, ins, [((S, W), BF16, "tile", 0)] * 3, rows=S, cols=W, tr=tr, tc=tc, name="rope_bwd")


def _band_biases():
    dist = (lax.broadcasted_iota(jnp.int32, (Q_BLOCK, 2 * Q_BLOCK), 0)
            - lax.broadcasted_iota(jnp.int32, (Q_BLOCK, 2 * Q_BLOCK), 1))
    first = jnp.where(dist >= 0, 0.0, NEG).astype(F32)
    other = jnp.where((dist + Q_BLOCK >= 0) & (dist <= 0), 0.0, NEG).astype(F32)
    return first, other


def _band_rows(it, c, d, nbc, biases):
    r, n = (0, it) if d == 1 else (it // nbc, it % nbc)
    ng = c * nbc + n
    k0 = pl.multiple_of(jnp.maximum(ng - 1, 0) * Q_BLOCK, Q_BLOCK)
    bias = jnp.where(ng == 0, biases[0], biases[1])
    if d == 1:
        return pl.ds(pl.multiple_of(it * Q_BLOCK, Q_BLOCK), Q_BLOCK), pl.ds(k0, 2 * Q_BLOCK), bias
    return (pl.ds(r + d * Q_BLOCK * n, Q_BLOCK, stride=d), pl.ds(r + d * k0, 2 * Q_BLOCK, stride=d), bias)


def _band_chunk(S):
    ch = min(S, BAND_CHUNK)
    assert S % ch == 0 and ch % (Q_BLOCK * max(DILATIONS)) == 0 and S >= 2 * Q_BLOCK * max(DILATIONS)
    return ch


def _band_fwd(qk, h, *, nha, S, v_off, z_off):
    CH = _band_chunk(S)
    W = nha * LANES
    slab = 256

    def body(q_ref, k_ref, v_ref, z_ref, o_ref, lse_ref, g_ref, *scratch):
        os_refs, ls_refs = scratch[:len(DILATIONS)], scratch[len(DILATIONS):]
        c = pl.program_id(1)
        biases = _band_biases()
        for gi, d in enumerate(DILATIONS):
            nbc = CH // (Q_BLOCK * d)

            def step(it, carry, gi=gi, d=d, nbc=nbc):
                rq, rk, bias = _band_rows(it, c, d, nbc, biases)
                q = q_ref[rq, :].astype(BF16)
                k = k_ref[rk, :].astype(BF16)
                v = v_ref[rk, :].astype(BF16)
                s = _dot(q, k, _NT) * ATT_SCALE + bias
                m = jnp.max(s, axis=1, keepdims=True)
                p = jnp.exp(s - m)
                den = jnp.sum(p, axis=1, keepdims=True)
                os_refs[gi][rq, :] = _dot(p.astype(BF16), v, _NN) / den
                ls_refs[gi][rq, :] = jnp.broadcast_to(m + jnp.log(den), (Q_BLOCK, LANES))
                return carry

            lax.fori_loop(0, CH // Q_BLOCK, step, 0, unroll=BAND_UNROLL)

        def mix(t, carry):
            rows = pl.ds(pl.multiple_of(t * slab, slab), slab)
            a, b, cc = (r[rows, :] for r in ls_refs)
            m = jnp.maximum(jnp.maximum(a, b), cc)
            ea, eb, ec = jnp.exp(a - m), jnp.exp(b - m), jnp.exp(cc - m)
            den = ea + eb + ec
            out = (ea * os_refs[0][rows, :] + eb * os_refs[1][rows, :] + ec * os_refs[2][rows, :]) / den
            z = z_ref[rows, :]
            o_ref[rows, :] = out
            lse_ref[rows, :] = m + jnp.log(den)
            g_ref[rows, :] = (out * (z * _sigmoid(z))).astype(BF16)
            return carry

        lax.fori_loop(0, CH // slab, mix, 0)

    chunk = pl.BlockSpec((CH, LANES), lambda hd, c: (c, hd))
    return pl.pallas_call(
        body, name="band_fwd", grid=(nha, S // CH),
        in_specs=[chunk,
                  pl.BlockSpec((S, LANES), lambda hd, c: (0, nha + hd)),
                  pl.BlockSpec((S, LANES), lambda hd, c: (0, v_off // LANES + hd)),
                  pl.BlockSpec((CH, LANES), lambda hd, c: (c, z_off // LANES + hd))],
        out_specs=[chunk, chunk, chunk],
        out_shape=[jax.ShapeDtypeStruct((S, W), F32), jax.ShapeDtypeStruct((S, W), F32),
                   jax.ShapeDtypeStruct((S, W), BF16)],
        scratch_shapes=[pltpu.VMEM((CH, LANES), F32)] * (2 * len(DILATIONS)),
        compiler_params=_params(("parallel", "arbitrary")),
    )(qk, qk, h, h)


def _band_bwd(qk, h, do, stats, *, nha, S, v_off):
    CH = _band_chunk(S)
    W = nha * LANES

    def body(q_ref, k_ref, v_ref, do_ref, st_ref, dq_ref, dk_ref, dv_ref):
        c = pl.program_id(1)

        @pl.when(c == 0)
        def _():
            dk_ref[...] = jnp.zeros_like(dk_ref)
            dv_ref[...] = jnp.zeros_like(dv_ref)

        biases = _band_biases()
        for gi, d in enumerate(DILATIONS):
            nbc = CH // (Q_BLOCK * d)

            def step(it, carry, gi=gi, d=d, nbc=nbc):
                rq, rk, bias = _band_rows(it, c, d, nbc, biases)
                q = q_ref[rq, :].astype(BF16)
                k = k_ref[rk, :].astype(BF16)
                v = v_ref[rk, :].astype(BF16)
                g = do_ref[rq, :].astype(BF16)
                st = st_ref[rq, :]
                p = jnp.exp(_dot(q, k, _NT) * ATT_SCALE + (bias - st[:, 0:1]))
                ds = (p * (_dot(g, v, _NT) - st[:, 1:2]) * ATT_SCALE).astype(BF16)
                dq = _dot(ds, k, _NN)
                if gi == 0:
                    dq_ref[rq, :] = dq
                else:
                    dq_ref[rq, :] += dq
                dk_ref[rk, :] += _dot(ds, q, _TN)
                dv_ref[rk, :] += _dot(p.astype(BF16), g, _TN)
                return carry

            lax.fori_loop(0, CH // Q_BLOCK, step, 0, unroll=BAND_UNROLL)

    chunk = pl.BlockSpec((CH, LANES), lambda hd, c: (c, hd))
    whole = pl.BlockSpec((S, LANES), lambda hd, c: (0, hd))
    return pl.pallas_call(
        body, name="band_bwd", grid=(nha, S // CH),
        in_specs=[chunk,
                  pl.BlockSpec((S, LANES), lambda hd, c: (0, nha + hd)),
                  pl.BlockSpec((S, LANES), lambda hd, c: (0, v_off // LANES + hd)),
                  chunk, chunk],
        out_specs=[chunk, whole, whole],
        out_shape=[jax.ShapeDtypeStruct((S, W), F32)] * 3,
        compiler_params=_params(("parallel", "arbitrary")),
    )(qk, qk, h, do, stats)


def _gate_bwd(dg, o, h, lane0, *, z_off, W, S, name, do_dtype):
    tr, tc = _tile(S, 512, 8), _tile(W, 512)

    def fn(i, ins, outs):
        g, out, z = ins[0][...], ins[1][...], ins[2][...]
        sg = _sigmoid(z)
        do = g * (z * sg)
        outs[0][...] = do.astype(do_dtype)
        outs[1][...] = (g * out * (sg * (1.0 + z * (1.0 - sg)))).astype(BF16)
        delta = jnp.concatenate([jnp.broadcast_to(jnp.sum(t, axis=1, keepdims=True), t.shape)
                                 for t in _heads(do * out)], axis=1)
        first = lax.broadcasted_iota(jnp.int32, delta.shape, 1) % LANES == 0
        outs[2][...] = jnp.where(first, ins[3][...], delta)

    ins = [(dg, "tile", 0), (o, "tile", 0), (h, "tile", z_off), (lane0, "tile", 0)]
    outs = [((S, W), do_dtype, "tile", 0), ((S, W), BF16, "tile", 0), ((S, W), F32, "tile", 0)]
    return _ew(fn, ins, outs, rows=S, cols=W, tr=tr, tc=tc, name=name)


def _split3(x):
    hi = x.astype(BF16)
    r = x - hi.astype(F32)
    mid = r.astype(BF16)
    lo = (r - mid.astype(F32)).astype(BF16)
    return hi, mid, lo


def _tri_dot(x, tri):
    hi, mid, lo = _split3(x)
    return _dot(hi, tri, _NN) + (_dot(mid, tri, _NN) + _dot(lo, tri, _NN))


def _log1p(u):
    w = 1.0 + u
    return jnp.where(w == 1.0, u, jnp.log(w) * (u / jnp.where(w == 1.0, 1.0, w - 1.0)))


def _lane_blocks(t):
    R, S = t.shape
    return t.reshape(R, S // LANES, LANES).transpose(1, 0, 2)


def _lane_unblocks(t):
    nb, R, _ = t.shape
    return t.transpose(1, 0, 2).reshape(R, nb * LANES)


def _forget_cumsum(f3, bf):
    nb, R, _ = f3.shape

    def body(f_ref, b_ref, c_ref):
        row = lax.broadcasted_iota(jnp.int32, (LANES, LANES), 0)
        colm = lax.broadcasted_iota(jnp.int32, (LANES, LANES), 1)
        tri = (row <= colm).astype(BF16)
        bias = b_ref[...]

        def step(n, carry):
            z = f_ref[n] + bias
            logf = jnp.minimum(z, 0.0) - _log1p(jnp.exp(-jnp.abs(z)))
            c = _tri_dot(logf, tri) + carry
            c_ref[n] = c
            return jnp.broadcast_to(c[:, LANES - 1:LANES], (R, LANES))

        lax.fori_loop(0, nb, step, jnp.zeros((R, LANES), F32))

    return pl.pallas_call(
        body, name="forget_cumsum", out_shape=jax.ShapeDtypeStruct(f3.shape, F32),
        compiler_params=_params(),
    )(f3, bf)


def _forget_bwd(dcq3, dck3, f3, bf):
    nb, R, _ = f3.shape

    def body(dcq_ref, dck_ref, f_ref, b_ref, df_ref, db_ref):
        row = lax.broadcasted_iota(jnp.int32, (LANES, LANES), 0)
        colm = lax.broadcasted_iota(jnp.int32, (LANES, LANES), 1)
        tri = (row >= colm).astype(BF16)
        bias = b_ref[...]

        def step(t, carry):
            tail, tot = carry
            n = nb - 1 - t
            r = _tri_dot(dcq_ref[n] + dck_ref[n], tri) + tail
            df = r * _sigmoid(-(f_ref[n] + bias))
            df_ref[n] = df
            tot = tot + jnp.broadcast_to(jnp.sum(df, axis=1, keepdims=True), (R, LANES))
            return jnp.broadcast_to(r[:, 0:1], (R, LANES)), tot

        _, tot = lax.fori_loop(0, nb, step, (jnp.zeros((R, LANES), F32), jnp.zeros((R, LANES), F32)))
        db_ref[...] = tot

    return pl.pallas_call(
        body, name="forget_bwd",
        out_shape=[jax.ShapeDtypeStruct(f3.shape, F32), jax.ShapeDtypeStruct((R, LANES), F32)],
        compiler_params=_params(),
    )(dcq3, dck3, f3, bf)


def _causal(t):
    return lax.broadcasted_iota(jnp.int32, (t, t), 0) >= lax.broadcasted_iota(jnp.int32, (t, t), 1)


def _fox_fwd(qkv, ck, h, *, z_off, nhb, S, tq):
    nq = S // tq
    ts = td = min(FOX_ROWS, tq)
    n_sub = n_diag = tq // ts

    def body(q_ref, k_ref, v_ref, ck_ref, z_ref, o_ref, lq_ref, g_ref):
        i = pl.program_id(1)
        qs = [q_ref[r * ts:(r + 1) * ts, :] for r in range(n_sub)]

        def absorb(q, k, v, ckj, m, l, acc, row0=None):
            s = _dot(q, k, _NT) - ckj
            if row0 is not None:
                rows = row0 + lax.broadcasted_iota(jnp.int32, s.shape, 0)
                s = jnp.where(rows >= lax.broadcasted_iota(jnp.int32, s.shape, 1), s, NEG)
            m_new = jnp.maximum(m, jnp.max(s, axis=1, keepdims=True))
            alpha = jnp.exp(m - m_new)
            p = jnp.exp(s - m_new)
            return m_new, alpha * l + jnp.sum(p, axis=1, keepdims=True), alpha * acc + _dot(p.astype(BF16), v, _NN)

        def step(j, carry):
            off = pl.multiple_of(j * tq, tq)
            k, v, ckj = k_ref[pl.ds(off, tq), :], v_ref[pl.ds(off, tq), :], ck_ref[j]
            return tuple(absorb(qs[r], k, v, ckj, *carry[r]) for r in range(n_sub))

        init = tuple((jnp.full((ts, 1), NEG, F32), jnp.zeros((ts, 1), F32), jnp.zeros((ts, LANES), F32))
                     for _ in range(n_sub))
        carry = lax.fori_loop(0, i, step, init)

        off = pl.multiple_of(i * tq, tq)
        k, v, cki = k_ref[pl.ds(off, tq), :], v_ref[pl.ds(off, tq), :], ck_ref[i]
        for f in range(n_diag):
            rows = slice(f * td, (f + 1) * td)
            part = slice(f * td % ts, f * td % ts + td)
            m, l, acc = (t[part] for t in carry[f * td // ts])
            nk = (f + 1) * td
            m, l, acc = absorb(q_ref[rows, :], k[:nk], v[:nk], cki[:, :nk], m, l, acc, row0=f * td)
            out = acc / l
            z = z_ref[rows, :]
            o_ref[rows, :] = out
            lq_ref[rows, :] = jnp.broadcast_to(m + jnp.log(l), (td, LANES))
            g_ref[rows, :] = (out * (z * _sigmoid(z))).astype(BF16)

    blk = pl.BlockSpec((tq, LANES), lambda hd, i: (i, hd))
    return pl.pallas_call(
        body, name="fox_fwd", grid=(nhb, nq),
        in_specs=[blk,
                  pl.BlockSpec((S, LANES), lambda hd, i: (0, nhb + hd)),
                  pl.BlockSpec((S, LANES), lambda hd, i: (0, 2 * nhb + hd)),
                  pl.BlockSpec((None, nq, 1, tq), lambda hd, i: (hd, 0, 0, 0)),
                  pl.BlockSpec((tq, LANES), lambda hd, i: (i, z_off // LANES + hd))],
        out_specs=[blk, blk, blk],
        out_shape=[jax.ShapeDtypeStruct((S, nhb * LANES), F32), jax.ShapeDtypeStruct((S, nhb * LANES), F32),
                   jax.ShapeDtypeStruct((S, nhb * LANES), BF16)],
        compiler_params=_params(("parallel", "arbitrary")),
    )(qkv, qkv, qkv, ck, h)


def _fox_bwd(qkv, do, stats, ck, *, nhb, S, tq, side=None):
    nq = S // tq
    ts, td = min(FOX_ROWS, tq), tq // FOX_DIAG_SUB
    n_side = 0 if side is None else len(side.ins)

    def body(*refs):
        q_ref, k_ref, v_ref, do_ref, st_ref, ck_ref = refs[:6]
        side_in = refs[6:6 + n_side]
        dq_ref, dk_ref, dv_ref, dr_ref, dc_ref = refs[6 + n_side:11 + n_side]
        side_out, sems = refs[11 + n_side:11 + 2 * n_side], refs[11 + 2 * n_side:]
        j = pl.program_id(1)
        if side is not None:
            @pl.when((pl.program_id(0) == 0) & (j == 0))
            def _():
                side.start(side_in, side_out, *sems)

        @pl.when(j == 0)
        def _():
            dq_ref[...] = jnp.zeros_like(dq_ref)
            dr_ref[...] = jnp.zeros_like(dr_ref)

        k = k_ref[...]
        v = v_ref[...]
        ckv = ck_ref[...]

        def group(start, n, nk, carry, row0=None):
            dk, dv, dc = carry
            rows = pl.ds(pl.multiple_of(start, n), n)
            q, g, st = q_ref[rows, :], do_ref[rows, :], st_ref[rows, :]
            p = jnp.exp(_dot(q, k[:nk], _NT) - ckv[:, :nk] - st[:, 0:1])
            if row0 is not None:
                rr = row0 + lax.broadcasted_iota(jnp.int32, (n, nk), 0)
                p = jnp.where(rr >= lax.broadcasted_iota(jnp.int32, (n, nk), 1), p, 0.0)
            ds = p * (_dot(g, v[:nk], _NT) - st[:, 1:2])
            dr_ref[rows, :] += jnp.broadcast_to(jnp.sum(ds, axis=1, keepdims=True), (n, LANES))
            dsb = ds.astype(BF16)
            dq_ref[rows, :] += _dot(dsb, k[:nk], _NN)
            upd = (_dot(dsb, q, _TN), _dot(p.astype(BF16), g, _TN), -jnp.sum(ds, axis=0, keepdims=True))
            if nk < tq:
                upd = (jnp.concatenate([upd[0], jnp.zeros((tq - nk, LANES), F32)], axis=0),
                       jnp.concatenate([upd[1], jnp.zeros((tq - nk, LANES), F32)], axis=0),
                       jnp.concatenate([upd[2], jnp.zeros((1, tq - nk), F32)], axis=1))
            return dk + upd[0], dv + upd[1], dc + upd[2]

        def step(i, carry):
            for r in range(tq // ts):
                carry = group(i * tq + r * ts, ts, tq, carry)
            return carry

        carry = (jnp.zeros((tq, LANES), F32), jnp.zeros((tq, LANES), F32), jnp.zeros((1, tq), F32))
        for f in range(FOX_DIAG_SUB):
            carry = group(j * tq + f * td, td, (f + 1) * td, carry, row0=f * td)
        dk, dv, dc = lax.fori_loop(j + 1, nq, step, carry)
        dk_ref[...] = dk.astype(BF16)
        dv_ref[...] = dv.astype(BF16)
        dc_ref[...] = dc

        @pl.when(j == nq - 1)
        def _():
            dq_ref[...] = dq_ref[...] * ATT_SCALE

        if side is not None:
            @pl.when((pl.program_id(0) == nhb - 1) & (j == nq - 1))
            def _():
                side.middle(side_in, side_out, *sems)
                side.finish(side_in, side_out, *sems)

    def whole(off):
        return pl.BlockSpec((S, LANES), lambda h, j: (0, off + h))

    kv_blk = pl.BlockSpec((tq, LANES), lambda h, j: (j, h))
    c_blk = pl.BlockSpec((None, None, 1, tq), lambda h, j: (h, j, 0, 0))
    in_specs = [whole(0),
                pl.BlockSpec((tq, LANES), lambda h, j: (j, nhb + h)),
                pl.BlockSpec((tq, LANES), lambda h, j: (j, 2 * nhb + h)),
                whole(0), whole(0), c_blk]
    out_specs = [whole(0), kv_blk, kv_blk, whole(0), c_blk]
    out_shape = [jax.ShapeDtypeStruct((S, nhb * LANES), F32),
                 jax.ShapeDtypeStruct((S, nhb * LANES), BF16),
                 jax.ShapeDtypeStruct((S, nhb * LANES), BF16),
                 jax.ShapeDtypeStruct((S, nhb * LANES), F32),
                 jax.ShapeDtypeStruct((nhb, nq, 1, tq), F32)]
    args = (qkv, qkv, qkv, do, stats, ck)
    if side is None:
        return pl.pallas_call(
            body, name="fox_bwd", grid=(nhb, nq), in_specs=in_specs, out_specs=out_specs, out_shape=out_shape,
            compiler_params=_params(("parallel", "arbitrary")),
        )(*args)
    res = pl.pallas_call(
        body, name="fox_bwd_xchg", grid=(nhb, nq), in_specs=in_specs + [_ANY] * n_side,
        out_specs=out_specs + [_ANY] * n_side, out_shape=out_shape + side.outs,
        scratch_shapes=_side_scratch(side), compiler_params=_params(("arbitrary", "arbitrary")),
    )(*args, *side.ins)
    return (*res[:5], res[5:])


def _merge(up_a, up_b, h, b_gate, *, g_off, D, S):
    tr, tc = _tile(S, 512, 8), _tile(D, 512)

    def fn(i, ins, outs):
        ua, ub, la, lb, ba, bb = (r[...] for r in ins)
        outs[0][...] = (_sigmoid(la + ba) * ua + _sigmoid(lb + bb) * ub).astype(BF16)

    ins = [(up_a, "tile", 0), (up_b, "tile", 0), (h, "tile", g_off), (h, "tile", g_off + D),
           (b_gate, "row", 0), (b_gate, "row", D)]
    return _ew(fn, ins, [((S, D), BF16, "tile", 0)], rows=S, cols=D, tr=tr, tc=tc, name="merge")[0]


def _acc_rows(i, ref, val):
    @pl.when(i == 0)
    def _():
        ref[...] = val

    @pl.when(i > 0)
    def _():
        ref[...] += val


def _merge_bwd(dmix, up_a, up_b, h, b_gate, *, g_off, D, S):
    tr, tc = _tile(S, 512, 8), _tile(D, 512)

    def fn(i, ins, outs):
        dm, ua, ub, la, lb, ba, bb = (r[...] for r in ins)
        ga, gb = _sigmoid(la + ba), _sigmoid(lb + bb)
        outs[0][...] = (ga * dm).astype(BF16)
        outs[1][...] = (gb * dm).astype(BF16)
        dla = ua * dm * (ga * (1.0 - ga))
        dlb = ub * dm * (gb * (1.0 - gb))
        outs[2][...] = dla.astype(BF16)
        outs[3][...] = dlb.astype(BF16)
        _acc_rows(i, outs[4], jnp.sum(dla, axis=0, keepdims=True))
        _acc_rows(i, outs[5], jnp.sum(dlb, axis=0, keepdims=True))

    ins = [(dmix, "tile", 0), (up_a, "tile", 0), (up_b, "tile", 0), (h, "tile", g_off), (h, "tile", g_off + D),
           (b_gate, "row", 0), (b_gate, "row", D)]
    outs = [((S, D), BF16, "tile", 0)] * 4 + [((1, D), F32, "acc", 0)] * 2
    return _ew(fn, ins, outs, rows=S, cols=D, tr=tr, tc=tc, name="merge_bwd")


def _out_proj_deepnorm(mix, w, x, g, b, *, alpha, D, S):
    tr = _tile(S, 512, 8)

    def body(m_ref, w_ref, x_ref, g_ref, b_ref, xn_ref, xb_ref, z_ref):
        z = alpha * x_ref[...] + _dot(m_ref[...], w_ref[...], _NN)
        mu = jnp.mean(z, axis=1, keepdims=True)
        zc = z - mu
        var = jnp.mean(zc * zc, axis=1, keepdims=True)
        xn = zc * lax.rsqrt(var + LN_EPS) * g_ref[...] + b_ref[...]
        xn_ref[...] = xn
        xb_ref[...] = xn.astype(BF16)
        z_ref[...] = z

    rows = pl.BlockSpec((tr, D), lambda i: (i, 0))
    vec = pl.BlockSpec((1, D), lambda i: (0, 0))
    return pl.pallas_call(
        body, name="out_proj_deepnorm", grid=(S // tr,),
        in_specs=[rows, pl.BlockSpec((D, D), lambda i: (0, 0)), rows, vec, vec],
        out_specs=[rows, rows, rows],
        out_shape=[jax.ShapeDtypeStruct((S, D), F32), jax.ShapeDtypeStruct((S, D), BF16),
                   jax.ShapeDtypeStruct((S, D), F32)],
        compiler_params=_params(("parallel",)),
    )(mix, w, x, g, b)


def _deepnorm_bwd(dxn, z, g, *, D, S):
    tr = _tile(S, 256, 8)

    def fn(i, ins, outs):
        dx, zv = ins[0][...], ins[1][...]
        mu = jnp.mean(zv, axis=1, keepdims=True)
        zc = zv - mu
        rstd = lax.rsqrt(jnp.mean(zc * zc, axis=1, keepdims=True) + LN_EPS)
        u = zc * rstd
        du = dx * ins[2][...]
        dz = rstd * (du - jnp.mean(du, axis=1, keepdims=True) - u * jnp.mean(du * u, axis=1, keepdims=True))
        outs[0][...] = dz
        outs[1][...] = dz.astype(BF16)
        _acc_rows(i, outs[2], jnp.sum(dx * u, axis=0, keepdims=True))
        _acc_rows(i, outs[3], jnp.sum(dx, axis=0, keepdims=True))

    ins = [(dxn, "tile", 0), (z, "tile", 0), (g, "row", 0)]
    outs = [((S, D), F32, "tile", 0), ((S, D), BF16, "tile", 0), ((1, D), F32, "acc", 0), ((1, D), F32, "acc", 0)]
    return _ew(fn, ins, outs, rows=S, cols=D, tr=tr, tc=D, name="deepnorm_bwd")


def _loss_head(y, target, *, D, S):
    tr = _tile(S, 256, 8)

    def body(y_ref, t_ref, d_ref, l_ref):
        err = y_ref[...] - t_ref[...]
        d_ref[...] = err * (1.0 / D)
        part = 0.5 * jnp.sum(jnp.sum(err * err, axis=1, keepdims=True) * (1.0 / D), axis=0, keepdims=True)
        _acc_rows(pl.program_id(0), l_ref, jnp.broadcast_to(part, (1, LANES)))

    return pl.pallas_call(
        body, name="loss_head", grid=(S // tr,),
        in_specs=[pl.BlockSpec((tr, D), lambda i: (i, 0))] * 2,
        out_specs=[pl.BlockSpec((tr, D), lambda i: (i, 0)), pl.BlockSpec((1, LANES), lambda i: (0, 0))],
        out_shape=[jax.ShapeDtypeStruct((S, D), F32), jax.ShapeDtypeStruct((1, LANES), F32)],
        compiler_params=_params(("arbitrary",)),
    )(y, target)


_MESH = pl.DeviceIdType.MESH
_ANY = pl.BlockSpec(memory_space=pl.ANY)


class _Gather:
    def __init__(self, xs):
        self.ins = list(xs)
        self.outs = [jax.ShapeDtypeStruct((NDEV,) + x.shape, x.dtype) for x in xs]

    def _ctx(self, a, x_ref, out_ref, send_sems, recv_sems):
        x, y, c = lax.axis_index("x"), lax.axis_index("y"), lax.axis_index("c")
        me, sibling = (x, y, c), (x, y, 1 - c)
        chips = [(1 - x, y), (x, 1 - y), (1 - x, 1 - y)]

        def slot(px, py, pc):
            return out_ref.at[4 * px + 2 * py + pc]

        def copy(k, block, to, src=None):
            return pltpu.make_async_remote_copy(
                src_ref=slot(*block) if src is None else src, dst_ref=slot(*block),
                send_sem=send_sems.at[7 * a + k], recv_sem=recv_sems.at[7 * a + k], device_id=to, device_id_type=_MESH)

        def first():
            return [copy(0, me, sibling, src=x_ref)] + [copy(1 + j, me, (*chip, c), src=x_ref)
                                                        for j, chip in enumerate(chips)]

        def passed():
            return [copy(4 + j, (*chip, c), sibling) for j, chip in enumerate(chips)]

        return c, me, sibling, chips, slot, copy, first, passed

    def start(self, in_refs, out_refs, send_sems, recv_sems, local_sems):
        for a, (x_ref, out_ref) in enumerate(zip(in_refs, out_refs)):
            _, me, _, _, slot, _, first, _ = self._ctx(a, x_ref, out_ref, send_sems, recv_sems)
            pltpu.make_async_copy(x_ref, slot(*me), local_sems.at[a]).start()
            for cp in first():
                cp.start()

    def middle(self, in_refs, out_refs, send_sems, recv_sems, local_sems):
        for a, (x_ref, out_ref) in enumerate(zip(in_refs, out_refs)):
            c, me, _, chips, _, copy, _, passed = self._ctx(a, x_ref, out_ref, send_sems, recv_sems)
            for j, (chip, cp) in enumerate(zip(chips, passed())):
                copy(1 + j, (*chip, c), me).wait_recv()
                cp.start()

    def finish(self, in_refs, out_refs, send_sems, recv_sems, local_sems):
        for a, (x_ref, out_ref) in enumerate(zip(in_refs, out_refs)):
            c, me, sibling, chips, slot, copy, first, passed = self._ctx(a, x_ref, out_ref, send_sems, recv_sems)
            copy(0, sibling, me).wait_recv()
            for j, chip in enumerate(chips):
                copy(4 + j, (*chip, 1 - c), me).wait_recv()
            for cp in first() + passed():
                cp.wait_send()
            pltpu.make_async_copy(x_ref, slot(*me), local_sems.at[a]).wait()


class _Exchange:
    def __init__(self, gs):
        self.ins = list(gs)
        self.outs = [jax.ShapeDtypeStruct(g.shape, g.dtype) for g in gs]

    def _copies(self, a, g_ref, out_ref, send_sems, recv_sems, arrivals=True):
        x, y, c = lax.axis_index("x"), lax.axis_index("y"), lax.axis_index("c")
        me = 4 * x + 2 * y + c
        out, back = [], []
        for k in range(1, NDEV):
            px, py, pc = x ^ ((k >> 2) & 1), y ^ ((k >> 1) & 1), c ^ (k & 1)
            peer = 4 * px + 2 * py + pc
            sems = dict(send_sem=send_sems.at[7 * a + k - 1], recv_sem=recv_sems.at[7 * a + k - 1],
                        device_id=(px, py, pc), device_id_type=_MESH)
            out.append(pltpu.make_async_remote_copy(src_ref=g_ref.at[peer], dst_ref=out_ref.at[me], **sems))
            if arrivals:
                back.append(pltpu.make_async_remote_copy(src_ref=g_ref.at[me], dst_ref=out_ref.at[peer], **sems))
        return me, out, back

    def start(self, in_refs, out_refs, send_sems, recv_sems, local_sems):
        for a, (g_ref, out_ref) in enumerate(zip(in_refs, out_refs)):
            me, out, _ = self._copies(a, g_ref, out_ref, send_sems, recv_sems, arrivals=False)
            pltpu.make_async_copy(g_ref.at[me], out_ref.at[me], local_sems.at[a]).start()
            for cp in out:
                cp.start()

    def middle(self, in_refs, out_refs, send_sems, recv_sems, local_sems):
        pass

    def finish(self, in_refs, out_refs, send_sems, recv_sems, local_sems):
        for a, (g_ref, out_ref) in enumerate(zip(in_refs, out_refs)):
            me, out, back = self._copies(a, g_ref, out_ref, send_sems, recv_sems)
            for cp in back:
                cp.wait_recv()
            for cp in out:
                cp.wait_send()
            pltpu.make_async_copy(g_ref.at[me], out_ref.at[me], local_sems.at[a]).wait()


def _side_scratch(side):
    n = len(side.ins)
    return [pltpu.SemaphoreType.DMA((7 * n,)), pltpu.SemaphoreType.DMA((7 * n,)), pltpu.SemaphoreType.DMA((n,))]


def _run_side(side, *, name):
    n = len(side.ins)

    def body(*refs):
        ins, outs, sems = refs[:n], refs[n:2 * n], refs[2 * n:]
        side.start(ins, outs, *sems)
        side.middle(ins, outs, *sems)
        side.finish(ins, outs, *sems)

    return pl.pallas_call(
        body, name=name, out_shape=side.outs, in_specs=[_ANY] * n, out_specs=[_ANY] * n,
        scratch_shapes=_side_scratch(side),
    )(*side.ins)


def _all_gather(x, *, name):
    return _run_side(_Gather([x]), name=name)[0]


def _sum_slots(parts, *, name):
    n, R, C = parts.shape
    tr, tc = _tile(R, 64, 16), _tile(C, 2048)

    def body(p_ref, o_ref):
        acc = p_ref[0].astype(F32)
        for k in range(1, n):
            acc = acc + p_ref[k].astype(F32)
        o_ref[...] = acc

    return pl.pallas_call(
        body, name=name, grid=(R // tr, C // tc),
        in_specs=[pl.BlockSpec((n, tr, tc), lambda i, j: (0, i, j))],
        out_specs=pl.BlockSpec((tr, tc), lambda i, j: (i, j)),
        out_shape=jax.ShapeDtypeStruct((R, C), F32),
        compiler_params=_params(("parallel", "parallel")),
    )(parts)


def _adamw(w, g, m, v, *, name):
    R, C = w.shape
    budget = 2 << 20
    tr = R if R * C * 4 <= budget else _tile(R, max(8, (budget // (C * 4)) // 8 * 8), 8)

    def body(w_ref, g_ref, m_ref, v_ref, d_ref, nm_ref, nv_ref):
        gv = g_ref[...]
        nm = ADAM_B1 * m_ref[...] + (1.0 - ADAM_B1) * gv
        nv = ADAM_B2 * v_ref[...] + (1.0 - ADAM_B2) * (gv * gv)
        m_hat = nm / (1.0 - ADAM_B1 ** ADAM_STEP)
        v_hat = nv / (1.0 - ADAM_B2 ** ADAM_STEP)
        d_ref[...] = -ADAM_LR * (m_hat / (jnp.sqrt(v_hat) + ADAM_EPS) + ADAM_WD * w_ref[...])
        nm_ref[...] = nm
        nv_ref[...] = nv

    spec = pl.BlockSpec((tr, C), lambda i: (i, 0))
    return pl.pallas_call(
        body, name=name, grid=(R // tr,), in_specs=[spec] * 4, out_specs=[spec] * 3,
        out_shape=[jax.ShapeDtypeStruct((R, C), F32)] * 3,
        compiler_params=_params(("parallel",)),
    )(w, g, m, v)


def _pad_cols(w, *, nq, ng):
    nf = w.shape[-1] - nq - ng
    pad = jnp.zeros(w.shape[:-1] + (FORGET_PAD - nf,), w.dtype)
    return jnp.concatenate([w[..., :nq], w[..., nq + nf:], w[..., nq:nq + nf], pad], axis=-1)


def _unpad_cols(w, *, nq, ng, nf):
    return jnp.concatenate([w[..., :nq], w[..., nq + ng:nq + ng + nf], w[..., nq:nq + ng]], axis=-1)


def kernel(x, w_in, b_forget, b_gate, w_up_a, w_up_b, w_out, ln_g, ln_b, loss_target, m_w_in, m_b_forget, m_b_gate, m_w_up_a, m_w_up_b, m_w_out, m_ln_g, m_ln_b, v_w_in, v_b_forget, v_b_gate, v_w_up_a, v_w_up_b, v_w_out, v_ln_g, v_ln_b):
    depth = w_in.shape[0]
    _, S, D = x.shape
    WA, WB = w_up_a.shape[1], w_up_b.shape[1]
    nha, nhb = WA // HEAD_DIM, WB // HEAD_DIM
    assert nhb == b_forget.shape[1] and WA == WB and nhb <= HEAD_ROWS
    nq, ng, nf = 4 * WA + 4 * WB, 2 * D, nhb
    assert w_in.shape[2] == nq + nf + ng
    ncp = nq + ng + FORGET_PAD
    off_va, off_za, off_b, off_zb, off_g, off_f = 2 * WA, 3 * WA, 4 * WA, 4 * WA + 3 * WB, nq, nq + ng
    alpha = float((2 * depth) ** 0.25)
    dsh = D // NDEV
    tq_f, tq_b = _tile(S, FOX_TILE_FWD), _tile(S, FOX_TILE_BWD)

    x0 = x[0]
    target = loss_target[0]
    cos, sin = _rope_tables(S)

    w_in_p = _pad_cols(w_in, nq=nq, ng=ng).astype(BF16)
    w_up_a_t = jnp.swapaxes(w_up_a, 1, 2).astype(BF16)
    w_up_b_t = jnp.swapaxes(w_up_b, 1, 2).astype(BF16)
    w_out_c = w_out.astype(BF16)
    W_in, W_ua, W_ub, W_out = ([None] * depth for _ in range(4))

    def others(l):
        return [w_up_a_t[l], w_up_b_t[l], w_out_c[l]]

    def keep_others(l, got):
        W_ua[l], W_ub[l], W_out[l] = got[0].reshape(D, WA), got[1].reshape(D, WB), got[2].reshape(D, D)

    W_in[0] = _run_side(_Gather([w_in_p[0]]), name="gather_w_in_0")[0].reshape(D, ncp)

    bf_rows = [jnp.broadcast_to(jnp.pad(b_forget[l], (0, HEAD_ROWS - nhb))[:, None], (HEAD_ROWS, LANES))
               for l in range(depth)]

    saved = []
    xf, xb = x0, x0.astype(BF16)
    for l in range(depth):
        wanted = (others(0) if l == 0 else []) + ([w_in_p[l + 1]] + others(l + 1) if l + 1 < depth else [])
        if wanted:
            h, got = _mm(xb, W_in[l], name="in_proj_gather", tm=1024, tn=1280, tk=2048, side=_Gather(wanted))
            if l == 0:
                keep_others(0, got[:3])
            if l + 1 < depth:
                W_in[l + 1] = got[-4].reshape(D, ncp)
                keep_others(l + 1, got[-3:])
        else:
            h = _mm(xb, W_in[l], name="in_proj", tm=1024, tn=1280, tk=2048)
        qk_a = _prep_a(h, cos, sin, W=WA, S=S)
        qkv_b = _prep_b(h, off=off_b, W=WB, S=S)
        out_a, lse_a, ga = _band_fwd(qk_a, h, nha=nha, S=S, v_off=off_va, z_off=off_za)
        f3 = _lane_blocks(jnp.pad(h[:, off_f:off_f + nhb].T, ((0, HEAD_ROWS - nhb), (0, 0))))
        c = _lane_unblocks(_forget_cumsum(f3, bf_rows[l]))[:nhb]
        out_b, lq_b, gb = _fox_fwd(qkv_b, c.reshape(nhb, S // tq_f, 1, tq_f), h, z_off=off_zb, nhb=nhb, S=S, tq=tq_f)
        ck = c.reshape(nhb, S // tq_b, 1, tq_b)
        up_a = _mm(ga, W_ua[l], tb=True, name="up_a", tm=1024, tn=1024, tk=1536)
        up_b = _mm(gb, W_ub[l], tb=True, name="up_b", tm=1024, tn=1024, tk=1536)
        bg = b_gate[l][None, :]
        mix = _merge(up_a, up_b, h, bg, g_off=off_g, D=D, S=S)
        xn_f, xn_b, z = _out_proj_deepnorm(mix, W_out[l], xf, ln_g[l][None, :], ln_b[l][None, :],
                                           alpha=alpha, D=D, S=S)
        saved.append(dict(xb=xb, h=h, qk_a=qk_a, qkv_b=qkv_b, out_a=out_a, lse_a=lse_a, ga=ga, f3=f3, ck=ck,
                          out_b=out_b, lq_b=lq_b, gb=gb, up_a=up_a, up_b=up_b, mix=mix, z=z, bg=bg))
        xf, xb = xn_f, xn_b

    dx, loss_row = _loss_head(xf, target, D=D, S=S)

    recv = [None] * depth
    small = [None] * depth
    pending = None
    for l in reversed(range(depth)):
        sv = saved[l]
        h = sv["h"]
        dz, dzb, d_lng, d_lnb = _deepnorm_bwd(dx, sv["z"], ln_g[l][None, :], D=D, S=S)
        g_w_out = _mm(sv["mix"], dzb, ta=True, name="g_w_out", tm=512, tn=1024, tk=4096, out_dtype=BF16)
        dmix = _mm(dzb, W_out[l], tb=True, name="d_mix", tm=1024, tn=1024, tk=2048)
        dup_a, dup_b, dgl_a, dgl_b, d_bga, d_bgb = _merge_bwd(dmix, sv["up_a"], sv["up_b"], h, sv["bg"],
                                                              g_off=off_g, D=D, S=S)
        g_w_ua = _mm(dup_a, sv["ga"], ta=True, name="g_w_up_a", tm=512, tn=1536, tk=4096, out_dtype=BF16)
        g_w_ub = _mm(dup_b, sv["gb"], ta=True, name="g_w_up_b", tm=512, tn=1536, tk=4096, out_dtype=BF16)
        dga = _mm(dup_a, W_ua[l], name="d_ga", tm=1024, tn=1536, tk=2048)
        dgb = _mm(dup_b, W_ub[l], name="d_gb", tm=1024, tn=1536, tk=2048)
        do_a, dza, stats_a = _gate_bwd(dga, sv["out_a"], h, sv["lse_a"], z_off=off_za, W=WA, S=S,
                                       name="gate_bwd_a", do_dtype=F32)
        do_b, dzb_, stats_b = _gate_bwd(dgb, sv["out_b"], h, sv["lq_b"], z_off=off_zb, W=WB, S=S,
                                        name="gate_bwd_b", do_dtype=BF16)
        dqa, dka, dva = _rope_bwd(*_band_bwd(sv["qk_a"], h, do_a, stats_a, nha=nha, S=S, v_off=off_va),
                                  cos, sin, W=WA, S=S)
        if pending is None:
            dqb, dkb, dvb, dcq, dck = _fox_bwd(sv["qkv_b"], do_b, stats_b, sv["ck"], nhb=nhb, S=S, tq=tq_b)
        else:
            dqb, dkb, dvb, dcq, dck, got = _fox_bwd(sv["qkv_b"], do_b, stats_b, sv["ck"], nhb=nhb, S=S, tq=tq_b,
                                                    side=_Exchange([pending]))
            recv[l + 1] = (got[0], *recv[l + 1])
        head_rows = lambda t: _lane_blocks(jnp.pad(t, ((0, HEAD_ROWS - nhb), (0, 0))))
        df3, d_bf = _forget_bwd(head_rows(dcq[:, ::LANES].T), head_rows(dck.reshape(nhb, S)), sv["f3"], bf_rows[l])
        dfl = jnp.pad(_lane_unblocks(df3)[:nhb].T, ((0, 0), (0, FORGET_PAD - nhb))).astype(BF16)
        dh = jnp.concatenate([dqa, dka, dva, dza, dqb.astype(BF16), dkb, dvb, dzb_, dgl_a, dgl_b, dfl], axis=1)
        chunks = lambda t: t.reshape(NDEV, dsh, t.shape[1])
        g_w_in, got3 = _mm(sv["xb"], dh, ta=True, name="g_w_in_xchg", tm=512, tn=1280, tk=4096, out_dtype=BF16,
                           side=_Exchange([chunks(g_w_ua), chunks(g_w_ub), chunks(g_w_out)]))
        if l > 0:
            dx = _mm(dh, W_in[l], tb=True, name="d_x", tm=1024, tn=512, tk=3328, add=dz, add_scale=alpha)
            pending, recv[l] = chunks(g_w_in), tuple(got3)
        else:
            dx, got1 = _mm(dh, W_in[l], tb=True, name="d_x_xchg", tm=1024, tn=512, tk=3328, add=dz, add_scale=alpha,
                           side=_Exchange([chunks(g_w_in)]))
            recv[l] = (got1[0], *got3)
        small[l] = jnp.concatenate([d_bf[:nhb, 0], jnp.zeros((LANES - nhb,), F32), d_bga[0], d_bgb[0],
                                    d_lng[0], d_lnb[0]])

    n_small = LANES + 2 * D + 2 * D
    flat = jnp.concatenate(small + [loss_row[0]])
    rows = -(-flat.shape[0] // (8 * LANES)) * 8
    flat = jnp.pad(flat, (0, rows * LANES - flat.shape[0])).reshape(rows, LANES)
    tot = _sum_slots(_all_gather(flat, name="gather_small"), name="sum_small").reshape(-1)
    loss = tot[depth * n_small]
    sm = tot[:depth * n_small].reshape(depth, n_small)
    g_bf, g_bg = sm[:, :nhb], sm[:, LANES:LANES + 2 * D]
    g_lg, g_lb = sm[:, LANES + 2 * D:LANES + 3 * D], sm[:, LANES + 3 * D:]

    g_in = jnp.stack([_unpad_cols(_sum_slots(recv[l][0], name="sum_w_in"), nq=nq, ng=ng, nf=nf) for l in range(depth)])
    g_ua = jnp.stack([_sum_slots(recv[l][1], name="sum_w_up_a").T for l in range(depth)])
    g_ub = jnp.stack([_sum_slots(recv[l][2], name="sum_w_up_b").T for l in range(depth)])
    g_out = jnp.stack([_sum_slots(recv[l][3], name="sum_w_out") for l in range(depth)])

    def adam(w, g, m, v, name):
        shp = w.shape
        d_, m_, v_ = _adamw(*[t.reshape(-1, shp[-1]) for t in (w, g, m, v)], name=name)
        return d_.reshape(shp), m_.reshape(shp), v_.reshape(shp)

    grads = [g_in, g_bf, g_bg, g_ua, g_ub, g_out, g_lg, g_lb]
    ws = [w_in, b_forget, b_gate, w_up_a, w_up_b, w_out, ln_g, ln_b]
    ms = [m_w_in, m_b_forget, m_b_gate, m_w_up_a, m_w_up_b, m_w_out, m_ln_g, m_ln_b]
    vs = [v_w_in, v_b_forget, v_b_gate, v_w_up_a, v_w_up_b, v_w_out, v_ln_g, v_ln_b]
    names = ["w_in", "b_forget", "b_gate", "w_up_a", "w_up_b", "w_out", "ln_g", "ln_b"]
    upd = [adam(w, g, m, v, f"adamw_{n}") for w, g, m, v, n in zip(ws, grads, ms, vs, names)]
    return (loss, dx[None], *grads, *[u[0] for u in upd], *[u[1] for u in upd], *[u[2] for u in upd])
```

```python
import jax
import jax.numpy as jnp
from jax import lax
from jax.experimental import pallas as pl
from jax.experimental.pallas import tpu as pltpu

F32 = jnp.float32
BF16 = jnp.bfloat16

NDEV = 8
HEAD_DIM = 128
LANES = 128
Q_BLOCK = 128
DILATIONS = (1, 4, 16)
ROPE_THETA = 10000.0
LN_EPS = 1e-5
ATT_SCALE = HEAD_DIM ** -0.5
NEG = -1e30
FORGET_PAD = 256
BAND_CHUNK = 2048
BAND_UNROLL = 16
FOX_TILE_FWD = 2048
FOX_TILE_BWD = 1024
FOX_ROWS = 512
GATE_BWD_ROWS = 512
FOX_DIAG_SUB = 4
HEAD_ROWS = 16
ADAM_LR, ADAM_B1, ADAM_B2, ADAM_EPS, ADAM_WD, ADAM_STEP = 0.001, 0.9, 0.999, 1e-08, 0.01, 10
VMEM_LIMIT = 56 * 1024 * 1024


def _tile(n, pref, mult=LANES):
    if n <= pref:
        return n
    t = (pref // mult) * mult
    while t >= mult:
        if n % t == 0:
            return t
        t -= mult
    return n


def _params(sem=None):
    return pltpu.CompilerParams(dimension_semantics=sem, vmem_limit_bytes=VMEM_LIMIT)


def _sigmoid(z):
    return 1.0 / (1.0 + jnp.exp(-z))


_NT = (((1,), (1,)), ((), ()))
_NN = (((1,), (0,)), ((), ()))
_TN = (((0,), (0,)), ((), ()))


def _dot(a, b, dims):
    return lax.dot_general(a, b, dims, preferred_element_type=F32)


def _mm(a, b, *, name, ta=False, tb=False, out_dtype=F32, tm=512, tn=512, tk=512, add=None, add_scale=1.0,
        side=None):
    if ta:
        K, M = a.shape
    else:
        M, K = a.shape
    if tb:
        N, K2 = b.shape
    else:
        K2, N = b.shape
    assert K == K2, (a.shape, b.shape, ta, tb)
    tm, tn, tk = _tile(M, tm), _tile(N, tn), _tile(K, tk)
    ni, nj, nk = M // tm, N // tn, K // tk
    dims = (((0 if ta else 1,), (1 if tb else 0,)), ((), ()))
    n_side = 0 if side is None else len(side.ins)
    n_in = 2 + (add is not None)

    def body(*refs):
        a_ref, b_ref = refs[:2]
        add_ref = refs[2] if add is not None else None
        side_in = refs[n_in:n_in + n_side]
        o_ref = refs[n_in + n_side]
        side_out = refs[n_in + n_side + 1:n_in + 2 * n_side + 1]
        acc_ref = refs[n_in + 2 * n_side + 1]
        sems = refs[n_in + 2 * n_side + 2:]
        k = pl.program_id(2)
        if side is not None:
            @pl.when((pl.program_id(0) == 0) & (pl.program_id(1) == 0) & (k == 0))
            def _():
                side.start(side_in, side_out, *sems)

        part = _dot(a_ref[...].astype(BF16), b_ref[...].astype(BF16), dims)

        @pl.when(k == 0)
        def _():
            acc_ref[...] = part

        @pl.when(k > 0)
        def _():
            acc_ref[...] += part

        @pl.when(k == nk - 1)
        def _():
            r = acc_ref[...]
            if add_ref is not None:
                r = r + add_scale * add_ref[...]
            o_ref[...] = r.astype(out_dtype)

        if side is not None:
            step = (pl.program_id(0) * nj + pl.program_id(1)) * nk + k
            steps = ni * nj * nk

            @pl.when(step == max((3 * steps) // 4 - 1, 0))
            def _():
                side.middle(side_in, side_out, *sems)

            @pl.when(step == steps - 1)
            def _():
                side.finish(side_in, side_out, *sems)

    a_spec = pl.BlockSpec((tk, tm), lambda i, j, k: (k, i)) if ta else pl.BlockSpec((tm, tk), lambda i, j, k: (i, k))
    b_spec = pl.BlockSpec((tn, tk), lambda i, j, k: (j, k)) if tb else pl.BlockSpec((tk, tn), lambda i, j, k: (k, j))
    o_spec = pl.BlockSpec((tm, tn), lambda i, j, k: (i, j))
    in_specs, args = [a_spec, b_spec], [a, b]
    if add is not None:
        in_specs.append(o_spec)
        args.append(add)
    out_shape = jax.ShapeDtypeStruct((M, N), out_dtype)
    scratch = [pltpu.VMEM((tm, tn), F32)]
    if side is None:
        return pl.pallas_call(
            body, name=name, grid=(ni, nj, nk), in_specs=in_specs, out_specs=o_spec, out_shape=out_shape,
            scratch_shapes=scratch, compiler_params=_params(("parallel", "parallel", "arbitrary")),
        )(*args)
    res = pl.pallas_call(
        body, name=name, grid=(ni, nj, nk), in_specs=in_specs + [_ANY] * n_side,
        out_specs=[o_spec] + [_ANY] * n_side, out_shape=[out_shape] + side.outs,
        scratch_shapes=scratch + _side_scratch(side),
        compiler_params=_params(("arbitrary", "arbitrary", "arbitrary")),
    )(*args, *side.ins)
    return res[0], res[1:]


def _ew(fn, ins, outs, *, rows, cols, tr, tc, name):
    assert rows % tr == 0 and cols % tc == 0

    def spec(kind, off):
        assert off % tc == 0, (off, tc)
        ob = off // tc
        if kind == "tile":
            return pl.BlockSpec((tr, tc), lambda j, i: (i, j + ob))
        if kind == "tab":
            return pl.BlockSpec((tr, LANES), lambda j, i: (i, 0))
        return pl.BlockSpec((1, tc), lambda j, i: (0, j + ob))

    n_in = len(ins)

    def body(*refs):
        fn(pl.program_id(1), refs[:n_in], refs[n_in:])

    return pl.pallas_call(
        body, name=name, grid=(cols // tc, rows // tr),
        in_specs=[spec(k, o) for _, k, o in ins],
        out_specs=[spec(k, o) for _, _, k, o in outs],
        out_shape=[jax.ShapeDtypeStruct(s, d) for s, d, _, _ in outs],
        compiler_params=_params(("parallel", "arbitrary")),
    )(*[a for a, _, _ in ins])


def _heads(t):
    return [t[:, e:e + LANES] for e in range(0, t.shape[1], LANES)]


def _rope_tables(S):
    half = HEAD_DIM // 2
    inv_freq = ROPE_THETA ** (-jnp.arange(half, dtype=F32) / half)
    ang = jnp.arange(S, dtype=jnp.int32).astype(F32)[:, None] * inv_freq[None, :]
    cos, sin = jnp.cos(ang), jnp.sin(ang)
    return jnp.concatenate([cos, cos], axis=1), jnp.concatenate([-sin, sin], axis=1)


def _rope(t, cs, sn):
    return jnp.concatenate([g * cs + pltpu.roll(g, HEAD_DIM // 2, 1) * sn for g in _heads(t)], axis=1)


def _rope_t(t, cs, sn):
    return jnp.concatenate([g * cs + pltpu.roll(g * sn, HEAD_DIM // 2, 1) for g in _heads(t)], axis=1)


def _prep_a(h, cos, sin, *, W, S):
    tr, tc = _tile(S, 1024, 8), _tile(W, 512)

    def fn(i, ins, outs):
        outs[0][...] = _rope(ins[0][...], ins[1][...], ins[2][...])

    return _ew(fn, [(h, "tile", 0), (cos, "tab", 0), (sin, "tab", 0)], [((S, 2 * W), F32, "tile", 0)],
               rows=S, cols=2 * W, tr=tr, tc=tc, name="prep_a")[0]


def _prep_b(h, *, off, W, S):
    tr, tc = _tile(S, 1024, 8), _tile(W, 512)

    def fn(i, ins, outs):
        scale = jnp.where(pl.program_id(0) * tc < W, ATT_SCALE, 1.0).astype(F32)
        outs[0][...] = (ins[0][...] * scale).astype(BF16)

    return _ew(fn, [(h, "tile", off)], [((S, 3 * W), BF16, "tile", 0)], rows=S, cols=3 * W, tr=tr, tc=tc,
               name="prep_b")[0]


def _rope_bwd(dq, dk, dv, cos, sin, *, W, S):
    tr, tc = _tile(S, 1024, 8), _tile(W, 512)

    def fn(i, ins, outs):
        cs, sn = ins[3][...], ins[4][...]
        outs[0][...] = _rope_t(ins[0][...], cs, sn).astype(BF16)
        outs[1][...] = _rope_t(ins[1][...], cs, sn).astype(BF16)
        outs[2][...] = ins[2][...].astype(BF16)

    ins = [(dq, "tile", 0), (dk, "tile", 0), (dv, "tile", 0), (cos, "tab", 0), (sin, "tab", 0)]
    return _ew(fn, ins, [((S, W), BF16, "tile", 0)] * 3, rows=S, cols=W, tr=tr, tc=tc, name="rope_bwd")


def _band_biases():
    dist = (lax.broadcasted_iota(jnp.int32, (Q_BLOCK, 2 * Q_BLOCK), 0)
            - lax.broadcasted_iota(jnp.int32, (Q_BLOCK, 2 * Q_BLOCK), 1))
    first = jnp.where(dist >= 0, 0.0, NEG).astype(F32)
    other = jnp.where((dist + Q_BLOCK >= 0) & (dist <= 0), 0.0, NEG).astype(F32)
    return first, other


def _band_rows(it, c, d, nbc, biases):
    r, n = (0, it) if d == 1 else (it // nbc, it % nbc)
    ng = c * nbc + n
    k0 = pl.multiple_of(jnp.maximum(ng - 1, 0) * Q_BLOCK, Q_BLOCK)
    bias = jnp.where(ng == 0, biases[0], biases[1])
    if d == 1:
        return pl.ds(pl.multiple_of(it * Q_BLOCK, Q_BLOCK), Q_BLOCK), pl.ds(k0, 2 * Q_BLOCK), bias
    return (pl.ds(r + d * Q_BLOCK * n, Q_BLOCK, stride=d), pl.ds(r + d * k0, 2 * Q_BLOCK, stride=d), bias)


def _band_chunk(S):
    ch = min(S, BAND_CHUNK)
    assert S % ch == 0 and ch % (Q_BLOCK * max(DILATIONS)) == 0 and S >= 2 * Q_BLOCK * max(DILATIONS)
    return ch


def _band_fwd(qk, h, *, nha, S, v_off, z_off):
    CH = _band_chunk(S)
    W = nha * LANES
    slab = 256

    def body(q_ref, k_ref, v_ref, z_ref, o_ref, lse_ref, g_ref, *scratch):
        os_refs, ls_refs = scratch[:len(DILATIONS)], scratch[len(DILATIONS):]
        c = pl.program_id(1)
        biases = _band_biases()
        for gi, d in enumerate(DILATIONS):
            nbc = CH // (Q_BLOCK * d)

            def step(it, carry, gi=gi, d=d, nbc=nbc):
                rq, rk, bias = _band_rows(it, c, d, nbc, biases)
                q = q_ref[rq, :].astype(BF16)
                k = k_ref[rk, :].astype(BF16)
                v = v_ref[rk, :].astype(BF16)
                s = _dot(q, k, _NT) * ATT_SCALE + bias
                m = jnp.max(s, axis=1, keepdims=True)
                p = jnp.exp(s - m)
                den = jnp.sum(p, axis=1, keepdims=True)
                os_refs[gi][rq, :] = _dot(p.astype(BF16), v, _NN) / den
                ls_refs[gi][rq, :] = jnp.broadcast_to(m + jnp.log(den), (Q_BLOCK, LANES))
                return carry

            lax.fori_loop(0, CH // Q_BLOCK, step, 0, unroll=BAND_UNROLL)

        def mix(t, carry):
            rows = pl.ds(pl.multiple_of(t * slab, slab), slab)
            a, b, cc = (r[rows, :] for r in ls_refs)
            m = jnp.maximum(jnp.maximum(a, b), cc)
            ea, eb, ec = jnp.exp(a - m), jnp.exp(b - m), jnp.exp(cc - m)
            den = ea + eb + ec
            out = (ea * os_refs[0][rows, :] + eb * os_refs[1][rows, :] + ec * os_refs[2][rows, :]) / den
            z = z_ref[rows, :]
            o_ref[rows, :] = out
            lse_ref[rows, :] = m + jnp.log(den)
            g_ref[rows, :] = (out * (z * _sigmoid(z))).astype(BF16)
            return carry

        lax.fori_loop(0, CH // slab, mix, 0)

    chunk = pl.BlockSpec((CH, LANES), lambda hd, c: (c, hd))
    return pl.pallas_call(
        body, name="band_fwd", grid=(nha, S // CH),
        in_specs=[chunk,
                  pl.BlockSpec((S, LANES), lambda hd, c: (0, nha + hd)),
                  pl.BlockSpec((S, LANES), lambda hd, c: (0, v_off // LANES + hd)),
                  pl.BlockSpec((CH, LANES), lambda hd, c: (c, z_off // LANES + hd))],
        out_specs=[chunk, chunk, chunk],
        out_shape=[jax.ShapeDtypeStruct((S, W), F32), jax.ShapeDtypeStruct((S, W), F32),
                   jax.ShapeDtypeStruct((S, W), BF16)],
        scratch_shapes=[pltpu.VMEM((CH, LANES), F32)] * (2 * len(DILATIONS)),
        compiler_params=_params(("parallel", "arbitrary")),
    )(qk, qk, h, h)


def _band_bwd(qk, h, do, stats, *, nha, S, v_off):
    CH = _band_chunk(S)
    W = nha * LANES

    def body(q_ref, k_ref, v_ref, do_ref, st_ref, dq_ref, dk_ref, dv_ref):
        c = pl.program_id(1)

        @pl.when(c == 0)
        def _():
            dk_ref[...] = jnp.zeros_like(dk_ref)
            dv_ref[...] = jnp.zeros_like(dv_ref)

        biases = _band_biases()
        for gi, d in enumerate(DILATIONS):
            nbc = CH // (Q_BLOCK * d)

            def step(it, carry, gi=gi, d=d, nbc=nbc):
                rq, rk, bias = _band_rows(it, c, d, nbc, biases)
                q = q_ref[rq, :].astype(BF16)
                k = k_ref[rk, :].astype(BF16)
                v = v_ref[rk, :].astype(BF16)
                g = do_ref[rq, :].astype(BF16)
                st = st_ref[rq, :]
                p = jnp.exp(_dot(q, k, _NT) * ATT_SCALE + (bias - st[:, 0:1]))
                ds = (p * (_dot(g, v, _NT) - st[:, 1:2]) * ATT_SCALE).astype(BF16)
                dq = _dot(ds, k, _NN)
                if gi == 0:
                    dq_ref[rq, :] = dq
                else:
                    dq_ref[rq, :] += dq
                dk_ref[rk, :] += _dot(ds, q, _TN)
                dv_ref[rk, :] += _dot(p.astype(BF16), g, _TN)
                return carry

            lax.fori_loop(0, CH // Q_BLOCK, step, 0, unroll=BAND_UNROLL)

    chunk = pl.BlockSpec((CH, LANES), lambda hd, c: (c, hd))
    whole = pl.BlockSpec((S, LANES), lambda hd, c: (0, hd))
    return pl.pallas_call(
        body, name="band_bwd", grid=(nha, S // CH),
        in_specs=[chunk,
                  pl.BlockSpec((S, LANES), lambda hd, c: (0, nha + hd)),
                  pl.BlockSpec((S, LANES), lambda hd, c: (0, v_off // LANES + hd)),
                  chunk, chunk],
        out_specs=[chunk, whole, whole],
        out_shape=[jax.ShapeDtypeStruct((S, W), F32)] * 3,
        compiler_params=_params(("parallel", "arbitrary")),
    )(qk, qk, h, do, stats)


def _gate_bwd(dup, w, o, h, lane0, *, z_off, W, S, name, do_dtype):
    tr = _tile(S, GATE_BWD_ROWS, 8)
    D = dup.shape[1]
    assert z_off % W == 0

    def body(d_ref, w_ref, o_ref, z_ref, l0_ref, do_ref, dz_ref, st_ref):
        g = _dot(d_ref[...], w_ref[...], _NN)
        out, z = o_ref[...], z_ref[...]
        sg = _sigmoid(z)
        do = g * (z * sg)
        do_ref[...] = do.astype(do_dtype)
        dz_ref[...] = (g * out * (sg * (1.0 + z * (1.0 - sg)))).astype(BF16)
        delta = jnp.concatenate([jnp.broadcast_to(jnp.sum(t, axis=1, keepdims=True), t.shape)
                                 for t in _heads(do * out)], axis=1)
        first = lax.broadcasted_iota(jnp.int32, delta.shape, 1) % LANES == 0
        st_ref[...] = jnp.where(first, l0_ref[...], delta)

    rows = pl.BlockSpec((tr, W), lambda i: (i, 0))
    return pl.pallas_call(
        body, name=name, grid=(S // tr,),
        in_specs=[pl.BlockSpec((tr, D), lambda i: (i, 0)),
                  pl.BlockSpec((D, W), lambda i: (0, 0), pipeline_mode=pl.Buffered(1)),
                  rows, pl.BlockSpec((tr, W), lambda i: (i, z_off // W)), rows],
        out_specs=[rows, rows, rows],
        out_shape=[jax.ShapeDtypeStruct((S, W), do_dtype), jax.ShapeDtypeStruct((S, W), BF16),
                   jax.ShapeDtypeStruct((S, W), F32)],
        compiler_params=_params(("parallel",)),
    )(dup, w, o, h, lane0)


def _split3(x):
    hi = x.astype(BF16)
    r = x - hi.astype(F32)
    mid = r.astype(BF16)
    lo = (r - mid.astype(F32)).astype(BF16)
    return hi, mid, lo


def _tri_dot(x, tri):
    hi, mid, lo = _split3(x)
    return _dot(hi, tri, _NN) + (_dot(mid, tri, _NN) + _dot(lo, tri, _NN))


def _log1p(u):
    w = 1.0 + u
    return jnp.where(w == 1.0, u, jnp.log(w) * (u / jnp.where(w == 1.0, 1.0, w - 1.0)))


def _lane_blocks(t):
    R, S = t.shape
    return t.reshape(R, S // LANES, LANES).transpose(1, 0, 2)


def _lane_unblocks(t):
    nb, R, _ = t.shape
    return t.transpose(1, 0, 2).reshape(R, nb * LANES)


def _forget_cumsum(f3, bf):
    nb, R, _ = f3.shape

    def body(f_ref, b_ref, c_ref):
        row = lax.broadcasted_iota(jnp.int32, (LANES, LANES), 0)
        colm = lax.broadcasted_iota(jnp.int32, (LANES, LANES), 1)
        tri = (row <= colm).astype(BF16)
        bias = b_ref[...]

        def step(n, carry):
            z = f_ref[n] + bias
            logf = jnp.minimum(z, 0.0) - _log1p(jnp.exp(-jnp.abs(z)))
            c = _tri_dot(logf, tri) + carry
            c_ref[n] = c
            return jnp.broadcast_to(c[:, LANES - 1:LANES], (R, LANES))

        lax.fori_loop(0, nb, step, jnp.zeros((R, LANES), F32))

    return pl.pallas_call(
        body, name="forget_cumsum", out_shape=jax.ShapeDtypeStruct(f3.shape, F32),
        compiler_params=_params(),
    )(f3, bf)


def _forget_bwd(dcq3, dck3, f3, bf):
    nb, R, _ = f3.shape

    def body(dcq_ref, dck_ref, f_ref, b_ref, df_ref, db_ref):
        row = lax.broadcasted_iota(jnp.int32, (LANES, LANES), 0)
        colm = lax.broadcasted_iota(jnp.int32, (LANES, LANES), 1)
        tri = (row >= colm).astype(BF16)
        bias = b_ref[...]

        def step(t, carry):
            tail, tot = carry
            n = nb - 1 - t
            r = _tri_dot(dcq_ref[n] + dck_ref[n], tri) + tail
            df = r * _sigmoid(-(f_ref[n] + bias))
            df_ref[n] = df
            tot = tot + jnp.broadcast_to(jnp.sum(df, axis=1, keepdims=True), (R, LANES))
            return jnp.broadcast_to(r[:, 0:1], (R, LANES)), tot

        _, tot = lax.fori_loop(0, nb, step, (jnp.zeros((R, LANES), F32), jnp.zeros((R, LANES), F32)))
        db_ref[...] = tot

    return pl.pallas_call(
        body, name="forget_bwd",
        out_shape=[jax.ShapeDtypeStruct(f3.shape, F32), jax.ShapeDtypeStruct((R, LANES), F32)],
        compiler_params=_params(),
    )(dcq3, dck3, f3, bf)


def _causal(t):
    return lax.broadcasted_iota(jnp.int32, (t, t), 0) >= lax.broadcasted_iota(jnp.int32, (t, t), 1)


def _fox_fwd(qkv, ck, h, *, z_off, nhb, S, tq):
    nq = S // tq
    ts = td = min(FOX_ROWS, tq)
    n_sub = n_diag = tq // ts

    def body(q_ref, k_ref, v_ref, ck_ref, z_ref, o_ref, lq_ref, g_ref):
        i = pl.program_id(1)
        qs = [q_ref[r * ts:(r + 1) * ts, :] for r in range(n_sub)]

        def absorb(q, k, v, ckj, m, l, acc, row0=None):
            s = _dot(q, k, _NT) - ckj
            if row0 is not None:
                rows = row0 + lax.broadcasted_iota(jnp.int32, s.shape, 0)
                s = jnp.where(rows >= lax.broadcasted_iota(jnp.int32, s.shape, 1), s, NEG)
            m_new = jnp.maximum(m, jnp.max(s, axis=1, keepdims=True))
            alpha = jnp.exp(m - m_new)
            p = jnp.exp(s - m_new)
            return m_new, alpha * l + jnp.sum(p, axis=1, keepdims=True), alpha * acc + _dot(p.astype(BF16), v, _NN)

        def step(j, carry):
            off = pl.multiple_of(j * tq, tq)
            k, v, ckj = k_ref[pl.ds(off, tq), :], v_ref[pl.ds(off, tq), :], ck_ref[j]
            return tuple(absorb(qs[r], k, v, ckj, *carry[r]) for r in range(n_sub))

        init = tuple((jnp.full((ts, 1), NEG, F32), jnp.zeros((ts, 1), F32), jnp.zeros((ts, LANES), F32))
                     for _ in range(n_sub))
        carry = lax.fori_loop(0, i, step, init)

        off = pl.multiple_of(i * tq, tq)
        k, v, cki = k_ref[pl.ds(off, tq), :], v_ref[pl.ds(off, tq), :], ck_ref[i]
        for f in range(n_diag):
            rows = slice(f * td, (f + 1) * td)
            part = slice(f * td % ts, f * td % ts + td)
            m, l, acc = (t[part] for t in carry[f * td // ts])
            nk = (f + 1) * td
            m, l, acc = absorb(q_ref[rows, :], k[:nk], v[:nk], cki[:, :nk], m, l, acc, row0=f * td)
            out = acc / l
            z = z_ref[rows, :]
            o_ref[rows, :] = out
            lq_ref[rows, :] = jnp.broadcast_to(m + jnp.log(l), (td, LANES))
            g_ref[rows, :] = (out * (z * _sigmoid(z))).astype(BF16)

    blk = pl.BlockSpec((tq, LANES), lambda hd, i: (i, hd))
    return pl.pallas_call(
        body, name="fox_fwd", grid=(nhb, nq),
        in_specs=[blk,
                  pl.BlockSpec((S, LANES), lambda hd, i: (0, nhb + hd)),
                  pl.BlockSpec((S, LANES), lambda hd, i: (0, 2 * nhb + hd)),
                  pl.BlockSpec((None, nq, 1, tq), lambda hd, i: (hd, 0, 0, 0)),
                  pl.BlockSpec((tq, LANES), lambda hd, i: (i, z_off // LANES + hd))],
        out_specs=[blk, blk, blk],
        out_shape=[jax.ShapeDtypeStruct((S, nhb * LANES), F32), jax.ShapeDtypeStruct((S, nhb * LANES), F32),
                   jax.ShapeDtypeStruct((S, nhb * LANES), BF16)],
        compiler_params=_params(("parallel", "arbitrary")),
    )(qkv, qkv, qkv, ck, h)


def _fox_bwd(qkv, do, stats, ck, *, nhb, S, tq, side=None):
    nq = S // tq
    ts, td = min(FOX_ROWS, tq), tq // FOX_DIAG_SUB
    n_side = 0 if side is None else len(side.ins)

    def body(*refs):
        q_ref, k_ref, v_ref, do_ref, st_ref, ck_ref = refs[:6]
        side_in = refs[6:6 + n_side]
        dq_ref, dk_ref, dv_ref, dr_ref, dc_ref = refs[6 + n_side:11 + n_side]
        side_out, sems = refs[11 + n_side:11 + 2 * n_side], refs[11 + 2 * n_side:]
        j = pl.program_id(1)
        if side is not None:
            @pl.when((pl.program_id(0) == 0) & (j == 0))
            def _():
                side.start(side_in, side_out, *sems)

        @pl.when(j == 0)
        def _():
            dq_ref[...] = jnp.zeros_like(dq_ref)
            dr_ref[...] = jnp.zeros_like(dr_ref)

        k = k_ref[...]
        v = v_ref[...]
        ckv = ck_ref[...]

        def group(start, n, nk, carry, row0=None):
            dk, dv, dc = carry
            rows = pl.ds(pl.multiple_of(start, n), n)
            q, g, st = q_ref[rows, :], do_ref[rows, :], st_ref[rows, :]
            p = jnp.exp(_dot(q, k[:nk], _NT) - ckv[:, :nk] - st[:, 0:1])
            if row0 is not None:
                rr = row0 + lax.broadcasted_iota(jnp.int32, (n, nk), 0)
                p = jnp.where(rr >= lax.broadcasted_iota(jnp.int32, (n, nk), 1), p, 0.0)
            ds = p * (_dot(g, v[:nk], _NT) - st[:, 1:2])
            dr_ref[rows, :] += jnp.broadcast_to(jnp.sum(ds, axis=1, keepdims=True), (n, LANES))
            dsb = ds.astype(BF16)
            dq_ref[rows, :] += _dot(dsb, k[:nk], _NN)
            upd = (_dot(dsb, q, _TN), _dot(p.astype(BF16), g, _TN), -jnp.sum(ds, axis=0, keepdims=True))
            if nk < tq:
                upd = (jnp.concatenate([upd[0], jnp.zeros((tq - nk, LANES), F32)], axis=0),
                       jnp.concatenate([upd[1], jnp.zeros((tq - nk, LANES), F32)], axis=0),
                       jnp.concatenate([upd[2], jnp.zeros((1, tq - nk), F32)], axis=1))
            return dk + upd[0], dv + upd[1], dc + upd[2]

        def step(i, carry):
            for r in range(tq // ts):
                carry = group(i * tq + r * ts, ts, tq, carry)
            return carry

        carry = (jnp.zeros((tq, LANES), F32), jnp.zeros((tq, LANES), F32), jnp.zeros((1, tq), F32))
        for f in range(FOX_DIAG_SUB):
            carry = group(j * tq + f * td, td, (f + 1) * td, carry, row0=f * td)
        dk, dv, dc = lax.fori_loop(j + 1, nq, step, carry)
        dk_ref[...] = dk.astype(BF16)
        dv_ref[...] = dv.astype(BF16)
        dc_ref[...] = dc

        @pl.when(j == nq - 1)
        def _():
            dq_ref[...] = dq_ref[...] * ATT_SCALE

        if side is not None:
            @pl.when((pl.program_id(0) == nhb - 1) & (j == nq - 1))
            def _():
                side.middle(side_in, side_out, *sems)
                side.finish(side_in, side_out, *sems)

    def whole(off):
        return pl.BlockSpec((S, LANES), lambda h, j: (0, off + h))

    kv_blk = pl.BlockSpec((tq, LANES), lambda h, j: (j, h))
    c_blk = pl.BlockSpec((None, None, 1, tq), lambda h, j: (h, j, 0, 0))
    in_specs = [whole(0),
                pl.BlockSpec((tq, LANES), lambda h, j: (j, nhb + h)),
                pl.BlockSpec((tq, LANES), lambda h, j: (j, 2 * nhb + h)),
                whole(0), whole(0), c_blk]
    out_specs = [whole(0), kv_blk, kv_blk, whole(0), c_blk]
    out_shape = [jax.ShapeDtypeStruct((S, nhb * LANES), F32),
                 jax.ShapeDtypeStruct((S, nhb * LANES), BF16),
                 jax.ShapeDtypeStruct((S, nhb * LANES), BF16),
                 jax.ShapeDtypeStruct((S, nhb * LANES), F32),
                 jax.ShapeDtypeStruct((nhb, nq, 1, tq), F32)]
    args = (qkv, qkv, qkv, do, stats, ck)
    if side is None:
        return pl.pallas_call(
            body, name="fox_bwd", grid=(nhb, nq), in_specs=in_specs, out_specs=out_specs, out_shape=out_shape,
            compiler_params=_params(("parallel", "arbitrary")),
        )(*args)
    res = pl.pallas_call(
        body, name="fox_bwd_xchg", grid=(nhb, nq), in_specs=in_specs + [_ANY] * n_side,
        out_specs=out_specs + [_ANY] * n_side, out_shape=out_shape + side.outs,
        scratch_shapes=_side_scratch(side), compiler_params=_params(("arbitrary", "arbitrary")),
    )(*args, *side.ins)
    return (*res[:5], res[5:])


def _merge(up_a, up_b, h, b_gate, *, g_off, D, S):
    tr, tc = _tile(S, 512, 8), _tile(D, 512)

    def fn(i, ins, outs):
        ua, ub, la, lb, ba, bb = (r[...] for r in ins)
        outs[0][...] = (_sigmoid(la + ba) * ua + _sigmoid(lb + bb) * ub).astype(BF16)

    ins = [(up_a, "tile", 0), (up_b, "tile", 0), (h, "tile", g_off), (h, "tile", g_off + D),
           (b_gate, "row", 0), (b_gate, "row", D)]
    return _ew(fn, ins, [((S, D), BF16, "tile", 0)], rows=S, cols=D, tr=tr, tc=tc, name="merge")[0]


def _acc_rows(i, ref, val):
    @pl.when(i == 0)
    def _():
        ref[...] = val

    @pl.when(i > 0)
    def _():
        ref[...] += val


def _merge_bwd(dmix, up_a, up_b, h, b_gate, *, g_off, D, S):
    tr, tc = _tile(S, 512, 8), _tile(D, 512)

    def fn(i, ins, outs):
        dm, ua, ub, la, lb, ba, bb = (r[...] for r in ins)
        ga, gb = _sigmoid(la + ba), _sigmoid(lb + bb)
        outs[0][...] = (ga * dm).astype(BF16)
        outs[1][...] = (gb * dm).astype(BF16)
        dla = ua * dm * (ga * (1.0 - ga))
        dlb = ub * dm * (gb * (1.0 - gb))
        outs[2][...] = dla.astype(BF16)
        outs[3][...] = dlb.astype(BF16)
        _acc_rows(i, outs[4], jnp.sum(dla, axis=0, keepdims=True))
        _acc_rows(i, outs[5], jnp.sum(dlb, axis=0, keepdims=True))

    ins = [(dmix, "tile", 0), (up_a, "tile", 0), (up_b, "tile", 0), (h, "tile", g_off), (h, "tile", g_off + D),
           (b_gate, "row", 0), (b_gate, "row", D)]
    outs = [((S, D), BF16, "tile", 0)] * 4 + [((1, D), F32, "acc", 0)] * 2
    return _ew(fn, ins, outs, rows=S, cols=D, tr=tr, tc=tc, name="merge_bwd")


def _out_proj_deepnorm(mix, w, x, g, b, *, alpha, D, S):
    tr = _tile(S, 512, 8)

    def body(m_ref, w_ref, x_ref, g_ref, b_ref, xn_ref, xb_ref, z_ref):
        z = alpha * x_ref[...] + _dot(m_ref[...], w_ref[...], _NN)
        mu = jnp.mean(z, axis=1, keepdims=True)
        zc = z - mu
        var = jnp.mean(zc * zc, axis=1, keepdims=True)
        xn = zc * lax.rsqrt(var + LN_EPS) * g_ref[...] + b_ref[...]
        xn_ref[...] = xn
        xb_ref[...] = xn.astype(BF16)
        z_ref[...] = z

    rows = pl.BlockSpec((tr, D), lambda i: (i, 0))
    vec = pl.BlockSpec((1, D), lambda i: (0, 0))
    return pl.pallas_call(
        body, name="out_proj_deepnorm", grid=(S // tr,),
        in_specs=[rows, pl.BlockSpec((D, D), lambda i: (0, 0)), rows, vec, vec],
        out_specs=[rows, rows, rows],
        out_shape=[jax.ShapeDtypeStruct((S, D), F32), jax.ShapeDtypeStruct((S, D), BF16),
                   jax.ShapeDtypeStruct((S, D), F32)],
        compiler_params=_params(("parallel",)),
    )(mix, w, x, g, b)


def _deepnorm_bwd(dxn, z, g, w, *, D, S):
    tr = _tile(S, 256, 8)

    def body(dx_ref, z_ref, g_ref, w_ref, dz_ref, dzb_ref, dm_ref, sg_ref, sb_ref):
        i = pl.program_id(0)
        dx, zv = dx_ref[...], z_ref[...]
        mu = jnp.mean(zv, axis=1, keepdims=True)
        zc = zv - mu
        rstd = lax.rsqrt(jnp.mean(zc * zc, axis=1, keepdims=True) + LN_EPS)
        u = zc * rstd
        du = dx * g_ref[...]
        dz = rstd * (du - jnp.mean(du, axis=1, keepdims=True) - u * jnp.mean(du * u, axis=1, keepdims=True))
        dzb = dz.astype(BF16)
        dz_ref[...] = dz
        dzb_ref[...] = dzb
        dm_ref[...] = _dot(dzb, w_ref[...], _NT)
        _acc_rows(i, sg_ref, jnp.sum(dx * u, axis=0, keepdims=True))
        _acc_rows(i, sb_ref, jnp.sum(dx, axis=0, keepdims=True))

    rows = pl.BlockSpec((tr, D), lambda i: (i, 0))
    vec = pl.BlockSpec((1, D), lambda i: (0, 0))
    return pl.pallas_call(
        body, name="deepnorm_bwd", grid=(S // tr,),
        in_specs=[rows, rows, vec, pl.BlockSpec((D, D), lambda i: (0, 0), pipeline_mode=pl.Buffered(1))],
        out_specs=[rows, rows, rows, vec, vec],
        out_shape=[jax.ShapeDtypeStruct((S, D), F32), jax.ShapeDtypeStruct((S, D), BF16),
                   jax.ShapeDtypeStruct((S, D), F32), jax.ShapeDtypeStruct((1, D), F32),
                   jax.ShapeDtypeStruct((1, D), F32)],
        compiler_params=_params(("arbitrary",)),
    )(dxn, z, g, w)


def _loss_head(y, target, *, D, S):
    tr = _tile(S, 256, 8)

    def body(y_ref, t_ref, d_ref, l_ref):
        err = y_ref[...] - t_ref[...]
        d_ref[...] = err * (1.0 / D)
        part = 0.5 * jnp.sum(jnp.sum(err * err, axis=1, keepdims=True) * (1.0 / D), axis=0, keepdims=True)
        _acc_rows(pl.program_id(0), l_ref, jnp.broadcast_to(part, (1, LANES)))

    return pl.pallas_call(
        body, name="loss_head", grid=(S // tr,),
        in_specs=[pl.BlockSpec((tr, D), lambda i: (i, 0))] * 2,
        out_specs=[pl.BlockSpec((tr, D), lambda i: (i, 0)), pl.BlockSpec((1, LANES), lambda i: (0, 0))],
        out_shape=[jax.ShapeDtypeStruct((S, D), F32), jax.ShapeDtypeStruct((1, LANES), F32)],
        compiler_params=_params(("arbitrary",)),
    )(y, target)


_MESH = pl.DeviceIdType.MESH
_ANY = pl.BlockSpec(memory_space=pl.ANY)


class _Gather:
    def __init__(self, xs):
        self.ins = list(xs)
        self.outs = [jax.ShapeDtypeStruct((NDEV,) + x.shape, x.dtype) for x in xs]

    def _ctx(self, a, x_ref, out_ref, send_sems, recv_sems):
        x, y, c = lax.axis_index("x"), lax.axis_index("y"), lax.axis_index("c")
        me, sibling = (x, y, c), (x, y, 1 - c)
        chips = [(1 - x, y), (x, 1 - y), (1 - x, 1 - y)]

        def slot(px, py, pc):
            return out_ref.at[4 * px + 2 * py + pc]

        def copy(k, block, to, src=None):
            return pltpu.make_async_remote_copy(
                src_ref=slot(*block) if src is None else src, dst_ref=slot(*block),
                send_sem=send_sems.at[7 * a + k], recv_sem=recv_sems.at[7 * a + k], device_id=to, device_id_type=_MESH)

        def first():
            return [copy(0, me, sibling, src=x_ref)] + [copy(1 + j, me, (*chip, c), src=x_ref)
                                                        for j, chip in enumerate(chips)]

        def passed():
            return [copy(4 + j, (*chip, c), sibling) for j, chip in enumerate(chips)]

        return c, me, sibling, chips, slot, copy, first, passed

    def start(self, in_refs, out_refs, send_sems, recv_sems, local_sems):
        for a, (x_ref, out_ref) in enumerate(zip(in_refs, out_refs)):
            _, me, _, _, slot, _, first, _ = self._ctx(a, x_ref, out_ref, send_sems, recv_sems)
            pltpu.make_async_copy(x_ref, slot(*me), local_sems.at[a]).start()
            for cp in first():
                cp.start()

    def middle(self, in_refs, out_refs, send_sems, recv_sems, local_sems):
        for a, (x_ref, out_ref) in enumerate(zip(in_refs, out_refs)):
            c, me, _, chips, _, copy, _, passed = self._ctx(a, x_ref, out_ref, send_sems, recv_sems)
            for j, (chip, cp) in enumerate(zip(chips, passed())):
                copy(1 + j, (*chip, c), me).wait_recv()
                cp.start()

    def finish(self, in_refs, out_refs, send_sems, recv_sems, local_sems):
        for a, (x_ref, out_ref) in enumerate(zip(in_refs, out_refs)):
            c, me, sibling, chips, slot, copy, first, passed = self._ctx(a, x_ref, out_ref, send_sems, recv_sems)
            copy(0, sibling, me).wait_recv()
            for j, chip in enumerate(chips):
                copy(4 + j, (*chip, 1 - c), me).wait_recv()
            for cp in first() + passed():
                cp.wait_send()
            pltpu.make_async_copy(x_ref, slot(*me), local_sems.at[a]).wait()


class _Exchange:
    def __init__(self, gs):
        self.ins = list(gs)
        self.outs = [jax.ShapeDtypeStruct(g.shape, g.dtype) for g in gs]

    def _copies(self, a, g_ref, out_ref, send_sems, recv_sems, arrivals=True):
        x, y, c = lax.axis_index("x"), lax.axis_index("y"), lax.axis_index("c")
        me = 4 * x + 2 * y + c
        out, back = [], []
        for k in range(1, NDEV):
            px, py, pc = x ^ ((k >> 2) & 1), y ^ ((k >> 1) & 1), c ^ (k & 1)
            peer = 4 * px + 2 * py + pc
            sems = dict(send_sem=send_sems.at[7 * a + k - 1], recv_sem=recv_sems.at[7 * a + k - 1],
                        device_id=(px, py, pc), device_id_type=_MESH)
            out.append(pltpu.make_async_remote_copy(src_ref=g_ref.at[peer], dst_ref=out_ref.at[me], **sems))
            if arrivals:
                back.append(pltpu.make_async_remote_copy(src_ref=g_ref.at[me], dst_ref=out_ref.at[peer], **sems))
        return me, out, back

    def start(self, in_refs, out_refs, send_sems, recv_sems, local_sems):
        for a, (g_ref, out_ref) in enumerate(zip(in_refs, out_refs)):
            me, out, _ = self._copies(a, g_ref, out_ref, send_sems, recv_sems, arrivals=False)
            pltpu.make_async_copy(g_ref.at[me], out_ref.at[me], local_sems.at[a]).start()
            for cp in out:
                cp.start()

    def middle(self, in_refs, out_refs, send_sems, recv_sems, local_sems):
        pass

    def finish(self, in_refs, out_refs, send_sems, recv_sems, local_sems):
        for a, (g_ref, out_ref) in enumerate(zip(in_refs, out_refs)):
            me, out, back = self._copies(a, g_ref, out_ref, send_sems, recv_sems)
            for cp in back:
                cp.wait_recv()
            for cp in out:
                cp.wait_send()
            pltpu.make_async_copy(g_ref.at[me], out_ref.at[me], local_sems.at[a]).wait()


def _side_scratch(side):
    n = len(side.ins)
    return [pltpu.SemaphoreType.DMA((7 * n,)), pltpu.SemaphoreType.DMA((7 * n,)), pltpu.SemaphoreType.DMA((n,))]


def _run_side(side, *, name):
    n = len(side.ins)

    def body(*refs):
        ins, outs, sems = refs[:n], refs[n:2 * n], refs[2 * n:]
        side.start(ins, outs, *sems)
        side.middle(ins, outs, *sems)
        side.finish(ins, outs, *sems)

    return pl.pallas_call(
        body, name=name, out_shape=side.outs, in_specs=[_ANY] * n, out_specs=[_ANY] * n,
        scratch_shapes=_side_scratch(side),
    )(*side.ins)


def _all_gather(x, *, name):
    return _run_side(_Gather([x]), name=name)[0]


def _sum_slots(parts, *, name):
    n, R, C = parts.shape
    tr, tc = _tile(R, 64, 16), _tile(C, 2048)

    def body(p_ref, o_ref):
        acc = p_ref[0].astype(F32)
        for k in range(1, n):
            acc = acc + p_ref[k].astype(F32)
        o_ref[...] = acc

    return pl.pallas_call(
        body, name=name, grid=(R // tr, C // tc),
        in_specs=[pl.BlockSpec((n, tr, tc), lambda i, j: (0, i, j))],
        out_specs=pl.BlockSpec((tr, tc), lambda i, j: (i, j)),
        out_shape=jax.ShapeDtypeStruct((R, C), F32),
        compiler_params=_params(("parallel", "parallel")),
    )(parts)


def _adamw(w, g, m, v, *, name):
    R, C = w.shape
    budget = 2 << 20
    tr = R if R * C * 4 <= budget else _tile(R, max(8, (budget // (C * 4)) // 8 * 8), 8)

    def body(w_ref, g_ref, m_ref, v_ref, d_ref, nm_ref, nv_ref):
        gv = g_ref[...]
        nm = ADAM_B1 * m_ref[...] + (1.0 - ADAM_B1) * gv
        nv = ADAM_B2 * v_ref[...] + (1.0 - ADAM_B2) * (gv * gv)
        m_hat = nm / (1.0 - ADAM_B1 ** ADAM_STEP)
        v_hat = nv / (1.0 - ADAM_B2 ** ADAM_STEP)
        d_ref[...] = -ADAM_LR * (m_hat / (jnp.sqrt(v_hat) + ADAM_EPS) + ADAM_WD * w_ref[...])
        nm_ref[...] = nm
        nv_ref[...] = nv

    spec = pl.BlockSpec((tr, C), lambda i: (i, 0))
    return pl.pallas_call(
        body, name=name, grid=(R // tr,), in_specs=[spec] * 4, out_specs=[spec] * 3,
        out_shape=[jax.ShapeDtypeStruct((R, C), F32)] * 3,
        compiler_params=_params(("parallel",)),
    )(w, g, m, v)


def _pad_cols(w, *, nq, ng):
    nf = w.shape[-1] - nq - ng
    pad = jnp.zeros(w.shape[:-1] + (FORGET_PAD - nf,), w.dtype)
    return jnp.concatenate([w[..., :nq], w[..., nq + nf:], w[..., nq:nq + nf], pad], axis=-1)


def _unpad_cols(w, *, nq, ng, nf):
    return jnp.concatenate([w[..., :nq], w[..., nq + ng:nq + ng + nf], w[..., nq:nq + ng]], axis=-1)


def kernel(x, w_in, b_forget, b_gate, w_up_a, w_up_b, w_out, ln_g, ln_b, loss_target, m_w_in, m_b_forget, m_b_gate, m_w_up_a, m_w_up_b, m_w_out, m_ln_g, m_ln_b, v_w_in, v_b_forget, v_b_gate, v_w_up_a, v_w_up_b, v_w_out, v_ln_g, v_ln_b):
    depth = w_in.shape[0]
    _, S, D = x.shape
    WA, WB = w_up_a.shape[1], w_up_b.shape[1]
    nha, nhb = WA // HEAD_DIM, WB // HEAD_DIM
    assert nhb == b_forget.shape[1] and WA == WB and nhb <= HEAD_ROWS
    nq, ng, nf = 4 * WA + 4 * WB, 2 * D, nhb
    assert w_in.shape[2] == nq + nf + ng
    ncp = nq + ng + FORGET_PAD
    off_va, off_za, off_b, off_zb, off_g, off_f = 2 * WA, 3 * WA, 4 * WA, 4 * WA + 3 * WB, nq, nq + ng
    alpha = float((2 * depth) ** 0.25)
    dsh = D // NDEV
    tq_f, tq_b = _tile(S, FOX_TILE_FWD), _tile(S, FOX_TILE_BWD)

    x0 = x[0]
    target = loss_target[0]
    cos, sin = _rope_tables(S)

    w_in_p = _pad_cols(w_in, nq=nq, ng=ng).astype(BF16)
    w_up_a_t = jnp.swapaxes(w_up_a, 1, 2).astype(BF16)
    w_up_b_t = jnp.swapaxes(w_up_b, 1, 2).astype(BF16)
    w_out_c = w_out.astype(BF16)
    W_in, W_ua, W_ub, W_out = ([None] * depth for _ in range(4))

    def others(l):
        return [w_up_a_t[l], w_up_b_t[l], w_out_c[l]]

    def keep_others(l, got):
        W_ua[l], W_ub[l], W_out[l] = got[0].reshape(D, WA), got[1].reshape(D, WB), got[2].reshape(D, D)

    W_in[0] = _run_side(_Gather([w_in_p[0]]), name="gather_w_in_0")[0].reshape(D, ncp)

    bf_rows = [jnp.broadcast_to(jnp.pad(b_forget[l], (0, HEAD_ROWS - nhb))[:, None], (HEAD_ROWS, LANES))
               for l in range(depth)]

    saved = []
    xf, xb = x0, x0.astype(BF16)
    for l in range(depth):
        wanted = (others(0) if l == 0 else []) + ([w_in_p[l + 1]] + others(l + 1) if l + 1 < depth else [])
        if wanted:
            h, got = _mm(xb, W_in[l], name="in_proj_gather", tm=1024, tn=1280, tk=2048, side=_Gather(wanted))
            if l == 0:
                keep_others(0, got[:3])
            if l + 1 < depth:
                W_in[l + 1] = got[-4].reshape(D, ncp)
                keep_others(l + 1, got[-3:])
        else:
            h = _mm(xb, W_in[l], name="in_proj", tm=1024, tn=1280, tk=2048)
        qk_a = _prep_a(h, cos, sin, W=WA, S=S)
        qkv_b = _prep_b(h, off=off_b, W=WB, S=S)
        out_a, lse_a, ga = _band_fwd(qk_a, h, nha=nha, S=S, v_off=off_va, z_off=off_za)
        f3 = _lane_blocks(jnp.pad(h[:, off_f:off_f + nhb].T, ((0, HEAD_ROWS - nhb), (0, 0))))
        c = _lane_unblocks(_forget_cumsum(f3, bf_rows[l]))[:nhb]
        out_b, lq_b, gb = _fox_fwd(qkv_b, c.reshape(nhb, S // tq_f, 1, tq_f), h, z_off=off_zb, nhb=nhb, S=S, tq=tq_f)
        ck = c.reshape(nhb, S // tq_b, 1, tq_b)
        up_a = _mm(ga, W_ua[l], tb=True, name="up_a", tm=1024, tn=1024, tk=1536)
        up_b = _mm(gb, W_ub[l], tb=True, name="up_b", tm=1024, tn=1024, tk=1536)
        bg = b_gate[l][None, :]
        mix = _merge(up_a, up_b, h, bg, g_off=off_g, D=D, S=S)
        xn_f, xn_b, z = _out_proj_deepnorm(mix, W_out[l], xf, ln_g[l][None, :], ln_b[l][None, :],
                                           alpha=alpha, D=D, S=S)
        saved.append(dict(xb=xb, h=h, qk_a=qk_a, qkv_b=qkv_b, out_a=out_a, lse_a=lse_a, ga=ga, f3=f3, ck=ck,
                          out_b=out_b, lq_b=lq_b, gb=gb, up_a=up_a, up_b=up_b, mix=mix, z=z, bg=bg))
        xf, xb = xn_f, xn_b

    dx, loss_row = _loss_head(xf, target, D=D, S=S)

    recv = [None] * depth
    small = [None] * depth
    pending = None
    for l in reversed(range(depth)):
        sv = saved[l]
        h = sv["h"]
        dz, dzb, dmix, d_lng, d_lnb = _deepnorm_bwd(dx, sv["z"], ln_g[l][None, :], W_out[l], D=D, S=S)
        g_w_out = _mm(sv["mix"], dzb, ta=True, name="g_w_out", tm=512, tn=1024, tk=4096, out_dtype=BF16)
        dup_a, dup_b, dgl_a, dgl_b, d_bga, d_bgb = _merge_bwd(dmix, sv["up_a"], sv["up_b"], h, sv["bg"],
                                                              g_off=off_g, D=D, S=S)
        g_w_ua = _mm(dup_a, sv["ga"], ta=True, name="g_w_up_a", tm=512, tn=1536, tk=4096, out_dtype=BF16)
        g_w_ub = _mm(dup_b, sv["gb"], ta=True, name="g_w_up_b", tm=512, tn=1536, tk=4096, out_dtype=BF16)
        do_a, dza, stats_a = _gate_bwd(dup_a, W_ua[l], sv["out_a"], h, sv["lse_a"], z_off=off_za, W=WA, S=S,
                                       name="gate_bwd_a", do_dtype=F32)
        do_b, dzb_, stats_b = _gate_bwd(dup_b, W_ub[l], sv["out_b"], h, sv["lq_b"], z_off=off_zb, W=WB, S=S,
                                        name="gate_bwd_b", do_dtype=BF16)
        dqa, dka, dva = _rope_bwd(*_band_bwd(sv["qk_a"], h, do_a, stats_a, nha=nha, S=S, v_off=off_va),
                                  cos, sin, W=WA, S=S)
        if pending is None:
            dqb, dkb, dvb, dcq, dck = _fox_bwd(sv["qkv_b"], do_b, stats_b, sv["ck"], nhb=nhb, S=S, tq=tq_b)
        else:
            dqb, dkb, dvb, dcq, dck, got = _fox_bwd(sv["qkv_b"], do_b, stats_b, sv["ck"], nhb=nhb, S=S, tq=tq_b,
                                                    side=_Exchange([pending]))
            recv[l + 1] = (got[0], *recv[l + 1])
        head_rows = lambda t: _lane_blocks(jnp.pad(t, ((0, HEAD_ROWS - nhb), (0, 0))))
        df3, d_bf = _forget_bwd(head_rows(dcq[:, ::LANES].T), head_rows(dck.reshape(nhb, S)), sv["f3"], bf_rows[l])
        dfl = jnp.pad(_lane_unblocks(df3)[:nhb].T, ((0, 0), (0, FORGET_PAD - nhb))).astype(BF16)
        dh = jnp.concatenate([dqa, dka, dva, dza, dqb.astype(BF16), dkb, dvb, dzb_, dgl_a, dgl_b, dfl], axis=1)
        chunks = lambda t: t.reshape(NDEV, dsh, t.shape[1])
        g_w_in, got3 = _mm(sv["xb"], dh, ta=True, name="g_w_in_xchg", tm=512, tn=1280, tk=4096, out_dtype=BF16,
                           side=_Exchange([chunks(g_w_ua), chunks(g_w_ub), chunks(g_w_out)]))
        if l > 0:
            dx = _mm(dh, W_in[l], tb=True, name="d_x", tm=1024, tn=512, tk=3328, add=dz, add_scale=alpha)
            pending, recv[l] = chunks(g_w_in), tuple(got3)
        else:
            dx, got1 = _mm(dh, W_in[l], tb=True, name="d_x_xchg", tm=1024, tn=512, tk=3328, add=dz, add_scale=alpha,
                           side=_Exchange([chunks(g_w_in)]))
            recv[l] = (got1[0], *got3)
        small[l] = jnp.concatenate([d_bf[:nhb, 0], jnp.zeros((LANES - nhb,), F32), d_bga[0], d_bgb[0],
                                    d_lng[0], d_lnb[0]])

    n_small = LANES + 2 * D + 2 * D
    flat = jnp.concatenate(small + [loss_row[0]])
    rows = -(-flat.shape[0] // (8 * LANES)) * 8
    flat = jnp.pad(flat, (0, rows * LANES - flat.shape[0])).reshape(rows, LANES)
    tot = _sum_slots(_all_gather(flat, name="gather_small"), name="sum_small").reshape(-1)
    loss = tot[depth * n_small]
    sm = tot[:depth * n_small].reshape(depth, n_small)
    g_bf, g_bg = sm[:, :nhb], sm[:, LANES:LANES + 2 * D]
    g_lg, g_lb = sm[:, LANES + 2 * D:LANES + 3 * D], sm[:, LANES + 3 * D:]

    g_in = jnp.stack([_unpad_cols(_sum_slots(recv[l][0], name="sum_w_in"), nq=nq, ng=ng, nf=nf) for l in range(depth)])
    g_ua = jnp.stack([_sum_slots(recv[l][1], name="sum_w_up_a").T for l in range(depth)])
    g_ub = jnp.stack([_sum_slots(recv[l][2], name="sum_w_up_b").T for l in range(depth)])
    g_out = jnp.stack([_sum_slots(recv[l][3], name="sum_w_out") for l in range(depth)])

    def adam(w, g, m, v, name):
        shp = w.shape
        d_, m_, v_ = _adamw(*[t.reshape(-1, shp[-1]) for t in (w, g, m, v)], name=name)
        return d_.reshape(shp), m_.reshape(shp), v_.reshape(shp)

    grads = [g_in, g_bf, g_bg, g_ua, g_ub, g_out, g_lg, g_lb]
    ws = [w_in, b_forget, b_gate, w_up_a, w_up_b, w_out, ln_g, ln_b]
    ms = [m_w_in, m_b_forget, m_b_gate, m_w_up_a, m_w_up_b, m_w_out, m_ln_g, m_ln_b]
    vs = [v_w_in, v_b_forget, v_b_gate, v_w_up_a, v_w_up_b, v_w_out, v_ln_g, v_ln_b]
    names = ["w_in", "b_forget", "b_gate", "w_up_a", "w_up_b", "w_out", "ln_g", "ln_b"]
    upd = [adam(w, g, m, v, f"adamw_{n}") for w, g, m, v, n in zip(ws, grads, ms, vs, names)]
    return (loss, dx[None], *grads, *[u[0] for u in upd], *[u[1] for u in upd], *[u[2] for u in upd])
```

```python
import jax
import jax.numpy as jnp
from jax import lax
from jax.experimental import pallas as pl
from jax.experimental.pallas import tpu as pltpu

F32 = jnp.float32
BF16 = jnp.bfloat16

NDEV = 8
HEAD_DIM = 128
LANES = 128
Q_BLOCK = 128
DILATIONS = (1, 4, 16)
ROPE_THETA = 10000.0
LN_EPS = 1e-5
ATT_SCALE = HEAD_DIM ** -0.5
NEG = -1e30
FORGET_PAD = 256
BAND_CHUNK = 2048
BAND_UNROLL = 16
FOX_TILE_FWD = 2048
FOX_TILE_BWD = 1024
FOX_ROWS = 512
GATE_BWD_ROWS = 512
FOX_DIAG_SUB = 4
HEAD_ROWS = 16
ADAM_LR, ADAM_B1, ADAM_B2, ADAM_EPS, ADAM_WD, ADAM_STEP = 0.001, 0.9, 0.999, 1e-08, 0.01, 10
VMEM_LIMIT = 56 * 1024 * 1024


def _tile(n, pref, mult=LANES):
    if n <= pref:
        return n
    t = (pref // mult) * mult
    while t >= mult:
        if n % t == 0:
            return t
        t -= mult
    return n


def _params(sem=None):
    return pltpu.CompilerParams(dimension_semantics=sem, vmem_limit_bytes=VMEM_LIMIT)


def _sigmoid(z):
    return 1.0 / (1.0 + jnp.exp(-z))


_NT = (((1,), (1,)), ((), ()))
_NN = (((1,), (0,)), ((), ()))
_TN = (((0,), (0,)), ((), ()))


def _dot(a, b, dims):
    return lax.dot_general(a, b, dims, preferred_element_type=F32)


def _mm(a, b, *, name, ta=False, tb=False, out_dtype=F32, tm=512, tn=512, tk=512, add=None, add_scale=1.0,
        side=None):
    if ta:
        K, M = a.shape
    else:
        M, K = a.shape
    if tb:
        N, K2 = b.shape
    else:
        K2, N = b.shape
    assert K == K2, (a.shape, b.shape, ta, tb)
    tm, tn, tk = _tile(M, tm), _tile(N, tn), _tile(K, tk)
    ni, nj, nk = M // tm, N // tn, K // tk
    dims = (((0 if ta else 1,), (1 if tb else 0,)), ((), ()))
    n_side = 0 if side is None else len(side.ins)
    n_in = 2 + (add is not None)

    def body(*refs):
        a_ref, b_ref = refs[:2]
        add_ref = refs[2] if add is not None else None
        side_in = refs[n_in:n_in + n_side]
        o_ref = refs[n_in + n_side]
        side_out = refs[n_in + n_side + 1:n_in + 2 * n_side + 1]
        acc_ref = refs[n_in + 2 * n_side + 1]
        sems = refs[n_in + 2 * n_side + 2:]
        k = pl.program_id(2)
        if side is not None:
            @pl.when((pl.program_id(0) == 0) & (pl.program_id(1) == 0) & (k == 0))
            def _():
                side.start(side_in, side_out, *sems)

        part = _dot(a_ref[...].astype(BF16), b_ref[...].astype(BF16), dims)

        @pl.when(k == 0)
        def _():
            acc_ref[...] = part

        @pl.when(k > 0)
        def _():
            acc_ref[...] += part

        @pl.when(k == nk - 1)
        def _():
            r = acc_ref[...]
            if add_ref is not None:
                r = r + add_scale * add_ref[...]
            o_ref[...] = r.astype(out_dtype)

        if side is not None:
            step = (pl.program_id(0) * nj + pl.program_id(1)) * nk + k
            steps = ni * nj * nk

            @pl.when(step == max((3 * steps) // 4 - 1, 0))
            def _():
                side.middle(side_in, side_out, *sems)

            @pl.when(step == steps - 1)
            def _():
                side.finish(side_in, side_out, *sems)

    a_spec = pl.BlockSpec((tk, tm), lambda i, j, k: (k, i)) if ta else pl.BlockSpec((tm, tk), lambda i, j, k: (i, k))
    b_spec = pl.BlockSpec((tn, tk), lambda i, j, k: (j, k)) if tb else pl.BlockSpec((tk, tn), lambda i, j, k: (k, j))
    o_spec = pl.BlockSpec((tm, tn), lambda i, j, k: (i, j))
    in_specs, args = [a_spec, b_spec], [a, b]
    if add is not None:
        in_specs.append(o_spec)
        args.append(add)
    out_shape = jax.ShapeDtypeStruct((M, N), out_dtype)
    scratch = [pltpu.VMEM((tm, tn), F32)]
    if side is None:
        return pl.pallas_call(
            body, name=name, grid=(ni, nj, nk), in_specs=in_specs, out_specs=o_spec, out_shape=out_shape,
            scratch_shapes=scratch, compiler_params=_params(("parallel", "parallel", "arbitrary")),
        )(*args)
    res = pl.pallas_call(
        body, name=name, grid=(ni, nj, nk), in_specs=in_specs + [_ANY] * n_side,
        out_specs=[o_spec] + [_ANY] * n_side, out_shape=[out_shape] + side.outs,
        scratch_shapes=scratch + _side_scratch(side),
        compiler_params=_params(("arbitrary", "arbitrary", "arbitrary")),
    )(*args, *side.ins)
    return res[0], res[1:]


def _ew(fn, ins, outs, *, rows, cols, tr, tc, name):
    assert rows % tr == 0 and cols % tc == 0

    def spec(kind, off):
        assert off % tc == 0, (off, tc)
        ob = off // tc
        if kind == "tile":
            return pl.BlockSpec((tr, tc), lambda j, i: (i, j + ob))
        if kind == "tab":
            return pl.BlockSpec((tr, LANES), lambda j, i: (i, 0))
        return pl.BlockSpec((1, tc), lambda j, i: (0, j + ob))

    n_in = len(ins)

    def body(*refs):
        fn(pl.program_id(1), refs[:n_in], refs[n_in:])

    return pl.pallas_call(
        body, name=name, grid=(cols // tc, rows // tr),
        in_specs=[spec(k, o) for _, k, o in ins],
        out_specs=[spec(k, o) for _, _, k, o in outs],
        out_shape=[jax.ShapeDtypeStruct(s, d) for s, d, _, _ in outs],
        compiler_params=_params(("parallel", "arbitrary")),
    )(*[a for a, _, _ in ins])


def _heads(t):
    return [t[:, e:e + LANES] for e in range(0, t.shape[1], LANES)]


def _rope_tables(S):
    half = HEAD_DIM // 2
    inv_freq = ROPE_THETA ** (-jnp.arange(half, dtype=F32) / half)
    ang = jnp.arange(S, dtype=jnp.int32).astype(F32)[:, None] * inv_freq[None, :]
    cos, sin = jnp.cos(ang), jnp.sin(ang)
    return jnp.concatenate([cos, cos], axis=1), jnp.concatenate([-sin, sin], axis=1)


def _rope(t, cs, sn):
    return jnp.concatenate([g * cs + pltpu.roll(g, HEAD_DIM // 2, 1) * sn for g in _heads(t)], axis=1)


def _rope_t(t, cs, sn):
    return jnp.concatenate([g * cs + pltpu.roll(g * sn, HEAD_DIM // 2, 1) for g in _heads(t)], axis=1)


def _prep_a(h, cos, sin, *, W, S):
    tr, tc = _tile(S, 1024, 8), _tile(W, 512)

    def fn(i, ins, outs):
        outs[0][...] = _rope(ins[0][...], ins[1][...], ins[2][...])

    return _ew(fn, [(h, "tile", 0), (cos, "tab", 0), (sin, "tab", 0)], [((S, 2 * W), F32, "tile", 0)],
               rows=S, cols=2 * W, tr=tr, tc=tc, name="prep_a")[0]


def _prep_b(h, *, off, W, S):
    tr, tc = _tile(S, 1024, 8), _tile(W, 512)

    def fn(i, ins, outs):
        scale = jnp.where(pl.program_id(0) * tc < W, ATT_SCALE, 1.0).astype(F32)
        outs[0][...] = (ins[0][...] * scale).astype(BF16)

    return _ew(fn, [(h, "tile", off)], [((S, 3 * W), BF16, "tile", 0)], rows=S, cols=3 * W, tr=tr, tc=tc,
               name="prep_b")[0]


def _rope_bwd(dq, dk, dv, cos, sin, *, W, S):
    tr, tc = _tile(S, 1024, 8), _tile(W, 512)

    def fn(i, ins, outs):
        cs, sn = ins[3][...], ins[4][...]
        outs[0][...] = _rope_t(ins[0][...], cs, sn).astype(BF16)
        outs[1][...] = _rope_t(ins[1][...], cs, sn).astype(BF16)
        outs[2][...] = ins[2][...].astype(BF16)

    ins = [(dq, "tile", 0), (dk, "tile", 0), (dv, "tile", 0), (cos, "tab", 0), (sin, "tab", 0)]
    return _ew(fn, ins, [((S, W), BF16, "tile", 0)] * 3, rows=S, cols=W, tr=tr, tc=tc, name="rope_bwd")


def _band_biases():
    dist = (lax.broadcasted_iota(jnp.int32, (Q_BLOCK, 2 * Q_BLOCK), 0)
            - lax.broadcasted_iota(jnp.int32, (Q_BLOCK, 2 * Q_BLOCK), 1))
    first = jnp.where(dist >= 0, 0.0, NEG).astype(F32)
    other = jnp.where((dist + Q_BLOCK >= 0) & (dist <= 0), 0.0, NEG).astype(F32)
    return first, other


def _band_rows(it, c, d, nbc, biases):
    r, n = (0, it) if d == 1 else (it // nbc, it % nbc)
    ng = c * nbc + n
    k0 = pl.multiple_of(jnp.maximum(ng - 1, 0) * Q_BLOCK, Q_BLOCK)
    bias = jnp.where(ng == 0, biases[0], biases[1])
    if d == 1:
        return pl.ds(pl.multiple_of(it * Q_BLOCK, Q_BLOCK), Q_BLOCK), pl.ds(k0, 2 * Q_BLOCK), bias
    return (pl.ds(r + d * Q_BLOCK * n, Q_BLOCK, stride=d), pl.ds(r + d * k0, 2 * Q_BLOCK, stride=d), bias)


def _band_chunk(S):
    ch = min(S, BAND_CHUNK)
    assert S % ch == 0 and ch % (Q_BLOCK * max(DILATIONS)) == 0 and S >= 2 * Q_BLOCK * max(DILATIONS)
    return ch


def _band_fwd(qk, h, *, nha, S, v_off, z_off):
    CH = _band_chunk(S)
    W = nha * LANES
    slab = 256

    def body(q_ref, k_ref, v_ref, z_ref, o_ref, lse_ref, g_ref, *scratch):
        os_refs, ls_refs = scratch[:len(DILATIONS)], scratch[len(DILATIONS):]
        c = pl.program_id(1)
        biases = _band_biases()
        for gi, d in enumerate(DILATIONS):
            nbc = CH // (Q_BLOCK * d)

            def step(it, carry, gi=gi, d=d, nbc=nbc):
                rq, rk, bias = _band_rows(it, c, d, nbc, biases)
                q = q_ref[rq, :].astype(BF16)
                k = k_ref[rk, :].astype(BF16)
                v = v_ref[rk, :].astype(BF16)
                s = _dot(q, k, _NT) * ATT_SCALE + bias
                m = jnp.max(s, axis=1, keepdims=True)
                p = jnp.exp(s - m)
                den = jnp.sum(p, axis=1, keepdims=True)
                os_refs[gi][rq, :] = _dot(p.astype(BF16), v, _NN) / den
                ls_refs[gi][rq, :] = jnp.broadcast_to(m + jnp.log(den), (Q_BLOCK, LANES))
                return carry

            lax.fori_loop(0, CH // Q_BLOCK, step, 0, unroll=BAND_UNROLL)

        def mix(t, carry):
            rows = pl.ds(pl.multiple_of(t * slab, slab), slab)
            a, b, cc = (r[rows, :] for r in ls_refs)
            m = jnp.maximum(jnp.maximum(a, b), cc)
            ea, eb, ec = jnp.exp(a - m), jnp.exp(b - m), jnp.exp(cc - m)
            den = ea + eb + ec
            out = (ea * os_refs[0][rows, :] + eb * os_refs[1][rows, :] + ec * os_refs[2][rows, :]) / den
            z = z_ref[rows, :]
            o_ref[rows, :] = out
            lse_ref[rows, :] = m + jnp.log(den)
            g_ref[rows, :] = (out * (z * _sigmoid(z))).astype(BF16)
            return carry

        lax.fori_loop(0, CH // slab, mix, 0)

    chunk = pl.BlockSpec((CH, LANES), lambda hd, c: (c, hd))
    return pl.pallas_call(
        body, name="band_fwd", grid=(nha, S // CH),
        in_specs=[chunk,
                  pl.BlockSpec((S, LANES), lambda hd, c: (0, nha + hd)),
                  pl.BlockSpec((S, LANES), lambda hd, c: (0, v_off // LANES + hd)),
                  pl.BlockSpec((CH, LANES), lambda hd, c: (c, z_off // LANES + hd))],
        out_specs=[chunk, chunk, chunk],
        out_shape=[jax.ShapeDtypeStruct((S, W), F32), jax.ShapeDtypeStruct((S, W), F32),
                   jax.ShapeDtypeStruct((S, W), BF16)],
        scratch_shapes=[pltpu.VMEM((CH, LANES), F32)] * (2 * len(DILATIONS)),
        compiler_params=_params(("parallel", "arbitrary")),
    )(qk, qk, h, h)


def _band_bwd(qk, h, do, stats, *, nha, S, v_off):
    CH = _band_chunk(S)
    W = nha * LANES

    def body(q_ref, k_ref, v_ref, do_ref, st_ref, dq_ref, dk_ref, dv_ref):
        c = pl.program_id(1)

        @pl.when(c == 0)
        def _():
            dk_ref[...] = jnp.zeros_like(dk_ref)
            dv_ref[...] = jnp.zeros_like(dv_ref)

        biases = _band_biases()
        for gi, d in enumerate(DILATIONS):
            nbc = CH // (Q_BLOCK * d)

            def step(it, carry, gi=gi, d=d, nbc=nbc):
                rq, rk, bias = _band_rows(it, c, d, nbc, biases)
                q = q_ref[rq, :].astype(BF16)
                k = k_ref[rk, :].astype(BF16)
                v = v_ref[rk, :].astype(BF16)
                g = do_ref[rq, :].astype(BF16)
                st = st_ref[rq, :]
                p = jnp.exp(_dot(q, k, _NT) * ATT_SCALE + (bias - st[:, 0:1]))
                ds = (p * (_dot(g, v, _NT) - st[:, 1:2]) * ATT_SCALE).astype(BF16)
                dq = _dot(ds, k, _NN)
                if gi == 0:
                    dq_ref[rq, :] = dq
                else:
                    dq_ref[rq, :] += dq
                dk_ref[rk, :] += _dot(ds, q, _TN)
                dv_ref[rk, :] += _dot(p.astype(BF16), g, _TN)
                return carry

            lax.fori_loop(0, CH // Q_BLOCK, step, 0, unroll=BAND_UNROLL)

    chunk = pl.BlockSpec((CH, LANES), lambda hd, c: (c, hd))
    whole = pl.BlockSpec((S, LANES), lambda hd, c: (0, hd))
    return pl.pallas_call(
        body, name="band_bwd", grid=(nha, S // CH),
        in_specs=[chunk,
                  pl.BlockSpec((S, LANES), lambda hd, c: (0, nha + hd)),
                  pl.BlockSpec((S, LANES), lambda hd, c: (0, v_off // LANES + hd)),
                  chunk, chunk],
        out_specs=[chunk, whole, whole],
        out_shape=[jax.ShapeDtypeStruct((S, W), F32)] * 3,
        compiler_params=_params(("parallel", "arbitrary")),
    )(qk, qk, h, do, stats)


def _gate_bwd(dup, w, o, h, lane0, *, z_off, W, S, name, do_dtype):
    tr = _tile(S, GATE_BWD_ROWS, 8)
    D = dup.shape[1]
    assert z_off % W == 0

    def body(d_ref, w_ref, o_ref, z_ref, l0_ref, do_ref, dz_ref, st_ref):
        g = _dot(d_ref[...], w_ref[...], _NN)
        out, z = o_ref[...], z_ref[...]
        sg = _sigmoid(z)
        do = g * (z * sg)
        do_ref[...] = do.astype(do_dtype)
        dz_ref[...] = (g * out * (sg * (1.0 + z * (1.0 - sg)))).astype(BF16)
        delta = jnp.concatenate([jnp.broadcast_to(jnp.sum(t, axis=1, keepdims=True), t.shape)
                                 for t in _heads(do * out)], axis=1)
        first = lax.broadcasted_iota(jnp.int32, delta.shape, 1) % LANES == 0
        st_ref[...] = jnp.where(first, l0_ref[...], delta)

    rows = pl.BlockSpec((tr, W), lambda i: (i, 0))
    return pl.pallas_call(
        body, name=name, grid=(S // tr,),
        in_specs=[pl.BlockSpec((tr, D), lambda i: (i, 0)),
                  pl.BlockSpec((D, W), lambda i: (0, 0), pipeline_mode=pl.Buffered(1)),
                  rows, pl.BlockSpec((tr, W), lambda i: (i, z_off // W)), rows],
        out_specs=[rows, rows, rows],
        out_shape=[jax.ShapeDtypeStruct((S, W), do_dtype), jax.ShapeDtypeStruct((S, W), BF16),
                   jax.ShapeDtypeStruct((S, W), F32)],
        compiler_params=_params(("parallel",)),
    )(dup, w, o, h, lane0)


def _split3(x):
    hi = x.astype(BF16)
    r = x - hi.astype(F32)
    mid = r.astype(BF16)
    lo = (r - mid.astype(F32)).astype(BF16)
    return hi, mid, lo


def _tri_dot(x, tri):
    hi, mid, lo = _split3(x)
    return _dot(hi, tri, _NN) + (_dot(mid, tri, _NN) + _dot(lo, tri, _NN))


def _log1p(u):
    w = 1.0 + u
    return jnp.where(w == 1.0, u, jnp.log(w) * (u / jnp.where(w == 1.0, 1.0, w - 1.0)))


def _lane_blocks(t):
    R, S = t.shape
    return t.reshape(R, S // LANES, LANES).transpose(1, 0, 2)


def _lane_unblocks(t):
    nb, R, _ = t.shape
    return t.transpose(1, 0, 2).reshape(R, nb * LANES)


def _forget_cumsum(f3, bf):
    nb, R, _ = f3.shape

    def body(f_ref, b_ref, c_ref):
        row = lax.broadcasted_iota(jnp.int32, (LANES, LANES), 0)
        colm = lax.broadcasted_iota(jnp.int32, (LANES, LANES), 1)
        tri = (row <= colm).astype(BF16)
        bias = b_ref[...]

        def step(n, carry):
            z = f_ref[n] + bias
            logf = jnp.minimum(z, 0.0) - _log1p(jnp.exp(-jnp.abs(z)))
            c = _tri_dot(logf, tri) + carry
            c_ref[n] = c
            return jnp.broadcast_to(c[:, LANES - 1:LANES], (R, LANES))

        lax.fori_loop(0, nb, step, jnp.zeros((R, LANES), F32))

    return pl.pallas_call(
        body, name="forget_cumsum", out_shape=jax.ShapeDtypeStruct(f3.shape, F32),
        compiler_params=_params(),
    )(f3, bf)


def _forget_bwd(dcq3, dck3, f3, bf):
    nb, R, _ = f3.shape

    def body(dcq_ref, dck_ref, f_ref, b_ref, df_ref, db_ref):
        row = lax.broadcasted_iota(jnp.int32, (LANES, LANES), 0)
        colm = lax.broadcasted_iota(jnp.int32, (LANES, LANES), 1)
        tri = (row >= colm).astype(BF16)
        bias = b_ref[...]

        def step(t, carry):
            tail, tot = carry
            n = nb - 1 - t
            r = _tri_dot(dcq_ref[n] + dck_ref[n], tri) + tail
            df = r * _sigmoid(-(f_ref[n] + bias))
            df_ref[n] = df
            tot = tot + jnp.broadcast_to(jnp.sum(df, axis=1, keepdims=True), (R, LANES))
            return jnp.broadcast_to(r[:, 0:1], (R, LANES)), tot

        _, tot = lax.fori_loop(0, nb, step, (jnp.zeros((R, LANES), F32), jnp.zeros((R, LANES), F32)))
        db_ref[...] = tot

    return pl.pallas_call(
        body, name="forget_bwd",
        out_shape=[jax.ShapeDtypeStruct(f3.shape, F32), jax.ShapeDtypeStruct((R, LANES), F32)],
        compiler_params=_params(),
    )(dcq3, dck3, f3, bf)


def _causal(t):
    return lax.broadcasted_iota(jnp.int32, (t, t), 0) >= lax.broadcasted_iota(jnp.int32, (t, t), 1)


def _fox_fwd(qkv, ck, h, *, z_off, nhb, S, tq):
    nq = S // tq
    ts = td = min(FOX_ROWS, tq)
    n_sub = n_diag = tq // ts

    def body(q_ref, k_ref, v_ref, ck_ref, z_ref, o_ref, lq_ref, g_ref):
        i = pl.program_id(1)
        qs = [q_ref[r * ts:(r + 1) * ts, :] for r in range(n_sub)]

        def absorb(q, k, v, ckj, m, l, acc, row0=None):
            s = _dot(q, k, _NT) - ckj
            if row0 is not None:
                rows = row0 + lax.broadcasted_iota(jnp.int32, s.shape, 0)
                s = jnp.where(rows >= lax.broadcasted_iota(jnp.int32, s.shape, 1), s, NEG)
            m_new = jnp.maximum(m, jnp.max(s, axis=1, keepdims=True))
            alpha = jnp.exp(m - m_new)
            p = jnp.exp(s - m_new)
            return m_new, alpha * l + jnp.sum(p, axis=1, keepdims=True), alpha * acc + _dot(p.astype(BF16), v, _NN)

        def step(j, carry):
            off = pl.multiple_of(j * tq, tq)
            k, v, ckj = k_ref[pl.ds(off, tq), :], v_ref[pl.ds(off, tq), :], ck_ref[j]
            return tuple(absorb(qs[r], k, v, ckj, *carry[r]) for r in range(n_sub))

        init = tuple((jnp.full((ts, 1), NEG, F32), jnp.zeros((ts, 1), F32), jnp.zeros((ts, LANES), F32))
                     for _ in range(n_sub))
        carry = lax.fori_loop(0, i, step, init)

        off = pl.multiple_of(i * tq, tq)
        k, v, cki = k_ref[pl.ds(off, tq), :], v_ref[pl.ds(off, tq), :], ck_ref[i]
        for f in range(n_diag):
            rows = slice(f * td, (f + 1) * td)
            part = slice(f * td % ts, f * td % ts + td)
            m, l, acc = (t[part] for t in carry[f * td // ts])
            nk = (f + 1) * td
            m, l, acc = absorb(q_ref[rows, :], k[:nk], v[:nk], cki[:, :nk], m, l, acc, row0=f * td)
            out = acc / l
            z = z_ref[rows, :]
            o_ref[rows, :] = out
            lq_ref[rows, :] = jnp.broadcast_to(m + jnp.log(l), (td, LANES))
            g_ref[rows, :] = (out * (z * _sigmoid(z))).astype(BF16)

    blk = pl.BlockSpec((tq, LANES), lambda hd, i: (i, hd))
    return pl.pallas_call(
        body, name="fox_fwd", grid=(nhb, nq),
        in_specs=[blk,
                  pl.BlockSpec((S, LANES), lambda hd, i: (0, nhb + hd)),
                  pl.BlockSpec((S, LANES), lambda hd, i: (0, 2 * nhb + hd)),
                  pl.BlockSpec((None, nq, 1, tq), lambda hd, i: (hd, 0, 0, 0)),
                  pl.BlockSpec((tq, LANES), lambda hd, i: (i, z_off // LANES + hd))],
        out_specs=[blk, blk, blk],
        out_shape=[jax.ShapeDtypeStruct((S, nhb * LANES), F32), jax.ShapeDtypeStruct((S, nhb * LANES), F32),
                   jax.ShapeDtypeStruct((S, nhb * LANES), BF16)],
        compiler_params=_params(("parallel", "arbitrary")),
    )(qkv, qkv, qkv, ck, h)


def _fox_bwd(qkv, do, stats, ck, *, nhb, S, tq, side=None):
    nq = S // tq
    ts, td = min(FOX_ROWS, tq), tq // FOX_DIAG_SUB
    n_side = 0 if side is None else len(side.ins)

    def body(*refs):
        q_ref, k_ref, v_ref, do_ref, st_ref, ck_ref = refs[:6]
        side_in = refs[6:6 + n_side]
        dq_ref, dk_ref, dv_ref, dr_ref, dc_ref = refs[6 + n_side:11 + n_side]
        side_out, sems = refs[11 + n_side:11 + 2 * n_side], refs[11 + 2 * n_side:]
        j = pl.program_id(1)
        if side is not None:
            @pl.when((pl.program_id(0) == 0) & (j == 0))
            def _():
                side.start(side_in, side_out, *sems)

        @pl.when(j == 0)
        def _():
            dq_ref[...] = jnp.zeros_like(dq_ref)
            dr_ref[...] = jnp.zeros_like(dr_ref)

        k = k_ref[...]
        v = v_ref[...]
        ckv = ck_ref[...]

        def group(start, n, nk, carry, row0=None):
            dk, dv, dc = carry
            rows = pl.ds(pl.multiple_of(start, n), n)
            q, g, st = q_ref[rows, :], do_ref[rows, :], st_ref[rows, :]
            p = jnp.exp(_dot(q, k[:nk], _NT) - ckv[:, :nk] - st[:, 0:1])
            if row0 is not None:
                rr = row0 + lax.broadcasted_iota(jnp.int32, (n, nk), 0)
                p = jnp.where(rr >= lax.broadcasted_iota(jnp.int32, (n, nk), 1), p, 0.0)
            ds = p * (_dot(g, v[:nk], _NT) - st[:, 1:2])
            dr_ref[rows, :] += jnp.broadcast_to(jnp.sum(ds, axis=1, keepdims=True), (n, LANES))
            dsb = ds.astype(BF16)
            dq_ref[rows, :] += _dot(dsb, k[:nk], _NN)
            upd = (_dot(dsb, q, _TN), _dot(p.astype(BF16), g, _TN), -jnp.sum(ds, axis=0, keepdims=True))
            if nk < tq:
                upd = (jnp.concatenate([upd[0], jnp.zeros((tq - nk, LANES), F32)], axis=0),
                       jnp.concatenate([upd[1], jnp.zeros((tq - nk, LANES), F32)], axis=0),
                       jnp.concatenate([upd[2], jnp.zeros((1, tq - nk), F32)], axis=1))
            return dk + upd[0], dv + upd[1], dc + upd[2]

        def step(i, carry):
            for r in range(tq // ts):
                carry = group(i * tq + r * ts, ts, tq, carry)
            return carry

        carry = (jnp.zeros((tq, LANES), F32), jnp.zeros((tq, LANES), F32), jnp.zeros((1, tq), F32))
        for f in range(FOX_DIAG_SUB):
            carry = group(j * tq + f * td, td, (f + 1) * td, carry, row0=f * td)
        dk, dv, dc = lax.fori_loop(j + 1, nq, step, carry)
        dk_ref[...] = dk.astype(BF16)
        dv_ref[...] = dv.astype(BF16)
        dc_ref[...] = dc

        @pl.when(j == nq - 1)
        def _():
            dq_ref[...] = dq_ref[...] * ATT_SCALE

        if side is not None:
            @pl.when((pl.program_id(0) == nhb - 1) & (j == nq - 1))
            def _():
                side.middle(side_in, side_out, *sems)
                side.finish(side_in, side_out, *sems)

    def whole(off):
        return pl.BlockSpec((S, LANES), lambda h, j: (0, off + h))

    kv_blk = pl.BlockSpec((tq, LANES), lambda h, j: (j, h))
    c_blk = pl.BlockSpec((None, None, 1, tq), lambda h, j: (h, j, 0, 0))
    in_specs = [whole(0),
                pl.BlockSpec((tq, LANES), lambda h, j: (j, nhb + h)),
                pl.BlockSpec((tq, LANES), lambda h, j: (j, 2 * nhb + h)),
                whole(0), whole(0), c_blk]
    out_specs = [whole(0), kv_blk, kv_blk, whole(0), c_blk]
    out_shape = [jax.ShapeDtypeStruct((S, nhb * LANES), F32),
                 jax.ShapeDtypeStruct((S, nhb * LANES), BF16),
                 jax.ShapeDtypeStruct((S, nhb * LANES), BF16),
                 jax.ShapeDtypeStruct((S, nhb * LANES), F32),
                 jax.ShapeDtypeStruct((nhb, nq, 1, tq), F32)]
    args = (qkv, qkv, qkv, do, stats, ck)
    if side is None:
        return pl.pallas_call(
            body, name="fox_bwd", grid=(nhb, nq), in_specs=in_specs, out_specs=out_specs, out_shape=out_shape,
            compiler_params=_params(("parallel", "arbitrary")),
        )(*args)
    res = pl.pallas_call(
        body, name="fox_bwd_xchg", grid=(nhb, nq), in_specs=in_specs + [_ANY] * n_side,
        out_specs=out_specs + [_ANY] * n_side, out_shape=out_shape + side.outs,
        scratch_shapes=_side_scratch(side), compiler_params=_params(("arbitrary", "arbitrary")),
    )(*args, *side.ins)
    return (*res[:5], res[5:])


def _up_merge(ga, gb, w_a, w_b, h, b_gate, *, g_off, D, S):
    tr = _tile(S, 256, 8)
    W = ga.shape[1]
    assert g_off % D == 0

    def body(ga_ref, gb_ref, wa_ref, wb_ref, la_ref, lb_ref, b_ref, ua_ref, ub_ref, mix_ref):
        ua = _dot(ga_ref[...], wa_ref[...], _NT)
        ub = _dot(gb_ref[...], wb_ref[...], _NT)
        ua_ref[...] = ua
        ub_ref[...] = ub
        mix_ref[...] = (_sigmoid(la_ref[...] + b_ref[:, :D]) * ua
                        + _sigmoid(lb_ref[...] + b_ref[:, D:]) * ub).astype(BF16)

    act = pl.BlockSpec((tr, W), lambda i: (i, 0))
    rows = pl.BlockSpec((tr, D), lambda i: (i, 0))
    whole = pl.BlockSpec((D, W), lambda i: (0, 0), pipeline_mode=pl.Buffered(1))
    return pl.pallas_call(
        body, name="up_merge", grid=(S // tr,),
        in_specs=[act, act, whole, whole,
                  pl.BlockSpec((tr, D), lambda i: (i, g_off // D)), pl.BlockSpec((tr, D), lambda i: (i, g_off // D + 1)),
                  pl.BlockSpec((1, 2 * D), lambda i: (0, 0))],
        out_specs=[rows, rows, rows],
        out_shape=[jax.ShapeDtypeStruct((S, D), F32), jax.ShapeDtypeStruct((S, D), F32),
                   jax.ShapeDtypeStruct((S, D), BF16)],
        compiler_params=_params(("parallel",)),
    )(ga, gb, w_a, w_b, h, h, b_gate)


def _acc_rows(i, ref, val):
    @pl.when(i == 0)
    def _():
        ref[...] = val

    @pl.when(i > 0)
    def _():
        ref[...] += val


def _merge_bwd(dmix, up_a, up_b, h, b_gate, *, g_off, D, S):
    tr, tc = _tile(S, 512, 8), _tile(D, 512)

    def fn(i, ins, outs):
        dm, ua, ub, la, lb, ba, bb = (r[...] for r in ins)
        ga, gb = _sigmoid(la + ba), _sigmoid(lb + bb)
        outs[0][...] = (ga * dm).astype(BF16)
        outs[1][...] = (gb * dm).astype(BF16)
        dla = ua * dm * (ga * (1.0 - ga))
        dlb = ub * dm * (gb * (1.0 - gb))
        outs[2][...] = dla.astype(BF16)
        outs[3][...] = dlb.astype(BF16)
        _acc_rows(i, outs[4], jnp.sum(dla, axis=0, keepdims=True))
        _acc_rows(i, outs[5], jnp.sum(dlb, axis=0, keepdims=True))

    ins = [(dmix, "tile", 0), (up_a, "tile", 0), (up_b, "tile", 0), (h, "tile", g_off), (h, "tile", g_off + D),
           (b_gate, "row", 0), (b_gate, "row", D)]
    outs = [((S, D), BF16, "tile", 0)] * 4 + [((1, D), F32, "acc", 0)] * 2
    return _ew(fn, ins, outs, rows=S, cols=D, tr=tr, tc=tc, name="merge_bwd")


def _out_proj_deepnorm(mix, w, x, g, b, *, alpha, D, S):
    tr = _tile(S, 512, 8)

    def body(m_ref, w_ref, x_ref, g_ref, b_ref, xn_ref, xb_ref, z_ref):
        z = alpha * x_ref[...] + _dot(m_ref[...], w_ref[...], _NN)
        mu = jnp.mean(z, axis=1, keepdims=True)
        zc = z - mu
        var = jnp.mean(zc * zc, axis=1, keepdims=True)
        xn = zc * lax.rsqrt(var + LN_EPS) * g_ref[...] + b_ref[...]
        xn_ref[...] = xn
        xb_ref[...] = xn.astype(BF16)
        z_ref[...] = z

    rows = pl.BlockSpec((tr, D), lambda i: (i, 0))
    vec = pl.BlockSpec((1, D), lambda i: (0, 0))
    return pl.pallas_call(
        body, name="out_proj_deepnorm", grid=(S // tr,),
        in_specs=[rows, pl.BlockSpec((D, D), lambda i: (0, 0)), rows, vec, vec],
        out_specs=[rows, rows, rows],
        out_shape=[jax.ShapeDtypeStruct((S, D), F32), jax.ShapeDtypeStruct((S, D), BF16),
                   jax.ShapeDtypeStruct((S, D), F32)],
        compiler_params=_params(("parallel",)),
    )(mix, w, x, g, b)


def _deepnorm_bwd(dxn, z, g, w, *, D, S):
    tr = _tile(S, 256, 8)

    def body(dx_ref, z_ref, g_ref, w_ref, dz_ref, dzb_ref, dm_ref, sg_ref, sb_ref):
        i = pl.program_id(0)
        dx, zv = dx_ref[...], z_ref[...]
        mu = jnp.mean(zv, axis=1, keepdims=True)
        zc = zv - mu
        rstd = lax.rsqrt(jnp.mean(zc * zc, axis=1, keepdims=True) + LN_EPS)
        u = zc * rstd
        du = dx * g_ref[...]
        dz = rstd * (du - jnp.mean(du, axis=1, keepdims=True) - u * jnp.mean(du * u, axis=1, keepdims=True))
        dzb = dz.astype(BF16)
        dz_ref[...] = dz
        dzb_ref[...] = dzb
        dm_ref[...] = _dot(dzb, w_ref[...], _NT)
        _acc_rows(i, sg_ref, jnp.sum(dx * u, axis=0, keepdims=True))
        _acc_rows(i, sb_ref, jnp.sum(dx, axis=0, keepdims=True))

    rows = pl.BlockSpec((tr, D), lambda i: (i, 0))
    vec = pl.BlockSpec((1, D), lambda i: (0, 0))
    return pl.pallas_call(
        body, name="deepnorm_bwd", grid=(S // tr,),
        in_specs=[rows, rows, vec, pl.BlockSpec((D, D), lambda i: (0, 0), pipeline_mode=pl.Buffered(1))],
        out_specs=[rows, rows, rows, vec, vec],
        out_shape=[jax.ShapeDtypeStruct((S, D), F32), jax.ShapeDtypeStruct((S, D), BF16),
                   jax.ShapeDtypeStruct((S, D), F32), jax.ShapeDtypeStruct((1, D), F32),
                   jax.ShapeDtypeStruct((1, D), F32)],
        compiler_params=_params(("arbitrary",)),
    )(dxn, z, g, w)


def _loss_head(y, target, *, D, S):
    tr = _tile(S, 256, 8)

    def body(y_ref, t_ref, d_ref, l_ref):
        err = y_ref[...] - t_ref[...]
        d_ref[...] = err * (1.0 / D)
        part = 0.5 * jnp.sum(jnp.sum(err * err, axis=1, keepdims=True) * (1.0 / D), axis=0, keepdims=True)
        _acc_rows(pl.program_id(0), l_ref, jnp.broadcast_to(part, (1, LANES)))

    return pl.pallas_call(
        body, name="loss_head", grid=(S // tr,),
        in_specs=[pl.BlockSpec((tr, D), lambda i: (i, 0))] * 2,
        out_specs=[pl.BlockSpec((tr, D), lambda i: (i, 0)), pl.BlockSpec((1, LANES), lambda i: (0, 0))],
        out_shape=[jax.ShapeDtypeStruct((S, D), F32), jax.ShapeDtypeStruct((1, LANES), F32)],
        compiler_params=_params(("arbitrary",)),
    )(y, target)


_MESH = pl.DeviceIdType.MESH
_ANY = pl.BlockSpec(memory_space=pl.ANY)


class _Gather:
    def __init__(self, xs):
        self.ins = list(xs)
        self.outs = [jax.ShapeDtypeStruct((NDEV,) + x.shape, x.dtype) for x in xs]

    def _ctx(self, a, x_ref, out_ref, send_sems, recv_sems):
        x, y, c = lax.axis_index("x"), lax.axis_index("y"), lax.axis_index("c")
        me, sibling = (x, y, c), (x, y, 1 - c)
        chips = [(1 - x, y), (x, 1 - y), (1 - x, 1 - y)]

        def slot(px, py, pc):
            return out_ref.at[4 * px + 2 * py + pc]

        def copy(k, block, to, src=None):
            return pltpu.make_async_remote_copy(
                src_ref=slot(*block) if src is None else src, dst_ref=slot(*block),
                send_sem=send_sems.at[7 * a + k], recv_sem=recv_sems.at[7 * a + k], device_id=to, device_id_type=_MESH)

        def first():
            return [copy(0, me, sibling, src=x_ref)] + [copy(1 + j, me, (*chip, c), src=x_ref)
                                                        for j, chip in enumerate(chips)]

        def passed():
            return [copy(4 + j, (*chip, c), sibling) for j, chip in enumerate(chips)]

        return c, me, sibling, chips, slot, copy, first, passed

    def start(self, in_refs, out_refs, send_sems, recv_sems, local_sems):
        for a, (x_ref, out_ref) in enumerate(zip(in_refs, out_refs)):
            _, me, _, _, slot, _, first, _ = self._ctx(a, x_ref, out_ref, send_sems, recv_sems)
            pltpu.make_async_copy(x_ref, slot(*me), local_sems.at[a]).start()
            for cp in first():
                cp.start()

    def middle(self, in_refs, out_refs, send_sems, recv_sems, local_sems):
        for a, (x_ref, out_ref) in enumerate(zip(in_refs, out_refs)):
            c, me, _, chips, _, copy, _, passed = self._ctx(a, x_ref, out_ref, send_sems, recv_sems)
            for j, (chip, cp) in enumerate(zip(chips, passed())):
                copy(1 + j, (*chip, c), me).wait_recv()
                cp.start()

    def finish(self, in_refs, out_refs, send_sems, recv_sems, local_sems):
        for a, (x_ref, out_ref) in enumerate(zip(in_refs, out_refs)):
            c, me, sibling, chips, slot, copy, first, passed = self._ctx(a, x_ref, out_ref, send_sems, recv_sems)
            copy(0, sibling, me).wait_recv()
            for j, chip in enumerate(chips):
                copy(4 + j, (*chip, 1 - c), me).wait_recv()
            for cp in first() + passed():
                cp.wait_send()
            pltpu.make_async_copy(x_ref, slot(*me), local_sems.at[a]).wait()


class _Exchange:
    def __init__(self, gs):
        self.ins = list(gs)
        self.outs = [jax.ShapeDtypeStruct(g.shape, g.dtype) for g in gs]

    def _copies(self, a, g_ref, out_ref, send_sems, recv_sems, arrivals=True):
        x, y, c = lax.axis_index("x"), lax.axis_index("y"), lax.axis_index("c")
        me = 4 * x + 2 * y + c
        out, back = [], []
        for k in range(1, NDEV):
            px, py, pc = x ^ ((k >> 2) & 1), y ^ ((k >> 1) & 1), c ^ (k & 1)
            peer = 4 * px + 2 * py + pc
            sems = dict(send_sem=send_sems.at[7 * a + k - 1], recv_sem=recv_sems.at[7 * a + k - 1],
                        device_id=(px, py, pc), device_id_type=_MESH)
            out.append(pltpu.make_async_remote_copy(src_ref=g_ref.at[peer], dst_ref=out_ref.at[me], **sems))
            if arrivals:
                back.append(pltpu.make_async_remote_copy(src_ref=g_ref.at[me], dst_ref=out_ref.at[peer], **sems))
        return me, out, back

    def start(self, in_refs, out_refs, send_sems, recv_sems, local_sems):
        for a, (g_ref, out_ref) in enumerate(zip(in_refs, out_refs)):
            me, out, _ = self._copies(a, g_ref, out_ref, send_sems, recv_sems, arrivals=False)
            pltpu.make_async_copy(g_ref.at[me], out_ref.at[me], local_sems.at[a]).start()
            for cp in out:
                cp.start()

    def middle(self, in_refs, out_refs, send_sems, recv_sems, local_sems):
        pass

    def finish(self, in_refs, out_refs, send_sems, recv_sems, local_sems):
        for a, (g_ref, out_ref) in enumerate(zip(in_refs, out_refs)):
            me, out, back = self._copies(a, g_ref, out_ref, send_sems, recv_sems)
            for cp in back:
                cp.wait_recv()
            for cp in out:
                cp.wait_send()
            pltpu.make_async_copy(g_ref.at[me], out_ref.at[me], local_sems.at[a]).wait()


def _side_scratch(side):
    n = len(side.ins)
    return [pltpu.SemaphoreType.DMA((7 * n,)), pltpu.SemaphoreType.DMA((7 * n,)), pltpu.SemaphoreType.DMA((n,))]


def _run_side(side, *, name):
    n = len(side.ins)

    def body(*refs):
        ins, outs, sems = refs[:n], refs[n:2 * n], refs[2 * n:]
        side.start(ins, outs, *sems)
        side.middle(ins, outs, *sems)
        side.finish(ins, outs, *sems)

    return pl.pallas_call(
        body, name=name, out_shape=side.outs, in_specs=[_ANY] * n, out_specs=[_ANY] * n,
        scratch_shapes=_side_scratch(side),
    )(*side.ins)


def _all_gather(x, *, name):
    return _run_side(_Gather([x]), name=name)[0]


def _sum_slots(parts, *, name):
    n, R, C = parts.shape
    tr, tc = _tile(R, 64, 16), _tile(C, 2048)

    def body(p_ref, o_ref):
        acc = p_ref[0].astype(F32)
        for k in range(1, n):
            acc = acc + p_ref[k].astype(F32)
        o_ref[...] = acc

    return pl.pallas_call(
        body, name=name, grid=(R // tr, C // tc),
        in_specs=[pl.BlockSpec((n, tr, tc), lambda i, j: (0, i, j))],
        out_specs=pl.BlockSpec((tr, tc), lambda i, j: (i, j)),
        out_shape=jax.ShapeDtypeStruct((R, C), F32),
        compiler_params=_params(("parallel", "parallel")),
    )(parts)


def _adamw(w, g, m, v, *, name):
    R, C = w.shape
    budget = 2 << 20
    tr = R if R * C * 4 <= budget else _tile(R, max(8, (budget // (C * 4)) // 8 * 8), 8)

    def body(w_ref, g_ref, m_ref, v_ref, d_ref, nm_ref, nv_ref):
        gv = g_ref[...]
        nm = ADAM_B1 * m_ref[...] + (1.0 - ADAM_B1) * gv
        nv = ADAM_B2 * v_ref[...] + (1.0 - ADAM_B2) * (gv * gv)
        m_hat = nm / (1.0 - ADAM_B1 ** ADAM_STEP)
        v_hat = nv / (1.0 - ADAM_B2 ** ADAM_STEP)
        d_ref[...] = -ADAM_LR * (m_hat / (jnp.sqrt(v_hat) + ADAM_EPS) + ADAM_WD * w_ref[...])
        nm_ref[...] = nm
        nv_ref[...] = nv

    spec = pl.BlockSpec((tr, C), lambda i: (i, 0))
    return pl.pallas_call(
        body, name=name, grid=(R // tr,), in_specs=[spec] * 4, out_specs=[spec] * 3,
        out_shape=[jax.ShapeDtypeStruct((R, C), F32)] * 3,
        compiler_params=_params(("parallel",)),
    )(w, g, m, v)


def _pad_cols(w, *, nq, ng):
    nf = w.shape[-1] - nq - ng
    pad = jnp.zeros(w.shape[:-1] + (FORGET_PAD - nf,), w.dtype)
    return jnp.concatenate([w[..., :nq], w[..., nq + nf:], w[..., nq:nq + nf], pad], axis=-1)


def _unpad_cols(w, *, nq, ng, nf):
    return jnp.concatenate([w[..., :nq], w[..., nq + ng:nq + ng + nf], w[..., nq:nq + ng]], axis=-1)


def kernel(x, w_in, b_forget, b_gate, w_up_a, w_up_b, w_out, ln_g, ln_b, loss_target, m_w_in, m_b_forget, m_b_gate, m_w_up_a, m_w_up_b, m_w_out, m_ln_g, m_ln_b, v_w_in, v_b_forget, v_b_gate, v_w_up_a, v_w_up_b, v_w_out, v_ln_g, v_ln_b):
    depth = w_in.shape[0]
    _, S, D = x.shape
    WA, WB = w_up_a.shape[1], w_up_b.shape[1]
    nha, nhb = WA // HEAD_DIM, WB // HEAD_DIM
    assert nhb == b_forget.shape[1] and WA == WB and nhb <= HEAD_ROWS
    nq, ng, nf = 4 * WA + 4 * WB, 2 * D, nhb
    assert w_in.shape[2] == nq + nf + ng
    ncp = nq + ng + FORGET_PAD
    off_va, off_za, off_b, off_zb, off_g, off_f = 2 * WA, 3 * WA, 4 * WA, 4 * WA + 3 * WB, nq, nq + ng
    alpha = float((2 * depth) ** 0.25)
    dsh = D // NDEV
    tq_f, tq_b = _tile(S, FOX_TILE_FWD), _tile(S, FOX_TILE_BWD)

    x0 = x[0]
    target = loss_target[0]
    cos, sin = _rope_tables(S)

    w_in_p = _pad_cols(w_in, nq=nq, ng=ng).astype(BF16)
    w_up_a_t = jnp.swapaxes(w_up_a, 1, 2).astype(BF16)
    w_up_b_t = jnp.swapaxes(w_up_b, 1, 2).astype(BF16)
    w_out_c = w_out.astype(BF16)
    W_in, W_ua, W_ub, W_out = ([None] * depth for _ in range(4))

    def others(l):
        return [w_up_a_t[l], w_up_b_t[l], w_out_c[l]]

    def keep_others(l, got):
        W_ua[l], W_ub[l], W_out[l] = got[0].reshape(D, WA), got[1].reshape(D, WB), got[2].reshape(D, D)

    W_in[0] = _run_side(_Gather([w_in_p[0]]), name="gather_w_in_0")[0].reshape(D, ncp)

    bf_rows = [jnp.broadcast_to(jnp.pad(b_forget[l], (0, HEAD_ROWS - nhb))[:, None], (HEAD_ROWS, LANES))
               for l in range(depth)]

    saved = []
    xf, xb = x0, x0.astype(BF16)
    for l in range(depth):
        wanted = (others(0) if l == 0 else []) + ([w_in_p[l + 1]] + others(l + 1) if l + 1 < depth else [])
        if wanted:
            h, got = _mm(xb, W_in[l], name="in_proj_gather", tm=1024, tn=1280, tk=2048, side=_Gather(wanted))
            if l == 0:
                keep_others(0, got[:3])
            if l + 1 < depth:
                W_in[l + 1] = got[-4].reshape(D, ncp)
                keep_others(l + 1, got[-3:])
        else:
            h = _mm(xb, W_in[l], name="in_proj", tm=1024, tn=1280, tk=2048)
        qk_a = _prep_a(h, cos, sin, W=WA, S=S)
        qkv_b = _prep_b(h, off=off_b, W=WB, S=S)
        out_a, lse_a, ga = _band_fwd(qk_a, h, nha=nha, S=S, v_off=off_va, z_off=off_za)
        f3 = _lane_blocks(jnp.pad(h[:, off_f:off_f + nhb].T, ((0, HEAD_ROWS - nhb), (0, 0))))
        c = _lane_unblocks(_forget_cumsum(f3, bf_rows[l]))[:nhb]
        out_b, lq_b, gb = _fox_fwd(qkv_b, c.reshape(nhb, S // tq_f, 1, tq_f), h, z_off=off_zb, nhb=nhb, S=S, tq=tq_f)
        ck = c.reshape(nhb, S // tq_b, 1, tq_b)
        bg = b_gate[l][None, :]
        up_a, up_b, mix = _up_merge(ga, gb, W_ua[l], W_ub[l], h, bg, g_off=off_g, D=D, S=S)
        xn_f, xn_b, z = _out_proj_deepnorm(mix, W_out[l], xf, ln_g[l][None, :], ln_b[l][None, :],
                                           alpha=alpha, D=D, S=S)
        saved.append(dict(xb=xb, h=h, qk_a=qk_a, qkv_b=qkv_b, out_a=out_a, lse_a=lse_a, ga=ga, f3=f3, ck=ck,
                          out_b=out_b, lq_b=lq_b, gb=gb, up_a=up_a, up_b=up_b, mix=mix, z=z, bg=bg))
        xf, xb = xn_f, xn_b

    dx, loss_row = _loss_head(xf, target, D=D, S=S)

    recv = [None] * depth
    small = [None] * depth
    pending = None
    for l in reversed(range(depth)):
        sv = saved[l]
        h = sv["h"]
        dz, dzb, dmix, d_lng, d_lnb = _deepnorm_bwd(dx, sv["z"], ln_g[l][None, :], W_out[l], D=D, S=S)
        g_w_out = _mm(sv["mix"], dzb, ta=True, name="g_w_out", tm=512, tn=1024, tk=4096, out_dtype=BF16)
        dup_a, dup_b, dgl_a, dgl_b, d_bga, d_bgb = _merge_bwd(dmix, sv["up_a"], sv["up_b"], h, sv["bg"],
                                                              g_off=off_g, D=D, S=S)
        g_w_ua = _mm(dup_a, sv["ga"], ta=True, name="g_w_up_a", tm=512, tn=1536, tk=4096, out_dtype=BF16)
        g_w_ub = _mm(dup_b, sv["gb"], ta=True, name="g_w_up_b", tm=512, tn=1536, tk=4096, out_dtype=BF16)
        do_a, dza, stats_a = _gate_bwd(dup_a, W_ua[l], sv["out_a"], h, sv["lse_a"], z_off=off_za, W=WA, S=S,
                                       name="gate_bwd_a", do_dtype=F32)
        do_b, dzb_, stats_b = _gate_bwd(dup_b, W_ub[l], sv["out_b"], h, sv["lq_b"], z_off=off_zb, W=WB, S=S,
                                        name="gate_bwd_b", do_dtype=BF16)
        dqa, dka, dva = _rope_bwd(*_band_bwd(sv["qk_a"], h, do_a, stats_a, nha=nha, S=S, v_off=off_va),
                                  cos, sin, W=WA, S=S)
        if pending is None:
            dqb, dkb, dvb, dcq, dck = _fox_bwd(sv["qkv_b"], do_b, stats_b, sv["ck"], nhb=nhb, S=S, tq=tq_b)
        else:
            dqb, dkb, dvb, dcq, dck, got = _fox_bwd(sv["qkv_b"], do_b, stats_b, sv["ck"], nhb=nhb, S=S, tq=tq_b,
                                                    side=_Exchange([pending]))
            recv[l + 1] = (got[0], *recv[l + 1])
        head_rows = lambda t: _lane_blocks(jnp.pad(t, ((0, HEAD_ROWS - nhb), (0, 0))))
        df3, d_bf = _forget_bwd(head_rows(dcq[:, ::LANES].T), head_rows(dck.reshape(nhb, S)), sv["f3"], bf_rows[l])
        dfl = jnp.pad(_lane_unblocks(df3)[:nhb].T, ((0, 0), (0, FORGET_PAD - nhb))).astype(BF16)
        dh = jnp.concatenate([dqa, dka, dva, dza, dqb.astype(BF16), dkb, dvb, dzb_, dgl_a, dgl_b, dfl], axis=1)
        chunks = lambda t: t.reshape(NDEV, dsh, t.shape[1])
        g_w_in, got3 = _mm(sv["xb"], dh, ta=True, name="g_w_in_xchg", tm=512, tn=1280, tk=4096, out_dtype=BF16,
                           side=_Exchange([chunks(g_w_ua), chunks(g_w_ub), chunks(g_w_out)]))
        if l > 0:
            dx = _mm(dh, W_in[l], tb=True, name="d_x", tm=1024, tn=512, tk=3328, add=dz, add_scale=alpha)
            pending, recv[l] = chunks(g_w_in), tuple(got3)
        else:
            dx, got1 = _mm(dh, W_in[l], tb=True, name="d_x_xchg", tm=1024, tn=512, tk=3328, add=dz, add_scale=alpha,
                           side=_Exchange([chunks(g_w_in)]))
            recv[l] = (got1[0], *got3)
        small[l] = jnp.concatenate([d_bf[:nhb, 0], jnp.zeros((LANES - nhb,), F32), d_bga[0], d_bgb[0],
                                    d_lng[0], d_lnb[0]])

    n_small = LANES + 2 * D + 2 * D
    flat = jnp.concatenate(small + [loss_row[0]])
    rows = -(-flat.shape[0] // (8 * LANES)) * 8
    flat = jnp.pad(flat, (0, rows * LANES - flat.shape[0])).reshape(rows, LANES)
    tot = _sum_slots(_all_gather(flat, name="gather_small"), name="sum_small").reshape(-1)
    loss = tot[depth * n_small]
    sm = tot[:depth * n_small].reshape(depth, n_small)
    g_bf, g_bg = sm[:, :nhb], sm[:, LANES:LANES + 2 * D]
    g_lg, g_lb = sm[:, LANES + 2 * D:LANES + 3 * D], sm[:, LANES + 3 * D:]

    g_in = jnp.stack([_unpad_cols(_sum_slots(recv[l][0], name="sum_w_in"), nq=nq, ng=ng, nf=nf) for l in range(depth)])
    g_ua = jnp.stack([_sum_slots(recv[l][1], name="sum_w_up_a").T for l in range(depth)])
    g_ub = jnp.stack([_sum_slots(recv[l][2], name="sum_w_up_b").T for l in range(depth)])
    g_out = jnp.stack([_sum_slots(recv[l][3], name="sum_w_out") for l in range(depth)])

    def adam(w, g, m, v, name):
        shp = w.shape
        d_, m_, v_ = _adamw(*[t.reshape(-1, shp[-1]) for t in (w, g, m, v)], name=name)
        return d_.reshape(shp), m_.reshape(shp), v_.reshape(shp)

    grads = [g_in, g_bf, g_bg, g_ua, g_ub, g_out, g_lg, g_lb]
    ws = [w_in, b_forget, b_gate, w_up_a, w_up_b, w_out, ln_g, ln_b]
    ms = [m_w_in, m_b_forget, m_b_gate, m_w_up_a, m_w_up_b, m_w_out, m_ln_g, m_ln_b]
    vs = [v_w_in, v_b_forget, v_b_gate, v_w_up_a, v_w_up_b, v_w_out, v_ln_g, v_ln_b]
    names = ["w_in", "b_forget", "b_gate", "w_up_a", "w_up_b", "w_out", "ln_g", "ln_b"]
    upd = [adam(w, g, m, v, f"adamw_{n}") for w, g, m, v, n in zip(ws, grads, ms, vs, names)]
    return (loss, dx[None], *grads, *[u[0] for u in upd], *[u[1] for u in upd], *[u[2] for u in upd])
```

```python
import jax
import jax.numpy as jnp
from jax import lax
from jax.experimental import pallas as pl
from jax.experimental.pallas import tpu as pltpu

F32 = jnp.float32
BF16 = jnp.bfloat16

NDEV = 8
HEAD_DIM = 128
LANES = 128
Q_BLOCK = 128
DILATIONS = (1, 4, 16)
ROPE_THETA = 10000.0
LN_EPS = 1e-5
ATT_SCALE = HEAD_DIM ** -0.5
NEG = -1e30
FORGET_PAD = 256
BAND_CHUNK = 2048
BAND_UNROLL = 16
FOX_TILE_FWD = 2048
FOX_TILE_BWD = 1024
FOX_ROWS = 512
GATE_BWD_ROWS = 512
FOX_DIAG_SUB = 4
HEAD_ROWS = 16
ADAM_LR, ADAM_B1, ADAM_B2, ADAM_EPS, ADAM_WD, ADAM_STEP = 0.001, 0.9, 0.999, 1e-08, 0.01, 10
VMEM_LIMIT = 56 * 1024 * 1024


def _tile(n, pref, mult=LANES):
    if n <= pref:
        return n
    t = (pref // mult) * mult
    while t >= mult:
        if n % t == 0:
            return t
        t -= mult
    return n


def _params(sem=None):
    return pltpu.CompilerParams(dimension_semantics=sem, vmem_limit_bytes=VMEM_LIMIT)


def _sigmoid(z):
    return 1.0 / (1.0 + jnp.exp(-z))


_NT = (((1,), (1,)), ((), ()))
_NN = (((1,), (0,)), ((), ()))
_TN = (((0,), (0,)), ((), ()))


def _dot(a, b, dims):
    return lax.dot_general(a, b, dims, preferred_element_type=F32)


def _mm(a, b, *, name, ta=False, tb=False, out_dtype=F32, tm=512, tn=512, tk=512, add=None, add_scale=1.0,
        side=None):
    if ta:
        K, M = a.shape
    else:
        M, K = a.shape
    if tb:
        N, K2 = b.shape
    else:
        K2, N = b.shape
    assert K == K2, (a.shape, b.shape, ta, tb)
    tm, tn, tk = _tile(M, tm), _tile(N, tn), _tile(K, tk)
    ni, nj, nk = M // tm, N // tn, K // tk
    dims = (((0 if ta else 1,), (1 if tb else 0,)), ((), ()))
    n_side = 0 if side is None else len(side.ins)
    n_in = 2 + (add is not None)

    def body(*refs):
        a_ref, b_ref = refs[:2]
        add_ref = refs[2] if add is not None else None
        side_in = refs[n_in:n_in + n_side]
        o_ref = refs[n_in + n_side]
        side_out = refs[n_in + n_side + 1:n_in + 2 * n_side + 1]
        acc_ref = refs[n_in + 2 * n_side + 1]
        sems = refs[n_in + 2 * n_side + 2:]
        k = pl.program_id(2)
        if side is not None:
            @pl.when((pl.program_id(0) == 0) & (pl.program_id(1) == 0) & (k == 0))
            def _():
                side.start(side_in, side_out, *sems)

        part = _dot(a_ref[...].astype(BF16), b_ref[...].astype(BF16), dims)

        @pl.when(k == 0)
        def _():
            acc_ref[...] = part

        @pl.when(k > 0)
        def _():
            acc_ref[...] += part

        @pl.when(k == nk - 1)
        def _():
            r = acc_ref[...]
            if add_ref is not None:
                r = r + add_scale * add_ref[...]
            o_ref[...] = r.astype(out_dtype)

        if side is not None:
            step = (pl.program_id(0) * nj + pl.program_id(1)) * nk + k
            steps = ni * nj * nk

            @pl.when(step == max((7 * steps) // 8 - 1, 0))
            def _():
                side.middle(side_in, side_out, *sems)

            @pl.when(step == steps - 1)
            def _():
                side.finish(side_in, side_out, *sems)

    a_spec = pl.BlockSpec((tk, tm), lambda i, j, k: (k, i)) if ta else pl.BlockSpec((tm, tk), lambda i, j, k: (i, k))
    b_spec = pl.BlockSpec((tn, tk), lambda i, j, k: (j, k)) if tb else pl.BlockSpec((tk, tn), lambda i, j, k: (k, j))
    o_spec = pl.BlockSpec((tm, tn), lambda i, j, k: (i, j))
    in_specs, args = [a_spec, b_spec], [a, b]
    if add is not None:
        in_specs.append(o_spec)
        args.append(add)
    out_shape = jax.ShapeDtypeStruct((M, N), out_dtype)
    scratch = [pltpu.VMEM((tm, tn), F32)]
    if side is None:
        return pl.pallas_call(
            body, name=name, grid=(ni, nj, nk), in_specs=in_specs, out_specs=o_spec, out_shape=out_shape,
            scratch_shapes=scratch, compiler_params=_params(("parallel", "parallel", "arbitrary")),
        )(*args)
    res = pl.pallas_call(
        body, name=name, grid=(ni, nj, nk), in_specs=in_specs + [_ANY] * n_side,
        out_specs=[o_spec] + [_ANY] * n_side, out_shape=[out_shape] + side.outs,
        scratch_shapes=scratch + _side_scratch(side),
        compiler_params=_params(("arbitrary", "arbitrary", "arbitrary")),
    )(*args, *side.ins)
    return res[0], res[1:]


def _ew(fn, ins, outs, *, rows, cols, tr, tc, name):
    assert rows % tr == 0 and cols % tc == 0

    def spec(kind, off):
        assert off % tc == 0, (off, tc)
        ob = off // tc
        if kind == "tile":
            return pl.BlockSpec((tr, tc), lambda j, i: (i, j + ob))
        if kind == "tab":
            return pl.BlockSpec((tr, LANES), lambda j, i: (i, 0))
        return pl.BlockSpec((1, tc), lambda j, i: (0, j + ob))

    n_in = len(ins)

    def body(*refs):
        fn(pl.program_id(1), refs[:n_in], refs[n_in:])

    return pl.pallas_call(
        body, name=name, grid=(cols // tc, rows // tr),
        in_specs=[spec(k, o) for _, k, o in ins],
        out_specs=[spec(k, o) for _, _, k, o in outs],
        out_shape=[jax.ShapeDtypeStruct(s, d) for s, d, _, _ in outs],
        compiler_params=_params(("parallel", "arbitrary")),
    )(*[a for a, _, _ in ins])


def _heads(t):
    return [t[:, e:e + LANES] for e in range(0, t.shape[1], LANES)]


def _rope_tables(S):
    half = HEAD_DIM // 2
    inv_freq = ROPE_THETA ** (-jnp.arange(half, dtype=F32) / half)
    ang = jnp.arange(S, dtype=jnp.int32).astype(F32)[:, None] * inv_freq[None, :]
    cos, sin = jnp.cos(ang), jnp.sin(ang)
    return jnp.concatenate([cos, cos], axis=1), jnp.concatenate([-sin, sin], axis=1)


def _rope(t, cs, sn):
    return jnp.concatenate([g * cs + pltpu.roll(g, HEAD_DIM // 2, 1) * sn for g in _heads(t)], axis=1)


def _rope_t(t, cs, sn):
    return jnp.concatenate([g * cs + pltpu.roll(g * sn, HEAD_DIM // 2, 1) for g in _heads(t)], axis=1)


def _prep_a(h, cos, sin, *, W, S):
    tr, tc = _tile(S, 1024, 8), _tile(W, 512)

    def fn(i, ins, outs):
        outs[0][...] = _rope(ins[0][...], ins[1][...], ins[2][...])

    return _ew(fn, [(h, "tile", 0), (cos, "tab", 0), (sin, "tab", 0)], [((S, 2 * W), F32, "tile", 0)],
               rows=S, cols=2 * W, tr=tr, tc=tc, name="prep_a")[0]


def _prep_b(h, *, off, W, S):
    tr, tc = _tile(S, 1024, 8), _tile(W, 512)

    def fn(i, ins, outs):
        scale = jnp.where(pl.program_id(0) * tc < W, ATT_SCALE, 1.0).astype(F32)
        outs[0][...] = (ins[0][...] * scale).astype(BF16)

    return _ew(fn, [(h, "tile", off)], [((S, 3 * W), BF16, "tile", 0)], rows=S, cols=3 * W, tr=tr, tc=tc,
               name="prep_b")[0]


def _rope_bwd(dq, dk, dv, cos, sin, *, W, S):
    tr, tc = _tile(S, 1024, 8), _tile(W, 512)

    def fn(i, ins, outs):
        cs, sn = ins[3][...], ins[4][...]
        outs[0][...] = _rope_t(ins[0][...], cs, sn).astype(BF16)
        outs[1][...] = _rope_t(ins[1][...], cs, sn).astype(BF16)
        outs[2][...] = ins[2][...].astype(BF16)

    ins = [(dq, "tile", 0), (dk, "tile", 0), (dv, "tile", 0), (cos, "tab", 0), (sin, "tab", 0)]
    return _ew(fn, ins, [((S, W), BF16, "tile", 0)] * 3, rows=S, cols=W, tr=tr, tc=tc, name="rope_bwd")


def _band_biases():
    dist = (lax.broadcasted_iota(jnp.int32, (Q_BLOCK, 2 * Q_BLOCK), 0)
            - lax.broadcasted_iota(jnp.int32, (Q_BLOCK, 2 * Q_BLOCK), 1))
    first = jnp.where(dist >= 0, 0.0, NEG).astype(F32)
    other = jnp.where((dist + Q_BLOCK >= 0) & (dist <= 0), 0.0, NEG).astype(F32)
    return first, other


def _band_rows(it, c, d, nbc, biases):
    r, n = (0, it) if d == 1 else (it // nbc, it % nbc)
    ng = c * nbc + n
    k0 = pl.multiple_of(jnp.maximum(ng - 1, 0) * Q_BLOCK, Q_BLOCK)
    bias = jnp.where(ng == 0, biases[0], biases[1])
    if d == 1:
        return pl.ds(pl.multiple_of(it * Q_BLOCK, Q_BLOCK), Q_BLOCK), pl.ds(k0, 2 * Q_BLOCK), bias
    return (pl.ds(r + d * Q_BLOCK * n, Q_BLOCK, stride=d), pl.ds(r + d * k0, 2 * Q_BLOCK, stride=d), bias)


def _band_chunk(S):
    ch = min(S, BAND_CHUNK)
    assert S % ch == 0 and ch % (Q_BLOCK * max(DILATIONS)) == 0 and S >= 2 * Q_BLOCK * max(DILATIONS)
    return ch


def _band_fwd(qk, h, *, nha, S, v_off, z_off):
    CH = _band_chunk(S)
    W = nha * LANES
    slab = 256

    def body(q_ref, k_ref, v_ref, z_ref, o_ref, lse_ref, g_ref, *scratch):
        os_refs, ls_refs = scratch[:len(DILATIONS)], scratch[len(DILATIONS):]
        c = pl.program_id(1)
        biases = _band_biases()
        for gi, d in enumerate(DILATIONS):
            nbc = CH // (Q_BLOCK * d)

            def step(it, carry, gi=gi, d=d, nbc=nbc):
                rq, rk, bias = _band_rows(it, c, d, nbc, biases)
                q = q_ref[rq, :].astype(BF16)
                k = k_ref[rk, :].astype(BF16)
                v = v_ref[rk, :].astype(BF16)
                s = _dot(q, k, _NT) * ATT_SCALE + bias
                m = jnp.max(s, axis=1, keepdims=True)
                p = jnp.exp(s - m)
                den = jnp.sum(p, axis=1, keepdims=True)
                os_refs[gi][rq, :] = _dot(p.astype(BF16), v, _NN) / den
                ls_refs[gi][rq, :] = jnp.broadcast_to(m + jnp.log(den), (Q_BLOCK, LANES))
                return carry

            lax.fori_loop(0, CH // Q_BLOCK, step, 0, unroll=BAND_UNROLL)

        def mix(t, carry):
            rows = pl.ds(pl.multiple_of(t * slab, slab), slab)
            a, b, cc = (r[rows, :] for r in ls_refs)
            m = jnp.maximum(jnp.maximum(a, b), cc)
            ea, eb, ec = jnp.exp(a - m), jnp.exp(b - m), jnp.exp(cc - m)
            den = ea + eb + ec
            out = (ea * os_refs[0][rows, :] + eb * os_refs[1][rows, :] + ec * os_refs[2][rows, :]) / den
            z = z_ref[rows, :]
            o_ref[rows, :] = out
            lse_ref[rows, :] = m + jnp.log(den)
            g_ref[rows, :] = (out * (z * _sigmoid(z))).astype(BF16)
            return carry

        lax.fori_loop(0, CH // slab, mix, 0)

    chunk = pl.BlockSpec((CH, LANES), lambda hd, c: (c, hd))
    return pl.pallas_call(
        body, name="band_fwd", grid=(nha, S // CH),
        in_specs=[chunk,
                  pl.BlockSpec((S, LANES), lambda hd, c: (0, nha + hd)),
                  pl.BlockSpec((S, LANES), lambda hd, c: (0, v_off // LANES + hd)),
                  pl.BlockSpec((CH, LANES), lambda hd, c: (c, z_off // LANES + hd))],
        out_specs=[chunk, chunk, chunk],
        out_shape=[jax.ShapeDtypeStruct((S, W), F32), jax.ShapeDtypeStruct((S, W), F32),
                   jax.ShapeDtypeStruct((S, W), BF16)],
        scratch_shapes=[pltpu.VMEM((CH, LANES), F32)] * (2 * len(DILATIONS)),
        compiler_params=_params(("parallel", "arbitrary")),
    )(qk, qk, h, h)


def _band_bwd(qk, h, do, stats, *, nha, S, v_off):
    CH = _band_chunk(S)
    W = nha * LANES

    def body(q_ref, k_ref, v_ref, do_ref, st_ref, dq_ref, dk_ref, dv_ref):
        c = pl.program_id(1)

        @pl.when(c == 0)
        def _():
            dk_ref[...] = jnp.zeros_like(dk_ref)
            dv_ref[...] = jnp.zeros_like(dv_ref)

        biases = _band_biases()
        for gi, d in enumerate(DILATIONS):
            nbc = CH // (Q_BLOCK * d)

            def step(it, carry, gi=gi, d=d, nbc=nbc):
                rq, rk, bias = _band_rows(it, c, d, nbc, biases)
                q = q_ref[rq, :].astype(BF16)
                k = k_ref[rk, :].astype(BF16)
                v = v_ref[rk, :].astype(BF16)
                g = do_ref[rq, :].astype(BF16)
                st = st_ref[rq, :]
                p = jnp.exp(_dot(q, k, _NT) * ATT_SCALE + (bias - st[:, 0:1]))
                ds = (p * (_dot(g, v, _NT) - st[:, 1:2]) * ATT_SCALE).astype(BF16)
                dq = _dot(ds, k, _NN)
                if gi == 0:
                    dq_ref[rq, :] = dq
                else:
                    dq_ref[rq, :] += dq
                dk_ref[rk, :] += _dot(ds, q, _TN)
                dv_ref[rk, :] += _dot(p.astype(BF16), g, _TN)
                return carry

            lax.fori_loop(0, CH // Q_BLOCK, step, 0, unroll=BAND_UNROLL)

    chunk = pl.BlockSpec((CH, LANES), lambda hd, c: (c, hd))
    whole = pl.BlockSpec((S, LANES), lambda hd, c: (0, hd))
    return pl.pallas_call(
        body, name="band_bwd", grid=(nha, S // CH),
        in_specs=[chunk,
                  pl.BlockSpec((S, LANES), lambda hd, c: (0, nha + hd)),
                  pl.BlockSpec((S, LANES), lambda hd, c: (0, v_off // LANES + hd)),
                  chunk, chunk],
        out_specs=[chunk, whole, whole],
        out_shape=[jax.ShapeDtypeStruct((S, W), F32)] * 3,
        compiler_params=_params(("parallel", "arbitrary")),
    )(qk, qk, h, do, stats)


def _gate_bwd(dup, w, o, h, lane0, *, z_off, W, S, name, do_dtype):
    tr = _tile(S, GATE_BWD_ROWS, 8)
    D = dup.shape[1]
    assert z_off % W == 0

    def body(d_ref, w_ref, o_ref, z_ref, l0_ref, do_ref, dz_ref, st_ref):
        g = _dot(d_ref[...], w_ref[...], _NN)
        out, z = o_ref[...], z_ref[...]
        sg = _sigmoid(z)
        do = g * (z * sg)
        do_ref[...] = do.astype(do_dtype)
        dz_ref[...] = (g * out * (sg * (1.0 + z * (1.0 - sg)))).astype(BF16)
        delta = jnp.concatenate([jnp.broadcast_to(jnp.sum(t, axis=1, keepdims=True), t.shape)
                                 for t in _heads(do * out)], axis=1)
        first = lax.broadcasted_iota(jnp.int32, delta.shape, 1) % LANES == 0
        st_ref[...] = jnp.where(first, l0_ref[...], delta)

    rows = pl.BlockSpec((tr, W), lambda i: (i, 0))
    return pl.pallas_call(
        body, name=name, grid=(S // tr,),
        in_specs=[pl.BlockSpec((tr, D), lambda i: (i, 0)),
                  pl.BlockSpec((D, W), lambda i: (0, 0), pipeline_mode=pl.Buffered(1)),
                  rows, pl.BlockSpec((tr, W), lambda i: (i, z_off // W)), rows],
        out_specs=[rows, rows, rows],
        out_shape=[jax.ShapeDtypeStruct((S, W), do_dtype), jax.ShapeDtypeStruct((S, W), BF16),
                   jax.ShapeDtypeStruct((S, W), F32)],
        compiler_params=_params(("parallel",)),
    )(dup, w, o, h, lane0)


def _split3(x):
    hi = x.astype(BF16)
    r = x - hi.astype(F32)
    mid = r.astype(BF16)
    lo = (r - mid.astype(F32)).astype(BF16)
    return hi, mid, lo


def _tri_dot(x, tri):
    hi, mid, lo = _split3(x)
    return _dot(hi, tri, _NN) + (_dot(mid, tri, _NN) + _dot(lo, tri, _NN))


def _log1p(u):
    w = 1.0 + u
    return jnp.where(w == 1.0, u, jnp.log(w) * (u / jnp.where(w == 1.0, 1.0, w - 1.0)))


def _lane_blocks(t):
    R, S = t.shape
    return t.reshape(R, S // LANES, LANES).transpose(1, 0, 2)


def _lane_unblocks(t):
    nb, R, _ = t.shape
    return t.transpose(1, 0, 2).reshape(R, nb * LANES)


def _forget_cumsum(f3, bf):
    nb, R, _ = f3.shape

    def body(f_ref, b_ref, c_ref):
        row = lax.broadcasted_iota(jnp.int32, (LANES, LANES), 0)
        colm = lax.broadcasted_iota(jnp.int32, (LANES, LANES), 1)
        tri = (row <= colm).astype(BF16)
        bias = b_ref[...]

        def step(n, carry):
            z = f_ref[n] + bias
            logf = jnp.minimum(z, 0.0) - _log1p(jnp.exp(-jnp.abs(z)))
            c = _tri_dot(logf, tri) + carry
            c_ref[n] = c
            return jnp.broadcast_to(c[:, LANES - 1:LANES], (R, LANES))

        lax.fori_loop(0, nb, step, jnp.zeros((R, LANES), F32))

    return pl.pallas_call(
        body, name="forget_cumsum", out_shape=jax.ShapeDtypeStruct(f3.shape, F32),
        compiler_params=_params(),
    )(f3, bf)


def _forget_bwd(dcq3, dck3, f3, bf):
    nb, R, _ = f3.shape

    def body(dcq_ref, dck_ref, f_ref, b_ref, df_ref, db_ref):
        row = lax.broadcasted_iota(jnp.int32, (LANES, LANES), 0)
        colm = lax.broadcasted_iota(jnp.int32, (LANES, LANES), 1)
        tri = (row >= colm).astype(BF16)
        bias = b_ref[...]

        def step(t, carry):
            tail, tot = carry
            n = nb - 1 - t
            r = _tri_dot(dcq_ref[n] + dck_ref[n], tri) + tail
            df = r * _sigmoid(-(f_ref[n] + bias))
            df_ref[n] = df
            tot = tot + jnp.broadcast_to(jnp.sum(df, axis=1, keepdims=True), (R, LANES))
            return jnp.broadcast_to(r[:, 0:1], (R, LANES)), tot

        _, tot = lax.fori_loop(0, nb, step, (jnp.zeros((R, LANES), F32), jnp.zeros((R, LANES), F32)))
        db_ref[...] = tot

    return pl.pallas_call(
        body, name="forget_bwd",
        out_shape=[jax.ShapeDtypeStruct(f3.shape, F32), jax.ShapeDtypeStruct((R, LANES), F32)],
        compiler_params=_params(),
    )(dcq3, dck3, f3, bf)


def _causal(t):
    return lax.broadcasted_iota(jnp.int32, (t, t), 0) >= lax.broadcasted_iota(jnp.int32, (t, t), 1)


def _fox_fwd(qkv, ck, h, *, z_off, nhb, S, tq):
    nq = S // tq
    ts = td = min(FOX_ROWS, tq)
    n_sub = n_diag = tq // ts

    def body(q_ref, k_ref, v_ref, ck_ref, z_ref, o_ref, lq_ref, g_ref):
        i = pl.program_id(1)
        qs = [q_ref[r * ts:(r + 1) * ts, :] for r in range(n_sub)]

        def absorb(q, k, v, ckj, m, l, acc, row0=None):
            s = _dot(q, k, _NT) - ckj
            if row0 is not None:
                rows = row0 + lax.broadcasted_iota(jnp.int32, s.shape, 0)
                s = jnp.where(rows >= lax.broadcasted_iota(jnp.int32, s.shape, 1), s, NEG)
            m_new = jnp.maximum(m, jnp.max(s, axis=1, keepdims=True))
            alpha = jnp.exp(m - m_new)
            p = jnp.exp(s - m_new)
            return m_new, alpha * l + jnp.sum(p, axis=1, keepdims=True), alpha * acc + _dot(p.astype(BF16), v, _NN)

        def step(j, carry):
            off = pl.multiple_of(j * tq, tq)
            k, v, ckj = k_ref[pl.ds(off, tq), :], v_ref[pl.ds(off, tq), :], ck_ref[j]
            return tuple(absorb(qs[r], k, v, ckj, *carry[r]) for r in range(n_sub))

        init = tuple((jnp.full((ts, 1), NEG, F32), jnp.zeros((ts, 1), F32), jnp.zeros((ts, LANES), F32))
                     for _ in range(n_sub))
        carry = lax.fori_loop(0, i, step, init)

        off = pl.multiple_of(i * tq, tq)
        k, v, cki = k_ref[pl.ds(off, tq), :], v_ref[pl.ds(off, tq), :], ck_ref[i]
        for f in range(n_diag):
            rows = slice(f * td, (f + 1) * td)
            part = slice(f * td % ts, f * td % ts + td)
            m, l, acc = (t[part] for t in carry[f * td // ts])
            nk = (f + 1) * td
            m, l, acc = absorb(q_ref[rows, :], k[:nk], v[:nk], cki[:, :nk], m, l, acc, row0=f * td)
            out = acc / l
            z = z_ref[rows, :]
            o_ref[rows, :] = out
            lq_ref[rows, :] = jnp.broadcast_to(m + jnp.log(l), (td, LANES))
            g_ref[rows, :] = (out * (z * _sigmoid(z))).astype(BF16)

    blk = pl.BlockSpec((tq, LANES), lambda hd, i: (i, hd))
    return pl.pallas_call(
        body, name="fox_fwd", grid=(nhb, nq),
        in_specs=[blk,
                  pl.BlockSpec((S, LANES), lambda hd, i: (0, nhb + hd)),
                  pl.BlockSpec((S, LANES), lambda hd, i: (0, 2 * nhb + hd)),
                  pl.BlockSpec((None, nq, 1, tq), lambda hd, i: (hd, 0, 0, 0)),
                  pl.BlockSpec((tq, LANES), lambda hd, i: (i, z_off // LANES + hd))],
        out_specs=[blk, blk, blk],
        out_shape=[jax.ShapeDtypeStruct((S, nhb * LANES), F32), jax.ShapeDtypeStruct((S, nhb * LANES), F32),
                   jax.ShapeDtypeStruct((S, nhb * LANES), BF16)],
        compiler_params=_params(("parallel", "arbitrary")),
    )(qkv, qkv, qkv, ck, h)


def _fox_bwd(qkv, do, stats, ck, *, nhb, S, tq, side=None):
    nq = S // tq
    ts, td = min(FOX_ROWS, tq), tq // FOX_DIAG_SUB
    n_side = 0 if side is None else len(side.ins)

    def body(*refs):
        q_ref, k_ref, v_ref, do_ref, st_ref, ck_ref = refs[:6]
        side_in = refs[6:6 + n_side]
        dq_ref, dk_ref, dv_ref, dr_ref, dc_ref = refs[6 + n_side:11 + n_side]
        side_out, sems = refs[11 + n_side:11 + 2 * n_side], refs[11 + 2 * n_side:]
        j = pl.program_id(1)
        if side is not None:
            @pl.when((pl.program_id(0) == 0) & (j == 0))
            def _():
                side.start(side_in, side_out, *sems)

        @pl.when(j == 0)
        def _():
            dq_ref[...] = jnp.zeros_like(dq_ref)
            dr_ref[...] = jnp.zeros_like(dr_ref)

        k = k_ref[...]
        v = v_ref[...]
        ckv = ck_ref[...]

        def group(start, n, nk, carry, row0=None):
            dk, dv, dc = carry
            rows = pl.ds(pl.multiple_of(start, n), n)
            q, g, st = q_ref[rows, :], do_ref[rows, :], st_ref[rows, :]
            p = jnp.exp(_dot(q, k[:nk], _NT) - ckv[:, :nk] - st[:, 0:1])
            if row0 is not None:
                rr = row0 + lax.broadcasted_iota(jnp.int32, (n, nk), 0)
                p = jnp.where(rr >= lax.broadcasted_iota(jnp.int32, (n, nk), 1), p, 0.0)
            ds = p * (_dot(g, v[:nk], _NT) - st[:, 1:2])
            dr_ref[rows, :] += jnp.broadcast_to(jnp.sum(ds, axis=1, keepdims=True), (n, LANES))
            dsb = ds.astype(BF16)
            dq_ref[rows, :] += _dot(dsb, k[:nk], _NN)
            upd = (_dot(dsb, q, _TN), _dot(p.astype(BF16), g, _TN), -jnp.sum(ds, axis=0, keepdims=True))
            if nk < tq:
                upd = (jnp.concatenate([upd[0], jnp.zeros((tq - nk, LANES), F32)], axis=0),
                       jnp.concatenate([upd[1], jnp.zeros((tq - nk, LANES), F32)], axis=0),
                       jnp.concatenate([upd[2], jnp.zeros((1, tq - nk), F32)], axis=1))
            return dk + upd[0], dv + upd[1], dc + upd[2]

        def step(i, carry):
            for r in range(tq // ts):
                carry = group(i * tq + r * ts, ts, tq, carry)
            return carry

        carry = (jnp.zeros((tq, LANES), F32), jnp.zeros((tq, LANES), F32), jnp.zeros((1, tq), F32))
        for f in range(FOX_DIAG_SUB):
            carry = group(j * tq + f * td, td, (f + 1) * td, carry, row0=f * td)
        dk, dv, dc = lax.fori_loop(j + 1, nq, step, carry)
        dk_ref[...] = dk.astype(BF16)
        dv_ref[...] = dv.astype(BF16)
        dc_ref[...] = dc

        @pl.when(j == nq - 1)
        def _():
            dq_ref[...] = dq_ref[...] * ATT_SCALE

        if side is not None:
            @pl.when((pl.program_id(0) == nhb - 1) & (j == nq - 1))
            def _():
                side.middle(side_in, side_out, *sems)
                side.finish(side_in, side_out, *sems)

    def whole(off):
        return pl.BlockSpec((S, LANES), lambda h, j: (0, off + h))

    kv_blk = pl.BlockSpec((tq, LANES), lambda h, j: (j, h))
    c_blk = pl.BlockSpec((None, None, 1, tq), lambda h, j: (h, j, 0, 0))
    in_specs = [whole(0),
                pl.BlockSpec((tq, LANES), lambda h, j: (j, nhb + h)),
                pl.BlockSpec((tq, LANES), lambda h, j: (j, 2 * nhb + h)),
                whole(0), whole(0), c_blk]
    out_specs = [whole(0), kv_blk, kv_blk, whole(0), c_blk]
    out_shape = [jax.ShapeDtypeStruct((S, nhb * LANES), F32),
                 jax.ShapeDtypeStruct((S, nhb * LANES), BF16),
                 jax.ShapeDtypeStruct((S, nhb * LANES), BF16),
                 jax.ShapeDtypeStruct((S, nhb * LANES), F32),
                 jax.ShapeDtypeStruct((nhb, nq, 1, tq), F32)]
    args = (qkv, qkv, qkv, do, stats, ck)
    if side is None:
        return pl.pallas_call(
            body, name="fox_bwd", grid=(nhb, nq), in_specs=in_specs, out_specs=out_specs, out_shape=out_shape,
            compiler_params=_params(("parallel", "arbitrary")),
        )(*args)
    res = pl.pallas_call(
        body, name="fox_bwd_xchg", grid=(nhb, nq), in_specs=in_specs + [_ANY] * n_side,
        out_specs=out_specs + [_ANY] * n_side, out_shape=out_shape + side.outs,
        scratch_shapes=_side_scratch(side), compiler_params=_params(("arbitrary", "arbitrary")),
    )(*args, *side.ins)
    return (*res[:5], res[5:])


def _up_merge(ga, gb, w_a, w_b, h, b_gate, *, g_off, D, S):
    tr = _tile(S, 256, 8)
    W = ga.shape[1]
    assert g_off % D == 0

    def body(ga_ref, gb_ref, wa_ref, wb_ref, la_ref, lb_ref, b_ref, ua_ref, ub_ref, mix_ref):
        ua = _dot(ga_ref[...], wa_ref[...], _NT)
        ub = _dot(gb_ref[...], wb_ref[...], _NT)
        ua_ref[...] = ua
        ub_ref[...] = ub
        mix_ref[...] = (_sigmoid(la_ref[...] + b_ref[:, :D]) * ua
                        + _sigmoid(lb_ref[...] + b_ref[:, D:]) * ub).astype(BF16)

    act = pl.BlockSpec((tr, W), lambda i: (i, 0))
    rows = pl.BlockSpec((tr, D), lambda i: (i, 0))
    whole = pl.BlockSpec((D, W), lambda i: (0, 0), pipeline_mode=pl.Buffered(1))
    return pl.pallas_call(
        body, name="up_merge", grid=(S // tr,),
        in_specs=[act, act, whole, whole,
                  pl.BlockSpec((tr, D), lambda i: (i, g_off // D)), pl.BlockSpec((tr, D), lambda i: (i, g_off // D + 1)),
                  pl.BlockSpec((1, 2 * D), lambda i: (0, 0))],
        out_specs=[rows, rows, rows],
        out_shape=[jax.ShapeDtypeStruct((S, D), F32), jax.ShapeDtypeStruct((S, D), F32),
                   jax.ShapeDtypeStruct((S, D), BF16)],
        compiler_params=_params(("parallel",)),
    )(ga, gb, w_a, w_b, h, h, b_gate)


def _acc_rows(i, ref, val):
    @pl.when(i == 0)
    def _():
        ref[...] = val

    @pl.when(i > 0)
    def _():
        ref[...] += val


def _merge_bwd(dmix, up_a, up_b, h, b_gate, *, g_off, D, S):
    tr, tc = _tile(S, 512, 8), _tile(D, 512)

    def fn(i, ins, outs):
        dm, ua, ub, la, lb, ba, bb = (r[...] for r in ins)
        ga, gb = _sigmoid(la + ba), _sigmoid(lb + bb)
        outs[0][...] = (ga * dm).astype(BF16)
        outs[1][...] = (gb * dm).astype(BF16)
        dla = ua * dm * (ga * (1.0 - ga))
        dlb = ub * dm * (gb * (1.0 - gb))
        outs[2][...] = dla.astype(BF16)
        outs[3][...] = dlb.astype(BF16)
        _acc_rows(i, outs[4], jnp.sum(dla, axis=0, keepdims=True))
        _acc_rows(i, outs[5], jnp.sum(dlb, axis=0, keepdims=True))

    ins = [(dmix, "tile", 0), (up_a, "tile", 0), (up_b, "tile", 0), (h, "tile", g_off), (h, "tile", g_off + D),
           (b_gate, "row", 0), (b_gate, "row", D)]
    outs = [((S, D), BF16, "tile", 0)] * 4 + [((1, D), F32, "acc", 0)] * 2
    return _ew(fn, ins, outs, rows=S, cols=D, tr=tr, tc=tc, name="merge_bwd")


def _out_proj_deepnorm(mix, w, x, g, b, *, alpha, D, S):
    tr = _tile(S, 512, 8)

    def body(m_ref, w_ref, x_ref, g_ref, b_ref, xn_ref, xb_ref, z_ref):
        z = alpha * x_ref[...] + _dot(m_ref[...], w_ref[...], _NN)
        mu = jnp.mean(z, axis=1, keepdims=True)
        zc = z - mu
        var = jnp.mean(zc * zc, axis=1, keepdims=True)
        xn = zc * lax.rsqrt(var + LN_EPS) * g_ref[...] + b_ref[...]
        xn_ref[...] = xn
        xb_ref[...] = xn.astype(BF16)
        z_ref[...] = z

    rows = pl.BlockSpec((tr, D), lambda i: (i, 0))
    vec = pl.BlockSpec((1, D), lambda i: (0, 0))
    return pl.pallas_call(
        body, name="out_proj_deepnorm", grid=(S // tr,),
        in_specs=[rows, pl.BlockSpec((D, D), lambda i: (0, 0)), rows, vec, vec],
        out_specs=[rows, rows, rows],
        out_shape=[jax.ShapeDtypeStruct((S, D), F32), jax.ShapeDtypeStruct((S, D), BF16),
                   jax.ShapeDtypeStruct((S, D), F32)],
        compiler_params=_params(("parallel",)),
    )(mix, w, x, g, b)


def _deepnorm_bwd(dxn, z, g, w, *, D, S):
    tr = _tile(S, 256, 8)

    def body(dx_ref, z_ref, g_ref, w_ref, dz_ref, dzb_ref, dm_ref, sg_ref, sb_ref):
        i = pl.program_id(0)
        dx, zv = dx_ref[...], z_ref[...]
        mu = jnp.mean(zv, axis=1, keepdims=True)
        zc = zv - mu
        rstd = lax.rsqrt(jnp.mean(zc * zc, axis=1, keepdims=True) + LN_EPS)
        u = zc * rstd
        du = dx * g_ref[...]
        dz = rstd * (du - jnp.mean(du, axis=1, keepdims=True) - u * jnp.mean(du * u, axis=1, keepdims=True))
        dzb = dz.astype(BF16)
        dz_ref[...] = dz
        dzb_ref[...] = dzb
        dm_ref[...] = _dot(dzb, w_ref[...], _NT)
        _acc_rows(i, sg_ref, jnp.sum(dx * u, axis=0, keepdims=True))
        _acc_rows(i, sb_ref, jnp.sum(dx, axis=0, keepdims=True))

    rows = pl.BlockSpec((tr, D), lambda i: (i, 0))
    vec = pl.BlockSpec((1, D), lambda i: (0, 0))
    return pl.pallas_call(
        body, name="deepnorm_bwd", grid=(S // tr,),
        in_specs=[rows, rows, vec, pl.BlockSpec((D, D), lambda i: (0, 0), pipeline_mode=pl.Buffered(1))],
        out_specs=[rows, rows, rows, vec, vec],
        out_shape=[jax.ShapeDtypeStruct((S, D), F32), jax.ShapeDtypeStruct((S, D), BF16),
                   jax.ShapeDtypeStruct((S, D), F32), jax.ShapeDtypeStruct((1, D), F32),
                   jax.ShapeDtypeStruct((1, D), F32)],
        compiler_params=_params(("arbitrary",)),
    )(dxn, z, g, w)


def _loss_head(y, target, *, D, S):
    tr = _tile(S, 256, 8)

    def body(y_ref, t_ref, d_ref, l_ref):
        err = y_ref[...] - t_ref[...]
        d_ref[...] = err * (1.0 / D)
        part = 0.5 * jnp.sum(jnp.sum(err * err, axis=1, keepdims=True) * (1.0 / D), axis=0, keepdims=True)
        _acc_rows(pl.program_id(0), l_ref, jnp.broadcast_to(part, (1, LANES)))

    return pl.pallas_call(
        body, name="loss_head", grid=(S // tr,),
        in_specs=[pl.BlockSpec((tr, D), lambda i: (i, 0))] * 2,
        out_specs=[pl.BlockSpec((tr, D), lambda i: (i, 0)), pl.BlockSpec((1, LANES), lambda i: (0, 0))],
        out_shape=[jax.ShapeDtypeStruct((S, D), F32), jax.ShapeDtypeStruct((1, LANES), F32)],
        compiler_params=_params(("arbitrary",)),
    )(y, target)


_MESH = pl.DeviceIdType.MESH
_ANY = pl.BlockSpec(memory_space=pl.ANY)


class _Gather:
    def __init__(self, xs):
        self.ins = list(xs)
        self.outs = [jax.ShapeDtypeStruct((NDEV,) + x.shape, x.dtype) for x in xs]

    def _ctx(self, a, x_ref, out_ref, send_sems, recv_sems):
        x, y, c = lax.axis_index("x"), lax.axis_index("y"), lax.axis_index("c")
        me, sibling = (x, y, c), (x, y, 1 - c)
        chips = [(1 - x, y), (x, 1 - y), (1 - x, 1 - y)]

        def slot(px, py, pc):
            return out_ref.at[4 * px + 2 * py + pc]

        def copy(k, block, to, src=None):
            return pltpu.make_async_remote_copy(
                src_ref=slot(*block) if src is None else src, dst_ref=slot(*block),
                send_sem=send_sems.at[7 * a + k], recv_sem=recv_sems.at[7 * a + k], device_id=to, device_id_type=_MESH)

        def first():
            return [copy(0, me, sibling, src=x_ref)] + [copy(1 + j, me, (*chip, c), src=x_ref)
                                                        for j, chip in enumerate(chips)]

        def passed():
            return [copy(4 + j, (*chip, c), sibling) for j, chip in enumerate(chips)]

        return c, me, sibling, chips, slot, copy, first, passed

    def start(self, in_refs, out_refs, send_sems, recv_sems, local_sems):
        for a, (x_ref, out_ref) in enumerate(zip(in_refs, out_refs)):
            _, me, _, _, slot, _, first, _ = self._ctx(a, x_ref, out_ref, send_sems, recv_sems)
            pltpu.make_async_copy(x_ref, slot(*me), local_sems.at[a]).start()
            for cp in first():
                cp.start()

    def middle(self, in_refs, out_refs, send_sems, recv_sems, local_sems):
        for a, (x_ref, out_ref) in enumerate(zip(in_refs, out_refs)):
            c, me, _, chips, _, copy, _, passed = self._ctx(a, x_ref, out_ref, send_sems, recv_sems)
            for j, (chip, cp) in enumerate(zip(chips, passed())):
                copy(1 + j, (*chip, c), me).wait_recv()
                cp.start()

    def finish(self, in_refs, out_refs, send_sems, recv_sems, local_sems):
        for a, (x_ref, out_ref) in enumerate(zip(in_refs, out_refs)):
            c, me, sibling, chips, slot, copy, first, passed = self._ctx(a, x_ref, out_ref, send_sems, recv_sems)
            copy(0, sibling, me).wait_recv()
            for j, chip in enumerate(chips):
                copy(4 + j, (*chip, 1 - c), me).wait_recv()
            for cp in first() + passed():
                cp.wait_send()
            pltpu.make_async_copy(x_ref, slot(*me), local_sems.at[a]).wait()


class _Exchange:
    def __init__(self, gs):
        self.ins = list(gs)
        self.outs = [jax.ShapeDtypeStruct(g.shape, g.dtype) for g in gs]

    def _copies(self, a, g_ref, out_ref, send_sems, recv_sems, arrivals=True):
        x, y, c = lax.axis_index("x"), lax.axis_index("y"), lax.axis_index("c")
        me = 4 * x + 2 * y + c
        out, back = [], []
        for k in range(1, NDEV):
            px, py, pc = x ^ ((k >> 2) & 1), y ^ ((k >> 1) & 1), c ^ (k & 1)
            peer = 4 * px + 2 * py + pc
            sems = dict(send_sem=send_sems.at[7 * a + k - 1], recv_sem=recv_sems.at[7 * a + k - 1],
                        device_id=(px, py, pc), device_id_type=_MESH)
            out.append(pltpu.make_async_remote_copy(src_ref=g_ref.at[peer], dst_ref=out_ref.at[me], **sems))
            if arrivals:
                back.append(pltpu.make_async_remote_copy(src_ref=g_ref.at[me], dst_ref=out_ref.at[peer], **sems))
        return me, out, back

    def start(self, in_refs, out_refs, send_sems, recv_sems, local_sems):
        for a, (g_ref, out_ref) in enumerate(zip(in_refs, out_refs)):
            me, out, _ = self._copies(a, g_ref, out_ref, send_sems, recv_sems, arrivals=False)
            pltpu.make_async_copy(g_ref.at[me], out_ref.at[me], local_sems.at[a]).start()
            for cp in out:
                cp.start()

    def middle(self, in_refs, out_refs, send_sems, recv_sems, local_sems):
        pass

    def finish(self, in_refs, out_refs, send_sems, recv_sems, local_sems):
        for a, (g_ref, out_ref) in enumerate(zip(in_refs, out_refs)):
            me, out, back = self._copies(a, g_ref, out_ref, send_sems, recv_sems)
            for cp in back:
                cp.wait_recv()
            for cp in out:
                cp.wait_send()
            pltpu.make_async_copy(g_ref.at[me], out_ref.at[me], local_sems.at[a]).wait()


def _side_scratch(side):
    n = len(side.ins)
    return [pltpu.SemaphoreType.DMA((7 * n,)), pltpu.SemaphoreType.DMA((7 * n,)), pltpu.SemaphoreType.DMA((n,))]


def _run_side(side, *, name):
    n = len(side.ins)

    def body(*refs):
        ins, outs, sems = refs[:n], refs[n:2 * n], refs[2 * n:]
        side.start(ins, outs, *sems)
        side.middle(ins, outs, *sems)
        side.finish(ins, outs, *sems)

    return pl.pallas_call(
        body, name=name, out_shape=side.outs, in_specs=[_ANY] * n, out_specs=[_ANY] * n,
        scratch_shapes=_side_scratch(side),
    )(*side.ins)


def _all_gather(x, *, name):
    return _run_side(_Gather([x]), name=name)[0]


def _sum_slots(parts, *, name):
    n, R, C = parts.shape
    tr, tc = _tile(R, 64, 16), _tile(C, 2048)

    def body(p_ref, o_ref):
        acc = p_ref[0].astype(F32)
        for k in range(1, n):
            acc = acc + p_ref[k].astype(F32)
        o_ref[...] = acc

    return pl.pallas_call(
        body, name=name, grid=(R // tr, C // tc),
        in_specs=[pl.BlockSpec((n, tr, tc), lambda i, j: (0, i, j))],
        out_specs=pl.BlockSpec((tr, tc), lambda i, j: (i, j)),
        out_shape=jax.ShapeDtypeStruct((R, C), F32),
        compiler_params=_params(("parallel", "parallel")),
    )(parts)


def _adamw(w, g, m, v, *, name):
    R, C = w.shape
    budget = 2 << 20
    tr = R if R * C * 4 <= budget else _tile(R, max(8, (budget // (C * 4)) // 8 * 8), 8)

    def body(w_ref, g_ref, m_ref, v_ref, d_ref, nm_ref, nv_ref):
        gv = g_ref[...]
        nm = ADAM_B1 * m_ref[...] + (1.0 - ADAM_B1) * gv
        nv = ADAM_B2 * v_ref[...] + (1.0 - ADAM_B2) * (gv * gv)
        m_hat = nm / (1.0 - ADAM_B1 ** ADAM_STEP)
        v_hat = nv / (1.0 - ADAM_B2 ** ADAM_STEP)
        d_ref[...] = -ADAM_LR * (m_hat / (jnp.sqrt(v_hat) + ADAM_EPS) + ADAM_WD * w_ref[...])
        nm_ref[...] = nm
        nv_ref[...] = nv

    spec = pl.BlockSpec((tr, C), lambda i: (i, 0))
    return pl.pallas_call(
        body, name=name, grid=(R // tr,), in_specs=[spec] * 4, out_specs=[spec] * 3,
        out_shape=[jax.ShapeDtypeStruct((R, C), F32)] * 3,
        compiler_params=_params(("parallel",)),
    )(w, g, m, v)


def _pad_cols(w, *, nq, ng):
    nf = w.shape[-1] - nq - ng
    pad = jnp.zeros(w.shape[:-1] + (FORGET_PAD - nf,), w.dtype)
    return jnp.concatenate([w[..., :nq], w[..., nq + nf:], w[..., nq:nq + nf], pad], axis=-1)


def _unpad_cols(w, *, nq, ng, nf):
    return jnp.concatenate([w[..., :nq], w[..., nq + ng:nq + ng + nf], w[..., nq:nq + ng]], axis=-1)


def kernel(x, w_in, b_forget, b_gate, w_up_a, w_up_b, w_out, ln_g, ln_b, loss_target, m_w_in, m_b_forget, m_b_gate, m_w_up_a, m_w_up_b, m_w_out, m_ln_g, m_ln_b, v_w_in, v_b_forget, v_b_gate, v_w_up_a, v_w_up_b, v_w_out, v_ln_g, v_ln_b):
    depth = w_in.shape[0]
    _, S, D = x.shape
    WA, WB = w_up_a.shape[1], w_up_b.shape[1]
    nha, nhb = WA // HEAD_DIM, WB // HEAD_DIM
    assert nhb == b_forget.shape[1] and WA == WB and nhb <= HEAD_ROWS
    nq, ng, nf = 4 * WA + 4 * WB, 2 * D, nhb
    assert w_in.shape[2] == nq + nf + ng
    ncp = nq + ng + FORGET_PAD
    off_va, off_za, off_b, off_zb, off_g, off_f = 2 * WA, 3 * WA, 4 * WA, 4 * WA + 3 * WB, nq, nq + ng
    alpha = float((2 * depth) ** 0.25)
    dsh = D // NDEV
    tq_f, tq_b = _tile(S, FOX_TILE_FWD), _tile(S, FOX_TILE_BWD)

    x0 = x[0]
    target = loss_target[0]
    cos, sin = _rope_tables(S)

    w_in_p = _pad_cols(w_in, nq=nq, ng=ng).astype(BF16)
    w_up_a_t = jnp.swapaxes(w_up_a, 1, 2).astype(BF16)
    w_up_b_t = jnp.swapaxes(w_up_b, 1, 2).astype(BF16)
    w_out_c = w_out.astype(BF16)
    W_in, W_ua, W_ub, W_out = ([None] * depth for _ in range(4))

    def others(l):
        return [w_up_a_t[l], w_up_b_t[l], w_out_c[l]]

    def keep_others(l, got):
        W_ua[l], W_ub[l], W_out[l] = got[0].reshape(D, WA), got[1].reshape(D, WB), got[2].reshape(D, D)

    W_in[0] = _run_side(_Gather([w_in_p[0]]), name="gather_w_in_0")[0].reshape(D, ncp)

    bf_rows = [jnp.broadcast_to(jnp.pad(b_forget[l], (0, HEAD_ROWS - nhb))[:, None], (HEAD_ROWS, LANES))
               for l in range(depth)]

    saved = []
    xf, xb = x0, x0.astype(BF16)
    for l in range(depth):
        wanted = (others(0) if l == 0 else []) + ([w_in_p[l + 1]] + others(l + 1) if l + 1 < depth else [])
        if wanted:
            h, got = _mm(xb, W_in[l], name="in_proj_gather", tm=1024, tn=1280, tk=2048, side=_Gather(wanted))
            if l == 0:
                keep_others(0, got[:3])
            if l + 1 < depth:
                W_in[l + 1] = got[-4].reshape(D, ncp)
                keep_others(l + 1, got[-3:])
        else:
            h = _mm(xb, W_in[l], name="in_proj", tm=1024, tn=1280, tk=2048)
        qk_a = _prep_a(h, cos, sin, W=WA, S=S)
        qkv_b = _prep_b(h, off=off_b, W=WB, S=S)
        out_a, lse_a, ga = _band_fwd(qk_a, h, nha=nha, S=S, v_off=off_va, z_off=off_za)
        f3 = _lane_blocks(jnp.pad(h[:, off_f:off_f + nhb].T, ((0, HEAD_ROWS - nhb), (0, 0))))
        c = _lane_unblocks(_forget_cumsum(f3, bf_rows[l]))[:nhb]
        out_b, lq_b, gb = _fox_fwd(qkv_b, c.reshape(nhb, S // tq_f, 1, tq_f), h, z_off=off_zb, nhb=nhb, S=S, tq=tq_f)
        ck = c.reshape(nhb, S // tq_b, 1, tq_b)
        bg = b_gate[l][None, :]
        up_a, up_b, mix = _up_merge(ga, gb, W_ua[l], W_ub[l], h, bg, g_off=off_g, D=D, S=S)
        xn_f, xn_b, z = _out_proj_deepnorm(mix, W_out[l], xf, ln_g[l][None, :], ln_b[l][None, :],
                                           alpha=alpha, D=D, S=S)
        saved.append(dict(xb=xb, h=h, qk_a=qk_a, qkv_b=qkv_b, out_a=out_a, lse_a=lse_a, ga=ga, f3=f3, ck=ck,
                          out_b=out_b, lq_b=lq_b, gb=gb, up_a=up_a, up_b=up_b, mix=mix, z=z, bg=bg))
        xf, xb = xn_f, xn_b

    dx, loss_row = _loss_head(xf, target, D=D, S=S)

    recv = [None] * depth
    small = [None] * depth
    pending = None
    for l in reversed(range(depth)):
        sv = saved[l]
        h = sv["h"]
        dz, dzb, dmix, d_lng, d_lnb = _deepnorm_bwd(dx, sv["z"], ln_g[l][None, :], W_out[l], D=D, S=S)
        g_w_out = _mm(sv["mix"], dzb, ta=True, name="g_w_out", tm=512, tn=1024, tk=4096, out_dtype=BF16)
        dup_a, dup_b, dgl_a, dgl_b, d_bga, d_bgb = _merge_bwd(dmix, sv["up_a"], sv["up_b"], h, sv["bg"],
                                                              g_off=off_g, D=D, S=S)
        g_w_ua = _mm(dup_a, sv["ga"], ta=True, name="g_w_up_a", tm=512, tn=1536, tk=4096, out_dtype=BF16)
        g_w_ub = _mm(dup_b, sv["gb"], ta=True, name="g_w_up_b", tm=512, tn=1536, tk=4096, out_dtype=BF16)
        do_a, dza, stats_a = _gate_bwd(dup_a, W_ua[l], sv["out_a"], h, sv["lse_a"], z_off=off_za, W=WA, S=S,
                                       name="gate_bwd_a", do_dtype=F32)
        do_b, dzb_, stats_b = _gate_bwd(dup_b, W_ub[l], sv["out_b"], h, sv["lq_b"], z_off=off_zb, W=WB, S=S,
                                        name="gate_bwd_b", do_dtype=BF16)
        dqa, dka, dva = _rope_bwd(*_band_bwd(sv["qk_a"], h, do_a, stats_a, nha=nha, S=S, v_off=off_va),
                                  cos, sin, W=WA, S=S)
        if pending is None:
            dqb, dkb, dvb, dcq, dck = _fox_bwd(sv["qkv_b"], do_b, stats_b, sv["ck"], nhb=nhb, S=S, tq=tq_b)
        else:
            dqb, dkb, dvb, dcq, dck, got = _fox_bwd(sv["qkv_b"], do_b, stats_b, sv["ck"], nhb=nhb, S=S, tq=tq_b,
                                                    side=_Exchange([pending]))
            recv[l + 1] = (got[0], *recv[l + 1])
        head_rows = lambda t: _lane_blocks(jnp.pad(t, ((0, HEAD_ROWS - nhb), (0, 0))))
        df3, d_bf = _forget_bwd(head_rows(dcq[:, ::LANES].T), head_rows(dck.reshape(nhb, S)), sv["f3"], bf_rows[l])
        dfl = jnp.pad(_lane_unblocks(df3)[:nhb].T, ((0, 0), (0, FORGET_PAD - nhb))).astype(BF16)
        dh = jnp.concatenate([dqa, dka, dva, dza, dqb.astype(BF16), dkb, dvb, dzb_, dgl_a, dgl_b, dfl], axis=1)
        chunks = lambda t: t.reshape(NDEV, dsh, t.shape[1])
        g_w_in, got3 = _mm(sv["xb"], dh, ta=True, name="g_w_in_xchg", tm=512, tn=1280, tk=4096, out_dtype=BF16,
                           side=_Exchange([chunks(g_w_ua), chunks(g_w_ub), chunks(g_w_out)]))
        if l > 0:
            dx = _mm(dh, W_in[l], tb=True, name="d_x", tm=1024, tn=512, tk=3328, add=dz, add_scale=alpha)
            pending, recv[l] = chunks(g_w_in), tuple(got3)
        else:
            dx, got1 = _mm(dh, W_in[l], tb=True, name="d_x_xchg", tm=1024, tn=512, tk=3328, add=dz, add_scale=alpha,
                           side=_Exchange([chunks(g_w_in)]))
            recv[l] = (got1[0], *got3)
        small[l] = jnp.concatenate([d_bf[:nhb, 0], jnp.zeros((LANES - nhb,), F32), d_bga[0], d_bgb[0],
                                    d_lng[0], d_lnb[0]])

    n_small = LANES + 2 * D + 2 * D
    flat = jnp.concatenate(small + [loss_row[0]])
    rows = -(-flat.shape[0] // (8 * LANES)) * 8
    flat = jnp.pad(flat, (0, rows * LANES - flat.shape[0])).reshape(rows, LANES)
    tot = _sum_slots(_all_gather(flat, name="gather_small"), name="sum_small").reshape(-1)
    loss = tot[depth * n_small]
    sm = tot[:depth * n_small].reshape(depth, n_small)
    g_bf, g_bg = sm[:, :nhb], sm[:, LANES:LANES + 2 * D]
    g_lg, g_lb = sm[:, LANES + 2 * D:LANES + 3 * D], sm[:, LANES + 3 * D:]

    g_in = jnp.stack([_unpad_cols(_sum_slots(recv[l][0], name="sum_w_in"), nq=nq, ng=ng, nf=nf) for l in range(depth)])
    g_ua = jnp.stack([_sum_slots(recv[l][1], name="sum_w_up_a").T for l in range(depth)])
    g_ub = jnp.stack([_sum_slots(recv[l][2], name="sum_w_up_b").T for l in range(depth)])
    g_out = jnp.stack([_sum_slots(recv[l][3], name="sum_w_out") for l in range(depth)])

    def adam(w, g, m, v, name):
        shp = w.shape
        d_, m_, v_ = _adamw(*[t.reshape(-1, shp[-1]) for t in (w, g, m, v)], name=name)
        return d_.reshape(shp), m_.reshape(shp), v_.reshape(shp)

    grads = [g_in, g_bf, g_bg, g_ua, g_ub, g_out, g_lg, g_lb]
    ws = [w_in, b_forget, b_gate, w_up_a, w_up_b, w_out, ln_g, ln_b]
    ms = [m_w_in, m_b_forget, m_b_gate, m_w_up_a, m_w_up_b, m_w_out, m_ln_g, m_ln_b]
    vs = [v_w_in, v_b_forget, v_b_gate, v_w_up_a, v_w_up_b, v_w_out, v_ln_g, v_ln_b]
    names = ["w_in", "b_forget", "b_gate", "w_up_a", "w_up_b", "w_out", "ln_g", "ln_b"]
    upd = [adam(w, g, m, v, f"adamw_{n}") for w, g, m, v, n in zip(ws, grads, ms, vs, names)]
    return (loss, dx[None], *grads, *[u[0] for u in upd], *[u[1] for u in upd], *[u[2] for u in upd])
```

```python
import jax
import jax.numpy as jnp
from jax import lax
from jax.experimental import pallas as pl
from jax.experimental.pallas import tpu as pltpu

F32 = jnp.float32
BF16 = jnp.bfloat16

NDEV = 8
HEAD_DIM = 128
LANES = 128
Q_BLOCK = 128
DILATIONS = (1, 4, 16)
ROPE_THETA = 10000.0
LN_EPS = 1e-5
ATT_SCALE = HEAD_DIM ** -0.5
NEG = -1e30
FORGET_PAD = 256
BAND_CHUNK = 2048
BAND_UNROLL = 16
FOX_TILE_FWD = 2048
FOX_TILE_BWD = 1024
FOX_ROWS = 512
GATE_BWD_ROWS = 512
FOX_DIAG_SUB = 4
HEAD_ROWS = 16
ADAM_LR, ADAM_B1, ADAM_B2, ADAM_EPS, ADAM_WD, ADAM_STEP = 0.001, 0.9, 0.999, 1e-08, 0.01, 10
VMEM_LIMIT = 56 * 1024 * 1024


def _tile(n, pref, mult=LANES):
    if n <= pref:
        return n
    t = (pref // mult) * mult
    while t >= mult:
        if n % t == 0:
            return t
        t -= mult
    return n


def _params(sem=None):
    return pltpu.CompilerParams(dimension_semantics=sem, vmem_limit_bytes=VMEM_LIMIT)


def _sigmoid(z):
    return 1.0 / (1.0 + jnp.exp(-z))


_NT = (((1,), (1,)), ((), ()))
_NN = (((1,), (0,)), ((), ()))
_TN = (((0,), (0,)), ((), ()))


def _dot(a, b, dims):
    return lax.dot_general(a, b, dims, preferred_element_type=F32)


def _mm(a, b, *, name, ta=False, tb=False, out_dtype=F32, tm=512, tn=512, tk=512, add=None, add_scale=1.0,
        side=None):
    if ta:
        K, M = a.shape
    else:
        M, K = a.shape
    if tb:
        N, K2 = b.shape
    else:
        K2, N = b.shape
    assert K == K2, (a.shape, b.shape, ta, tb)
    tm, tn, tk = _tile(M, tm), _tile(N, tn), _tile(K, tk)
    ni, nj, nk = M // tm, N // tn, K // tk
    dims = (((0 if ta else 1,), (1 if tb else 0,)), ((), ()))
    n_side = 0 if side is None else len(side.ins)
    n_in = 2 + (add is not None)

    def body(*refs):
        a_ref, b_ref = refs[:2]
        add_ref = refs[2] if add is not None else None
        side_in = refs[n_in:n_in + n_side]
        o_ref = refs[n_in + n_side]
        side_out = refs[n_in + n_side + 1:n_in + 2 * n_side + 1]
        acc_ref = refs[n_in + 2 * n_side + 1]
        sems = refs[n_in + 2 * n_side + 2:]
        k = pl.program_id(2)
        if side is not None:
            @pl.when((pl.program_id(0) == 0) & (pl.program_id(1) == 0) & (k == 0))
            def _():
                side.start(side_in, side_out, *sems)

        part = _dot(a_ref[...].astype(BF16), b_ref[...].astype(BF16), dims)

        @pl.when(k == 0)
        def _():
            acc_ref[...] = part

        @pl.when(k > 0)
        def _():
            acc_ref[...] += part

        @pl.when(k == nk - 1)
        def _():
            r = acc_ref[...]
            if add_ref is not None:
                r = r + add_scale * add_ref[...]
            o_ref[...] = r.astype(out_dtype)

        if side is not None:
            step = (pl.program_id(0) * nj + pl.program_id(1)) * nk + k
            steps = ni * nj * nk

            @pl.when(step == max((7 * steps) // 8 - 1, 0))
            def _():
                side.middle(side_in, side_out, *sems)

            @pl.when(step == steps - 1)
            def _():
                side.finish(side_in, side_out, *sems)

    a_spec = pl.BlockSpec((tk, tm), lambda i, j, k: (k, i)) if ta else pl.BlockSpec((tm, tk), lambda i, j, k: (i, k))
    b_spec = pl.BlockSpec((tn, tk), lambda i, j, k: (j, k)) if tb else pl.BlockSpec((tk, tn), lambda i, j, k: (k, j))
    o_spec = pl.BlockSpec((tm, tn), lambda i, j, k: (i, j))
    in_specs, args = [a_spec, b_spec], [a, b]
    if add is not None:
        in_specs.append(o_spec)
        args.append(add)
    out_shape = jax.ShapeDtypeStruct((M, N), out_dtype)
    scratch = [pltpu.VMEM((tm, tn), F32)]
    if side is None:
        return pl.pallas_call(
            body, name=name, grid=(ni, nj, nk), in_specs=in_specs, out_specs=o_spec, out_shape=out_shape,
            scratch_shapes=scratch, compiler_params=_params(("parallel", "parallel", "arbitrary")),
        )(*args)
    res = pl.pallas_call(
        body, name=name, grid=(ni, nj, nk), in_specs=in_specs + [_ANY] * n_side,
        out_specs=[o_spec] + [_ANY] * n_side, out_shape=[out_shape] + side.outs,
        scratch_shapes=scratch + _side_scratch(side),
        compiler_params=_params(("arbitrary", "arbitrary", "arbitrary")),
    )(*args, *side.ins)
    return res[0], res[1:]


def _ew(fn, ins, outs, *, rows, cols, tr, tc, name):
    assert rows % tr == 0 and cols % tc == 0

    def spec(kind, off):
        assert off % tc == 0, (off, tc)
        ob = off // tc
        if kind == "tile":
            return pl.BlockSpec((tr, tc), lambda j, i: (i, j + ob))
        if kind == "tab":
            return pl.BlockSpec((tr, LANES), lambda j, i: (i, 0))
        return pl.BlockSpec((1, tc), lambda j, i: (0, j + ob))

    n_in = len(ins)

    def body(*refs):
        fn(pl.program_id(1), refs[:n_in], refs[n_in:])

    return pl.pallas_call(
        body, name=name, grid=(cols // tc, rows // tr),
        in_specs=[spec(k, o) for _, k, o in ins],
        out_specs=[spec(k, o) for _, _, k, o in outs],
        out_shape=[jax.ShapeDtypeStruct(s, d) for s, d, _, _ in outs],
        compiler_params=_params(("parallel", "arbitrary")),
    )(*[a for a, _, _ in ins])


def _heads(t):
    return [t[:, e:e + LANES] for e in range(0, t.shape[1], LANES)]


def _rope_tables(S):
    half = HEAD_DIM // 2
    inv_freq = ROPE_THETA ** (-jnp.arange(half, dtype=F32) / half)
    ang = jnp.arange(S, dtype=jnp.int32).astype(F32)[:, None] * inv_freq[None, :]
    cos, sin = jnp.cos(ang), jnp.sin(ang)
    return jnp.concatenate([cos, cos], axis=1), jnp.concatenate([-sin, sin], axis=1)


def _rope(t, cs, sn):
    return jnp.concatenate([g * cs + pltpu.roll(g, HEAD_DIM // 2, 1) * sn for g in _heads(t)], axis=1)


def _rope_t(t, cs, sn):
    return jnp.concatenate([g * cs + pltpu.roll(g * sn, HEAD_DIM // 2, 1) for g in _heads(t)], axis=1)


def _prep_a(h, cos, sin, *, W, S):
    tr, tc = _tile(S, 1024, 8), _tile(W, 512)

    def fn(i, ins, outs):
        outs[0][...] = _rope(ins[0][...], ins[1][...], ins[2][...])

    return _ew(fn, [(h, "tile", 0), (cos, "tab", 0), (sin, "tab", 0)], [((S, 2 * W), F32, "tile", 0)],
               rows=S, cols=2 * W, tr=tr, tc=tc, name="prep_a")[0]


def _prep_b(h, *, off, W, S):
    tr, tc = _tile(S, 1024, 8), _tile(W, 512)

    def fn(i, ins, outs):
        scale = jnp.where(pl.program_id(0) * tc < W, ATT_SCALE, 1.0).astype(F32)
        outs[0][...] = (ins[0][...] * scale).astype(BF16)

    return _ew(fn, [(h, "tile", off)], [((S, 3 * W), BF16, "tile", 0)], rows=S, cols=3 * W, tr=tr, tc=tc,
               name="prep_b")[0]


def _rope_bwd(dq, dk, dv, cos, sin, *, W, S):
    tr, tc = _tile(S, 1024, 8), _tile(W, 512)

    def fn(i, ins, outs):
        cs, sn = ins[3][...], ins[4][...]
        outs[0][...] = _rope_t(ins[0][...], cs, sn).astype(BF16)
        outs[1][...] = _rope_t(ins[1][...], cs, sn).astype(BF16)
        outs[2][...] = ins[2][...].astype(BF16)

    ins = [(dq, "tile", 0), (dk, "tile", 0), (dv, "tile", 0), (cos, "tab", 0), (sin, "tab", 0)]
    return _ew(fn, ins, [((S, W), BF16, "tile", 0)] * 3, rows=S, cols=W, tr=tr, tc=tc, name="rope_bwd")


def _band_biases():
    dist = (lax.broadcasted_iota(jnp.int32, (Q_BLOCK, 2 * Q_BLOCK), 0)
            - lax.broadcasted_iota(jnp.int32, (Q_BLOCK, 2 * Q_BLOCK), 1))
    first = jnp.where(dist >= 0, 0.0, NEG).astype(F32)
    other = jnp.where((dist + Q_BLOCK >= 0) & (dist <= 0), 0.0, NEG).astype(F32)
    return first, other


def _band_rows(it, c, d, nbc, biases):
    r, n = (0, it) if d == 1 else (it // nbc, it % nbc)
    ng = c * nbc + n
    k0 = pl.multiple_of(jnp.maximum(ng - 1, 0) * Q_BLOCK, Q_BLOCK)
    bias = jnp.where(ng == 0, biases[0], biases[1])
    if d == 1:
        return pl.ds(pl.multiple_of(it * Q_BLOCK, Q_BLOCK), Q_BLOCK), pl.ds(k0, 2 * Q_BLOCK), bias
    return (pl.ds(r + d * Q_BLOCK * n, Q_BLOCK, stride=d), pl.ds(r + d * k0, 2 * Q_BLOCK, stride=d), bias)


def _band_chunk(S):
    ch = min(S, BAND_CHUNK)
    assert S % ch == 0 and ch % (Q_BLOCK * max(DILATIONS)) == 0 and S >= 2 * Q_BLOCK * max(DILATIONS)
    return ch


def _band_fwd(qk, h, *, nha, S, v_off, z_off):
    CH = _band_chunk(S)
    W = nha * LANES
    slab = 256

    def body(q_ref, k_ref, v_ref, z_ref, o_ref, lse_ref, g_ref, *scratch):
        os_refs, ls_refs = scratch[:len(DILATIONS)], scratch[len(DILATIONS):]
        c = pl.program_id(1)
        biases = _band_biases()
        for gi, d in enumerate(DILATIONS):
            nbc = CH // (Q_BLOCK * d)

            def step(it, carry, gi=gi, d=d, nbc=nbc):
                rq, rk, bias = _band_rows(it, c, d, nbc, biases)
                q = q_ref[rq, :].astype(BF16)
                k = k_ref[rk, :].astype(BF16)
                v = v_ref[rk, :].astype(BF16)
                s = _dot(q, k, _NT) * ATT_SCALE + bias
                m = jnp.max(s, axis=1, keepdims=True)
                p = jnp.exp(s - m)
                den = jnp.sum(p, axis=1, keepdims=True)
                os_refs[gi][rq, :] = _dot(p.astype(BF16), v, _NN) / den
                ls_refs[gi][rq, :] = jnp.broadcast_to(m + jnp.log(den), (Q_BLOCK, LANES))
                return carry

            lax.fori_loop(0, CH // Q_BLOCK, step, 0, unroll=BAND_UNROLL)

        def mix(t, carry):
            rows = pl.ds(pl.multiple_of(t * slab, slab), slab)
            a, b, cc = (r[rows, :] for r in ls_refs)
            m = jnp.maximum(jnp.maximum(a, b), cc)
            ea, eb, ec = jnp.exp(a - m), jnp.exp(b - m), jnp.exp(cc - m)
            den = ea + eb + ec
            out = (ea * os_refs[0][rows, :] + eb * os_refs[1][rows, :] + ec * os_refs[2][rows, :]) / den
            z = z_ref[rows, :]
            o_ref[rows, :] = out
            lse_ref[rows, :] = m + jnp.log(den)
            g_ref[rows, :] = (out * (z * _sigmoid(z))).astype(BF16)
            return carry

        lax.fori_loop(0, CH // slab, mix, 0)

    chunk = pl.BlockSpec((CH, LANES), lambda hd, c: (c, hd))
    return pl.pallas_call(
        body, name="band_fwd", grid=(nha, S // CH),
        in_specs=[chunk,
                  pl.BlockSpec((S, LANES), lambda hd, c: (0, nha + hd)),
                  pl.BlockSpec((S, LANES), lambda hd, c: (0, v_off // LANES + hd)),
                  pl.BlockSpec((CH, LANES), lambda hd, c: (c, z_off // LANES + hd))],
        out_specs=[chunk, chunk, chunk],
        out_shape=[jax.ShapeDtypeStruct((S, W), F32), jax.ShapeDtypeStruct((S, W), F32),
                   jax.ShapeDtypeStruct((S, W), BF16)],
        scratch_shapes=[pltpu.VMEM((CH, LANES), F32)] * (2 * len(DILATIONS)),
        compiler_params=_params(("parallel", "arbitrary")),
    )(qk, qk, h, h)


def _band_bwd(qk, h, do, stats, *, nha, S, v_off):
    CH = _band_chunk(S)
    W = nha * LANES

    def body(q_ref, k_ref, v_ref, do_ref, st_ref, dq_ref, dk_ref, dv_ref):
        c = pl.program_id(1)

        @pl.when(c == 0)
        def _():
            dk_ref[...] = jnp.zeros_like(dk_ref)
            dv_ref[...] = jnp.zeros_like(dv_ref)

        biases = _band_biases()
        for gi, d in enumerate(DILATIONS):
            nbc = CH // (Q_BLOCK * d)

            def step(it, carry, gi=gi, d=d, nbc=nbc):
                rq, rk, bias = _band_rows(it, c, d, nbc, biases)
                q = q_ref[rq, :].astype(BF16)
                k = k_ref[rk, :].astype(BF16)
                v = v_ref[rk, :].astype(BF16)
                g = do_ref[rq, :].astype(BF16)
                st = st_ref[rq, :]
                p = jnp.exp(_dot(q, k, _NT) * ATT_SCALE + (bias - st[:, 0:1]))
                ds = (p * (_dot(g, v, _NT) - st[:, 1:2]) * ATT_SCALE).astype(BF16)
                dq = _dot(ds, k, _NN)
                if gi == 0:
                    dq_ref[rq, :] = dq
                else:
                    dq_ref[rq, :] += dq
                dk_ref[rk, :] += _dot(ds, q, _TN)
                dv_ref[rk, :] += _dot(p.astype(BF16), g, _TN)
                return carry

            lax.fori_loop(0, CH // Q_BLOCK, step, 0, unroll=BAND_UNROLL)

    chunk = pl.BlockSpec((CH, LANES), lambda hd, c: (c, hd))
    whole = pl.BlockSpec((S, LANES), lambda hd, c: (0, hd))
    return pl.pallas_call(
        body, name="band_bwd", grid=(nha, S // CH),
        in_specs=[chunk,
                  pl.BlockSpec((S, LANES), lambda hd, c: (0, nha + hd)),
                  pl.BlockSpec((S, LANES), lambda hd, c: (0, v_off // LANES + hd)),
                  chunk, chunk],
        out_specs=[chunk, whole, whole],
        out_shape=[jax.ShapeDtypeStruct((S, W), F32)] * 3,
        compiler_params=_params(("parallel", "arbitrary")),
    )(qk, qk, h, do, stats)


def _gate_bwd(dup, w, o, h, lane0, *, z_off, W, S, name, do_dtype):
    tr = _tile(S, GATE_BWD_ROWS, 8)
    D = dup.shape[1]
    assert z_off % W == 0

    def body(d_ref, w_ref, o_ref, z_ref, l0_ref, do_ref, dz_ref, st_ref):
        g = _dot(d_ref[...], w_ref[...], _NN)
        out, z = o_ref[...], z_ref[...]
        sg = _sigmoid(z)
        do = g * (z * sg)
        do_ref[...] = do.astype(do_dtype)
        dz_ref[...] = (g * out * (sg * (1.0 + z * (1.0 - sg)))).astype(BF16)
        delta = jnp.concatenate([jnp.broadcast_to(jnp.sum(t, axis=1, keepdims=True), t.shape)
                                 for t in _heads(do * out)], axis=1)
        first = lax.broadcasted_iota(jnp.int32, delta.shape, 1) % LANES == 0
        st_ref[...] = jnp.where(first, l0_ref[...], delta)

    rows = pl.BlockSpec((tr, W), lambda i: (i, 0))
    return pl.pallas_call(
        body, name=name, grid=(S // tr,),
        in_specs=[pl.BlockSpec((tr, D), lambda i: (i, 0)),
                  pl.BlockSpec((D, W), lambda i: (0, 0), pipeline_mode=pl.Buffered(1)),
                  rows, pl.BlockSpec((tr, W), lambda i: (i, z_off // W)), rows],
        out_specs=[rows, rows, rows],
        out_shape=[jax.ShapeDtypeStruct((S, W), do_dtype), jax.ShapeDtypeStruct((S, W), BF16),
                   jax.ShapeDtypeStruct((S, W), F32)],
        compiler_params=_params(("parallel",)),
    )(dup, w, o, h, lane0)


def _split3(x):
    hi = x.astype(BF16)
    r = x - hi.astype(F32)
    mid = r.astype(BF16)
    lo = (r - mid.astype(F32)).astype(BF16)
    return hi, mid, lo


def _tri_dot(x, tri):
    hi, mid, lo = _split3(x)
    return _dot(hi, tri, _NN) + (_dot(mid, tri, _NN) + _dot(lo, tri, _NN))


def _log1p(u):
    w = 1.0 + u
    return jnp.where(w == 1.0, u, jnp.log(w) * (u / jnp.where(w == 1.0, 1.0, w - 1.0)))


def _lane_blocks(t):
    R, S = t.shape
    return t.reshape(R, S // LANES, LANES).transpose(1, 0, 2)


def _lane_unblocks(t):
    nb, R, _ = t.shape
    return t.transpose(1, 0, 2).reshape(R, nb * LANES)


def _forget_cumsum(f3, bf):
    nb, R, _ = f3.shape

    def body(f_ref, b_ref, c_ref):
        row = lax.broadcasted_iota(jnp.int32, (LANES, LANES), 0)
        colm = lax.broadcasted_iota(jnp.int32, (LANES, LANES), 1)
        tri = (row <= colm).astype(BF16)
        bias = b_ref[...]

        def step(n, carry):
            z = f_ref[n] + bias
            logf = jnp.minimum(z, 0.0) - _log1p(jnp.exp(-jnp.abs(z)))
            c = _tri_dot(logf, tri) + carry
            c_ref[n] = c
            return jnp.broadcast_to(c[:, LANES - 1:LANES], (R, LANES))

        lax.fori_loop(0, nb, step, jnp.zeros((R, LANES), F32))

    return pl.pallas_call(
        body, name="forget_cumsum", out_shape=jax.ShapeDtypeStruct(f3.shape, F32),
        compiler_params=_params(),
    )(f3, bf)


def _forget_bwd(dcq3, dck3, f3, bf):
    nb, R, _ = f3.shape

    def body(dcq_ref, dck_ref, f_ref, b_ref, df_ref, db_ref):
        row = lax.broadcasted_iota(jnp.int32, (LANES, LANES), 0)
        colm = lax.broadcasted_iota(jnp.int32, (LANES, LANES), 1)
        tri = (row >= colm).astype(BF16)
        bias = b_ref[...]

        def step(t, carry):
            tail, tot = carry
            n = nb - 1 - t
            r = _tri_dot(dcq_ref[n] + dck_ref[n], tri) + tail
            df = r * _sigmoid(-(f_ref[n] + bias))
            df_ref[n] = df
            tot = tot + jnp.broadcast_to(jnp.sum(df, axis=1, keepdims=True), (R, LANES))
            return jnp.broadcast_to(r[:, 0:1], (R, LANES)), tot

        _, tot = lax.fori_loop(0, nb, step, (jnp.zeros((R, LANES), F32), jnp.zeros((R, LANES), F32)))
        db_ref[...] = tot

    return pl.pallas_call(
        body, name="forget_bwd",
        out_shape=[jax.ShapeDtypeStruct(f3.shape, F32), jax.ShapeDtypeStruct((R, LANES), F32)],
        compiler_params=_params(),
    )(dcq3, dck3, f3, bf)


def _causal(t):
    return lax.broadcasted_iota(jnp.int32, (t, t), 0) >= lax.broadcasted_iota(jnp.int32, (t, t), 1)


def _fox_fwd(qkv, ck, h, *, z_off, nhb, S, tq):
    nq = S // tq
    ts = td = min(FOX_ROWS, tq)
    n_sub = n_diag = tq // ts

    def body(q_ref, k_ref, v_ref, ck_ref, z_ref, o_ref, lq_ref, g_ref):
        i = pl.program_id(1)
        qs = [q_ref[r * ts:(r + 1) * ts, :] for r in range(n_sub)]

        def absorb(q, k, v, ckj, m, l, acc, row0=None):
            s = _dot(q, k, _NT) - ckj
            if row0 is not None:
                rows = row0 + lax.broadcasted_iota(jnp.int32, s.shape, 0)
                s = jnp.where(rows >= lax.broadcasted_iota(jnp.int32, s.shape, 1), s, NEG)
            m_new = jnp.maximum(m, jnp.max(s, axis=1, keepdims=True))
            alpha = jnp.exp(m - m_new)
            p = jnp.exp(s - m_new)
            return m_new, alpha * l + jnp.sum(p, axis=1, keepdims=True), alpha * acc + _dot(p.astype(BF16), v, _NN)

        def step(j, carry):
            off = pl.multiple_of(j * tq, tq)
            k, v, ckj = k_ref[pl.ds(off, tq), :], v_ref[pl.ds(off, tq), :], ck_ref[j]
            return tuple(absorb(qs[r], k, v, ckj, *carry[r]) for r in range(n_sub))

        init = tuple((jnp.full((ts, 1), NEG, F32), jnp.zeros((ts, 1), F32), jnp.zeros((ts, LANES), F32))
                     for _ in range(n_sub))
        carry = lax.fori_loop(0, i, step, init)

        off = pl.multiple_of(i * tq, tq)
        k, v, cki = k_ref[pl.ds(off, tq), :], v_ref[pl.ds(off, tq), :], ck_ref[i]
        for f in range(n_diag):
            rows = slice(f * td, (f + 1) * td)
            part = slice(f * td % ts, f * td % ts + td)
            m, l, acc = (t[part] for t in carry[f * td // ts])
            nk = (f + 1) * td
            m, l, acc = absorb(q_ref[rows, :], k[:nk], v[:nk], cki[:, :nk], m, l, acc, row0=f * td)
            out = acc / l
            z = z_ref[rows, :]
            o_ref[rows, :] = out
            lq_ref[rows, :] = jnp.broadcast_to(m + jnp.log(l), (td, LANES))
            g_ref[rows, :] = (out * (z * _sigmoid(z))).astype(BF16)

    blk = pl.BlockSpec((tq, LANES), lambda hd, i: (i, hd))
    return pl.pallas_call(
        body, name="fox_fwd", grid=(nhb, nq),
        in_specs=[blk,
                  pl.BlockSpec((S, LANES), lambda hd, i: (0, nhb + hd)),
                  pl.BlockSpec((S, LANES), lambda hd, i: (0, 2 * nhb + hd)),
                  pl.BlockSpec((None, nq, 1, tq), lambda hd, i: (hd, 0, 0, 0)),
                  pl.BlockSpec((tq, LANES), lambda hd, i: (i, z_off // LANES + hd))],
        out_specs=[blk, blk, blk],
        out_shape=[jax.ShapeDtypeStruct((S, nhb * LANES), F32), jax.ShapeDtypeStruct((S, nhb * LANES), F32),
                   jax.ShapeDtypeStruct((S, nhb * LANES), BF16)],
        compiler_params=_params(("parallel", "arbitrary")),
    )(qkv, qkv, qkv, ck, h)


def _fox_bwd(qkv, do, stats, ck, *, nhb, S, tq, side=None):
    nq = S // tq
    ts, td = min(FOX_ROWS, tq), tq // FOX_DIAG_SUB
    n_side = 0 if side is None else len(side.ins)

    def body(*refs):
        q_ref, k_ref, v_ref, do_ref, st_ref, ck_ref = refs[:6]
        side_in = refs[6:6 + n_side]
        dq_ref, dk_ref, dv_ref, dr_ref, dc_ref = refs[6 + n_side:11 + n_side]
        side_out, sems = refs[11 + n_side:11 + 2 * n_side], refs[11 + 2 * n_side:]
        j = pl.program_id(1)
        if side is not None:
            @pl.when((pl.program_id(0) == 0) & (j == 0))
            def _():
                side.start(side_in, side_out, *sems)

        @pl.when(j == 0)
        def _():
            dq_ref[...] = jnp.zeros_like(dq_ref)
            dr_ref[...] = jnp.zeros_like(dr_ref)

        k = k_ref[...]
        v = v_ref[...]
        ckv = ck_ref[...]

        def group(start, n, nk, carry, row0=None):
            dk, dv, dc = carry
            rows = pl.ds(pl.multiple_of(start, n), n)
            q, g, st = q_ref[rows, :], do_ref[rows, :], st_ref[rows, :]
            p = jnp.exp(_dot(q, k[:nk], _NT) - ckv[:, :nk] - st[:, 0:1])
            if row0 is not None:
                rr = row0 + lax.broadcasted_iota(jnp.int32, (n, nk), 0)
                p = jnp.where(rr >= lax.broadcasted_iota(jnp.int32, (n, nk), 1), p, 0.0)
            ds = p * (_dot(g, v[:nk], _NT) - st[:, 1:2])
            dr_ref[rows, :] += jnp.broadcast_to(jnp.sum(ds, axis=1, keepdims=True), (n, LANES))
            dsb = ds.astype(BF16)
            dq_ref[rows, :] += _dot(dsb, k[:nk], _NN)
            upd = (_dot(dsb, q, _TN), _dot(p.astype(BF16), g, _TN), -jnp.sum(ds, axis=0, keepdims=True))
            if nk < tq:
                upd = (jnp.concatenate([upd[0], jnp.zeros((tq - nk, LANES), F32)], axis=0),
                       jnp.concatenate([upd[1], jnp.zeros((tq - nk, LANES), F32)], axis=0),
                       jnp.concatenate([upd[2], jnp.zeros((1, tq - nk), F32)], axis=1))
            return dk + upd[0], dv + upd[1], dc + upd[2]

        def step(i, carry):
            for r in range(tq // ts):
                carry = group(i * tq + r * ts, ts, tq, carry)
            return carry

        carry = (jnp.zeros((tq, LANES), F32), jnp.zeros((tq, LANES), F32), jnp.zeros((1, tq), F32))
        for f in range(FOX_DIAG_SUB):
            carry = group(j * tq + f * td, td, (f + 1) * td, carry, row0=f * td)
        dk, dv, dc = lax.fori_loop(j + 1, nq, step, carry)
        dk_ref[...] = dk.astype(BF16)
        dv_ref[...] = dv.astype(BF16)
        dc_ref[...] = dc

        @pl.when(j == nq - 1)
        def _():
            dq_ref[...] = dq_ref[...] * ATT_SCALE

        if side is not None:
            @pl.when((pl.program_id(0) == nhb - 1) & (j == nq - 1))
            def _():
                side.middle(side_in, side_out, *sems)
                side.finish(side_in, side_out, *sems)

    def whole(off):
        return pl.BlockSpec((S, LANES), lambda h, j: (0, off + h))

    kv_blk = pl.BlockSpec((tq, LANES), lambda h, j: (j, h))
    c_blk = pl.BlockSpec((None, None, 1, tq), lambda h, j: (h, j, 0, 0))
    in_specs = [whole(0),
                pl.BlockSpec((tq, LANES), lambda h, j: (j, nhb + h)),
                pl.BlockSpec((tq, LANES), lambda h, j: (j, 2 * nhb + h)),
                whole(0), whole(0), c_blk]
    out_specs = [whole(0), kv_blk, kv_blk, whole(0), c_blk]
    out_shape = [jax.ShapeDtypeStruct((S, nhb * LANES), F32),
                 jax.ShapeDtypeStruct((S, nhb * LANES), BF16),
                 jax.ShapeDtypeStruct((S, nhb * LANES), BF16),
                 jax.ShapeDtypeStruct((S, nhb * LANES), F32),
                 jax.ShapeDtypeStruct((nhb, nq, 1, tq), F32)]
    args = (qkv, qkv, qkv, do, stats, ck)
    if side is None:
        return pl.pallas_call(
            body, name="fox_bwd", grid=(nhb, nq), in_specs=in_specs, out_specs=out_specs, out_shape=out_shape,
            compiler_params=_params(("parallel", "arbitrary")),
        )(*args)
    res = pl.pallas_call(
        body, name="fox_bwd_xchg", grid=(nhb, nq), in_specs=in_specs + [_ANY] * n_side,
        out_specs=out_specs + [_ANY] * n_side, out_shape=out_shape + side.outs,
        scratch_shapes=_side_scratch(side), compiler_params=_params(("arbitrary", "arbitrary")),
    )(*args, *side.ins)
    return (*res[:5], res[5:])


def _up_merge(ga, gb, w_a, w_b, h, b_gate, *, g_off, D, S):
    tr = _tile(S, 256, 8)
    W = ga.shape[1]
    assert g_off % D == 0

    def body(ga_ref, gb_ref, wa_ref, wb_ref, la_ref, lb_ref, b_ref, ua_ref, ub_ref, mix_ref):
        ua = _dot(ga_ref[...], wa_ref[...], _NT)
        ub = _dot(gb_ref[...], wb_ref[...], _NT)
        ua_ref[...] = ua
        ub_ref[...] = ub
        mix_ref[...] = (_sigmoid(la_ref[...] + b_ref[:, :D]) * ua
                        + _sigmoid(lb_ref[...] + b_ref[:, D:]) * ub).astype(BF16)

    act = pl.BlockSpec((tr, W), lambda i: (i, 0))
    rows = pl.BlockSpec((tr, D), lambda i: (i, 0))
    whole = pl.BlockSpec((D, W), lambda i: (0, 0), pipeline_mode=pl.Buffered(1))
    return pl.pallas_call(
        body, name="up_merge", grid=(S // tr,),
        in_specs=[act, act, whole, whole,
                  pl.BlockSpec((tr, D), lambda i: (i, g_off // D)), pl.BlockSpec((tr, D), lambda i: (i, g_off // D + 1)),
                  pl.BlockSpec((1, 2 * D), lambda i: (0, 0))],
        out_specs=[rows, rows, rows],
        out_shape=[jax.ShapeDtypeStruct((S, D), F32), jax.ShapeDtypeStruct((S, D), F32),
                   jax.ShapeDtypeStruct((S, D), BF16)],
        compiler_params=_params(("parallel",)),
    )(ga, gb, w_a, w_b, h, h, b_gate)


def _acc_rows(i, ref, val):
    @pl.when(i == 0)
    def _():
        ref[...] = val

    @pl.when(i > 0)
    def _():
        ref[...] += val


def _deepnorm_merge_bwd(dxn, z, g, w, up_a, up_b, h, b_gate, *, g_off, D, S):
    tr = _tile(S, 256, 8)
    assert g_off % D == 0

    def body(dx_ref, z_ref, g_ref, w_ref, ua_ref, ub_ref, la_ref, lb_ref, b_ref,
             dz_ref, dzb_ref, dua_ref, dub_ref, dla_ref, dlb_ref, sg_ref, sb_ref, sa_ref, sbb_ref):
        i = pl.program_id(0)
        dx, zv = dx_ref[...], z_ref[...]
        mu = jnp.mean(zv, axis=1, keepdims=True)
        zc = zv - mu
        rstd = lax.rsqrt(jnp.mean(zc * zc, axis=1, keepdims=True) + LN_EPS)
        u = zc * rstd
        du = dx * g_ref[...]
        dz = rstd * (du - jnp.mean(du, axis=1, keepdims=True) - u * jnp.mean(du * u, axis=1, keepdims=True))
        dzb = dz.astype(BF16)
        dz_ref[...] = dz
        dzb_ref[...] = dzb
        _acc_rows(i, sg_ref, jnp.sum(dx * u, axis=0, keepdims=True))
        _acc_rows(i, sb_ref, jnp.sum(dx, axis=0, keepdims=True))
        dm = _dot(dzb, w_ref[...], _NT)
        ga, gb = _sigmoid(la_ref[...] + b_ref[:, :D]), _sigmoid(lb_ref[...] + b_ref[:, D:])
        dua_ref[...] = (ga * dm).astype(BF16)
        dub_ref[...] = (gb * dm).astype(BF16)
        dla = ua_ref[...] * dm * (ga * (1.0 - ga))
        dlb = ub_ref[...] * dm * (gb * (1.0 - gb))
        dla_ref[...] = dla.astype(BF16)
        dlb_ref[...] = dlb.astype(BF16)
        _acc_rows(i, sa_ref, jnp.sum(dla, axis=0, keepdims=True))
        _acc_rows(i, sbb_ref, jnp.sum(dlb, axis=0, keepdims=True))

    rows = pl.BlockSpec((tr, D), lambda i: (i, 0))
    vec = pl.BlockSpec((1, D), lambda i: (0, 0))
    return pl.pallas_call(
        body, name="deepnorm_merge_bwd", grid=(S // tr,),
        in_specs=[rows, rows, vec, pl.BlockSpec((D, D), lambda i: (0, 0), pipeline_mode=pl.Buffered(1)), rows, rows,
                  pl.BlockSpec((tr, D), lambda i: (i, g_off // D)), pl.BlockSpec((tr, D), lambda i: (i, g_off // D + 1)),
                  pl.BlockSpec((1, 2 * D), lambda i: (0, 0))],
        out_specs=[rows] * 6 + [vec] * 4,
        out_shape=[jax.ShapeDtypeStruct((S, D), F32)] + [jax.ShapeDtypeStruct((S, D), BF16)] * 5
        + [jax.ShapeDtypeStruct((1, D), F32)] * 4,
        compiler_params=_params(("arbitrary",)),
    )(dxn, z, g, w, up_a, up_b, h, h, b_gate)


def _out_proj_deepnorm(mix, w, x, g, b, *, alpha, D, S):
    tr = _tile(S, 512, 8)

    def body(m_ref, w_ref, x_ref, g_ref, b_ref, xn_ref, xb_ref, z_ref):
        z = alpha * x_ref[...] + _dot(m_ref[...], w_ref[...], _NN)
        mu = jnp.mean(z, axis=1, keepdims=True)
        zc = z - mu
        var = jnp.mean(zc * zc, axis=1, keepdims=True)
        xn = zc * lax.rsqrt(var + LN_EPS) * g_ref[...] + b_ref[...]
        xn_ref[...] = xn
        xb_ref[...] = xn.astype(BF16)
        z_ref[...] = z

    rows = pl.BlockSpec((tr, D), lambda i: (i, 0))
    vec = pl.BlockSpec((1, D), lambda i: (0, 0))
    return pl.pallas_call(
        body, name="out_proj_deepnorm", grid=(S // tr,),
        in_specs=[rows, pl.BlockSpec((D, D), lambda i: (0, 0)), rows, vec, vec],
        out_specs=[rows, rows, rows],
        out_shape=[jax.ShapeDtypeStruct((S, D), F32), jax.ShapeDtypeStruct((S, D), BF16),
                   jax.ShapeDtypeStruct((S, D), F32)],
        compiler_params=_params(("parallel",)),
    )(mix, w, x, g, b)


def _loss_head(y, target, *, D, S):
    tr = _tile(S, 256, 8)

    def body(y_ref, t_ref, d_ref, l_ref):
        err = y_ref[...] - t_ref[...]
        d_ref[...] = err * (1.0 / D)
        part = 0.5 * jnp.sum(jnp.sum(err * err, axis=1, keepdims=True) * (1.0 / D), axis=0, keepdims=True)
        _acc_rows(pl.program_id(0), l_ref, jnp.broadcast_to(part, (1, LANES)))

    return pl.pallas_call(
        body, name="loss_head", grid=(S // tr,),
        in_specs=[pl.BlockSpec((tr, D), lambda i: (i, 0))] * 2,
        out_specs=[pl.BlockSpec((tr, D), lambda i: (i, 0)), pl.BlockSpec((1, LANES), lambda i: (0, 0))],
        out_shape=[jax.ShapeDtypeStruct((S, D), F32), jax.ShapeDtypeStruct((1, LANES), F32)],
        compiler_params=_params(("arbitrary",)),
    )(y, target)


_MESH = pl.DeviceIdType.MESH
_ANY = pl.BlockSpec(memory_space=pl.ANY)


class _Gather:
    def __init__(self, xs):
        self.ins = list(xs)
        self.outs = [jax.ShapeDtypeStruct((NDEV,) + x.shape, x.dtype) for x in xs]

    def _ctx(self, a, x_ref, out_ref, send_sems, recv_sems):
        x, y, c = lax.axis_index("x"), lax.axis_index("y"), lax.axis_index("c")
        me, sibling = (x, y, c), (x, y, 1 - c)
        chips = [(1 - x, y), (x, 1 - y), (1 - x, 1 - y)]

        def slot(px, py, pc):
            return out_ref.at[4 * px + 2 * py + pc]

        def copy(k, block, to, src=None):
            return pltpu.make_async_remote_copy(
                src_ref=slot(*block) if src is None else src, dst_ref=slot(*block),
                send_sem=send_sems.at[7 * a + k], recv_sem=recv_sems.at[7 * a + k], device_id=to, device_id_type=_MESH)

        def first():
            return [copy(0, me, sibling, src=x_ref)] + [copy(1 + j, me, (*chip, c), src=x_ref)
                                                        for j, chip in enumerate(chips)]

        def passed():
            return [copy(4 + j, (*chip, c), sibling) for j, chip in enumerate(chips)]

        return c, me, sibling, chips, slot, copy, first, passed

    def start(self, in_refs, out_refs, send_sems, recv_sems, local_sems):
        for a, (x_ref, out_ref) in enumerate(zip(in_refs, out_refs)):
            _, me, _, _, slot, _, first, _ = self._ctx(a, x_ref, out_ref, send_sems, recv_sems)
            pltpu.make_async_copy(x_ref, slot(*me), local_sems.at[a]).start()
            for cp in first():
                cp.start()

    def middle(self, in_refs, out_refs, send_sems, recv_sems, local_sems):
        for a, (x_ref, out_ref) in enumerate(zip(in_refs, out_refs)):
            c, me, _, chips, _, copy, _, passed = self._ctx(a, x_ref, out_ref, send_sems, recv_sems)
            for j, (chip, cp) in enumerate(zip(chips, passed())):
                copy(1 + j, (*chip, c), me).wait_recv()
                cp.start()

    def finish(self, in_refs, out_refs, send_sems, recv_sems, local_sems):
        for a, (x_ref, out_ref) in enumerate(zip(in_refs, out_refs)):
            c, me, sibling, chips, slot, copy, first, passed = self._ctx(a, x_ref, out_ref, send_sems, recv_sems)
            copy(0, sibling, me).wait_recv()
            for j, chip in enumerate(chips):
                copy(4 + j, (*chip, 1 - c), me).wait_recv()
            for cp in first() + passed():
                cp.wait_send()
            pltpu.make_async_copy(x_ref, slot(*me), local_sems.at[a]).wait()


class _Exchange:
    def __init__(self, gs):
        self.ins = list(gs)
        self.outs = [jax.ShapeDtypeStruct(g.shape, g.dtype) for g in gs]

    def _copies(self, a, g_ref, out_ref, send_sems, recv_sems, arrivals=True):
        x, y, c = lax.axis_index("x"), lax.axis_index("y"), lax.axis_index("c")
        me = 4 * x + 2 * y + c
        out, back = [], []
        for k in range(1, NDEV):
            px, py, pc = x ^ ((k >> 2) & 1), y ^ ((k >> 1) & 1), c ^ (k & 1)
            peer = 4 * px + 2 * py + pc
            sems = dict(send_sem=send_sems.at[7 * a + k - 1], recv_sem=recv_sems.at[7 * a + k - 1],
                        device_id=(px, py, pc), device_id_type=_MESH)
            out.append(pltpu.make_async_remote_copy(src_ref=g_ref.at[peer], dst_ref=out_ref.at[me], **sems))
            if arrivals:
                back.append(pltpu.make_async_remote_copy(src_ref=g_ref.at[me], dst_ref=out_ref.at[peer], **sems))
        return me, out, back

    def start(self, in_refs, out_refs, send_sems, recv_sems, local_sems):
        for a, (g_ref, out_ref) in enumerate(zip(in_refs, out_refs)):
            me, out, _ = self._copies(a, g_ref, out_ref, send_sems, recv_sems, arrivals=False)
            pltpu.make_async_copy(g_ref.at[me], out_ref.at[me], local_sems.at[a]).start()
            for cp in out:
                cp.start()

    def middle(self, in_refs, out_refs, send_sems, recv_sems, local_sems):
        pass

    def finish(self, in_refs, out_refs, send_sems, recv_sems, local_sems):
        for a, (g_ref, out_ref) in enumerate(zip(in_refs, out_refs)):
            me, out, back = self._copies(a, g_ref, out_ref, send_sems, recv_sems)
            for cp in back:
                cp.wait_recv()
            for cp in out:
                cp.wait_send()
            pltpu.make_async_copy(g_ref.at[me], out_ref.at[me], local_sems.at[a]).wait()


def _side_scratch(side):
    n = len(side.ins)
    return [pltpu.SemaphoreType.DMA((7 * n,)), pltpu.SemaphoreType.DMA((7 * n,)), pltpu.SemaphoreType.DMA((n,))]


def _run_side(side, *, name):
    n = len(side.ins)

    def body(*refs):
        ins, outs, sems = refs[:n], refs[n:2 * n], refs[2 * n:]
        side.start(ins, outs, *sems)
        side.middle(ins, outs, *sems)
        side.finish(ins, outs, *sems)

    return pl.pallas_call(
        body, name=name, out_shape=side.outs, in_specs=[_ANY] * n, out_specs=[_ANY] * n,
        scratch_shapes=_side_scratch(side),
    )(*side.ins)


def _all_gather(x, *, name):
    return _run_side(_Gather([x]), name=name)[0]


def _sum_slots(parts, *, name):
    n, R, C = parts.shape
    tr, tc = _tile(R, 64, 16), _tile(C, 2048)

    def body(p_ref, o_ref):
        acc = p_ref[0].astype(F32)
        for k in range(1, n):
            acc = acc + p_ref[k].astype(F32)
        o_ref[...] = acc

    return pl.pallas_call(
        body, name=name, grid=(R // tr, C // tc),
        in_specs=[pl.BlockSpec((n, tr, tc), lambda i, j: (0, i, j))],
        out_specs=pl.BlockSpec((tr, tc), lambda i, j: (i, j)),
        out_shape=jax.ShapeDtypeStruct((R, C), F32),
        compiler_params=_params(("parallel", "parallel")),
    )(parts)


def _adamw(w, g, m, v, *, name):
    R, C = w.shape
    budget = 2 << 20
    tr = R if R * C * 4 <= budget else _tile(R, max(8, (budget // (C * 4)) // 8 * 8), 8)

    def body(w_ref, g_ref, m_ref, v_ref, d_ref, nm_ref, nv_ref):
        gv = g_ref[...]
        nm = ADAM_B1 * m_ref[...] + (1.0 - ADAM_B1) * gv
        nv = ADAM_B2 * v_ref[...] + (1.0 - ADAM_B2) * (gv * gv)
        m_hat = nm / (1.0 - ADAM_B1 ** ADAM_STEP)
        v_hat = nv / (1.0 - ADAM_B2 ** ADAM_STEP)
        d_ref[...] = -ADAM_LR * (m_hat / (jnp.sqrt(v_hat) + ADAM_EPS) + ADAM_WD * w_ref[...])
        nm_ref[...] = nm
        nv_ref[...] = nv

    spec = pl.BlockSpec((tr, C), lambda i: (i, 0))
    return pl.pallas_call(
        body, name=name, grid=(R // tr,), in_specs=[spec] * 4, out_specs=[spec] * 3,
        out_shape=[jax.ShapeDtypeStruct((R, C), F32)] * 3,
        compiler_params=_params(("parallel",)),
    )(w, g, m, v)


def _pad_cols(w, *, nq, ng):
    nf = w.shape[-1] - nq - ng
    pad = jnp.zeros(w.shape[:-1] + (FORGET_PAD - nf,), w.dtype)
    return jnp.concatenate([w[..., :nq], w[..., nq + nf:], w[..., nq:nq + nf], pad], axis=-1)


def _unpad_cols(w, *, nq, ng, nf):
    return jnp.concatenate([w[..., :nq], w[..., nq + ng:nq + ng + nf], w[..., nq:nq + ng]], axis=-1)


def kernel(x, w_in, b_forget, b_gate, w_up_a, w_up_b, w_out, ln_g, ln_b, loss_target, m_w_in, m_b_forget, m_b_gate, m_w_up_a, m_w_up_b, m_w_out, m_ln_g, m_ln_b, v_w_in, v_b_forget, v_b_gate, v_w_up_a, v_w_up_b, v_w_out, v_ln_g, v_ln_b):
    depth = w_in.shape[0]
    _, S, D = x.shape
    WA, WB = w_up_a.shape[1], w_up_b.shape[1]
    nha, nhb = WA // HEAD_DIM, WB // HEAD_DIM
    assert nhb == b_forget.shape[1] and WA == WB and nhb <= HEAD_ROWS
    nq, ng, nf = 4 * WA + 4 * WB, 2 * D, nhb
    assert w_in.shape[2] == nq + nf + ng
    ncp = nq + ng + FORGET_PAD
    off_va, off_za, off_b, off_zb, off_g, off_f = 2 * WA, 3 * WA, 4 * WA, 4 * WA + 3 * WB, nq, nq + ng
    alpha = float((2 * depth) ** 0.25)
    dsh = D // NDEV
    tq_f, tq_b = _tile(S, FOX_TILE_FWD), _tile(S, FOX_TILE_BWD)

    x0 = x[0]
    target = loss_target[0]
    cos, sin = _rope_tables(S)

    w_in_p = _pad_cols(w_in, nq=nq, ng=ng).astype(BF16)
    w_up_a_t = jnp.swapaxes(w_up_a, 1, 2).astype(BF16)
    w_up_b_t = jnp.swapaxes(w_up_b, 1, 2).astype(BF16)
    w_out_c = w_out.astype(BF16)
    W_in, W_ua, W_ub, W_out = ([None] * depth for _ in range(4))

    def others(l):
        return [w_up_a_t[l], w_up_b_t[l], w_out_c[l]]

    def keep_others(l, got):
        W_ua[l], W_ub[l], W_out[l] = got[0].reshape(D, WA), got[1].reshape(D, WB), got[2].reshape(D, D)

    W_in[0] = _run_side(_Gather([w_in_p[0]]), name="gather_w_in_0")[0].reshape(D, ncp)

    bf_rows = [jnp.broadcast_to(jnp.pad(b_forget[l], (0, HEAD_ROWS - nhb))[:, None], (HEAD_ROWS, LANES))
               for l in range(depth)]

    saved = []
    xf, xb = x0, x0.astype(BF16)
    for l in range(depth):
        wanted = (others(0) if l == 0 else []) + ([w_in_p[l + 1]] + others(l + 1) if l + 1 < depth else [])
        if wanted:
            h, got = _mm(xb, W_in[l], name="in_proj_gather", tm=1024, tn=1280, tk=2048, side=_Gather(wanted))
            if l == 0:
                keep_others(0, got[:3])
            if l + 1 < depth:
                W_in[l + 1] = got[-4].reshape(D, ncp)
                keep_others(l + 1, got[-3:])
        else:
            h = _mm(xb, W_in[l], name="in_proj", tm=1024, tn=1280, tk=2048)
        qk_a = _prep_a(h, cos, sin, W=WA, S=S)
        qkv_b = _prep_b(h, off=off_b, W=WB, S=S)
        out_a, lse_a, ga = _band_fwd(qk_a, h, nha=nha, S=S, v_off=off_va, z_off=off_za)
        f3 = _lane_blocks(jnp.pad(h[:, off_f:off_f + nhb].T, ((0, HEAD_ROWS - nhb), (0, 0))))
        c = _lane_unblocks(_forget_cumsum(f3, bf_rows[l]))[:nhb]
        out_b, lq_b, gb = _fox_fwd(qkv_b, c.reshape(nhb, S // tq_f, 1, tq_f), h, z_off=off_zb, nhb=nhb, S=S, tq=tq_f)
        ck = c.reshape(nhb, S // tq_b, 1, tq_b)
        bg = b_gate[l][None, :]
        up_a, up_b, mix = _up_merge(ga, gb, W_ua[l], W_ub[l], h, bg, g_off=off_g, D=D, S=S)
        xn_f, xn_b, z = _out_proj_deepnorm(mix, W_out[l], xf, ln_g[l][None, :], ln_b[l][None, :],
                                           alpha=alpha, D=D, S=S)
        saved.append(dict(xb=xb, h=h, qk_a=qk_a, qkv_b=qkv_b, out_a=out_a, lse_a=lse_a, ga=ga, f3=f3, ck=ck,
                          out_b=out_b, lq_b=lq_b, gb=gb, up_a=up_a, up_b=up_b, mix=mix, z=z, bg=bg))
        xf, xb = xn_f, xn_b

    dx, loss_row = _loss_head(xf, target, D=D, S=S)

    recv = [None] * depth
    small = [None] * depth
    pending = None
    for l in reversed(range(depth)):
        sv = saved[l]
        h = sv["h"]
        dz, dzb, dup_a, dup_b, dgl_a, dgl_b, d_lng, d_lnb, d_bga, d_bgb = _deepnorm_merge_bwd(
            dx, sv["z"], ln_g[l][None, :], W_out[l], sv["up_a"], sv["up_b"], h, sv["bg"], g_off=off_g, D=D, S=S)
        g_w_out = _mm(sv["mix"], dzb, ta=True, name="g_w_out", tm=512, tn=1024, tk=4096, out_dtype=BF16)
        g_w_ua = _mm(dup_a, sv["ga"], ta=True, name="g_w_up_a", tm=512, tn=1536, tk=4096, out_dtype=BF16)
        g_w_ub = _mm(dup_b, sv["gb"], ta=True, name="g_w_up_b", tm=512, tn=1536, tk=4096, out_dtype=BF16)
        do_a, dza, stats_a = _gate_bwd(dup_a, W_ua[l], sv["out_a"], h, sv["lse_a"], z_off=off_za, W=WA, S=S,
                                       name="gate_bwd_a", do_dtype=F32)
        do_b, dzb_, stats_b = _gate_bwd(dup_b, W_ub[l], sv["out_b"], h, sv["lq_b"], z_off=off_zb, W=WB, S=S,
                                        name="gate_bwd_b", do_dtype=BF16)
        dqa, dka, dva = _rope_bwd(*_band_bwd(sv["qk_a"], h, do_a, stats_a, nha=nha, S=S, v_off=off_va),
                                  cos, sin, W=WA, S=S)
        if pending is None:
            dqb, dkb, dvb, dcq, dck = _fox_bwd(sv["qkv_b"], do_b, stats_b, sv["ck"], nhb=nhb, S=S, tq=tq_b)
        else:
            dqb, dkb, dvb, dcq, dck, got = _fox_bwd(sv["qkv_b"], do_b, stats_b, sv["ck"], nhb=nhb, S=S, tq=tq_b,
                                                    side=_Exchange([pending]))
            recv[l + 1] = (got[0], *recv[l + 1])
        head_rows = lambda t: _lane_blocks(jnp.pad(t, ((0, HEAD_ROWS - nhb), (0, 0))))
        df3, d_bf = _forget_bwd(head_rows(dcq[:, ::LANES].T), head_rows(dck.reshape(nhb, S)), sv["f3"], bf_rows[l])
        dfl = jnp.pad(_lane_unblocks(df3)[:nhb].T, ((0, 0), (0, FORGET_PAD - nhb))).astype(BF16)
        dh = jnp.concatenate([dqa, dka, dva, dza, dqb.astype(BF16), dkb, dvb, dzb_, dgl_a, dgl_b, dfl], axis=1)
        chunks = lambda t: t.reshape(NDEV, dsh, t.shape[1])
        g_w_in, got3 = _mm(sv["xb"], dh, ta=True, name="g_w_in_xchg", tm=512, tn=1280, tk=4096, out_dtype=BF16,
                           side=_Exchange([chunks(g_w_ua), chunks(g_w_ub), chunks(g_w_out)]))
        if l > 0:
            dx = _mm(dh, W_in[l], tb=True, name="d_x", tm=1024, tn=512, tk=3328, add=dz, add_scale=alpha)
            pending, recv[l] = chunks(g_w_in), tuple(got3)
        else:
            dx, got1 = _mm(dh, W_in[l], tb=True, name="d_x_xchg", tm=1024, tn=512, tk=3328, add=dz, add_scale=alpha,
                           side=_Exchange([chunks(g_w_in)]))
            recv[l] = (got1[0], *got3)
        small[l] = jnp.concatenate([d_bf[:nhb, 0], jnp.zeros((LANES - nhb,), F32), d_bga[0], d_bgb[0],
                                    d_lng[0], d_lnb[0]])

    n_small = LANES + 2 * D + 2 * D
    flat = jnp.concatenate(small + [loss_row[0]])
    rows = -(-flat.shape[0] // (8 * LANES)) * 8
    flat = jnp.pad(flat, (0, rows * LANES - flat.shape[0])).reshape(rows, LANES)
    tot = _sum_slots(_all_gather(flat, name="gather_small"), name="sum_small").reshape(-1)
    loss = tot[depth * n_small]
    sm = tot[:depth * n_small].reshape(depth, n_small)
    g_bf, g_bg = sm[:, :nhb], sm[:, LANES:LANES + 2 * D]
    g_lg, g_lb = sm[:, LANES + 2 * D:LANES + 3 * D], sm[:, LANES + 3 * D:]

    g_in = jnp.stack([_unpad_cols(_sum_slots(recv[l][0], name="sum_w_in"), nq=nq, ng=ng, nf=nf) for l in range(depth)])
    g_ua = jnp.stack([_sum_slots(recv[l][1], name="sum_w_up_a").T for l in range(depth)])
    g_ub = jnp.stack([_sum_slots(recv[l][2], name="sum_w_up_b").T for l in range(depth)])
    g_out = jnp.stack([_sum_slots(recv[l][3], name="sum_w_out") for l in range(depth)])

    def adam(w, g, m, v, name):
        shp = w.shape
        d_, m_, v_ = _adamw(*[t.reshape(-1, shp[-1]) for t in (w, g, m, v)], name=name)
        return d_.reshape(shp), m_.reshape(shp), v_.reshape(shp)

    grads = [g_in, g_bf, g_bg, g_ua, g_ub, g_out, g_lg, g_lb]
    ws = [w_in, b_forget, b_gate, w_up_a, w_up_b, w_out, ln_g, ln_b]
    ms = [m_w_in, m_b_forget, m_b_gate, m_w_up_a, m_w_up_b, m_w_out, m_ln_g, m_ln_b]
    vs = [v_w_in, v_b_forget, v_b_gate, v_w_up_a, v_w_up_b, v_w_out, v_ln_g, v_ln_b]
    names = ["w_in", "b_forget", "b_gate", "w_up_a", "w_up_b", "w_out", "ln_g", "ln_b"]
    upd = [adam(w, g, m, v, f"adamw_{n}") for w, g, m, v, n in zip(ws, grads, ms, vs, names)]
    return (loss, dx[None], *grads, *[u[0] for u in upd], *[u[1] for u in upd], *[u[2] for u in upd])
```

```python
import jax
import jax.numpy as jnp
from jax import lax
from jax.experimental import pallas as pl
from jax.experimental.pallas import tpu as pltpu

F32 = jnp.float32
BF16 = jnp.bfloat16

NDEV = 8
HEAD_DIM = 128
LANES = 128
Q_BLOCK = 128
DILATIONS = (1, 4, 16)
ROPE_THETA = 10000.0
LN_EPS = 1e-5
ATT_SCALE = HEAD_DIM ** -0.5
NEG = -1e30
FORGET_PAD = 256
BAND_CHUNK = 2048
BAND_UNROLL = 16
FOX_TILE_FWD = 2048
FOX_TILE_BWD = 1024
FOX_ROWS = 512
GATE_BWD_ROWS = 512
FOX_DIAG_SUB = 4
HEAD_ROWS = 16
ADAM_LR, ADAM_B1, ADAM_B2, ADAM_EPS, ADAM_WD, ADAM_STEP = 0.001, 0.9, 0.999, 1e-08, 0.01, 10
VMEM_LIMIT = 56 * 1024 * 1024


def _tile(n, pref, mult=LANES):
    if n <= pref:
        return n
    t = (pref // mult) * mult
    while t >= mult:
        if n % t == 0:
            return t
        t -= mult
    return n


def _params(sem=None):
    return pltpu.CompilerParams(dimension_semantics=sem, vmem_limit_bytes=VMEM_LIMIT)


def _sigmoid(z):
    return 1.0 / (1.0 + jnp.exp(-z))


_NT = (((1,), (1,)), ((), ()))
_NN = (((1,), (0,)), ((), ()))
_TN = (((0,), (0,)), ((), ()))


def _dot(a, b, dims):
    return lax.dot_general(a, b, dims, preferred_element_type=F32)


def _mm(a, b, *, name, ta=False, tb=False, out_dtype=F32, tm=512, tn=512, tk=512, add=None, add_scale=1.0,
        side=None):
    if ta:
        K, M = a.shape
    else:
        M, K = a.shape
    if tb:
        N, K2 = b.shape
    else:
        K2, N = b.shape
    assert K == K2, (a.shape, b.shape, ta, tb)
    tm, tn, tk = _tile(M, tm), _tile(N, tn), _tile(K, tk)
    ni, nj, nk = M // tm, N // tn, K // tk
    dims = (((0 if ta else 1,), (1 if tb else 0,)), ((), ()))
    n_side = 0 if side is None else len(side.ins)
    n_in = 2 + (add is not None)

    def body(*refs):
        a_ref, b_ref = refs[:2]
        add_ref = refs[2] if add is not None else None
        side_in = refs[n_in:n_in + n_side]
        o_ref = refs[n_in + n_side]
        side_out = refs[n_in + n_side + 1:n_in + 2 * n_side + 1]
        acc_ref = refs[n_in + 2 * n_side + 1]
        sems = refs[n_in + 2 * n_side + 2:]
        k = pl.program_id(2)
        if side is not None:
            @pl.when((pl.program_id(0) == 0) & (pl.program_id(1) == 0) & (k == 0))
            def _():
                side.start(side_in, side_out, *sems)

        part = _dot(a_ref[...].astype(BF16), b_ref[...].astype(BF16), dims)

        @pl.when(k == 0)
        def _():
            acc_ref[...] = part

        @pl.when(k > 0)
        def _():
            acc_ref[...] += part

        @pl.when(k == nk - 1)
        def _():
            r = acc_ref[...]
            if add_ref is not None:
                r = r + add_scale * add_ref[...]
            o_ref[...] = r.astype(out_dtype)

        if side is not None:
            step = (pl.program_id(0) * nj + pl.program_id(1)) * nk + k
            steps = ni * nj * nk

            @pl.when(step == max((7 * steps) // 8 - 1, 0))
            def _():
                side.middle(side_in, side_out, *sems)

            @pl.when(step == steps - 1)
            def _():
                side.finish(side_in, side_out, *sems)

    a_spec = pl.BlockSpec((tk, tm), lambda i, j, k: (k, i)) if ta else pl.BlockSpec((tm, tk), lambda i, j, k: (i, k))
    b_spec = pl.BlockSpec((tn, tk), lambda i, j, k: (j, k)) if tb else pl.BlockSpec((tk, tn), lambda i, j, k: (k, j))
    o_spec = pl.BlockSpec((tm, tn), lambda i, j, k: (i, j))
    in_specs, args = [a_spec, b_spec], [a, b]
    if add is not None:
        in_specs.append(o_spec)
        args.append(add)
    out_shape = jax.ShapeDtypeStruct((M, N), out_dtype)
    scratch = [pltpu.VMEM((tm, tn), F32)]
    if side is None:
        return pl.pallas_call(
            body, name=name, grid=(ni, nj, nk), in_specs=in_specs, out_specs=o_spec, out_shape=out_shape,
            scratch_shapes=scratch, compiler_params=_params(("parallel", "parallel", "arbitrary")),
        )(*args)
    res = pl.pallas_call(
        body, name=name, grid=(ni, nj, nk), in_specs=in_specs + [_ANY] * n_side,
        out_specs=[o_spec] + [_ANY] * n_side, out_shape=[out_shape] + side.outs,
        scratch_shapes=scratch + _side_scratch(side),
        compiler_params=_params(("arbitrary", "arbitrary", "arbitrary")),
    )(*args, *side.ins)
    return res[0], res[1:]


def _ew(fn, ins, outs, *, rows, cols, tr, tc, name):
    assert rows % tr == 0 and cols % tc == 0

    def spec(kind, off):
        assert off % tc == 0, (off, tc)
        ob = off // tc
        if kind == "tile":
            return pl.BlockSpec((tr, tc), lambda j, i: (i, j + ob))
        if kind == "tab":
            return pl.BlockSpec((tr, LANES), lambda j, i: (i, 0))
        return pl.BlockSpec((1, tc), lambda j, i: (0, j + ob))

    n_in = len(ins)

    def body(*refs):
        fn(pl.program_id(1), refs[:n_in], refs[n_in:])

    return pl.pallas_call(
        body, name=name, grid=(cols // tc, rows // tr),
        in_specs=[spec(k, o) for _, k, o in ins],
        out_specs=[spec(k, o) for _, _, k, o in outs],
        out_shape=[jax.ShapeDtypeStruct(s, d) for s, d, _, _ in outs],
        compiler_params=_params(("parallel", "arbitrary")),
    )(*[a for a, _, _ in ins])


def _heads(t):
    return [t[:, e:e + LANES] for e in range(0, t.shape[1], LANES)]


def _rope_tables(S):
    half = HEAD_DIM // 2
    inv_freq = ROPE_THETA ** (-jnp.arange(half, dtype=F32) / half)
    ang = jnp.arange(S, dtype=jnp.int32).astype(F32)[:, None] * inv_freq[None, :]
    cos, sin = jnp.cos(ang), jnp.sin(ang)
    return jnp.concatenate([cos, cos], axis=1), jnp.concatenate([-sin, sin], axis=1)


def _rope(t, cs, sn):
    return jnp.concatenate([g * cs + pltpu.roll(g, HEAD_DIM // 2, 1) * sn for g in _heads(t)], axis=1)


def _rope_t(t, cs, sn):
    return jnp.concatenate([g * cs + pltpu.roll(g * sn, HEAD_DIM // 2, 1) for g in _heads(t)], axis=1)


def _prep_a(h, cos, sin, *, W, S):
    tr, tc = _tile(S, 1024, 8), _tile(W, 512)

    def fn(i, ins, outs):
        outs[0][...] = _rope(ins[0][...], ins[1][...], ins[2][...])

    return _ew(fn, [(h, "tile", 0), (cos, "tab", 0), (sin, "tab", 0)], [((S, 2 * W), F32, "tile", 0)],
               rows=S, cols=2 * W, tr=tr, tc=tc, name="prep_a")[0]


def _prep_b(h, *, off, W, S):
    tr, tc = _tile(S, 1024, 8), _tile(W, 512)

    def fn(i, ins, outs):
        scale = jnp.where(pl.program_id(0) * tc < W, ATT_SCALE, 1.0).astype(F32)
        outs[0][...] = (ins[0][...] * scale).astype(BF16)

    return _ew(fn, [(h, "tile", off)], [((S, 3 * W), BF16, "tile", 0)], rows=S, cols=3 * W, tr=tr, tc=tc,
               name="prep_b")[0]


def _rope_bwd(dq, dk, dv, cos, sin, *, W, S):
    tr, tc = _tile(S, 1024, 8), _tile(W, 512)

    def fn(i, ins, outs):
        cs, sn = ins[3][...], ins[4][...]
        outs[0][...] = _rope_t(ins[0][...], cs, sn).astype(BF16)
        outs[1][...] = _rope_t(ins[1][...], cs, sn).astype(BF16)
        outs[2][...] = ins[2][...].astype(BF16)

    ins = [(dq, "tile", 0), (dk, "tile", 0), (dv, "tile", 0), (cos, "tab", 0), (sin, "tab", 0)]
    return _ew(fn, ins, [((S, W), BF16, "tile", 0)] * 3, rows=S, cols=W, tr=tr, tc=tc, name="rope_bwd")


def _band_biases():
    dist = (lax.broadcasted_iota(jnp.int32, (Q_BLOCK, 2 * Q_BLOCK), 0)
            - lax.broadcasted_iota(jnp.int32, (Q_BLOCK, 2 * Q_BLOCK), 1))
    first = jnp.where(dist >= 0, 0.0, NEG).astype(F32)
    other = jnp.where((dist + Q_BLOCK >= 0) & (dist <= 0), 0.0, NEG).astype(F32)
    return first, other


def _band_rows(it, c, d, nbc, biases):
    r, n = (0, it) if d == 1 else (it // nbc, it % nbc)
    ng = c * nbc + n
    k0 = pl.multiple_of(jnp.maximum(ng - 1, 0) * Q_BLOCK, Q_BLOCK)
    bias = jnp.where(ng == 0, biases[0], biases[1])
    if d == 1:
        return pl.ds(pl.multiple_of(it * Q_BLOCK, Q_BLOCK), Q_BLOCK), pl.ds(k0, 2 * Q_BLOCK), bias
    return (pl.ds(r + d * Q_BLOCK * n, Q_BLOCK, stride=d), pl.ds(r + d * k0, 2 * Q_BLOCK, stride=d), bias)


def _band_chunk(S):
    ch = min(S, BAND_CHUNK)
    assert S % ch == 0 and ch % (Q_BLOCK * max(DILATIONS)) == 0 and S >= 2 * Q_BLOCK * max(DILATIONS)
    return ch


def _band_fwd(qk, h, *, nha, S, v_off, z_off):
    CH = _band_chunk(S)
    W = nha * LANES
    slab = 256

    def body(q_ref, k_ref, v_ref, z_ref, o_ref, lse_ref, g_ref, *scratch):
        os_refs, ls_refs = scratch[:len(DILATIONS)], scratch[len(DILATIONS):]
        c = pl.program_id(1)
        biases = _band_biases()
        for gi, d in enumerate(DILATIONS):
            nbc = CH // (Q_BLOCK * d)

            def step(it, carry, gi=gi, d=d, nbc=nbc):
                rq, rk, bias = _band_rows(it, c, d, nbc, biases)
                q = q_ref[rq, :].astype(BF16)
                k = k_ref[rk, :].astype(BF16)
                v = v_ref[rk, :].astype(BF16)
                s = _dot(q, k, _NT) * ATT_SCALE + bias
                m = jnp.max(s, axis=1, keepdims=True)
                p = jnp.exp(s - m)
                den = jnp.sum(p, axis=1, keepdims=True)
                os_refs[gi][rq, :] = _dot(p.astype(BF16), v, _NN) / den
                ls_refs[gi][rq, :] = jnp.broadcast_to(m + jnp.log(den), (Q_BLOCK, LANES))
                return carry

            lax.fori_loop(0, CH // Q_BLOCK, step, 0, unroll=BAND_UNROLL)

        def mix(t, carry):
            rows = pl.ds(pl.multiple_of(t * slab, slab), slab)
            a, b, cc = (r[rows, :] for r in ls_refs)
            m = jnp.maximum(jnp.maximum(a, b), cc)
            ea, eb, ec = jnp.exp(a - m), jnp.exp(b - m), jnp.exp(cc - m)
            den = ea + eb + ec
            out = (ea * os_refs[0][rows, :] + eb * os_refs[1][rows, :] + ec * os_refs[2][rows, :]) / den
            z = z_ref[rows, :]
            o_ref[rows, :] = out
            lse_ref[rows, :] = m + jnp.log(den)
            g_ref[rows, :] = (out * (z * _sigmoid(z))).astype(BF16)
            return carry

        lax.fori_loop(0, CH // slab, mix, 0)

    chunk = pl.BlockSpec((CH, LANES), lambda hd, c: (c, hd))
    return pl.pallas_call(
        body, name="band_fwd", grid=(nha, S // CH),
        in_specs=[chunk,
                  pl.BlockSpec((S, LANES), lambda hd, c: (0, nha + hd)),
                  pl.BlockSpec((S, LANES), lambda hd, c: (0, v_off // LANES + hd)),
                  pl.BlockSpec((CH, LANES), lambda hd, c: (c, z_off // LANES + hd))],
        out_specs=[chunk, chunk, chunk],
        out_shape=[jax.ShapeDtypeStruct((S, W), F32), jax.ShapeDtypeStruct((S, W), F32),
                   jax.ShapeDtypeStruct((S, W), BF16)],
        scratch_shapes=[pltpu.VMEM((CH, LANES), F32)] * (2 * len(DILATIONS)),
        compiler_params=_params(("parallel", "arbitrary")),
    )(qk, qk, h, h)


def _band_bwd(qk, h, do, stats, *, nha, S, v_off):
    CH = _band_chunk(S)
    W = nha * LANES

    def body(q_ref, k_ref, v_ref, do_ref, st_ref, dq_ref, dk_ref, dv_ref):
        c = pl.program_id(1)

        @pl.when(c == 0)
        def _():
            dk_ref[...] = jnp.zeros_like(dk_ref)
            dv_ref[...] = jnp.zeros_like(dv_ref)

        biases = _band_biases()
        for gi, d in enumerate(DILATIONS):
            nbc = CH // (Q_BLOCK * d)

            def step(it, carry, gi=gi, d=d, nbc=nbc):
                rq, rk, bias = _band_rows(it, c, d, nbc, biases)
                q = q_ref[rq, :].astype(BF16)
                k = k_ref[rk, :].astype(BF16)
                v = v_ref[rk, :].astype(BF16)
                g = do_ref[rq, :].astype(BF16)
                st = st_ref[rq, :]
                p = jnp.exp(_dot(q, k, _NT) * ATT_SCALE + (bias - st[:, 0:1]))
                ds = (p * (_dot(g, v, _NT) - st[:, 1:2]) * ATT_SCALE).astype(BF16)
                dq = _dot(ds, k, _NN)
                if gi == 0:
                    dq_ref[rq, :] = dq
                else:
                    dq_ref[rq, :] += dq
                dk_ref[rk, :] += _dot(ds, q, _TN)
                dv_ref[rk, :] += _dot(p.astype(BF16), g, _TN)
                return carry

            lax.fori_loop(0, CH // Q_BLOCK, step, 0, unroll=BAND_UNROLL)

    chunk = pl.BlockSpec((CH, LANES), lambda hd, c: (c, hd))
    whole = pl.BlockSpec((S, LANES), lambda hd, c: (0, hd))
    return pl.pallas_call(
        body, name="band_bwd", grid=(nha, S // CH),
        in_specs=[chunk,
                  pl.BlockSpec((S, LANES), lambda hd, c: (0, nha + hd)),
                  pl.BlockSpec((S, LANES), lambda hd, c: (0, v_off // LANES + hd)),
                  chunk, chunk],
        out_specs=[chunk, whole, whole],
        out_shape=[jax.ShapeDtypeStruct((S, W), F32)] * 3,
        compiler_params=_params(("parallel", "arbitrary")),
    )(qk, qk, h, do, stats)


def _gate_bwd(dup, w, o, h, lane0, *, z_off, W, S, name, do_dtype):
    tr = _tile(S, GATE_BWD_ROWS, 8)
    D = dup.shape[1]
    assert z_off % W == 0

    def body(d_ref, w_ref, o_ref, z_ref, l0_ref, do_ref, dz_ref, st_ref):
        g = _dot(d_ref[...], w_ref[...], _NN)
        out, z = o_ref[...], z_ref[...]
        sg = _sigmoid(z)
        do = g * (z * sg)
        do_ref[...] = do.astype(do_dtype)
        dz_ref[...] = (g * out * (sg * (1.0 + z * (1.0 - sg)))).astype(BF16)
        delta = jnp.concatenate([jnp.broadcast_to(jnp.sum(t, axis=1, keepdims=True), t.shape)
                                 for t in _heads(do * out)], axis=1)
        first = lax.broadcasted_iota(jnp.int32, delta.shape, 1) % LANES == 0
        st_ref[...] = jnp.where(first, l0_ref[...], delta)

    rows = pl.BlockSpec((tr, W), lambda i: (i, 0))
    return pl.pallas_call(
        body, name=name, grid=(S // tr,),
        in_specs=[pl.BlockSpec((tr, D), lambda i: (i, 0)),
                  pl.BlockSpec((D, W), lambda i: (0, 0), pipeline_mode=pl.Buffered(1)),
                  rows, pl.BlockSpec((tr, W), lambda i: (i, z_off // W)), rows],
        out_specs=[rows, rows, rows],
        out_shape=[jax.ShapeDtypeStruct((S, W), do_dtype), jax.ShapeDtypeStruct((S, W), BF16),
                   jax.ShapeDtypeStruct((S, W), F32)],
        compiler_params=_params(("parallel",)),
    )(dup, w, o, h, lane0)


def _split3(x):
    hi = x.astype(BF16)
    r = x - hi.astype(F32)
    mid = r.astype(BF16)
    lo = (r - mid.astype(F32)).astype(BF16)
    return hi, mid, lo


def _tri_dot(x, tri):
    hi, mid, lo = _split3(x)
    return _dot(hi, tri, _NN) + (_dot(mid, tri, _NN) + _dot(lo, tri, _NN))


def _log1p(u):
    w = 1.0 + u
    return jnp.where(w == 1.0, u, jnp.log(w) * (u / jnp.where(w == 1.0, 1.0, w - 1.0)))


def _lane_blocks(t):
    R, S = t.shape
    return t.reshape(R, S // LANES, LANES).transpose(1, 0, 2)


def _lane_unblocks(t):
    nb, R, _ = t.shape
    return t.transpose(1, 0, 2).reshape(R, nb * LANES)


def _forget_cumsum(f3, bf):
    nb, R, _ = f3.shape

    def body(f_ref, b_ref, c_ref):
        row = lax.broadcasted_iota(jnp.int32, (LANES, LANES), 0)
        colm = lax.broadcasted_iota(jnp.int32, (LANES, LANES), 1)
        tri = (row <= colm).astype(BF16)
        bias = b_ref[...]

        def step(n, carry):
            z = f_ref[n] + bias
            logf = jnp.minimum(z, 0.0) - _log1p(jnp.exp(-jnp.abs(z)))
            c = _tri_dot(logf, tri) + carry
            c_ref[n] = c
            return jnp.broadcast_to(c[:, LANES - 1:LANES], (R, LANES))

        lax.fori_loop(0, nb, step, jnp.zeros((R, LANES), F32))

    return pl.pallas_call(
        body, name="forget_cumsum", out_shape=jax.ShapeDtypeStruct(f3.shape, F32),
        compiler_params=_params(),
    )(f3, bf)


def _forget_bwd(dcq3, dck3, f3, bf):
    nb, R, _ = f3.shape

    def body(dcq_ref, dck_ref, f_ref, b_ref, df_ref, db_ref):
        row = lax.broadcasted_iota(jnp.int32, (LANES, LANES), 0)
        colm = lax.broadcasted_iota(jnp.int32, (LANES, LANES), 1)
        tri = (row >= colm).astype(BF16)
        bias = b_ref[...]

        def step(t, carry):
            tail, tot = carry
            n = nb - 1 - t
            r = _tri_dot(dcq_ref[n] + dck_ref[n], tri) + tail
            df = r * _sigmoid(-(f_ref[n] + bias))
            df_ref[n] = df
            tot = tot + jnp.broadcast_to(jnp.sum(df, axis=1, keepdims=True), (R, LANES))
            return jnp.broadcast_to(r[:, 0:1], (R, LANES)), tot

        _, tot = lax.fori_loop(0, nb, step, (jnp.zeros((R, LANES), F32), jnp.zeros((R, LANES), F32)))
        db_ref[...] = tot

    return pl.pallas_call(
        body, name="forget_bwd",
        out_shape=[jax.ShapeDtypeStruct(f3.shape, F32), jax.ShapeDtypeStruct((R, LANES), F32)],
        compiler_params=_params(),
    )(dcq3, dck3, f3, bf)


def _causal(t):
    return lax.broadcasted_iota(jnp.int32, (t, t), 0) >= lax.broadcasted_iota(jnp.int32, (t, t), 1)


def _fox_fwd(qkv, ck, h, *, z_off, nhb, S, tq):
    nq = S // tq
    ts = td = min(FOX_ROWS, tq)
    n_sub = n_diag = tq // ts

    def body(q_ref, k_ref, v_ref, ck_ref, z_ref, o_ref, lq_ref, g_ref):
        i = pl.program_id(1)
        qs = [q_ref[r * ts:(r + 1) * ts, :] for r in range(n_sub)]

        def absorb(q, k, v, ckj, m, l, acc, row0=None):
            s = _dot(q, k, _NT) - ckj
            if row0 is not None:
                rows = row0 + lax.broadcasted_iota(jnp.int32, s.shape, 0)
                s = jnp.where(rows >= lax.broadcasted_iota(jnp.int32, s.shape, 1), s, NEG)
            m_new = jnp.maximum(m, jnp.max(s, axis=1, keepdims=True))
            alpha = jnp.exp(m - m_new)
            p = jnp.exp(s - m_new)
            return m_new, alpha * l + jnp.sum(p, axis=1, keepdims=True), alpha * acc + _dot(p.astype(BF16), v, _NN)

        def step(j, carry):
            off = pl.multiple_of(j * tq, tq)
            k, v, ckj = k_ref[pl.ds(off, tq), :], v_ref[pl.ds(off, tq), :], ck_ref[j]
            return tuple(absorb(qs[r], k, v, ckj, *carry[r]) for r in range(n_sub))

        init = tuple((jnp.full((ts, 1), NEG, F32), jnp.zeros((ts, 1), F32), jnp.zeros((ts, LANES), F32))
                     for _ in range(n_sub))
        carry = lax.fori_loop(0, i, step, init)

        off = pl.multiple_of(i * tq, tq)
        k, v, cki = k_ref[pl.ds(off, tq), :], v_ref[pl.ds(off, tq), :], ck_ref[i]
        for f in range(n_diag):
            rows = slice(f * td, (f + 1) * td)
            part = slice(f * td % ts, f * td % ts + td)
            m, l, acc = (t[part] for t in carry[f * td // ts])
            nk = (f + 1) * td
            m, l, acc = absorb(q_ref[rows, :], k[:nk], v[:nk], cki[:, :nk], m, l, acc, row0=f * td)
            out = acc / l
            z = z_ref[rows, :]
            o_ref[rows, :] = out
            lq_ref[rows, :] = jnp.broadcast_to(m + jnp.log(l), (td, LANES))
            g_ref[rows, :] = (out * (z * _sigmoid(z))).astype(BF16)

    blk = pl.BlockSpec((tq, LANES), lambda hd, i: (i, hd))
    return pl.pallas_call(
        body, name="fox_fwd", grid=(nhb, nq),
        in_specs=[blk,
                  pl.BlockSpec((S, LANES), lambda hd, i: (0, nhb + hd), pipeline_mode=pl.Buffered(1)),
                  pl.BlockSpec((S, LANES), lambda hd, i: (0, 2 * nhb + hd), pipeline_mode=pl.Buffered(1)),
                  pl.BlockSpec((None, nq, 1, tq), lambda hd, i: (hd, 0, 0, 0)),
                  pl.BlockSpec((tq, LANES), lambda hd, i: (i, z_off // LANES + hd))],
        out_specs=[blk, blk, blk],
        out_shape=[jax.ShapeDtypeStruct((S, nhb * LANES), F32), jax.ShapeDtypeStruct((S, nhb * LANES), F32),
                   jax.ShapeDtypeStruct((S, nhb * LANES), BF16)],
        compiler_params=_params(("parallel", "arbitrary")),
    )(qkv, qkv, qkv, ck, h)


def _fox_bwd(qkv, do, stats, ck, *, nhb, S, tq, side=None):
    nq = S // tq
    ts, td = min(FOX_ROWS, tq), tq // FOX_DIAG_SUB
    n_side = 0 if side is None else len(side.ins)

    def body(*refs):
        q_ref, k_ref, v_ref, do_ref, st_ref, ck_ref = refs[:6]
        side_in = refs[6:6 + n_side]
        dq_ref, dk_ref, dv_ref, dr_ref, dc_ref = refs[6 + n_side:11 + n_side]
        side_out, sems = refs[11 + n_side:11 + 2 * n_side], refs[11 + 2 * n_side:]
        j = pl.program_id(1)
        if side is not None:
            @pl.when((pl.program_id(0) == 0) & (j == 0))
            def _():
                side.start(side_in, side_out, *sems)

        @pl.when(j == 0)
        def _():
            dq_ref[...] = jnp.zeros_like(dq_ref)
            dr_ref[...] = jnp.zeros_like(dr_ref)

        k = k_ref[...]
        v = v_ref[...]
        ckv = ck_ref[...]

        def group(start, n, nk, carry, row0=None):
            dk, dv, dc = carry
            rows = pl.ds(pl.multiple_of(start, n), n)
            q, g, st = q_ref[rows, :], do_ref[rows, :], st_ref[rows, :]
            p = jnp.exp(_dot(q, k[:nk], _NT) - ckv[:, :nk] - st[:, 0:1])
            if row0 is not None:
                rr = row0 + lax.broadcasted_iota(jnp.int32, (n, nk), 0)
                p = jnp.where(rr >= lax.broadcasted_iota(jnp.int32, (n, nk), 1), p, 0.0)
            ds = p * (_dot(g, v[:nk], _NT) - st[:, 1:2])
            dr_ref[rows, :] += jnp.broadcast_to(jnp.sum(ds, axis=1, keepdims=True), (n, LANES))
            dsb = ds.astype(BF16)
            dq_ref[rows, :] += _dot(dsb, k[:nk], _NN)
            upd = (_dot(dsb, q, _TN), _dot(p.astype(BF16), g, _TN), -jnp.sum(ds, axis=0, keepdims=True))
            if nk < tq:
                upd = (jnp.concatenate([upd[0], jnp.zeros((tq - nk, LANES), F32)], axis=0),
                       jnp.concatenate([upd[1], jnp.zeros((tq - nk, LANES), F32)], axis=0),
                       jnp.concatenate([upd[2], jnp.zeros((1, tq - nk), F32)], axis=1))
            return dk + upd[0], dv + upd[1], dc + upd[2]

        def step(i, carry):
            for r in range(tq // ts):
                carry = group(i * tq + r * ts, ts, tq, carry)
            return carry

        carry = (jnp.zeros((tq, LANES), F32), jnp.zeros((tq, LANES), F32), jnp.zeros((1, tq), F32))
        for f in range(FOX_DIAG_SUB):
            carry = group(j * tq + f * td, td, (f + 1) * td, carry, row0=f * td)
        dk, dv, dc = lax.fori_loop(j + 1, nq, step, carry)
        dk_ref[...] = dk.astype(BF16)
        dv_ref[...] = dv.astype(BF16)
        dc_ref[...] = dc

        @pl.when(j == nq - 1)
        def _():
            dq_ref[...] = dq_ref[...] * ATT_SCALE

        if side is not None:
            @pl.when((pl.program_id(0) == nhb - 1) & (j == nq - 1))
            def _():
                side.middle(side_in, side_out, *sems)
                side.finish(side_in, side_out, *sems)

    def whole(off):
        return pl.BlockSpec((S, LANES), lambda h, j: (0, off + h))

    kv_blk = pl.BlockSpec((tq, LANES), lambda h, j: (j, h))
    c_blk = pl.BlockSpec((None, None, 1, tq), lambda h, j: (h, j, 0, 0))
    def once(off):
        return pl.BlockSpec((S, LANES), lambda h, j: (0, off + h), pipeline_mode=pl.Buffered(1))

    in_specs = [once(0),
                pl.BlockSpec((tq, LANES), lambda h, j: (j, nhb + h)),
                pl.BlockSpec((tq, LANES), lambda h, j: (j, 2 * nhb + h)),
                once(0), once(0), c_blk]
    out_specs = [whole(0), kv_blk, kv_blk, whole(0), c_blk]
    out_shape = [jax.ShapeDtypeStruct((S, nhb * LANES), F32),
                 jax.ShapeDtypeStruct((S, nhb * LANES), BF16),
                 jax.ShapeDtypeStruct((S, nhb * LANES), BF16),
                 jax.ShapeDtypeStruct((S, nhb * LANES), F32),
                 jax.ShapeDtypeStruct((nhb, nq, 1, tq), F32)]
    args = (qkv, qkv, qkv, do, stats, ck)
    if side is None:
        return pl.pallas_call(
            body, name="fox_bwd", grid=(nhb, nq), in_specs=in_specs, out_specs=out_specs, out_shape=out_shape,
            compiler_params=_params(("parallel", "arbitrary")),
        )(*args)
    res = pl.pallas_call(
        body, name="fox_bwd_xchg", grid=(nhb, nq), in_specs=in_specs + [_ANY] * n_side,
        out_specs=out_specs + [_ANY] * n_side, out_shape=out_shape + side.outs,
        scratch_shapes=_side_scratch(side), compiler_params=_params(("arbitrary", "arbitrary")),
    )(*args, *side.ins)
    return (*res[:5], res[5:])


def _up_merge(ga, gb, w_a, w_b, h, b_gate, *, g_off, D, S):
    tr = _tile(S, 256, 8)
    W = ga.shape[1]
    assert g_off % D == 0

    def body(ga_ref, gb_ref, wa_ref, wb_ref, la_ref, lb_ref, b_ref, ua_ref, ub_ref, mix_ref):
        ua = _dot(ga_ref[...], wa_ref[...], _NT)
        ub = _dot(gb_ref[...], wb_ref[...], _NT)
        ua_ref[...] = ua
        ub_ref[...] = ub
        mix_ref[...] = (_sigmoid(la_ref[...] + b_ref[:, :D]) * ua
                        + _sigmoid(lb_ref[...] + b_ref[:, D:]) * ub).astype(BF16)

    act = pl.BlockSpec((tr, W), lambda i: (i, 0))
    rows = pl.BlockSpec((tr, D), lambda i: (i, 0))
    whole = pl.BlockSpec((D, W), lambda i: (0, 0), pipeline_mode=pl.Buffered(1))
    return pl.pallas_call(
        body, name="up_merge", grid=(S // tr,),
        in_specs=[act, act, whole, whole,
                  pl.BlockSpec((tr, D), lambda i: (i, g_off // D)), pl.BlockSpec((tr, D), lambda i: (i, g_off // D + 1)),
                  pl.BlockSpec((1, 2 * D), lambda i: (0, 0))],
        out_specs=[rows, rows, rows],
        out_shape=[jax.ShapeDtypeStruct((S, D), F32), jax.ShapeDtypeStruct((S, D), F32),
                   jax.ShapeDtypeStruct((S, D), BF16)],
        compiler_params=_params(("parallel",)),
    )(ga, gb, w_a, w_b, h, h, b_gate)


def _acc_rows(i, ref, val):
    @pl.when(i == 0)
    def _():
        ref[...] = val

    @pl.when(i > 0)
    def _():
        ref[...] += val


def _deepnorm_merge_bwd(dxn, z, g, w, up_a, up_b, h, b_gate, *, g_off, D, S):
    tr = _tile(S, 256, 8)
    assert g_off % D == 0

    def body(dx_ref, z_ref, g_ref, w_ref, ua_ref, ub_ref, la_ref, lb_ref, b_ref,
             dz_ref, dzb_ref, dua_ref, dub_ref, dla_ref, dlb_ref, sg_ref, sb_ref, sa_ref, sbb_ref):
        i = pl.program_id(0)
        dx, zv = dx_ref[...], z_ref[...]
        mu = jnp.mean(zv, axis=1, keepdims=True)
        zc = zv - mu
        rstd = lax.rsqrt(jnp.mean(zc * zc, axis=1, keepdims=True) + LN_EPS)
        u = zc * rstd
        du = dx * g_ref[...]
        dz = rstd * (du - jnp.mean(du, axis=1, keepdims=True) - u * jnp.mean(du * u, axis=1, keepdims=True))
        dzb = dz.astype(BF16)
        dz_ref[...] = dz
        dzb_ref[...] = dzb
        _acc_rows(i, sg_ref, jnp.sum(dx * u, axis=0, keepdims=True))
        _acc_rows(i, sb_ref, jnp.sum(dx, axis=0, keepdims=True))
        dm = _dot(dzb, w_ref[...], _NT)
        ga, gb = _sigmoid(la_ref[...] + b_ref[:, :D]), _sigmoid(lb_ref[...] + b_ref[:, D:])
        dua_ref[...] = (ga * dm).astype(BF16)
        dub_ref[...] = (gb * dm).astype(BF16)
        dla = ua_ref[...] * dm * (ga * (1.0 - ga))
        dlb = ub_ref[...] * dm * (gb * (1.0 - gb))
        dla_ref[...] = dla.astype(BF16)
        dlb_ref[...] = dlb.astype(BF16)
        _acc_rows(i, sa_ref, jnp.sum(dla, axis=0, keepdims=True))
        _acc_rows(i, sbb_ref, jnp.sum(dlb, axis=0, keepdims=True))

    rows = pl.BlockSpec((tr, D), lambda i: (i, 0))
    vec = pl.BlockSpec((1, D), lambda i: (0, 0))
    return pl.pallas_call(
        body, name="deepnorm_merge_bwd", grid=(S // tr,),
        in_specs=[rows, rows, vec, pl.BlockSpec((D, D), lambda i: (0, 0), pipeline_mode=pl.Buffered(1)), rows, rows,
                  pl.BlockSpec((tr, D), lambda i: (i, g_off // D)), pl.BlockSpec((tr, D), lambda i: (i, g_off // D + 1)),
                  pl.BlockSpec((1, 2 * D), lambda i: (0, 0))],
        out_specs=[rows] * 6 + [vec] * 4,
        out_shape=[jax.ShapeDtypeStruct((S, D), F32)] + [jax.ShapeDtypeStruct((S, D), BF16)] * 5
        + [jax.ShapeDtypeStruct((1, D), F32)] * 4,
        compiler_params=_params(("arbitrary",)),
    )(dxn, z, g, w, up_a, up_b, h, h, b_gate)


def _out_proj_deepnorm(mix, w, x, g, b, *, alpha, D, S):
    tr = _tile(S, 512, 8)

    def body(m_ref, w_ref, x_ref, g_ref, b_ref, xn_ref, xb_ref, z_ref):
        z = alpha * x_ref[...] + _dot(m_ref[...], w_ref[...], _NN)
        mu = jnp.mean(z, axis=1, keepdims=True)
        zc = z - mu
        var = jnp.mean(zc * zc, axis=1, keepdims=True)
        xn = zc * lax.rsqrt(var + LN_EPS) * g_ref[...] + b_ref[...]
        xn_ref[...] = xn
        xb_ref[...] = xn.astype(BF16)
        z_ref[...] = z

    rows = pl.BlockSpec((tr, D), lambda i: (i, 0))
    vec = pl.BlockSpec((1, D), lambda i: (0, 0))
    return pl.pallas_call(
        body, name="out_proj_deepnorm", grid=(S // tr,),
        in_specs=[rows, pl.BlockSpec((D, D), lambda i: (0, 0)), rows, vec, vec],
        out_specs=[rows, rows, rows],
        out_shape=[jax.ShapeDtypeStruct((S, D), F32), jax.ShapeDtypeStruct((S, D), BF16),
                   jax.ShapeDtypeStruct((S, D), F32)],
        compiler_params=_params(("parallel",)),
    )(mix, w, x, g, b)


def _loss_head(y, target, *, D, S):
    tr = _tile(S, 256, 8)

    def body(y_ref, t_ref, d_ref, l_ref):
        err = y_ref[...] - t_ref[...]
        d_ref[...] = err * (1.0 / D)
        part = 0.5 * jnp.sum(jnp.sum(err * err, axis=1, keepdims=True) * (1.0 / D), axis=0, keepdims=True)
        _acc_rows(pl.program_id(0), l_ref, jnp.broadcast_to(part, (1, LANES)))

    return pl.pallas_call(
        body, name="loss_head", grid=(S // tr,),
        in_specs=[pl.BlockSpec((tr, D), lambda i: (i, 0))] * 2,
        out_specs=[pl.BlockSpec((tr, D), lambda i: (i, 0)), pl.BlockSpec((1, LANES), lambda i: (0, 0))],
        out_shape=[jax.ShapeDtypeStruct((S, D), F32), jax.ShapeDtypeStruct((1, LANES), F32)],
        compiler_params=_params(("arbitrary",)),
    )(y, target)


_MESH = pl.DeviceIdType.MESH
_ANY = pl.BlockSpec(memory_space=pl.ANY)


class _Gather:
    def __init__(self, xs):
        self.ins = list(xs)
        self.outs = [jax.ShapeDtypeStruct((NDEV,) + x.shape, x.dtype) for x in xs]

    def _ctx(self, a, x_ref, out_ref, send_sems, recv_sems):
        x, y, c = lax.axis_index("x"), lax.axis_index("y"), lax.axis_index("c")
        me, sibling = (x, y, c), (x, y, 1 - c)
        chips = [(1 - x, y), (x, 1 - y), (1 - x, 1 - y)]

        def slot(px, py, pc):
            return out_ref.at[4 * px + 2 * py + pc]

        def copy(k, block, to, src=None):
            return pltpu.make_async_remote_copy(
                src_ref=slot(*block) if src is None else src, dst_ref=slot(*block),
                send_sem=send_sems.at[7 * a + k], recv_sem=recv_sems.at[7 * a + k], device_id=to, device_id_type=_MESH)

        def first():
            return [copy(0, me, sibling, src=x_ref)] + [copy(1 + j, me, (*chip, c), src=x_ref)
                                                        for j, chip in enumerate(chips)]

        def passed():
            return [copy(4 + j, (*chip, c), sibling) for j, chip in enumerate(chips)]

        return c, me, sibling, chips, slot, copy, first, passed

    def start(self, in_refs, out_refs, send_sems, recv_sems, local_sems):
        for a, (x_ref, out_ref) in enumerate(zip(in_refs, out_refs)):
            _, me, _, _, slot, _, first, _ = self._ctx(a, x_ref, out_ref, send_sems, recv_sems)
            pltpu.make_async_copy(x_ref, slot(*me), local_sems.at[a]).start()
            for cp in first():
                cp.start()

    def middle(self, in_refs, out_refs, send_sems, recv_sems, local_sems):
        for a, (x_ref, out_ref) in enumerate(zip(in_refs, out_refs)):
            c, me, _, chips, _, copy, _, passed = self._ctx(a, x_ref, out_ref, send_sems, recv_sems)
            for j, (chip, cp) in enumerate(zip(chips, passed())):
                copy(1 + j, (*chip, c), me).wait_recv()
                cp.start()

    def finish(self, in_refs, out_refs, send_sems, recv_sems, local_sems):
        for a, (x_ref, out_ref) in enumerate(zip(in_refs, out_refs)):
            c, me, sibling, chips, slot, copy, first, passed = self._ctx(a, x_ref, out_ref, send_sems, recv_sems)
            copy(0, sibling, me).wait_recv()
            for j, chip in enumerate(chips):
                copy(4 + j, (*chip, 1 - c), me).wait_recv()
            for cp in first() + passed():
                cp.wait_send()
            pltpu.make_async_copy(x_ref, slot(*me), local_sems.at[a]).wait()


class _Exchange:
    def __init__(self, gs):
        self.ins = list(gs)
        self.outs = [jax.ShapeDtypeStruct(g.shape, g.dtype) for g in gs]

    def _copies(self, a, g_ref, out_ref, send_sems, recv_sems, arrivals=True):
        x, y, c = lax.axis_index("x"), lax.axis_index("y"), lax.axis_index("c")
        me = 4 * x + 2 * y + c
        out, back = [], []
        for k in range(1, NDEV):
            px, py, pc = x ^ ((k >> 2) & 1), y ^ ((k >> 1) & 1), c ^ (k & 1)
            peer = 4 * px + 2 * py + pc
            sems = dict(send_sem=send_sems.at[7 * a + k - 1], recv_sem=recv_sems.at[7 * a + k - 1],
                        device_id=(px, py, pc), device_id_type=_MESH)
            out.append(pltpu.make_async_remote_copy(src_ref=g_ref.at[peer], dst_ref=out_ref.at[me], **sems))
            if arrivals:
                back.append(pltpu.make_async_remote_copy(src_ref=g_ref.at[me], dst_ref=out_ref.at[peer], **sems))
        return me, out, back

    def start(self, in_refs, out_refs, send_sems, recv_sems, local_sems):
        for a, (g_ref, out_ref) in enumerate(zip(in_refs, out_refs)):
            me, out, _ = self._copies(a, g_ref, out_ref, send_sems, recv_sems, arrivals=False)
            pltpu.make_async_copy(g_ref.at[me], out_ref.at[me], local_sems.at[a]).start()
            for cp in out:
                cp.start()

    def middle(self, in_refs, out_refs, send_sems, recv_sems, local_sems):
        pass

    def finish(self, in_refs, out_refs, send_sems, recv_sems, local_sems):
        for a, (g_ref, out_ref) in enumerate(zip(in_refs, out_refs)):
            me, out, back = self._copies(a, g_ref, out_ref, send_sems, recv_sems)
            for cp in back:
                cp.wait_recv()
            for cp in out:
                cp.wait_send()
            pltpu.make_async_copy(g_ref.at[me], out_ref.at[me], local_sems.at[a]).wait()


def _side_scratch(side):
    n = len(side.ins)
    return [pltpu.SemaphoreType.DMA((7 * n,)), pltpu.SemaphoreType.DMA((7 * n,)), pltpu.SemaphoreType.DMA((n,))]


def _run_side(side, *, name):
    n = len(side.ins)

    def body(*refs):
        ins, outs, sems = refs[:n], refs[n:2 * n], refs[2 * n:]
        side.start(ins, outs, *sems)
        side.middle(ins, outs, *sems)
        side.finish(ins, outs, *sems)

    return pl.pallas_call(
        body, name=name, out_shape=side.outs, in_specs=[_ANY] * n, out_specs=[_ANY] * n,
        scratch_shapes=_side_scratch(side),
    )(*side.ins)


def _all_gather(x, *, name):
    return _run_side(_Gather([x]), name=name)[0]


def _sum_slots(parts, *, name):
    n, R, C = parts.shape
    tr, tc = _tile(R, 64, 16), _tile(C, 2048)

    def body(p_ref, o_ref):
        acc = p_ref[0].astype(F32)
        for k in range(1, n):
            acc = acc + p_ref[k].astype(F32)
        o_ref[...] = acc

    return pl.pallas_call(
        body, name=name, grid=(R // tr, C // tc),
        in_specs=[pl.BlockSpec((n, tr, tc), lambda i, j: (0, i, j))],
        out_specs=pl.BlockSpec((tr, tc), lambda i, j: (i, j)),
        out_shape=jax.ShapeDtypeStruct((R, C), F32),
        compiler_params=_params(("parallel", "parallel")),
    )(parts)


def _adamw(w, g, m, v, *, name):
    R, C = w.shape
    budget = 2 << 20
    tr = R if R * C * 4 <= budget else _tile(R, max(8, (budget // (C * 4)) // 8 * 8), 8)

    def body(w_ref, g_ref, m_ref, v_ref, d_ref, nm_ref, nv_ref):
        gv = g_ref[...]
        nm = ADAM_B1 * m_ref[...] + (1.0 - ADAM_B1) * gv
        nv = ADAM_B2 * v_ref[...] + (1.0 - ADAM_B2) * (gv * gv)
        m_hat = nm / (1.0 - ADAM_B1 ** ADAM_STEP)
        v_hat = nv / (1.0 - ADAM_B2 ** ADAM_STEP)
        d_ref[...] = -ADAM_LR * (m_hat / (jnp.sqrt(v_hat) + ADAM_EPS) + ADAM_WD * w_ref[...])
        nm_ref[...] = nm
        nv_ref[...] = nv

    spec = pl.BlockSpec((tr, C), lambda i: (i, 0))
    return pl.pallas_call(
        body, name=name, grid=(R // tr,), in_specs=[spec] * 4, out_specs=[spec] * 3,
        out_shape=[jax.ShapeDtypeStruct((R, C), F32)] * 3,
        compiler_params=_params(("parallel",)),
    )(w, g, m, v)


def _pad_cols(w, *, nq, ng):
    nf = w.shape[-1] - nq - ng
    pad = jnp.zeros(w.shape[:-1] + (FORGET_PAD - nf,), w.dtype)
    return jnp.concatenate([w[..., :nq], w[..., nq + nf:], w[..., nq:nq + nf], pad], axis=-1)


def _unpad_cols(w, *, nq, ng, nf):
    return jnp.concatenate([w[..., :nq], w[..., nq + ng:nq + ng + nf], w[..., nq:nq + ng]], axis=-1)


def kernel(x, w_in, b_forget, b_gate, w_up_a, w_up_b, w_out, ln_g, ln_b, loss_target, m_w_in, m_b_forget, m_b_gate, m_w_up_a, m_w_up_b, m_w_out, m_ln_g, m_ln_b, v_w_in, v_b_forget, v_b_gate, v_w_up_a, v_w_up_b, v_w_out, v_ln_g, v_ln_b):
    depth = w_in.shape[0]
    _, S, D = x.shape
    WA, WB = w_up_a.shape[1], w_up_b.shape[1]
    nha, nhb = WA // HEAD_DIM, WB // HEAD_DIM
    assert nhb == b_forget.shape[1] and WA == WB and nhb <= HEAD_ROWS
    nq, ng, nf = 4 * WA + 4 * WB, 2 * D, nhb
    assert w_in.shape[2] == nq + nf + ng
    ncp = nq + ng + FORGET_PAD
    off_va, off_za, off_b, off_zb, off_g, off_f = 2 * WA, 3 * WA, 4 * WA, 4 * WA + 3 * WB, nq, nq + ng
    alpha = float((2 * depth) ** 0.25)
    dsh = D // NDEV
    tq_f, tq_b = _tile(S, FOX_TILE_FWD), _tile(S, FOX_TILE_BWD)

    x0 = x[0]
    target = loss_target[0]
    cos, sin = _rope_tables(S)

    w_in_p = _pad_cols(w_in, nq=nq, ng=ng).astype(BF16)
    w_up_a_t = jnp.swapaxes(w_up_a, 1, 2).astype(BF16)
    w_up_b_t = jnp.swapaxes(w_up_b, 1, 2).astype(BF16)
    w_out_c = w_out.astype(BF16)
    W_in, W_ua, W_ub, W_out = ([None] * depth for _ in range(4))

    def others(l):
        return [w_up_a_t[l], w_up_b_t[l], w_out_c[l]]

    def keep_others(l, got):
        W_ua[l], W_ub[l], W_out[l] = got[0].reshape(D, WA), got[1].reshape(D, WB), got[2].reshape(D, D)

    W_in[0] = _run_side(_Gather([w_in_p[0]]), name="gather_w_in_0")[0].reshape(D, ncp)

    bf_rows = [jnp.broadcast_to(jnp.pad(b_forget[l], (0, HEAD_ROWS - nhb))[:, None], (HEAD_ROWS, LANES))
               for l in range(depth)]

    saved = []
    xf, xb = x0, x0.astype(BF16)
    for l in range(depth):
        wanted = (others(0) if l == 0 else []) + ([w_in_p[l + 1]] + others(l + 1) if l + 1 < depth else [])
        if wanted:
            h, got = _mm(xb, W_in[l], name="in_proj_gather", tm=1024, tn=1280, tk=2048, side=_Gather(wanted))
            if l == 0:
                keep_others(0, got[:3])
            if l + 1 < depth:
                W_in[l + 1] = got[-4].reshape(D, ncp)
                keep_others(l + 1, got[-3:])
        else:
            h = _mm(xb, W_in[l], name="in_proj", tm=1024, tn=1280, tk=2048)
        qk_a = _prep_a(h, cos, sin, W=WA, S=S)
        qkv_b = _prep_b(h, off=off_b, W=WB, S=S)
        out_a, lse_a, ga = _band_fwd(qk_a, h, nha=nha, S=S, v_off=off_va, z_off=off_za)
        f3 = _lane_blocks(jnp.pad(h[:, off_f:off_f + nhb].T, ((0, HEAD_ROWS - nhb), (0, 0))))
        c = _lane_unblocks(_forget_cumsum(f3, bf_rows[l]))[:nhb]
        out_b, lq_b, gb = _fox_fwd(qkv_b, c.reshape(nhb, S // tq_f, 1, tq_f), h, z_off=off_zb, nhb=nhb, S=S, tq=tq_f)
        ck = c.reshape(nhb, S // tq_b, 1, tq_b)
        bg = b_gate[l][None, :]
        up_a, up_b, mix = _up_merge(ga, gb, W_ua[l], W_ub[l], h, bg, g_off=off_g, D=D, S=S)
        xn_f, xn_b, z = _out_proj_deepnorm(mix, W_out[l], xf, ln_g[l][None, :], ln_b[l][None, :],
                                           alpha=alpha, D=D, S=S)
        saved.append(dict(xb=xb, h=h, qk_a=qk_a, qkv_b=qkv_b, out_a=out_a, lse_a=lse_a, ga=ga, f3=f3, ck=ck,
                          out_b=out_b, lq_b=lq_b, gb=gb, up_a=up_a, up_b=up_b, mix=mix, z=z, bg=bg))
        xf, xb = xn_f, xn_b

    dx, loss_row = _loss_head(xf, target, D=D, S=S)

    recv = [None] * depth
    small = [None] * depth
    pending = None
    for l in reversed(range(depth)):
        sv = saved[l]
        h = sv["h"]
        dz, dzb, dup_a, dup_b, dgl_a, dgl_b, d_lng, d_lnb, d_bga, d_bgb = _deepnorm_merge_bwd(
            dx, sv["z"], ln_g[l][None, :], W_out[l], sv["up_a"], sv["up_b"], h, sv["bg"], g_off=off_g, D=D, S=S)
        g_w_out = _mm(sv["mix"], dzb, ta=True, name="g_w_out", tm=512, tn=1024, tk=4096, out_dtype=BF16)
        g_w_ua = _mm(dup_a, sv["ga"], ta=True, name="g_w_up_a", tm=512, tn=1536, tk=4096, out_dtype=BF16)
        g_w_ub = _mm(dup_b, sv["gb"], ta=True, name="g_w_up_b", tm=512, tn=1536, tk=4096, out_dtype=BF16)
        do_a, dza, stats_a = _gate_bwd(dup_a, W_ua[l], sv["out_a"], h, sv["lse_a"], z_off=off_za, W=WA, S=S,
                                       name="gate_bwd_a", do_dtype=F32)
        do_b, dzb_, stats_b = _gate_bwd(dup_b, W_ub[l], sv["out_b"], h, sv["lq_b"], z_off=off_zb, W=WB, S=S,
                                        name="gate_bwd_b", do_dtype=BF16)
        dqa, dka, dva = _rope_bwd(*_band_bwd(sv["qk_a"], h, do_a, stats_a, nha=nha, S=S, v_off=off_va),
                                  cos, sin, W=WA, S=S)
        if pending is None:
            dqb, dkb, dvb, dcq, dck = _fox_bwd(sv["qkv_b"], do_b, stats_b, sv["ck"], nhb=nhb, S=S, tq=tq_b)
        else:
            dqb, dkb, dvb, dcq, dck, got = _fox_bwd(sv["qkv_b"], do_b, stats_b, sv["ck"], nhb=nhb, S=S, tq=tq_b,
                                                    side=_Exchange([pending]))
            recv[l + 1] = (got[0], *recv[l + 1])
        head_rows = lambda t: _lane_blocks(jnp.pad(t, ((0, HEAD_ROWS - nhb), (0, 0))))
        df3, d_bf = _forget_bwd(head_rows(dcq[:, ::LANES].T), head_rows(dck.reshape(nhb, S)), sv["f3"], bf_rows[l])
        dfl = jnp.pad(_lane_unblocks(df3)[:nhb].T, ((0, 0), (0, FORGET_PAD - nhb))).astype(BF16)
        dh = jnp.concatenate([dqa, dka, dva, dza, dqb.astype(BF16), dkb, dvb, dzb_, dgl_a, dgl_b, dfl], axis=1)
        chunks = lambda t: t.reshape(NDEV, dsh, t.shape[1])
        g_w_in, got3 = _mm(sv["xb"], dh, ta=True, name="g_w_in_xchg", tm=512, tn=1280, tk=4096, out_dtype=BF16,
                           side=_Exchange([chunks(g_w_ua), chunks(g_w_ub), chunks(g_w_out)]))
        if l > 0:
            dx = _mm(dh, W_in[l], tb=True, name="d_x", tm=1024, tn=512, tk=3328, add=dz, add_scale=alpha)
            pending, recv[l] = chunks(g_w_in), tuple(got3)
        else:
            dx, got1 = _mm(dh, W_in[l], tb=True, name="d_x_xchg", tm=1024, tn=512, tk=3328, add=dz, add_scale=alpha,
                           side=_Exchange([chunks(g_w_in)]))
            recv[l] = (got1[0], *got3)
        small[l] = jnp.concatenate([d_bf[:nhb, 0], jnp.zeros((LANES - nhb,), F32), d_bga[0], d_bgb[0],
                                    d_lng[0], d_lnb[0]])

    n_small = LANES + 2 * D + 2 * D
    flat = jnp.concatenate(small + [loss_row[0]])
    rows = -(-flat.shape[0] // (8 * LANES)) * 8
    flat = jnp.pad(flat, (0, rows * LANES - flat.shape[0])).reshape(rows, LANES)
    tot = _sum_slots(_all_gather(flat, name="gather_small"), name="sum_small").reshape(-1)
    loss = tot[depth * n_small]
    sm = tot[:depth * n_small].reshape(depth, n_small)
    g_bf, g_bg = sm[:, :nhb], sm[:, LANES:LANES + 2 * D]
    g_lg, g_lb = sm[:, LANES + 2 * D:LANES + 3 * D], sm[:, LANES + 3 * D:]

    g_in = jnp.stack([_unpad_cols(_sum_slots(recv[l][0], name="sum_w_in"), nq=nq, ng=ng, nf=nf) for l in range(depth)])
    g_ua = jnp.stack([_sum_slots(recv[l][1], name="sum_w_up_a").T for l in range(depth)])
    g_ub = jnp.stack([_sum_slots(recv[l][2], name="sum_w_up_b").T for l in range(depth)])
    g_out = jnp.stack([_sum_slots(recv[l][3], name="sum_w_out") for l in range(depth)])

    def adam(w, g, m, v, name):
        shp = w.shape
        d_, m_, v_ = _adamw(*[t.reshape(-1, shp[-1]) for t in (w, g, m, v)], name=name)
        return d_.reshape(shp), m_.reshape(shp), v_.reshape(shp)

    grads = [g_in, g_bf, g_bg, g_ua, g_ub, g_out, g_lg, g_lb]
    ws = [w_in, b_forget, b_gate, w_up_a, w_up_b, w_out, ln_g, ln_b]
    ms = [m_w_in, m_b_forget, m_b_gate, m_w_up_a, m_w_up_b, m_w_out, m_ln_g, m_ln_b]
    vs = [v_w_in, v_b_forget, v_b_gate, v_w_up_a, v_w_up_b, v_w_out, v_ln_g, v_ln_b]
    names = ["w_in", "b_forget", "b_gate", "w_up_a", "w_up_b", "w_out", "ln_g", "ln_b"]
    upd = [adam(w, g, m, v, f"adamw_{n}") for w, g, m, v, n in zip(ws, grads, ms, vs, names)]
    return (loss, dx[None], *grads, *[u[0] for u in upd], *[u[1] for u in upd], *[u[2] for u in upd])
```
